```python
import math
import jax, jax.numpy as jnp
from jax import lax
import numpy as np

D_MODEL = 1024
BATCH = 8
SEQ = 8192
DEPTH = 1

CHUNK = 64
Q_BLOCK = 128
HG_HEADS = 8
HG_DK = 128
HG_DV = D_MODEL // HG_HEADS
HG_KW = HG_HEADS * HG_DK
HG_VW = HG_HEADS * HG_DV
FOX_HEADS = 8
FOX_DH = 128
FOX_W = FOX_HEADS * FOX_DH
MEM_LEN = 256
MEM_HEADS = 4
MEM_DH = D_MODEL // MEM_HEADS
D_FF = 2816
N_BRANCH = 2
EPS = 1e-6
IN_SPLITS = [HG_KW, HG_KW, HG_VW, HG_VW, FOX_W, FOX_W, FOX_W, FOX_HEADS, N_BRANCH * D_MODEL]
IN_COLS = sum(IN_SPLITS)
IN_OFFSETS = list(np.cumsum(IN_SPLITS)[:-1])

kernel_name = "hgrn2_fox_macaron_sandwich_hybrid"


def rmsnorm(x, g):
    x32 = x.astype(jnp.float32)
    y = x32 * lax.rsqrt(jnp.mean(x32 * x32, axis=-1, keepdims=True) + EPS)
    return (y * g.astype(jnp.float32)).astype(x.dtype)


def swiglu(h, w_in, w_down):
    gate, up = jnp.split(h @ w_in, 2, axis=-1)
    return (jax.nn.silu(gate) * up) @ w_down


def hgrn2_chunkwise(q, f_logit, inp, lb):
    B, S = q.shape[0], q.shape[1]
    n = S // CHUNK
    f = lb + (1.0 - lb) * jax.nn.sigmoid(f_logit.astype(jnp.float32))
    logf = jnp.log(f)
    k = 1.0 - f
    q = jax.nn.silu(q.astype(jnp.float32))
    inp = inp.astype(jnp.float32)

    def to_chunks(t):
        return t.reshape(B, n, CHUNK, t.shape[2], t.shape[3]).transpose(1, 0, 3, 2, 4)

    qc, kc, ic = to_chunks(q), to_chunks(k), to_chunks(inp)
    bc = jnp.cumsum(to_chunks(logf), axis=3)
    mask = jnp.tril(jnp.ones((CHUNK, CHUNK), dtype=bool))[:, :, None]

    def step(state, xs):
        qt, kt, it, bt = xs
        diff = bt[:, :, :, None, :] - bt[:, :, None, :, :]
        decay = jnp.exp(jnp.where(mask, diff, -jnp.inf))
        a = jnp.einsum('bhtd,bhsd,bhtsd->bhts', qt, kt, decay)
        o_intra = jnp.einsum('bhts,bhsv->bhtv', a, it)
        o_inter = jnp.einsum('bhtd,bhdv->bhtv', qt * jnp.exp(bt), state)
        b_last = bt[:, :, -1:, :]
        new_state = jnp.exp(b_last[:, :, 0, :])[..., None] * state + jnp.einsum(
            'bhsd,bhsv->bhdv', kt * jnp.exp(b_last - bt), it)
        return new_state, o_intra + o_inter

    s0 = jnp.zeros((B, q.shape[2], HG_DK, HG_DV), jnp.float32)
    _, o = lax.scan(step, s0, (qc, kc, ic, bc))
    return o.transpose(1, 0, 3, 2, 4).reshape(B, S, HG_VW)


def fox_attention(q, k, v, f_logit):
    B, S, H, dh = q.shape
    n_blk = S // Q_BLOCK
    scale = 1.0 / math.sqrt(dh)
    c = jnp.cumsum(jax.nn.log_sigmoid(f_logit.astype(jnp.float32)), axis=1).transpose(0, 2, 1)
    k32 = k.astype(jnp.float32)
    v32 = v.astype(jnp.float32)
    qb = q.astype(jnp.float32).reshape(B, n_blk, Q_BLOCK, H, dh).transpose(1, 0, 3, 2, 4)
    cb = c.reshape(B, H, n_blk, Q_BLOCK).transpose(2, 0, 1, 3)
    kpos = jnp.arange(S)

    def block(args):
        qi, ci, blk = args
        s = jnp.einsum('bhqd,bkhd->bhqk', qi, k32) * scale
        s = s + ci[..., None] - c[:, :, None, :]
        qpos = blk * Q_BLOCK + jnp.arange(Q_BLOCK)
        s = jnp.where(kpos[None, :] <= qpos[:, None], s, -jnp.inf)
        p = jax.nn.softmax(s, axis=-1)
        return jnp.einsum('bhqk,bkhd->bqhd', p, v32)

    o = lax.map(block, (qb, cb, jnp.arange(n_blk)))
    return o.transpose(1, 0, 2, 3, 4).reshape(B, S, H * dh)


def mem_cross_attention(h, mem_n, w_mq, w_mkv, w_mo):
    B, S, _ = h.shape
    q = (h @ w_mq).reshape(B, S, MEM_HEADS, MEM_DH).astype(jnp.float32)
    k, v = jnp.split(mem_n @ w_mkv, 2, axis=-1)
    k = k.reshape(B, MEM_LEN, MEM_HEADS, MEM_DH).astype(jnp.float32)
    v = v.reshape(B, MEM_LEN, MEM_HEADS, MEM_DH).astype(jnp.float32)
    s = jnp.einsum('bqhd,bkhd->bhqk', q, k) * (1.0 / math.sqrt(MEM_DH))
    p = jax.nn.softmax(s, axis=-1)
    o = jnp.einsum('bhqk,bkhd->bqhd', p, v).reshape(B, S, D_MODEL).astype(h.dtype)
    return o @ w_mo


def _fwd_setup_inputs(seed: int = 0) -> dict:
    key = jax.random.key(seed)
    ks = iter(jax.random.split(key, 40))
    f32 = jnp.float32

    def w(shape, fan_in):
        return jax.random.normal(next(ks), shape, f32) * (fan_in ** -0.5)

    def gain(shape):
        return 1.0 + 0.05 * jax.random.normal(next(ks), shape, f32)

    L, D = DEPTH, D_MODEL
    return {
        "x": jax.random.normal(next(ks), (BATCH, SEQ, D), f32),
        "mem": jax.random.normal(next(ks), (BATCH, MEM_LEN, D), f32),
        "ffn1_pre_g": gain((L, D)),
        "ffn1_w_in": w((L, D, 2 * D_FF), D),
        "ffn1_w_down": w((L, D_FF, D), D_FF),
        "ffn1_post_g": gain((L, D)),
        "mix_pre_g": gain((L, D)),
        "w_in": w((L, D, IN_COLS), D),
        "hg_lb_logits": 0.5 * jax.random.normal(next(ks), (L + 1, HG_HEADS, HG_DK), f32),
        "hg_norm_g": gain((L, HG_VW)),
        "fox_f_bias": 1.0 + 0.5 * jax.random.normal(next(ks), (L, FOX_HEADS), f32),
        "w_branch_a": w((L, HG_VW, D), HG_VW),
        "w_branch_b": w((L, FOX_W, D), FOX_W),
        "b_gate": 0.02 * jax.random.normal(next(ks), (L, N_BRANCH * D), f32),
        "w_out": w((L, D, D), D),
        "mix_post_g": gain((L, D)),
        "mem_pre_g": gain((L, D)),
        "mem_kv_g": gain((L, D)),
        "w_mq": w((L, D, D), D),
        "w_mkv": w((L, D, 2 * D), D),
        "w_mo": w((L, D, D), D),
        "mem_post_g": gain((L, D)),
        "ffn2_pre_g": gain((L, D)),
        "ffn2_w_in": w((L, D, 2 * D_FF), D),
        "ffn2_w_down": w((L, D_FF, D), D_FF),
        "ffn2_post_g": gain((L, D)),
    }


def _fwd_reference(x, mem, ffn1_pre_g, ffn1_w_in, ffn1_w_down, ffn1_post_g,
              mix_pre_g, w_in, hg_lb_logits, hg_norm_g, fox_f_bias,
              w_branch_a, w_branch_b, b_gate, w_out, mix_post_g,
              mem_pre_g, mem_kv_g, w_mq, w_mkv, w_mo, mem_post_g,
              ffn2_pre_g, ffn2_w_in, ffn2_w_down, ffn2_post_g):
    B, S, D = x.shape
    lb_all = jnp.cumsum(jax.nn.softmax(hg_lb_logits.astype(jnp.float32), axis=0), axis=0)
    for l in range(DEPTH):
        h = rmsnorm(x, ffn1_pre_g[l])
        x = x + 0.5 * rmsnorm(swiglu(h, ffn1_w_in[l], ffn1_w_down[l]), ffn1_post_g[l])

        h = rmsnorm(x, mix_pre_g[l])
        q_a, f_a, i_a, g_a, q_b, k_b, v_b, f_b, gates = jnp.split(h @ w_in[l], IN_OFFSETS, axis=-1)

        o_a = hgrn2_chunkwise(q_a.reshape(B, S, HG_HEADS, HG_DK),
                              f_a.reshape(B, S, HG_HEADS, HG_DK),
                              i_a.reshape(B, S, HG_HEADS, HG_DV), lb_all[l])
        o_a = rmsnorm(o_a.astype(x.dtype), hg_norm_g[l]) * jax.nn.silu(g_a)
        y_a = o_a @ w_branch_a[l]

        o_b = fox_attention(q_b.reshape(B, S, FOX_HEADS, FOX_DH),
                            k_b.reshape(B, S, FOX_HEADS, FOX_DH),
                            v_b.reshape(B, S, FOX_HEADS, FOX_DH),
                            f_b + fox_f_bias[l])
        y_b = o_b.astype(x.dtype) @ w_branch_b[l]

        gate = jax.nn.sigmoid((gates + b_gate[l]).astype(jnp.float32)).reshape(B, S, N_BRANCH, D)
        y = (gate[:, :, 0] * y_a.astype(jnp.float32) + gate[:, :, 1] * y_b.astype(jnp.float32)).astype(x.dtype)
        x = x + rmsnorm(y @ w_out[l], mix_post_g[l])

        h = rmsnorm(x, mem_pre_g[l])
        mem_n = rmsnorm(mem, mem_kv_g[l])
        x = x + rmsnorm(mem_cross_attention(h, mem_n, w_mq[l], w_mkv[l], w_mo[l]), mem_post_g[l])

        h = rmsnorm(x, ffn2_pre_g[l])
        x = x + 0.5 * rmsnorm(swiglu(h, ffn2_w_in[l], ffn2_w_down[l]), ffn2_post_g[l])
    return x


import jax as _jax
import jax.numpy as _jnp

TWIN_FORMAT = 'train_step'
FWD_PARAMS = ['x', 'mem', 'ffn1_pre_g', 'ffn1_w_in', 'ffn1_w_down', 'ffn1_post_g', 'mix_pre_g', 'w_in', 'hg_lb_logits', 'hg_norm_g', 'fox_f_bias', 'w_branch_a', 'w_branch_b', 'b_gate', 'w_out', 'mix_post_g', 'mem_pre_g', 'mem_kv_g', 'w_mq', 'w_mkv', 'w_mo', 'mem_post_g', 'ffn2_pre_g', 'ffn2_w_in', 'ffn2_w_down', 'ffn2_post_g']
TWIN_WEIGHTS = ['ffn1_pre_g', 'ffn1_w_in', 'ffn1_w_down', 'ffn1_post_g', 'mix_pre_g', 'w_in', 'hg_lb_logits', 'hg_norm_g', 'fox_f_bias', 'w_branch_a', 'w_branch_b', 'b_gate', 'w_out', 'mix_post_g', 'mem_pre_g', 'mem_kv_g', 'w_mq', 'w_mkv', 'w_mo', 'mem_post_g', 'ffn2_pre_g', 'ffn2_w_in', 'ffn2_w_down', 'ffn2_post_g']
TWIN_DIFF_INPUT = 'x'
TWIN_INPUTS = ['x', 'mem', 'ffn1_pre_g', 'ffn1_w_in', 'ffn1_w_down', 'ffn1_post_g', 'mix_pre_g', 'w_in', 'hg_lb_logits', 'hg_norm_g', 'fox_f_bias', 'w_branch_a', 'w_branch_b', 'b_gate', 'w_out', 'mix_post_g', 'mem_pre_g', 'mem_kv_g', 'w_mq', 'w_mkv', 'w_mo', 'mem_post_g', 'ffn2_pre_g', 'ffn2_w_in', 'ffn2_w_down', 'ffn2_post_g', 'loss_target', 'm_ffn1_pre_g', 'm_ffn1_w_in', 'm_ffn1_w_down', 'm_ffn1_post_g', 'm_mix_pre_g', 'm_w_in', 'm_hg_lb_logits', 'm_hg_norm_g', 'm_fox_f_bias', 'm_w_branch_a', 'm_w_branch_b', 'm_b_gate', 'm_w_out', 'm_mix_post_g', 'm_mem_pre_g', 'm_mem_kv_g', 'm_w_mq', 'm_w_mkv', 'm_w_mo', 'm_mem_post_g', 'm_ffn2_pre_g', 'm_ffn2_w_in', 'm_ffn2_w_down', 'm_ffn2_post_g', 'v_ffn1_pre_g', 'v_ffn1_w_in', 'v_ffn1_w_down', 'v_ffn1_post_g', 'v_mix_pre_g', 'v_w_in', 'v_hg_lb_logits', 'v_hg_norm_g', 'v_fox_f_bias', 'v_w_branch_a', 'v_w_branch_b', 'v_b_gate', 'v_w_out', 'v_mix_post_g', 'v_mem_pre_g', 'v_mem_kv_g', 'v_w_mq', 'v_w_mkv', 'v_w_mo', 'v_mem_post_g', 'v_ffn2_pre_g', 'v_ffn2_w_in', 'v_ffn2_w_down', 'v_ffn2_post_g']
TWIN_OUTPUTS = ['loss', 'grad_x', 'grad_ffn1_pre_g', 'grad_ffn1_w_in', 'grad_ffn1_w_down', 'grad_ffn1_post_g', 'grad_mix_pre_g', 'grad_w_in', 'grad_hg_lb_logits', 'grad_hg_norm_g', 'grad_fox_f_bias', 'grad_w_branch_a', 'grad_w_branch_b', 'grad_b_gate', 'grad_w_out', 'grad_mix_post_g', 'grad_mem_pre_g', 'grad_mem_kv_g', 'grad_w_mq', 'grad_w_mkv', 'grad_w_mo', 'grad_mem_post_g', 'grad_ffn2_pre_g', 'grad_ffn2_w_in', 'grad_ffn2_w_down', 'grad_ffn2_post_g', 'delta_ffn1_pre_g', 'delta_ffn1_w_in', 'delta_ffn1_w_down', 'delta_ffn1_post_g', 'delta_mix_pre_g', 'delta_w_in', 'delta_hg_lb_logits', 'delta_hg_norm_g', 'delta_fox_f_bias', 'delta_w_branch_a', 'delta_w_branch_b', 'delta_b_gate', 'delta_w_out', 'delta_mix_post_g', 'delta_mem_pre_g', 'delta_mem_kv_g', 'delta_w_mq', 'delta_w_mkv', 'delta_w_mo', 'delta_mem_post_g', 'delta_ffn2_pre_g', 'delta_ffn2_w_in', 'delta_ffn2_w_down', 'delta_ffn2_post_g', 'new_m_ffn1_pre_g', 'new_m_ffn1_w_in', 'new_m_ffn1_w_down', 'new_m_ffn1_post_g', 'new_m_mix_pre_g', 'new_m_w_in', 'new_m_hg_lb_logits', 'new_m_hg_norm_g', 'new_m_fox_f_bias', 'new_m_w_branch_a', 'new_m_w_branch_b', 'new_m_b_gate', 'new_m_w_out', 'new_m_mix_post_g', 'new_m_mem_pre_g', 'new_m_mem_kv_g', 'new_m_w_mq', 'new_m_w_mkv', 'new_m_w_mo', 'new_m_mem_post_g', 'new_m_ffn2_pre_g', 'new_m_ffn2_w_in', 'new_m_ffn2_w_down', 'new_m_ffn2_post_g', 'new_v_ffn1_pre_g', 'new_v_ffn1_w_in', 'new_v_ffn1_w_down', 'new_v_ffn1_post_g', 'new_v_mix_pre_g', 'new_v_w_in', 'new_v_hg_lb_logits', 'new_v_hg_norm_g', 'new_v_fox_f_bias', 'new_v_w_branch_a', 'new_v_w_branch_b', 'new_v_b_gate', 'new_v_w_out', 'new_v_mix_post_g', 'new_v_mem_pre_g', 'new_v_mem_kv_g', 'new_v_w_mq', 'new_v_w_mkv', 'new_v_w_mo', 'new_v_mem_post_g', 'new_v_ffn2_pre_g', 'new_v_ffn2_w_in', 'new_v_ffn2_w_down', 'new_v_ffn2_post_g']
TWIN_LEAF_KINDS = {'loss': 'loss', 'grad_x': 'grad_x', 'grad_ffn1_pre_g': 'grad_w', 'grad_ffn1_w_in': 'grad_w', 'grad_ffn1_w_down': 'grad_w', 'grad_ffn1_post_g': 'grad_w', 'grad_mix_pre_g': 'grad_w', 'grad_w_in': 'grad_w', 'grad_hg_lb_logits': 'grad_w', 'grad_hg_norm_g': 'grad_w', 'grad_fox_f_bias': 'grad_w', 'grad_w_branch_a': 'grad_w', 'grad_w_branch_b': 'grad_w', 'grad_b_gate': 'grad_w', 'grad_w_out': 'grad_w', 'grad_mix_post_g': 'grad_w', 'grad_mem_pre_g': 'grad_w', 'grad_mem_kv_g': 'grad_w', 'grad_w_mq': 'grad_w', 'grad_w_mkv': 'grad_w', 'grad_w_mo': 'grad_w', 'grad_mem_post_g': 'grad_w', 'grad_ffn2_pre_g': 'grad_w', 'grad_ffn2_w_in': 'grad_w', 'grad_ffn2_w_down': 'grad_w', 'grad_ffn2_post_g': 'grad_w', 'delta_ffn1_pre_g': 'delta_w', 'delta_ffn1_w_in': 'delta_w', 'delta_ffn1_w_down': 'delta_w', 'delta_ffn1_post_g': 'delta_w', 'delta_mix_pre_g': 'delta_w', 'delta_w_in': 'delta_w', 'delta_hg_lb_logits': 'delta_w', 'delta_hg_norm_g': 'delta_w', 'delta_fox_f_bias': 'delta_w', 'delta_w_branch_a': 'delta_w', 'delta_w_branch_b': 'delta_w', 'delta_b_gate': 'delta_w', 'delta_w_out': 'delta_w', 'delta_mix_post_g': 'delta_w', 'delta_mem_pre_g': 'delta_w', 'delta_mem_kv_g': 'delta_w', 'delta_w_mq': 'delta_w', 'delta_w_mkv': 'delta_w', 'delta_w_mo': 'delta_w', 'delta_mem_post_g': 'delta_w', 'delta_ffn2_pre_g': 'delta_w', 'delta_ffn2_w_in': 'delta_w', 'delta_ffn2_w_down': 'delta_w', 'delta_ffn2_post_g': 'delta_w', 'new_m_ffn1_pre_g': 'new_m', 'new_m_ffn1_w_in': 'new_m', 'new_m_ffn1_w_down': 'new_m', 'new_m_ffn1_post_g': 'new_m', 'new_m_mix_pre_g': 'new_m', 'new_m_w_in': 'new_m', 'new_m_hg_lb_logits': 'new_m', 'new_m_hg_norm_g': 'new_m', 'new_m_fox_f_bias': 'new_m', 'new_m_w_branch_a': 'new_m', 'new_m_w_branch_b': 'new_m', 'new_m_b_gate': 'new_m', 'new_m_w_out': 'new_m', 'new_m_mix_post_g': 'new_m', 'new_m_mem_pre_g': 'new_m', 'new_m_mem_kv_g': 'new_m', 'new_m_w_mq': 'new_m', 'new_m_w_mkv': 'new_m', 'new_m_w_mo': 'new_m', 'new_m_mem_post_g': 'new_m', 'new_m_ffn2_pre_g': 'new_m', 'new_m_ffn2_w_in': 'new_m', 'new_m_ffn2_w_down': 'new_m', 'new_m_ffn2_post_g': 'new_m', 'new_v_ffn1_pre_g': 'new_v', 'new_v_ffn1_w_in': 'new_v', 'new_v_ffn1_w_down': 'new_v', 'new_v_ffn1_post_g': 'new_v', 'new_v_mix_pre_g': 'new_v', 'new_v_w_in': 'new_v', 'new_v_hg_lb_logits': 'new_v', 'new_v_hg_norm_g': 'new_v', 'new_v_fox_f_bias': 'new_v', 'new_v_w_branch_a': 'new_v', 'new_v_w_branch_b': 'new_v', 'new_v_b_gate': 'new_v', 'new_v_w_out': 'new_v', 'new_v_mix_post_g': 'new_v', 'new_v_mem_pre_g': 'new_v', 'new_v_mem_kv_g': 'new_v', 'new_v_w_mq': 'new_v', 'new_v_w_mkv': 'new_v', 'new_v_w_mo': 'new_v', 'new_v_mem_post_g': 'new_v', 'new_v_ffn2_pre_g': 'new_v', 'new_v_ffn2_w_in': 'new_v', 'new_v_ffn2_w_down': 'new_v', 'new_v_ffn2_post_g': 'new_v'}


def _forward(args):
    return _fwd_reference(*[args[k] for k in FWD_PARAMS])


def _output_shape():
    def fwd():
        inp = _fwd_setup_inputs(0)
        return _fwd_reference(*[inp[k] for k in FWD_PARAMS])
    out = _jax.eval_shape(fwd)
    return out.shape, out.dtype

N_MICROBATCH = 1
ADAM_LR = 0.001
ADAM_B1 = 0.9
ADAM_B2 = 0.999
ADAM_EPS = 1e-08
ADAM_WD = 0.01
ADAM_STEP = 10
PER_EXAMPLE_BATCH_AXIS = {'x': 0, 'mem': 0, 'loss_target': 0}
SHARED_INPUTS = []
_WEIGHT_DTYPES = {'ffn1_pre_g': _jnp.float32, 'ffn1_w_in': _jnp.float32, 'ffn1_w_down': _jnp.float32, 'ffn1_post_g': _jnp.float32, 'mix_pre_g': _jnp.float32, 'w_in': _jnp.float32, 'hg_lb_logits': _jnp.float32, 'hg_norm_g': _jnp.float32, 'fox_f_bias': _jnp.float32, 'w_branch_a': _jnp.float32, 'w_branch_b': _jnp.float32, 'b_gate': _jnp.float32, 'w_out': _jnp.float32, 'mix_post_g': _jnp.float32, 'mem_pre_g': _jnp.float32, 'mem_kv_g': _jnp.float32, 'w_mq': _jnp.float32, 'w_mkv': _jnp.float32, 'w_mo': _jnp.float32, 'mem_post_g': _jnp.float32, 'ffn2_pre_g': _jnp.float32, 'ffn2_w_in': _jnp.float32, 'ffn2_w_down': _jnp.float32, 'ffn2_post_g': _jnp.float32}
MOMENT_SCALE = {'ffn1_pre_g': 8.227584e-01, 'ffn1_w_in': 3.527829e-01, 'ffn1_w_down': 6.098337e-01, 'ffn1_post_g': 1.564506e+01, 'mix_pre_g': 1.197544e+00, 'w_in': 3.625209e-01, 'hg_lb_logits': 5.182955e-02, 'hg_norm_g': 5.592048e-01, 'fox_f_bias': 2.260427e+00, 'w_branch_a': 5.690666e-01, 'w_branch_b': 6.890626e-01, 'b_gate': 2.670677e-01, 'w_out': 1.005832e+00, 'mix_post_g': 6.407377e+01, 'mem_pre_g': 9.415929e-01, 'mem_kv_g': 2.194915e+00, 'w_mq': 9.283924e-01, 'w_mkv': 1.278908e+00, 'w_mo': 1.634851e+00, 'mem_post_g': 6.505182e+01, 'ffn2_pre_g': 1.131251e+00, 'ffn2_w_in': 4.524013e-01, 'ffn2_w_down': 9.545448e-01, 'ffn2_post_g': 1.611975e+01}


def _to_microbatches(a, axis):
    t = _jnp.moveaxis(a, axis, 0)
    t = t.reshape((N_MICROBATCH, t.shape[0] // N_MICROBATCH) + t.shape[1:])
    return _jnp.moveaxis(t, 1, axis + 1)


def setup_inputs(seed: int = 0) -> dict:
    inp = _fwd_setup_inputs(seed)
    key = _jax.random.fold_in(_jax.random.key(seed), 7919)
    shape, _ = _output_shape()
    out = dict(inp)
    out["loss_target"] = _jax.random.normal(_jax.random.fold_in(key, 0), shape, _jnp.float32)
    for i, name in enumerate(TWIN_WEIGHTS):
        w = inp[name].astype(_jnp.float32)
        if MOMENT_SCALE is None:
            s = _jnp.sqrt(_jnp.mean(_jnp.square(w)) + 1e-30)
        else:
            s = MOMENT_SCALE[name]
        km, kv = _jax.random.split(_jax.random.fold_in(key, i + 1))
        out[name] = w
        out["m_" + name] = s * _jax.random.normal(km, w.shape, _jnp.float32)
        out["v_" + name] = (s * s) * _jax.random.uniform(kv, w.shape, _jnp.float32, 0.5, 1.5)
    if N_MICROBATCH > 1:
        for name, axis in PER_EXAMPLE_BATCH_AXIS.items():
            out[name] = _to_microbatches(out[name], axis)
    return {'x': out['x'], 'mem': out['mem'], 'ffn1_pre_g': out['ffn1_pre_g'], 'ffn1_w_in': out['ffn1_w_in'], 'ffn1_w_down': out['ffn1_w_down'], 'ffn1_post_g': out['ffn1_post_g'], 'mix_pre_g': out['mix_pre_g'], 'w_in': out['w_in'], 'hg_lb_logits': out['hg_lb_logits'], 'hg_norm_g': out['hg_norm_g'], 'fox_f_bias': out['fox_f_bias'], 'w_branch_a': out['w_branch_a'], 'w_branch_b': out['w_branch_b'], 'b_gate': out['b_gate'], 'w_out': out['w_out'], 'mix_post_g': out['mix_post_g'], 'mem_pre_g': out['mem_pre_g'], 'mem_kv_g': out['mem_kv_g'], 'w_mq': out['w_mq'], 'w_mkv': out['w_mkv'], 'w_mo': out['w_mo'], 'mem_post_g': out['mem_post_g'], 'ffn2_pre_g': out['ffn2_pre_g'], 'ffn2_w_in': out['ffn2_w_in'], 'ffn2_w_down': out['ffn2_w_down'], 'ffn2_post_g': out['ffn2_post_g'], 'loss_target': out['loss_target'], 'm_ffn1_pre_g': out['m_ffn1_pre_g'], 'm_ffn1_w_in': out['m_ffn1_w_in'], 'm_ffn1_w_down': out['m_ffn1_w_down'], 'm_ffn1_post_g': out['m_ffn1_post_g'], 'm_mix_pre_g': out['m_mix_pre_g'], 'm_w_in': out['m_w_in'], 'm_hg_lb_logits': out['m_hg_lb_logits'], 'm_hg_norm_g': out['m_hg_norm_g'], 'm_fox_f_bias': out['m_fox_f_bias'], 'm_w_branch_a': out['m_w_branch_a'], 'm_w_branch_b': out['m_w_branch_b'], 'm_b_gate': out['m_b_gate'], 'm_w_out': out['m_w_out'], 'm_mix_post_g': out['m_mix_post_g'], 'm_mem_pre_g': out['m_mem_pre_g'], 'm_mem_kv_g': out['m_mem_kv_g'], 'm_w_mq': out['m_w_mq'], 'm_w_mkv': out['m_w_mkv'], 'm_w_mo': out['m_w_mo'], 'm_mem_post_g': out['m_mem_post_g'], 'm_ffn2_pre_g': out['m_ffn2_pre_g'], 'm_ffn2_w_in': out['m_ffn2_w_in'], 'm_ffn2_w_down': out['m_ffn2_w_down'], 'm_ffn2_post_g': out['m_ffn2_post_g'], 'v_ffn1_pre_g': out['v_ffn1_pre_g'], 'v_ffn1_w_in': out['v_ffn1_w_in'], 'v_ffn1_w_down': out['v_ffn1_w_down'], 'v_ffn1_post_g': out['v_ffn1_post_g'], 'v_mix_pre_g': out['v_mix_pre_g'], 'v_w_in': out['v_w_in'], 'v_hg_lb_logits': out['v_hg_lb_logits'], 'v_hg_norm_g': out['v_hg_norm_g'], 'v_fox_f_bias': out['v_fox_f_bias'], 'v_w_branch_a': out['v_w_branch_a'], 'v_w_branch_b': out['v_w_branch_b'], 'v_b_gate': out['v_b_gate'], 'v_w_out': out['v_w_out'], 'v_mix_post_g': out['v_mix_post_g'], 'v_mem_pre_g': out['v_mem_pre_g'], 'v_mem_kv_g': out['v_mem_kv_g'], 'v_w_mq': out['v_w_mq'], 'v_w_mkv': out['v_w_mkv'], 'v_w_mo': out['v_w_mo'], 'v_mem_post_g': out['v_mem_post_g'], 'v_ffn2_pre_g': out['v_ffn2_pre_g'], 'v_ffn2_w_in': out['v_ffn2_w_in'], 'v_ffn2_w_down': out['v_ffn2_w_down'], 'v_ffn2_post_g': out['v_ffn2_post_g']}


def _loss(weights, diff, rest, loss_target):
    with _jax.named_scope("forward"):
        args = {**rest, TWIN_DIFF_INPUT: diff, **{k: w.astype(_WEIGHT_DTYPES[k]) for k, w in weights.items()}}
        y = _forward(args)
    with _jax.named_scope("loss_head"):
        err = _jnp.square(y.astype(_jnp.float32) - loss_target)
        return 0.5 * _jnp.sum(_jnp.mean(err, axis=-1)) if err.ndim else 0.5 * err


def _adamw(w, g, m, v):
    m = ADAM_B1 * m + (1.0 - ADAM_B1) * g
    v = ADAM_B2 * v + (1.0 - ADAM_B2) * _jnp.square(g)
    m_hat = m / (1.0 - ADAM_B1 ** ADAM_STEP)
    v_hat = v / (1.0 - ADAM_B2 ** ADAM_STEP)
    delta = -ADAM_LR * (m_hat / (_jnp.sqrt(v_hat) + ADAM_EPS) + ADAM_WD * w)
    return delta, m, v


def reference(x, mem, ffn1_pre_g, ffn1_w_in, ffn1_w_down, ffn1_post_g, mix_pre_g, w_in, hg_lb_logits, hg_norm_g, fox_f_bias, w_branch_a, w_branch_b, b_gate, w_out, mix_post_g, mem_pre_g, mem_kv_g, w_mq, w_mkv, w_mo, mem_post_g, ffn2_pre_g, ffn2_w_in, ffn2_w_down, ffn2_post_g, loss_target, m_ffn1_pre_g, m_ffn1_w_in, m_ffn1_w_down, m_ffn1_post_g, m_mix_pre_g, m_w_in, m_hg_lb_logits, m_hg_norm_g, m_fox_f_bias, m_w_branch_a, m_w_branch_b, m_b_gate, m_w_out, m_mix_post_g, m_mem_pre_g, m_mem_kv_g, m_w_mq, m_w_mkv, m_w_mo, m_mem_post_g, m_ffn2_pre_g, m_ffn2_w_in, m_ffn2_w_down, m_ffn2_post_g, v_ffn1_pre_g, v_ffn1_w_in, v_ffn1_w_down, v_ffn1_post_g, v_mix_pre_g, v_w_in, v_hg_lb_logits, v_hg_norm_g, v_fox_f_bias, v_w_branch_a, v_w_branch_b, v_b_gate, v_w_out, v_mix_post_g, v_mem_pre_g, v_mem_kv_g, v_w_mq, v_w_mkv, v_w_mo, v_mem_post_g, v_ffn2_pre_g, v_ffn2_w_in, v_ffn2_w_down, v_ffn2_post_g):
    given = dict(x=x, mem=mem, ffn1_pre_g=ffn1_pre_g, ffn1_w_in=ffn1_w_in, ffn1_w_down=ffn1_w_down, ffn1_post_g=ffn1_post_g, mix_pre_g=mix_pre_g, w_in=w_in, hg_lb_logits=hg_lb_logits, hg_norm_g=hg_norm_g, fox_f_bias=fox_f_bias, w_branch_a=w_branch_a, w_branch_b=w_branch_b, b_gate=b_gate, w_out=w_out, mix_post_g=mix_post_g, mem_pre_g=mem_pre_g, mem_kv_g=mem_kv_g, w_mq=w_mq, w_mkv=w_mkv, w_mo=w_mo, mem_post_g=mem_post_g, ffn2_pre_g=ffn2_pre_g, ffn2_w_in=ffn2_w_in, ffn2_w_down=ffn2_w_down, ffn2_post_g=ffn2_post_g, loss_target=loss_target, m_ffn1_pre_g=m_ffn1_pre_g, m_ffn1_w_in=m_ffn1_w_in, m_ffn1_w_down=m_ffn1_w_down, m_ffn1_post_g=m_ffn1_post_g, m_mix_pre_g=m_mix_pre_g, m_w_in=m_w_in, m_hg_lb_logits=m_hg_lb_logits, m_hg_norm_g=m_hg_norm_g, m_fox_f_bias=m_fox_f_bias, m_w_branch_a=m_w_branch_a, m_w_branch_b=m_w_branch_b, m_b_gate=m_b_gate, m_w_out=m_w_out, m_mix_post_g=m_mix_post_g, m_mem_pre_g=m_mem_pre_g, m_mem_kv_g=m_mem_kv_g, m_w_mq=m_w_mq, m_w_mkv=m_w_mkv, m_w_mo=m_w_mo, m_mem_post_g=m_mem_post_g, m_ffn2_pre_g=m_ffn2_pre_g, m_ffn2_w_in=m_ffn2_w_in, m_ffn2_w_down=m_ffn2_w_down, m_ffn2_post_g=m_ffn2_post_g, v_ffn1_pre_g=v_ffn1_pre_g, v_ffn1_w_in=v_ffn1_w_in, v_ffn1_w_down=v_ffn1_w_down, v_ffn1_post_g=v_ffn1_post_g, v_mix_pre_g=v_mix_pre_g, v_w_in=v_w_in, v_hg_lb_logits=v_hg_lb_logits, v_hg_norm_g=v_hg_norm_g, v_fox_f_bias=v_fox_f_bias, v_w_branch_a=v_w_branch_a, v_w_branch_b=v_w_branch_b, v_b_gate=v_b_gate, v_w_out=v_w_out, v_mix_post_g=v_mix_post_g, v_mem_pre_g=v_mem_pre_g, v_mem_kv_g=v_mem_kv_g, v_w_mq=v_w_mq, v_w_mkv=v_w_mkv, v_w_mo=v_w_mo, v_mem_post_g=v_mem_post_g, v_ffn2_pre_g=v_ffn2_pre_g, v_ffn2_w_in=v_ffn2_w_in, v_ffn2_w_down=v_ffn2_w_down, v_ffn2_post_g=v_ffn2_post_g)
    weights = {n: given[n] for n in TWIN_WEIGHTS}
    shared = {n: given[n] for n in SHARED_INPUTS}
    per_example = {n: given[n] for n in ['x', 'mem']}
    grad_fn = _jax.value_and_grad(_loss, argnums=(0, 1))

    def one_microbatch(ex, loss_target):
        ex = dict(ex)
        diff = ex.pop(TWIN_DIFF_INPUT)
        return grad_fn(weights, diff, {**shared, **ex}, loss_target)

    if N_MICROBATCH == 1:
        loss, (grad_w, grad_x) = one_microbatch(per_example, given["loss_target"])
    else:
        def body(carry, xs):
            loss_sum, grad_sum = carry
            l_k, (gw_k, gx_k) = one_microbatch(xs[0], xs[1])
            with _jax.named_scope("update"):
                return (loss_sum + l_k, _jax.tree.map(_jnp.add, grad_sum, gw_k)), gx_k

        init = (_jnp.zeros((), _jnp.float32), _jax.tree.map(_jnp.zeros_like, weights))
        (loss, grad_w), grad_x = _jax.lax.scan(body, init, (per_example, given["loss_target"]))
    with _jax.named_scope("update"):
        delta_w, new_m, new_v = {}, {}, {}
        for n in TWIN_WEIGHTS:
            delta_w[n], new_m[n], new_v[n] = _adamw(weights[n], grad_w[n], given["m_" + n], given["v_" + n])
    return (loss, grad_x, *[grad_w[n] for n in TWIN_WEIGHTS], *[delta_w[n] for n in TWIN_WEIGHTS],
            *[new_m[n] for n in TWIN_WEIGHTS], *[new_v[n] for n in TWIN_WEIGHTS])
```

```python
import functools
import math

import jax
import jax.numpy as jnp
from jax import lax
from jax.experimental import pallas as pl
from jax.experimental.pallas import tpu as pltpu

F32 = jnp.float32
BF16 = jnp.bfloat16

D_MODEL = 1024
D_FF = 2816
HEADS = 8
DH = 128
MEM_HEADS = 4
MEM_DH = 256
MEM_LEN = 256
EPS = 1e-6
SUB = 16
LANE = 128
SUBLANE = 8
VMEM_LIMIT = 56 * 1024 * 1024

ADAM_LR = 0.001
ADAM_B1 = 0.9
ADAM_B2 = 0.999
ADAM_EPS = 1e-08
ADAM_WD = 0.01
ADAM_STEP = 10

HIGHEST = lax.Precision.HIGHEST


def _params(*sem):
    return pltpu.CompilerParams(dimension_semantics=sem, vmem_limit_bytes=VMEM_LIMIT)


def _sigmoid(v):
    return 1.0 / (1.0 + jnp.exp(-v))


def _silu(v):
    return v * _sigmoid(v)


def _dsilu(v):
    s = _sigmoid(v)
    return s * (1.0 + v * (1.0 - s))


def _dot(a, b, dims):
    return lax.dot_general(a.astype(BF16), b.astype(BF16), (dims, ((), ())), preferred_element_type=F32)


NN = ((1,), (0,))
NT = ((1,), (1,))
TN = ((0,), (0,))


def _mm(pairs, mode, *, tm, tn, tk, out_dtypes, name, epilogue=None, tiles=(), b_koff=None):
    a0, b0 = pairs[0]
    if mode == "nn":
        (M, K), N = a0.shape, b0.shape[1]
    elif mode == "nt":
        (M, K), N = a0.shape, b0.shape[0]
    else:
        (K, M), N = a0.shape, b0.shape[1]
    tm, tn, tk = min(tm, M), min(tn, N), min(tk, K)
    assert M % tm == 0 and N % tn == 0 and K % tk == 0, (name, M, N, K, tm, tn, tk)
    nk = K // tk
    npair = len(pairs)
    koff = [0] * npair if b_koff is None else [o // tk for o in b_koff]
    if b_koff is not None:
        assert all(o % tk == 0 for o in b_koff)
    in_specs, args = [], []
    for p, (a, b) in enumerate(pairs):
        if mode == "nn":
            sa = pl.BlockSpec((tm, tk), lambda i, j, k: (i, k))
            sb = pl.BlockSpec((tk, tn), lambda i, j, k, o=koff[p]: (k + o, j))
            dims = NN
        elif mode == "nt":
            sa = pl.BlockSpec((tm, tk), lambda i, j, k: (i, k))
            sb = pl.BlockSpec((tn, tk), lambda i, j, k, o=koff[p]: (j, k + o))
            dims = NT
        else:
            sa = pl.BlockSpec((tk, tm), lambda i, j, k: (k, i))
            sb = pl.BlockSpec((tk, tn), lambda i, j, k, o=koff[p]: (k + o, j))
            dims = TN
        in_specs += [sa, sb]
        args += [a, b]
    for t in tiles:
        in_specs.append(pl.BlockSpec((tm, tn), lambda i, j, k: (i, j)))
        args.append(t)
    nt_ = len(tiles)
    nout = len(out_dtypes)

    def body(*refs):
        ab = refs[: 2 * npair]
        tl = refs[2 * npair: 2 * npair + nt_]
        outs = refs[2 * npair + nt_: 2 * npair + nt_ + nout]
        acc_ref = refs[-1] if nk > 1 else None

        def partial_sum():
            s = _dot(ab[0][...], ab[1][...], dims)
            for p in range(1, npair):
                s = s + _dot(ab[2 * p][...], ab[2 * p + 1][...], dims)
            return s

        def finish(acc):
            res = (acc,) if epilogue is None else epilogue(acc, *[t[...] for t in tl])
            for o, r in zip(outs, res):
                o[...] = r.astype(o.dtype)

        if nk == 1:
            finish(partial_sum())
        else:
            k = pl.program_id(2)

            @pl.when(k == 0)
            def _():
                acc_ref[...] = jnp.zeros_like(acc_ref)

            acc_ref[...] += partial_sum()

            @pl.when(k == nk - 1)
            def _():
                finish(acc_ref[...])

    out_shape = [jax.ShapeDtypeStruct((M, N), dt) for dt in out_dtypes]
    out_specs = [pl.BlockSpec((tm, tn), lambda i, j, k: (i, j)) for _ in out_dtypes]
    res = pl.pallas_call(
        body, name=name, grid=(M // tm, N // tn, nk), in_specs=in_specs, out_specs=out_specs, out_shape=out_shape,
        scratch_shapes=[pltpu.VMEM((tm, tn), F32)] if nk > 1 else [],
        compiler_params=_params("parallel", "parallel", "arbitrary"),
    )(*args)
    return res[0] if nout == 1 else res


def _col(arr, tm, width, cb):
    return pl.BlockSpec((tm, width), lambda i, cb=cb: (i, cb))


def _rms_fwd(x, g, *, out_dtype, name, mul=None, res=None, coeff=1.0, tm=512):
    T, D = x.shape
    tm = min(tm, T)
    args, in_specs = [x, g], [pl.BlockSpec((tm, D), lambda i: (i, 0)), pl.BlockSpec((1, D), lambda i: (0, 0))]
    if mul is not None:
        args.append(mul[0])
        in_specs.append(_col(mul[0], tm, D, mul[1]))
    if res is not None:
        args.append(res)
        in_specs.append(pl.BlockSpec((tm, D), lambda i: (i, 0)))

    def body(*refs):
        xv = refs[0][...].astype(F32)
        r = lax.rsqrt(jnp.mean(xv * xv, axis=-1, keepdims=True) + EPS)
        y = (xv * r) * refs[1][...]
        n = 2
        if mul is not None:
            y = y * _silu(refs[n][...])
            n += 1
        if res is not None:
            y = refs[n][...] + coeff * y
        refs[-1][...] = y.astype(out_dtype)

    return pl.pallas_call(
        body, name=name, grid=(T // tm,), in_specs=in_specs, out_specs=pl.BlockSpec((tm, D), lambda i: (i, 0)),
        out_shape=jax.ShapeDtypeStruct((T, D), out_dtype), compiler_params=_params("parallel"),
    )(*args)


def _fold8(v):
    tm, d = v.shape
    return v.reshape(tm // SUBLANE, SUBLANE, d).sum(axis=0)


def _rms_bwd(x, g, dy, *, name, coeff=1.0, add=None, mul=None, dx_dtype=F32, tm=512):
    T, D = x.shape
    tm = min(tm, T)
    row = pl.BlockSpec((tm, D), lambda i: (i, 0))
    args, in_specs = [x, g, dy], [row, pl.BlockSpec((1, D), lambda i: (0, 0)), row]
    if add is not None:
        args.append(add)
        in_specs.append(row)
    if mul is not None:
        args.append(mul[0])
        in_specs.append(_col(mul[0], tm, D, mul[1]))
    nin = len(args)

    def body(*refs):
        xv = refs[0][...].astype(F32)
        gv = refs[1][...]
        dyv = refs[2][...].astype(F32) * coeff
        r = lax.rsqrt(jnp.mean(xv * xv, axis=-1, keepdims=True) + EPS)
        nrm = xv * r
        n = 3
        addv = None
        if add is not None:
            addv = refs[n][...]
            n += 1
        if mul is not None:
            mv = refs[n][...]
            sm = _silu(mv)
            refs[nin + 2][...] = (dyv * nrm * gv * _dsilu(mv)).astype(refs[nin + 2].dtype)
            dyv = dyv * sm
        dn = dyv * gv
        dx = r * (dn - nrm * jnp.mean(dn * nrm, axis=-1, keepdims=True))
        if addv is not None:
            dx = dx + addv
        refs[nin][...] = dx.astype(dx_dtype)
        dg_ref = refs[nin + 1]

        @pl.when(pl.program_id(0) == 0)
        def _():
            dg_ref[...] = jnp.zeros_like(dg_ref)

        dg_ref[...] += _fold8(dyv * nrm)

    out_shape = [jax.ShapeDtypeStruct((T, D), dx_dtype), jax.ShapeDtypeStruct((SUBLANE, D), F32)]
    out_specs = [row, pl.BlockSpec((SUBLANE, D), lambda i: (0, 0))]
    if mul is not None:
        out_shape.append(jax.ShapeDtypeStruct((T, D), BF16))
        out_specs.append(row)
    return pl.pallas_call(
        body, name=name, grid=(T // tm,), in_specs=in_specs, out_specs=out_specs, out_shape=out_shape,
        compiler_params=_params("arbitrary"),
    )(*args)


def _ffn_in(h, wg, wu, *, name, tm=1024, tn=256):
    T, D = h.shape
    F = wg.shape[1]
    tm = min(tm, T)
    assert F % tn == 0

    def body(h_ref, wg_ref, wu_ref, a_ref, g_ref, u_ref):
        hv = h_ref[...]
        gt = _dot(hv, wg_ref[...], NN)
        up = _dot(hv, wu_ref[...], NN)
        a_ref[...] = (_silu(gt) * up).astype(BF16)
        g_ref[...] = gt.astype(BF16)
        u_ref[...] = up.astype(BF16)

    o = pl.BlockSpec((tm, tn), lambda i, j: (i, j))
    w = pl.BlockSpec((D, tn), lambda i, j: (0, j))
    return pl.pallas_call(
        body, name=name, grid=(T // tm, F // tn), in_specs=[pl.BlockSpec((tm, D), lambda i, j: (i, 0)), w, w],
        out_specs=[o, o, o], out_shape=[jax.ShapeDtypeStruct((T, F), BF16)] * 3,
        compiler_params=_params("parallel", "parallel"),
    )(h, wg, wu)


def _swiglu_bwd_epilogue(da, gt, up):
    gt = gt.astype(F32)
    up = up.astype(F32)
    return da * up * _dsilu(gt), da * _silu(gt)


GATE_CB = 7


def _gatemix_fwd(z, b_gate, ya, yb, *, name, tm=512):
    T, D = ya.shape
    tm = min(tm, T)
    row = pl.BlockSpec((tm, D), lambda i: (i, 0))

    def body(z0, z1, b0, b1, ya_ref, yb_ref, y_ref):
        g0 = _sigmoid(z0[...] + b0[...])
        g1 = _sigmoid(z1[...] + b1[...])
        y_ref[...] = (g0 * ya_ref[...] + g1 * yb_ref[...]).astype(y_ref.dtype)

    bs = lambda c: pl.BlockSpec((1, D), lambda i, c=c: (0, c))
    return pl.pallas_call(
        body, name=name, grid=(T // tm,),
        in_specs=[_col(z, tm, D, GATE_CB), _col(z, tm, D, GATE_CB + 1), bs(0), bs(1), row, row],
        out_specs=row, out_shape=jax.ShapeDtypeStruct((T, D), BF16), compiler_params=_params("parallel"),
    )(z, z, b_gate, b_gate, ya, yb)


def _gatemix_bwd(z, b_gate, ya, yb, dy, *, name, tm=512):
    T, D = ya.shape
    tm = min(tm, T)
    row = pl.BlockSpec((tm, D), lambda i: (i, 0))
    part = pl.BlockSpec((SUBLANE, D), lambda i: (0, 0))

    def body(z0, z1, b0, b1, ya_ref, yb_ref, dy_ref, dya, dyb, dz0, dz1, s0, s1):
        g0 = _sigmoid(z0[...] + b0[...])
        g1 = _sigmoid(z1[...] + b1[...])
        dyv = dy_ref[...]
        dya[...] = (dyv * g0).astype(BF16)
        dyb[...] = (dyv * g1).astype(BF16)
        d0 = dyv * ya_ref[...] * (g0 * (1.0 - g0))
        d1 = dyv * yb_ref[...] * (g1 * (1.0 - g1))
        dz0[...] = d0.astype(BF16)
        dz1[...] = d1.astype(BF16)

        @pl.when(pl.program_id(0) == 0)
        def _():
            s0[...] = jnp.zeros_like(s0)
            s1[...] = jnp.zeros_like(s1)

        s0[...] += _fold8(d0)
        s1[...] += _fold8(d1)

    bs = lambda c: pl.BlockSpec((1, D), lambda i, c=c: (0, c))
    act = jax.ShapeDtypeStruct((T, D), BF16)
    ps = jax.ShapeDtypeStruct((SUBLANE, D), F32)
    return pl.pallas_call(
        body, name=name, grid=(T // tm,),
        in_specs=[_col(z, tm, D, GATE_CB), _col(z, tm, D, GATE_CB + 1), bs(0), bs(1), row, row, row],
        out_specs=[row, row, row, row, part, part], out_shape=[act, act, act, act, ps, ps],
        compiler_params=_params("arbitrary"),
    )(z, z, b_gate, b_gate, ya, yb, dy)


def _loss_head(x, target, *, name, tm=512):
    T, D = x.shape
    tm = min(tm, T)
    row = pl.BlockSpec((tm, D), lambda i: (i, 0))

    def body(x_ref, t_ref, dx_ref, s_ref):
        e = x_ref[...] - t_ref[...]
        dx_ref[...] = e * (1.0 / D)

        @pl.when(pl.program_id(0) == 0)
        def _():
            s_ref[...] = jnp.zeros_like(s_ref)

        s_ref[...] += _fold8(e * e)

    return pl.pallas_call(
        body, name=name, grid=(T // tm,), in_specs=[row, row],
        out_specs=[row, pl.BlockSpec((SUBLANE, D), lambda i: (0, 0))],
        out_shape=[jax.ShapeDtypeStruct((T, D), F32), jax.ShapeDtypeStruct((SUBLANE, D), F32)],
        compiler_params=_params("arbitrary"),
    )(x, target)


def _tri(n, reverse):
    r = lax.broadcasted_iota(jnp.int32, (n, n), 0)
    c = lax.broadcasted_iota(jnp.int32, (n, n), 1)
    return jnp.where((c >= r) if reverse else (c <= r), 1.0, 0.0).astype(F32)


def _cumsum_t(xs, *, name, width, pre, reverse=False, rows=(), post=None, out_dtypes=(F32,), fold=None, tb=256):
    T = xs[0][0].shape[0]
    tb = min(tb, T)
    nb = T // tb
    tblk = (lambda i: nb - 1 - i) if reverse else (lambda i: i)
    args = [a for a, _ in xs] + [a for a, _ in rows]
    in_specs = [pl.BlockSpec((tb, width), lambda i, cb=cb: (tblk(i), cb)) for _, cb in xs]
    in_specs += [pl.BlockSpec((1, width), lambda i, cb=cb: (0, cb)) for _, cb in rows]
    nin, nout = len(args), len(out_dtypes)

    def body(*refs):
        vals = [r[...] for r in refs[:nin]]
        outs = refs[nin:nin + nout]
        carry = refs[-1]
        first = pl.program_id(0) == 0

        @pl.when(first)
        def _():
            carry[...] = jnp.zeros_like(carry)

        cum = jnp.dot(_tri(tb, reverse), pre(*vals), precision=HIGHEST, preferred_element_type=F32) + carry[...]
        carry[...] = cum[0:1, :] if reverse else cum[tb - 1:tb, :]
        res = (cum,) if post is None else post(cum, *vals)
        for o, r in zip(outs, res):
            o[...] = r.astype(o.dtype)
        if fold is not None:
            f_ref = refs[nin + nout]

            @pl.when(first)
            def _():
                f_ref[...] = jnp.zeros_like(f_ref)

            f_ref[...] += _fold8(fold(cum, *vals))

    tspec = pl.BlockSpec((tb, width), lambda i: (tblk(i), 0))
    out_shape = [jax.ShapeDtypeStruct((T, width), dt) for dt in out_dtypes]
    out_specs = [tspec] * nout
    if fold is not None:
        out_shape.append(jax.ShapeDtypeStruct((SUBLANE, width), F32))
        out_specs.append(pl.BlockSpec((SUBLANE, width), lambda i: (0, 0)))
    res = pl.pallas_call(
        body, name=name, grid=(nb,), in_specs=in_specs, out_specs=out_specs, out_shape=out_shape,
        scratch_shapes=[pltpu.VMEM((1, width), F32)], compiler_params=_params("arbitrary"),
    )(*args)
    return res[0] if len(res) == 1 else res


def _logsigmoid(v):
    return jnp.minimum(v, 0.0) - jnp.log(1.0 + jnp.exp(-jnp.abs(v)))


HG_TB = 256
HG_Q_CB, HG_F_CB, HG_I_CB = 0, HEADS, 2 * HEADS
NEG = -1e30


def _hg_block(q_ref, f_ref, i_ref, lb, rows):
    qr = q_ref[rows, :]
    sg = _sigmoid(f_ref[rows, :])
    f = lb + (1.0 - lb) * sg
    b = jnp.dot(_tri(SUB, False), jnp.log(f), precision=HIGHEST, preferred_element_type=F32)
    return _silu(qr), 1.0 - f, i_ref[rows, :], b, qr, sg, f


def _hg_specs(tb, tmap):
    return [pl.BlockSpec((tb, DH), lambda h, t: (tmap(t), HG_Q_CB + h)),
            pl.BlockSpec((tb, DH), lambda h, t: (tmap(t), HG_F_CB + h)),
            pl.BlockSpec((tb, DH), lambda h, t: (tmap(t), HG_I_CB + h)),
            pl.BlockSpec((None, 1, DH), lambda h, t: (h, 0, 0))]


def _hgrn2_fwd(z, lb3, *, name):
    T = z.shape[0]
    tb = min(HG_TB, T)
    nb, nsub = T // tb, tb // SUB

    def body(q_ref, f_ref, i_ref, lb_ref, o_ref, st_ref, state):
        @pl.when(pl.program_id(1) == 0)
        def _():
            state[...] = jnp.zeros_like(state)

        st_ref[...] = state[...]
        lb = lb_ref[...]
        rowid = lax.broadcasted_iota(jnp.int32, (SUB, DH), 0)

        def step(c, carry):
            rows = pl.ds(pl.multiple_of(c * SUB, SUB), SUB)
            q, k, iv, b = _hg_block(q_ref, f_ref, i_ref, lb, rows)[:4]
            bl = b[SUB - 1:SUB, :]
            sv = state[...]
            o = _dot(q * jnp.exp(b), sv, NT)
            for s in range(SUB):
                e = jnp.exp(jnp.where(rowid >= s, b - b[s:s + 1, :], NEG))
                a = jnp.sum(q * e * k[s:s + 1, :], axis=-1, keepdims=True)
                o = o + a * iv[s:s + 1, :]
            o_ref[rows, :] = o
            state[...] = sv * jnp.exp(bl) + _dot(iv, k * jnp.exp(bl - b), TN)
            return carry

        lax.fori_loop(0, nsub, step, 0)

    return pl.pallas_call(
        body, name=name, grid=(HEADS, nb), in_specs=_hg_specs(tb, lambda t: t),
        out_specs=[pl.BlockSpec((tb, DH), lambda h, t: (t, h)),
                   pl.BlockSpec((None, None, DH, DH), lambda h, t: (t, h, 0, 0))],
        out_shape=[jax.ShapeDtypeStruct((T, D_MODEL), F32), jax.ShapeDtypeStruct((nb, HEADS, DH, DH), F32)],
        scratch_shapes=[pltpu.VMEM((DH, DH), F32)], compiler_params=_params("parallel", "arbitrary"),
    )(z, z, z, lb3)


def _hgrn2_bwd(z, lb3, states, do, *, name):
    T = z.shape[0]
    tb = min(HG_TB, T)
    nb, nsub = T // tb, tb // SUB
    rev = lambda t: nb - 1 - t

    def body(q_ref, f_ref, i_ref, lb_ref, st_ref, do_ref, dq_ref, dfl_ref, di_ref, dlb_ref, dstate, starts):
        @pl.when(pl.program_id(1) == 0)
        def _():
            dstate[...] = jnp.zeros_like(dstate)
            dlb_ref[...] = jnp.zeros_like(dlb_ref)

        lb = lb_ref[...]
        rowid = lax.broadcasted_iota(jnp.int32, (SUB, DH), 0)

        def fwd_step(c, sv):
            rows = pl.ds(pl.multiple_of(c * SUB, SUB), SUB)
            starts[c] = sv
            k, iv, b = _hg_block(q_ref, f_ref, i_ref, lb, rows)[1:4]
            bl = b[SUB - 1:SUB, :]
            return sv * jnp.exp(bl) + _dot(iv, k * jnp.exp(bl - b), TN)

        starts[nsub] = lax.fori_loop(0, nsub, fwd_step, st_ref[...])

        def bwd_step(cc, carry):
            c = nsub - 1 - cc
            rows = pl.ds(pl.multiple_of(c * SUB, SUB), SUB)
            q, k, iv, b, qr, sg, f = _hg_block(q_ref, f_ref, i_ref, lb, rows)
            bl = b[SUB - 1:SUB, :]
            eb, ebl = jnp.exp(b), jnp.exp(bl - b)
            sv, dsv = starts[c], dstate[...]
            from_later = jnp.sum(dsv * starts[c + 1], axis=0, keepdims=True)
            dov = do_ref[rows, :]
            dq = _dot(dov, sv, NN) * eb
            dk = _dot(iv, dsv, NN) * ebl
            di = _dot(k * ebl, dsv, NT)
            for s in range(SUB):
                e = jnp.exp(jnp.where(rowid >= s, b - b[s:s + 1, :], NEG))
                ks, isv = k[s:s + 1, :], iv[s:s + 1, :]
                qe = q * e
                a = jnp.sum(qe * ks, axis=-1, keepdims=True)
                p = jnp.sum(dov * isv, axis=-1, keepdims=True)
                dq = dq + p * (e * ks)
                dks = jnp.sum(p * qe, axis=0, keepdims=True)
                dis = jnp.sum(a * dov, axis=0, keepdims=True)
                dk = dk + jnp.where(rowid == s, dks, 0.0)
                di = di + jnp.where(rowid == s, dis, 0.0)
            dstate[...] = dsv * jnp.exp(bl) + _dot(dov, q * eb, TN)
            dlogf = jnp.dot(_tri(SUB, True), q * dq - k * dk, precision=HIGHEST,
                            preferred_element_type=F32) + from_later
            df = dlogf / f - dk
            dlb_ref[...] += jnp.sum(df * (1.0 - sg), axis=0, keepdims=True)
            dfl_ref[rows, :] = (df * (1.0 - lb) * (sg * (1.0 - sg))).astype(BF16)
            dq_ref[rows, :] = (dq * _dsilu(qr)).astype(BF16)
            di_ref[rows, :] = di.astype(BF16)
            return carry

        lax.fori_loop(0, nsub, bwd_step, 0)

    tile = pl.BlockSpec((tb, DH), lambda h, t: (rev(t), h))
    act = jax.ShapeDtypeStruct((T, D_MODEL), BF16)
    return pl.pallas_call(
        body, name=name, grid=(HEADS, nb),
        in_specs=_hg_specs(tb, rev) + [pl.BlockSpec((None, None, DH, DH), lambda h, t: (rev(t), h, 0, 0)), tile],
        out_specs=[tile, tile, tile, pl.BlockSpec((None, 1, DH), lambda h, t: (h, 0, 0))],
        out_shape=[act, act, act, jax.ShapeDtypeStruct((HEADS, 1, DH), F32)],
        scratch_shapes=[pltpu.VMEM((DH, DH), F32), pltpu.VMEM((nsub + 1, DH, DH), F32)],
        compiler_params=_params("parallel", "arbitrary"),
    )(z, z, z, lb3, states, do)


FOX_Q_CB, FOX_K_CB, FOX_V_CB = 4 * HEADS, 5 * HEADS, 6 * HEADS
FOX_SCALE = 1.0 / math.sqrt(DH)


def _fox_tile(T):
    return 512 if T >= 2048 else 128


def _fox_scores(q, k, cc, cr, i, j, tq, masked):
    s = _dot(q, k, NT) * FOX_SCALE + (cc - cr)
    if masked:
        rpos = i * tq + lax.broadcasted_iota(jnp.int32, (tq, tq), 0)
        cpos = j * tq + lax.broadcasted_iota(jnp.int32, (tq, tq), 1)
        s = jnp.where(cpos <= rpos, s, NEG)
    return s


def _fox_fwd(z, c_col, c_row, *, name):
    T = z.shape[0]
    tq = _fox_tile(T)
    nq = T // tq

    def body(q_ref, k_ref, v_ref, cc_ref, cr_ref, o_ref, lse_ref, m_scr, l_scr, acc):
        i, j = pl.program_id(1), pl.program_id(2)

        @pl.when(j == 0)
        def _():
            m_scr[...] = jnp.full_like(m_scr, NEG)
            l_scr[...] = jnp.zeros_like(l_scr)
            acc[...] = jnp.zeros_like(acc)

        def update(masked):
            s = _fox_scores(q_ref[...], k_ref[...], cc_ref[...], cr_ref[...], i, j, tq, masked)
            m_old = m_scr[...]
            m_new = jnp.maximum(m_old, jnp.max(s, axis=-1, keepdims=True))
            alpha = jnp.exp(m_old - m_new)
            p = jnp.exp(s - m_new)
            l_scr[...] = alpha * l_scr[...] + jnp.sum(p, axis=-1, keepdims=True)
            acc[...] = alpha * acc[...] + _dot(p, v_ref[...], NN)
            m_scr[...] = m_new

        @pl.when(j < i)
        def _():
            update(False)

        @pl.when(j == i)
        def _():
            update(True)
            o_ref[...] = acc[...] / l_scr[...]
            lse_ref[...] = m_scr[...] + jnp.log(l_scr[...])

    kj = lambda i, j: jnp.minimum(i, j)
    return pl.pallas_call(
        body, name=name, grid=(HEADS, nq, nq),
        in_specs=[pl.BlockSpec((tq, DH), lambda h, i, j: (i, FOX_Q_CB + h)),
                  pl.BlockSpec((tq, DH), lambda h, i, j: (kj(i, j), FOX_K_CB + h)),
                  pl.BlockSpec((tq, DH), lambda h, i, j: (kj(i, j), FOX_V_CB + h)),
                  pl.BlockSpec((None, tq, 1), lambda h, i, j: (h, i, 0)),
                  pl.BlockSpec((None, 1, tq), lambda h, i, j: (h, 0, kj(i, j)))],
        out_specs=[pl.BlockSpec((tq, DH), lambda h, i, j: (i, h)),
                   pl.BlockSpec((None, tq, 1), lambda h, i, j: (h, i, 0))],
        out_shape=[jax.ShapeDtypeStruct((T, D_MODEL), F32), jax.ShapeDtypeStruct((HEADS, T, 1), F32)],
        scratch_shapes=[pltpu.VMEM((tq, 1), F32), pltpu.VMEM((tq, 1), F32), pltpu.VMEM((tq, DH), F32)],
        compiler_params=_params("parallel", "parallel", "arbitrary"),
    )(z, z, z, c_col, c_row)


def _fox_bwd(z, c_col, c_row, o, lse, do, *, name):
    T = z.shape[0]
    tq = _fox_tile(T)
    nq = T // tq

    def body(q_ref, k_ref, v_ref, cc_ref, cr_ref, o_ref, lse_ref, do_ref, dq_ref, dk_ref, dv_ref, dc_ref, dcq_ref,
             dk_acc, dv_acc, dc_acc):
        j, i = pl.program_id(1), pl.program_id(2)

        @pl.when((j == 0) & (i == 0))
        def _():
            dq_ref[...] = jnp.zeros_like(dq_ref)
            dcq_ref[...] = jnp.zeros_like(dcq_ref)

        def update(masked):
            q, k, dov = q_ref[...], k_ref[...], do_ref[...]
            s = _fox_scores(q, k, cc_ref[...], cr_ref[...], i, j, tq, masked)
            p = jnp.exp(s - lse_ref[...])
            dp = _dot(dov, v_ref[...], NT)
            delta = jnp.sum(dov * o_ref[...], axis=-1, keepdims=True)
            ds = p * (dp - delta)
            dv_new = _dot(p, dov, TN)
            dk_new = _dot(ds, q, TN) * FOX_SCALE
            dc_new = -jnp.sum(ds, axis=0, keepdims=True)
            rows = pl.ds(pl.multiple_of(i * tq, tq), tq)
            dq_ref[rows, :] += _dot(ds, k, NN) * FOX_SCALE
            dcq_ref[rows, :] += jnp.sum(ds, axis=-1, keepdims=True)
            return dk_new, dv_new, dc_new

        @pl.when(i == j)
        def _():
            dk_new, dv_new, dc_new = update(True)
            dk_acc[...] = dk_new
            dv_acc[...] = dv_new
            dc_acc[...] = dc_new

        @pl.when(i > j)
        def _():
            dk_new, dv_new, dc_new = update(False)
            dk_acc[...] += dk_new
            dv_acc[...] += dv_new
            dc_acc[...] += dc_new

        @pl.when(i == nq - 1)
        def _():
            dk_ref[...] = dk_acc[...].astype(BF16)
            dv_ref[...] = dv_acc[...].astype(BF16)
            dc_ref[...] = dc_acc[...]

    qi = lambda j, i: jnp.maximum(i, j)
    qtile = lambda cb: pl.BlockSpec((tq, DH), lambda h, j, i, cb=cb: (qi(j, i), cb + h))
    ktile = lambda cb: pl.BlockSpec((tq, DH), lambda h, j, i, cb=cb: (j, cb + h))
    qcol = pl.BlockSpec((None, tq, 1), lambda h, j, i: (h, qi(j, i), 0))
    return pl.pallas_call(
        body, name=name, grid=(HEADS, nq, nq),
        in_specs=[qtile(FOX_Q_CB), ktile(FOX_K_CB), ktile(FOX_V_CB), qcol,
                  pl.BlockSpec((None, 1, tq), lambda h, j, i: (h, 0, j)), qtile(0), qcol, qtile(0)],
        out_specs=[pl.BlockSpec((T, DH), lambda h, j, i: (0, h)), ktile(0), ktile(0),
                   pl.BlockSpec((None, 1, tq), lambda h, j, i: (h, 0, j)),
                   pl.BlockSpec((None, T, 1), lambda h, j, i: (h, 0, 0))],
        out_shape=[jax.ShapeDtypeStruct((T, D_MODEL), F32), jax.ShapeDtypeStruct((T, D_MODEL), BF16),
                   jax.ShapeDtypeStruct((T, D_MODEL), BF16), jax.ShapeDtypeStruct((HEADS, 1, T), F32),
                   jax.ShapeDtypeStruct((HEADS, T, 1), F32)],
        scratch_shapes=[pltpu.VMEM((tq, DH), F32), pltpu.VMEM((tq, DH), F32), pltpu.VMEM((1, tq), F32)],
        compiler_params=_params("parallel", "arbitrary", "arbitrary"),
    )(z, z, z, c_col, c_row, o, lse, do)


MEM_SCALE = 1.0 / math.sqrt(MEM_DH)


def _mem_probs(qh, kh):
    s = _dot(qh, kh, NT) * MEM_SCALE
    p = jnp.exp(s - jnp.max(s, axis=-1, keepdims=True))
    return p / jnp.sum(p, axis=-1, keepdims=True)


def _mem_fwd(q, kv, *, name, tq=512):
    T = q.shape[0]
    tq = min(tq, T)

    def body(q_ref, kv_ref, o_ref):
        for h in range(MEM_HEADS):
            cols = slice(h * MEM_DH, (h + 1) * MEM_DH)
            vcols = slice(D_MODEL + h * MEM_DH, D_MODEL + (h + 1) * MEM_DH)
            p = _mem_probs(q_ref[:, cols], kv_ref[:, cols])
            o_ref[:, cols] = _dot(p, kv_ref[:, vcols], NN).astype(o_ref.dtype)

    return pl.pallas_call(
        body, name=name, grid=(T // tq,),
        in_specs=[pl.BlockSpec((tq, D_MODEL), lambda i: (i, 0)), pl.BlockSpec((MEM_LEN, 2 * D_MODEL), lambda i: (0, 0))],
        out_specs=pl.BlockSpec((tq, D_MODEL), lambda i: (i, 0)), out_shape=jax.ShapeDtypeStruct((T, D_MODEL), BF16),
        compiler_params=_params("parallel"),
    )(q, kv)


def _mem_bwd(q, kv, do, *, name, tq=512):
    T = q.shape[0]
    tq = min(tq, T)

    def body(q_ref, kv_ref, do_ref, dq_ref, dkv_ref):
        @pl.when(pl.program_id(0) == 0)
        def _():
            dkv_ref[...] = jnp.zeros_like(dkv_ref)

        for h in range(MEM_HEADS):
            cols = slice(h * MEM_DH, (h + 1) * MEM_DH)
            vcols = slice(D_MODEL + h * MEM_DH, D_MODEL + (h + 1) * MEM_DH)
            qh, kh, doh = q_ref[:, cols], kv_ref[:, cols], do_ref[:, cols]
            p = _mem_probs(qh, kh)
            dp = _dot(doh, kv_ref[:, vcols], NT)
            ds = p * (dp - jnp.sum(p * dp, axis=-1, keepdims=True))
            dq_ref[:, cols] = (_dot(ds, kh, NN) * MEM_SCALE).astype(dq_ref.dtype)
            dkv_ref[:, cols] += _dot(ds, qh, TN) * MEM_SCALE
            dkv_ref[:, vcols] += _dot(p, doh, TN)

    row = pl.BlockSpec((tq, D_MODEL), lambda i: (i, 0))
    full = pl.BlockSpec((MEM_LEN, 2 * D_MODEL), lambda i: (0, 0))
    return pl.pallas_call(
        body, name=name, grid=(T // tq,), in_specs=[row, full, row], out_specs=[row, full],
        out_shape=[jax.ShapeDtypeStruct((T, D_MODEL), BF16), jax.ShapeDtypeStruct((MEM_LEN, 2 * D_MODEL), F32)],
        compiler_params=_params("arbitrary"),
    )(q, kv, do)


def _adamw(w, g, m, v, *, name, tm=256):
    R, C = w.shape
    tm = min(tm, R)
    assert R % tm == 0
    nsum = g.shape[0] if g.ndim == 3 else 0

    def body(w_ref, g_ref, m_ref, v_ref, go_ref, d_ref, mo_ref, vo_ref):
        if nsum:
            gv = g_ref[0]
            for n in range(1, nsum):
                gv = gv + g_ref[n]
        else:
            gv = g_ref[...]
        mv = ADAM_B1 * m_ref[...] + (1.0 - ADAM_B1) * gv
        vv = ADAM_B2 * v_ref[...] + (1.0 - ADAM_B2) * jnp.square(gv)
        m_hat = mv / (1.0 - ADAM_B1 ** ADAM_STEP)
        v_hat = vv / (1.0 - ADAM_B2 ** ADAM_STEP)
        d_ref[...] = -ADAM_LR * (m_hat / (jnp.sqrt(v_hat) + ADAM_EPS) + ADAM_WD * w_ref[...])
        go_ref[...] = gv
        mo_ref[...] = mv
        vo_ref[...] = vv

    row = pl.BlockSpec((tm, C), lambda i: (i, 0))
    gspec = pl.BlockSpec((nsum, tm, C), lambda i: (0, i, 0)) if nsum else row
    return pl.pallas_call(
        body, name=name, grid=(R // tm,), in_specs=[row, gspec, row, row], out_specs=[row] * 4,
        out_shape=[jax.ShapeDtypeStruct((R, C), F32)] * 4, compiler_params=_params("parallel"),
    )(w, g, m, v)


def _act_mm(a, w, name, out_dtype=F32):
    return _mm([(a, w)], "nn", tm=1024, tn=512, tk=w.shape[0], out_dtypes=[out_dtype], name=name)


def _act_mm_t(a, w, name, out_dtype=F32):
    return _mm([(a, w)], "nt", tm=1024, tn=512, tk=1024, out_dtypes=[out_dtype], name=name)


def _wgrad(a, dy, name, tm=1024):
    tn = 512 if dy.shape[1] % 512 == 0 else 256
    return _mm([(a, dy)], "tn", tm=tm, tn=tn, tk=512, out_dtypes=[F32], name=name)


def _colsum8(p):
    return jnp.sum(p, axis=0, keepdims=True)


def _ffn_fwd(x, pre_g, post_g, wg, wu, wd, tag):
    h = _rms_fwd(x, pre_g, out_dtype=BF16, name=tag + "_pre")
    act, gate, up = _ffn_in(h, wg, wu, name=tag + "_in")
    d = _mm([(act, wd)], "nn", tm=1024, tn=512, tk=D_FF, out_dtypes=[F32], name=tag + "_down")
    xo = _rms_fwd(d, post_g, out_dtype=F32, name=tag + "_post", res=x, coeff=0.5)
    return xo, (h, act, gate, up, d)


def _ffn_bwd(x, dxo, saved, pre_g, post_g, wg, wu, wd, tag):
    h, act, gate, up, d = saved
    dd, dg_post = _rms_bwd(d, post_g, dxo, name=tag + "_post_b", coeff=0.5, dx_dtype=BF16)
    dgate, dup = _mm([(dd, wd)], "nt", tm=1024, tn=256, tk=D_MODEL, out_dtypes=[BF16, BF16], name=tag + "_act_b",
                     epilogue=_swiglu_bwd_epilogue, tiles=(gate, up))
    dwd = _wgrad(act, dd, tag + "_dwd", tm=D_FF // 2)
    dh = _mm([(dgate, wg), (dup, wu)], "nt", tm=512, tn=512, tk=D_FF, out_dtypes=[F32], name=tag + "_in_b")
    dwg = _wgrad(h, dgate, tag + "_dwg")
    dwu = _wgrad(h, dup, tag + "_dwu")
    dx, dg_pre = _rms_bwd(x, pre_g, dh, name=tag + "_pre_b", add=dxo)
    return dx, dict(pre_g=_colsum8(dg_pre), post_g=_colsum8(dg_post), wg=dwg, wu=dwu, wd=dwd)


def _local_step(x, mem, target, W, P):
    T = x.shape[0]
    G = {}
    logits = P["hg_lb_logits"]
    lb = _sigmoid(logits[0] - logits[1])
    lb3, lb_row = lb.reshape(HEADS, 1, DH), lb.reshape(1, D_MODEL)
    fbias_row = jnp.pad(P["fox_f_bias"], ((0, 0), (0, LANE - HEADS)))

    x1, ffn1_saved = _ffn_fwd(x, P["ffn1_pre_g"], P["ffn1_post_g"], W["f1g"], W["f1u"], W["f1d"], "ffn1")
    h2 = _rms_fwd(x1, P["mix_pre_g"], out_dtype=BF16, name="mix_pre")
    z = _act_mm(h2, W["w_main"], "mix_in")
    zfb = _mm([(h2, W["w_fb"])], "nn", tm=1024, tn=LANE, tk=D_MODEL, out_dtypes=[F32], name="mix_in_fb")
    oa_pre, states = _hgrn2_fwd(z, lb3, name="hgrn2_f")
    o_a = _rms_fwd(oa_pre, P["hg_norm_g"], out_dtype=BF16, name="hgrn2_post", mul=(z, 3))
    y_a = _act_mm(o_a, W["wa"], "branch_a")
    c = _cumsum_t([(zfb, 0)], name="fox_c", width=LANE, rows=[(fbias_row, 0)], pre=lambda v, r: _logsigmoid(v + r))
    ct = c[:, :HEADS].T
    c_col, c_row = ct[:, :, None], ct[:, None, :]
    o_b, lse = _fox_fwd(z, c_col, c_row, name="fox_f")
    y_b = _act_mm(o_b, W["wb"], "branch_b")
    y = _gatemix_fwd(z, P["b_gate"], y_a, y_b, name="gatemix")
    m = _act_mm(y, W["wo"], "mix_out")
    x2 = _rms_fwd(m, P["mix_post_g"], out_dtype=F32, name="mix_post", res=x1)
    h3 = _rms_fwd(x2, P["mem_pre_g"], out_dtype=BF16, name="mem_pre")
    mem_n = _rms_fwd(mem, P["mem_kv_g"], out_dtype=BF16, name="mem_kvn")
    qm = _act_mm(h3, W["wmq"], "mem_q")
    kv = _act_mm(mem_n, W["wmkv"], "mem_kv")
    om = _mem_fwd(qm, kv, name="mem_attn")
    mo = _act_mm(om, W["wmo"], "mem_o")
    x3 = _rms_fwd(mo, P["mem_post_g"], out_dtype=F32, name="mem_post", res=x2)
    x4, ffn2_saved = _ffn_fwd(x3, P["ffn2_pre_g"], P["ffn2_post_g"], W["f2g"], W["f2u"], W["f2d"], "ffn2")
    dx4, sq = _loss_head(x4, target, name="loss_head")

    dx3, g = _ffn_bwd(x3, dx4, ffn2_saved, P["ffn2_pre_g"], P["ffn2_post_g"], W["f2g"], W["f2u"], W["f2d"], "ffn2")
    G.update(ffn2_pre_g=g["pre_g"], ffn2_post_g=g["post_g"], f2g=g["wg"], f2u=g["wu"], f2d=g["wd"])

    dmo, dgp = _rms_bwd(mo, P["mem_post_g"], dx3, name="mem_post_b", dx_dtype=BF16)
    G["mem_post_g"] = _colsum8(dgp)
    dom = _act_mm_t(dmo, W["wmo"], "mem_o_b", BF16)
    G["wmo"] = _wgrad(om, dmo, "mem_o_w")
    dqm, dkv = _mem_bwd(qm, kv, dom, name="mem_attn_b")
    dh3 = _act_mm_t(dqm, W["wmq"], "mem_q_b")
    G["wmq"] = _wgrad(h3, dqm, "mem_q_w")
    G["wmkv"] = _mm([(mem_n, dkv)], "tn", tm=1024, tn=512, tk=MEM_LEN, out_dtypes=[F32], name="mem_kv_w")
    dmem_n = _mm([(dkv, W["wmkv"])], "nt", tm=MEM_LEN, tn=512, tk=2 * D_MODEL, out_dtypes=[F32], name="mem_kv_b")
    _, dgp = _rms_bwd(mem, P["mem_kv_g"], dmem_n, name="mem_kvn_b")
    G["mem_kv_g"] = _colsum8(dgp)
    dx2, dgp = _rms_bwd(x2, P["mem_pre_g"], dh3, name="mem_pre_b", add=dx3)
    G["mem_pre_g"] = _colsum8(dgp)

    dm, dgp = _rms_bwd(m, P["mix_post_g"], dx2, name="mix_post_b", dx_dtype=BF16)
    G["mix_post_g"] = _colsum8(dgp)
    dy = _act_mm_t(dm, W["wo"], "mix_out_b")
    G["wo"] = _wgrad(y, dm, "mix_out_w")
    dya, dyb, dz0, dz1, s0, s1 = _gatemix_bwd(z, P["b_gate"], y_a, y_b, dy, name="gatemix_b")
    G["b_gate"] = jnp.concatenate([_colsum8(s0), _colsum8(s1)], axis=1)
    do_a = _act_mm_t(dya, W["wa"], "branch_a_b")
    G["wa"] = _wgrad(o_a, dya, "branch_a_w")
    do_b = _act_mm_t(dyb, W["wb"], "branch_b_b")
    G["wb"] = _wgrad(o_b, dyb, "branch_b_w")
    doa_pre, dgp, dga = _rms_bwd(oa_pre, P["hg_norm_g"], do_a, name="hgrn2_post_b", mul=(z, 3))
    G["hg_norm_g"] = _colsum8(dgp)
    dq_a, dfl_a, di_a, dlb = _hgrn2_bwd(z, lb3, states, doa_pre, name="hgrn2_b")
    dl0 = (dlb * lb3 * (1.0 - lb3)).reshape(1, HEADS, DH)
    G["hg_lb_logits"] = jnp.concatenate([dl0, -dl0], axis=0)
    dq_b, dk_b, dv_b, dcr, dcq = _fox_bwd(z, c_col, c_row, o_b, lse, do_b, name="fox_b")
    dc_pad = jnp.pad((dcr[:, 0, :] + dcq[:, :, 0]).T, ((0, 0), (0, LANE - HEADS)))
    gate_b = lambda cum, dc, zf, r: cum * _sigmoid(-(zf + r))
    dfl_b, dfb = _cumsum_t([(dc_pad, 0), (zfb, 0)], name="fox_c_b", width=LANE, reverse=True, rows=[(fbias_row, 0)],
                           pre=lambda dc, zf, r: dc, post=lambda *a: (gate_b(*a),), fold=gate_b)
    G["fox_f_bias"] = _colsum8(dfb)[:, :HEADS]

    pieces = [dq_a, dfl_a, di_a, dga, dq_b, dk_b, dv_b, dz0, dz1]
    dh2 = _mm([(dfl_b, W["w_fb"])], "nt", tm=512, tn=D_MODEL, tk=LANE, out_dtypes=[F32], name="mix_in_fb_b")
    for lo, hi in ((0, 5), (5, 9)):
        dh2 = _mm([(p, W["w_main"]) for p in pieces[lo:hi]], "nt", tm=512, tn=D_MODEL, tk=D_MODEL, out_dtypes=[F32],
                  name=f"mix_in_b{lo}", b_koff=[n * D_MODEL for n in range(lo, hi)],
                  epilogue=lambda acc, t: (acc + t,), tiles=(dh2,))
    G["w_main"] = [_wgrad(h2, p, f"mix_in_w{n}") for n, p in enumerate(pieces)]
    G["w_fb"] = _mm([(h2, dfl_b)], "tn", tm=1024, tn=LANE, tk=512, out_dtypes=[F32], name="mix_in_fb_w")
    dx1, dgp = _rms_bwd(x1, P["mix_pre_g"], dh2, name="mix_pre_b", add=dx2)
    G["mix_pre_g"] = _colsum8(dgp)

    dx0, g = _ffn_bwd(x, dx1, ffn1_saved, P["ffn1_pre_g"], P["ffn1_post_g"], W["f1g"], W["f1u"], W["f1d"], "ffn1")
    G.update(ffn1_pre_g=g["pre_g"], ffn1_post_g=g["post_g"], f1g=g["wg"], f1u=g["wu"], f1d=g["wd"])
    return sq, dx0, G


N_CHIP = 4
N_DEV = 8
IN_COLS = 9224
FB_COL = 7 * D_MODEL
SHARDED = (
    ("ffn1_w_in", (D_MODEL, 2 * D_FF), 1), ("ffn1_w_down", (D_FF, D_MODEL), 0), ("w_in", (D_MODEL, IN_COLS), 1),
    ("w_branch_a", (D_MODEL, D_MODEL), 0), ("w_branch_b", (D_MODEL, D_MODEL), 0), ("w_out", (D_MODEL, D_MODEL), 0),
    ("w_mq", (D_MODEL, D_MODEL), 0), ("w_mkv", (D_MODEL, 2 * D_MODEL), 1), ("w_mo", (D_MODEL, D_MODEL), 0),
    ("ffn2_w_in", (D_MODEL, 2 * D_FF), 1), ("ffn2_w_down", (D_FF, D_MODEL), 0),
)
SMALL = ("ffn1_pre_g", "ffn1_post_g", "mix_pre_g", "hg_norm_g", "mix_post_g", "mem_pre_g", "mem_kv_g", "mem_post_g",
         "ffn2_pre_g", "ffn2_post_g", "b_gate", "hg_lb_logits", "fox_f_bias")
SMALL_SHAPES = dict(b_gate=(1, 2 * D_MODEL), hg_lb_logits=(2, HEADS, DH), fox_f_bias=(1, HEADS))
SMALL_ROWS = 16
WEIGHT_ORDER = ("ffn1_pre_g", "ffn1_w_in", "ffn1_w_down", "ffn1_post_g", "mix_pre_g", "w_in", "hg_lb_logits", "hg_norm_g",
                "fox_f_bias", "w_branch_a", "w_branch_b", "b_gate", "w_out", "mix_post_g", "mem_pre_g", "mem_kv_g", "w_mq",
                "w_mkv", "w_mo", "mem_post_g", "ffn2_pre_g", "ffn2_w_in", "ffn2_w_down", "ffn2_post_g")


def _shard_shape(shape, axis):
    s = list(shape)
    s[axis] //= N_CHIP
    return tuple(s)


PACK_USED = sum(math.prod(shape) // N_CHIP // D_MODEL for _, shape, _ in SHARDED)
PACK_ROWS = -(-PACK_USED // 256) * 256
HALF = PACK_ROWS // 2


def _pack(shards, dtype):
    rows = [shards[name].astype(dtype).reshape(-1, D_MODEL) for name, _, _ in SHARDED]
    rows.append(jnp.zeros((PACK_ROWS - PACK_USED, D_MODEL), dtype))
    return jnp.concatenate(rows, axis=0)


def _unpack(slab):
    out, r = {}, 0
    for name, shape, axis in SHARDED:
        ss = _shard_shape(shape, axis)
        n = math.prod(ss) // D_MODEL
        out[name] = slab[r:r + n].reshape(ss)
        r += n
    return out


def _pack_small(vals):
    rows = []
    for name in SMALL:
        v = vals[name].astype(F32).reshape(-1)
        rows.append(jnp.pad(v, (0, -v.shape[0] % D_MODEL)).reshape(-1, D_MODEL))
    rows = jnp.concatenate(rows, axis=0)
    return jnp.pad(rows, ((0, SMALL_ROWS - rows.shape[0]), (0, 0)))


def _unpack_small(slab):
    out, r = {}, 0
    for name in SMALL:
        shape = SMALL_SHAPES.get(name, (1, D_MODEL))
        size = math.prod(shape)
        n = -(-size // D_MODEL)
        out[name] = slab[r:r + n].reshape(-1)[:size].reshape(shape)
        r += n
    return out


ANY = pl.BlockSpec(memory_space=pl.ANY)
MESH = pl.DeviceIdType.MESH
CHIP_FLIPS = ((1, 0), (0, 1), (1, 1))


def _place():
    x, y, c = lax.axis_index("x"), lax.axis_index("y"), lax.axis_index("c")
    chips = [(x ^ fx, y ^ fy) for fx, fy in CHIP_FLIPS]
    return x, y, c, chips


def _remote(src, dst, sems, k, dev):
    return pltpu.make_async_remote_copy(src_ref=src, dst_ref=dst, send_sem=sems[0].at[k], recv_sem=sems[1].at[k],
                                        device_id=dev, device_id_type=MESH)


def _gather_weights(wpack):
    def body(w_ref, out_ref, send_sems, recv_sems, local_sem):
        x, y, c, chips = _place()
        sems = (send_sems, recv_sems)
        me = 2 * x + y
        mine, theirs = pl.ds(c * HALF, HALF), pl.ds((1 - c) * HALF, HALF)
        local = pltpu.make_async_copy(w_ref, out_ref.at[me], local_sem)
        local.start()
        sent = [_remote(w_ref.at[mine], out_ref.at[me, mine], sems, k, (px, py, c)) for k, (px, py) in enumerate(chips)]
        for cp in sent:
            cp.start()
        passed = []
        for k, (px, py) in enumerate(chips):
            rows = out_ref.at[2 * px + py, mine]
            _remote(rows, rows, sems, k, (px, py, c)).wait_recv()
            cp = _remote(rows, rows, sems, 3 + k, (x, y, 1 - c))
            cp.start()
            passed.append(cp)
        for k, (px, py) in enumerate(chips):
            rows = out_ref.at[2 * px + py, theirs]
            _remote(rows, rows, sems, 3 + k, (x, y, 1 - c)).wait_recv()
        for cp in sent + passed:
            cp.wait_send()
        local.wait()

    return pl.pallas_call(
        body, name="gather_weights", out_shape=jax.ShapeDtypeStruct((N_CHIP, PACK_ROWS, D_MODEL), wpack.dtype),
        in_specs=[ANY], out_specs=ANY,
        scratch_shapes=[pltpu.SemaphoreType.DMA((6,)), pltpu.SemaphoreType.DMA((6,)), pltpu.SemaphoreType.DMA],
    )(wpack)


def _swap_halves(g):
    def body(g_ref, out_ref, send_sems, recv_sems):
        x, y, c, _ = _place()
        sems = (send_sems, recv_sems)
        theirs = pl.ds((1 - c) * HALF, HALF)
        sent = [_remote(g_ref.at[j, theirs], out_ref.at[j], sems, j, (x, y, 1 - c)) for j in range(N_CHIP)]
        for cp in sent:
            cp.start()
        for cp in sent:
            cp.wait_recv()
        for cp in sent:
            cp.wait_send()

    return pl.pallas_call(
        body, name="swap_halves", out_shape=jax.ShapeDtypeStruct((N_CHIP, HALF, D_MODEL), g.dtype),
        in_specs=[ANY], out_specs=ANY,
        scratch_shapes=[pltpu.SemaphoreType.DMA((N_CHIP,)), pltpu.SemaphoreType.DMA((N_CHIP,))],
    )(g)


def _pair_sum(g, got, place, *, tm=128):
    nb = HALF // tm

    def body(s_ref, g_ref, a_ref, bf_ref, own_ref):
        v = g_ref[...] + a_ref[...]
        bf_ref[...] = v.astype(BF16)

        @pl.when(pl.program_id(1) == s_ref[0])
        def _():
            own_ref[...] = v

    return pl.pallas_call(
        body, name="pair_sum",
        grid_spec=pltpu.PrefetchScalarGridSpec(
            num_scalar_prefetch=1, grid=(nb, N_CHIP),
            in_specs=[pl.BlockSpec((None, tm, D_MODEL), lambda i, j, s: (j, s[1] * nb + i, 0)),
                      pl.BlockSpec((None, tm, D_MODEL), lambda i, j, s: (j, i, 0))],
            out_specs=[pl.BlockSpec((None, tm, D_MODEL), lambda i, j, s: (j, i, 0)),
                       pl.BlockSpec((tm, D_MODEL), lambda i, j, s: (i, 0))]),
        out_shape=[jax.ShapeDtypeStruct((N_CHIP, HALF, D_MODEL), BF16), jax.ShapeDtypeStruct((HALF, D_MODEL), F32)],
        compiler_params=_params("arbitrary", "arbitrary"),
    )(place, g, got)


def _scatter_partials(pbf):
    def body(p_ref, out_ref, send_sems, recv_sems):
        x, y, c, chips = _place()
        sems = (send_sems, recv_sems)
        sent = [_remote(p_ref.at[2 * px + py], out_ref.at[k], sems, k, (px, py, c)) for k, (px, py) in enumerate(chips)]
        for cp in sent:
            cp.start()
        for cp in sent:
            cp.wait_recv()
        for cp in sent:
            cp.wait_send()

    return pl.pallas_call(
        body, name="scatter_partials", out_shape=jax.ShapeDtypeStruct((3, HALF, D_MODEL), pbf.dtype),
        in_specs=[ANY], out_specs=ANY, scratch_shapes=[pltpu.SemaphoreType.DMA((3,)), pltpu.SemaphoreType.DMA((3,))],
    )(pbf)


def _chip_sum(own, got, *, tm=128):
    def body(o_ref, g_ref, r_ref):
        r_ref[...] = ((o_ref[...] + g_ref[0].astype(F32)) + g_ref[1].astype(F32)) + g_ref[2].astype(F32)

    row = pl.BlockSpec((tm, D_MODEL), lambda i: (i, 0))
    return pl.pallas_call(
        body, name="chip_sum", grid=(HALF // tm,),
        in_specs=[row, pl.BlockSpec((3, tm, D_MODEL), lambda i: (0, i, 0))], out_specs=row,
        out_shape=jax.ShapeDtypeStruct((HALF, D_MODEL), F32), compiler_params=_params("parallel"),
    )(own, got)


def _join_halves(r):
    def body(r_ref, out_ref, send_sems, recv_sems, local_sem):
        x, y, c, _ = _place()
        mine, theirs = pl.ds(c * HALF, HALF), pl.ds((1 - c) * HALF, HALF)
        local = pltpu.make_async_copy(r_ref, out_ref.at[mine], local_sem)
        local.start()
        cp = _remote(r_ref, out_ref.at[mine], (send_sems, recv_sems), 0, (x, y, 1 - c))
        cp.start()
        _remote(r_ref, out_ref.at[theirs], (send_sems, recv_sems), 0, (x, y, 1 - c)).wait_recv()
        cp.wait_send()
        local.wait()

    return pl.pallas_call(
        body, name="join_halves", out_shape=jax.ShapeDtypeStruct((PACK_ROWS, D_MODEL), r.dtype),
        in_specs=[ANY], out_specs=ANY,
        scratch_shapes=[pltpu.SemaphoreType.DMA((1,)), pltpu.SemaphoreType.DMA((1,)), pltpu.SemaphoreType.DMA],
    )(r)


def _gather_small(s):
    flips = [(fx, fy, fc) for fx in (0, 1) for fy in (0, 1) for fc in (0, 1)][1:]

    def body(s_ref, out_ref, send_sems, recv_sems, local_sem):
        x, y, c, _ = _place()
        sems = (send_sems, recv_sems)
        me = 4 * x + 2 * y + c
        local = pltpu.make_async_copy(s_ref, out_ref.at[me], local_sem)
        local.start()
        sent = [_remote(s_ref, out_ref.at[me], sems, k, (x ^ fx, y ^ fy, c ^ fc)) for k, (fx, fy, fc) in enumerate(flips)]
        for cp in sent:
            cp.start()
        for k, (fx, fy, fc) in enumerate(flips):
            peer = (x ^ fx, y ^ fy, c ^ fc)
            _remote(s_ref, out_ref.at[4 * peer[0] + 2 * peer[1] + peer[2]], sems, k, peer).wait_recv()
        for cp in sent:
            cp.wait_send()
        local.wait()

    return pl.pallas_call(
        body, name="gather_small", out_shape=jax.ShapeDtypeStruct((N_DEV, SMALL_ROWS, D_MODEL), s.dtype),
        in_specs=[ANY], out_specs=ANY,
        scratch_shapes=[pltpu.SemaphoreType.DMA((7,)), pltpu.SemaphoreType.DMA((7,)), pltpu.SemaphoreType.DMA],
    )(s)


def _full_weights(gathered):
    per_chip = [_unpack(gathered[j]) for j in range(N_CHIP)]
    full = {name: jnp.concatenate([pc[name] for pc in per_chip], axis=axis) for name, _, axis in SHARDED}
    w_in = full["w_in"]
    return dict(
        f1g=full["ffn1_w_in"][:, :D_FF], f1u=full["ffn1_w_in"][:, D_FF:], f1d=full["ffn1_w_down"],
        f2g=full["ffn2_w_in"][:, :D_FF], f2u=full["ffn2_w_in"][:, D_FF:], f2d=full["ffn2_w_down"],
        w_main=jnp.concatenate([w_in[:, :FB_COL], w_in[:, FB_COL + HEADS:]], axis=1),
        w_fb=jnp.pad(w_in[:, FB_COL:FB_COL + HEADS], ((0, 0), (0, LANE - HEADS))),
        wa=full["w_branch_a"], wb=full["w_branch_b"], wo=full["w_out"],
        wmq=full["w_mq"], wmkv=full["w_mkv"], wmo=full["w_mo"],
    )


def _grad_slabs(G):
    main = jnp.concatenate(G["w_main"], axis=1)
    full = dict(
        ffn1_w_in=jnp.concatenate([G["f1g"], G["f1u"]], axis=1), ffn1_w_down=G["f1d"],
        ffn2_w_in=jnp.concatenate([G["f2g"], G["f2u"]], axis=1), ffn2_w_down=G["f2d"],
        w_in=jnp.concatenate([main[:, :FB_COL], G["w_fb"][:, :HEADS], main[:, FB_COL:]], axis=1),
        w_branch_a=G["wa"], w_branch_b=G["wb"], w_out=G["wo"], w_mq=G["wmq"], w_mkv=G["wmkv"], w_mo=G["wmo"],
    )
    slabs = []
    for j in range(N_CHIP):
        shards = {}
        for name, shape, axis in SHARDED:
            n = shape[axis] // N_CHIP
            shards[name] = lax.slice_in_dim(full[name], j * n, (j + 1) * n, axis=axis)
        slabs.append(_pack(shards, F32))
    return jnp.stack(slabs, axis=0)


def kernel(x, mem, ffn1_pre_g, ffn1_w_in, ffn1_w_down, ffn1_post_g, mix_pre_g, w_in, hg_lb_logits, hg_norm_g, fox_f_bias, w_branch_a, w_branch_b, b_gate, w_out, mix_post_g, mem_pre_g, mem_kv_g, w_mq, w_mkv, w_mo, mem_post_g, ffn2_pre_g, ffn2_w_in, ffn2_w_down, ffn2_post_g, loss_target, m_ffn1_pre_g, m_ffn1_w_in, m_ffn1_w_down, m_ffn1_post_g, m_mix_pre_g, m_w_in, m_hg_lb_logits, m_hg_norm_g, m_fox_f_bias, m_w_branch_a, m_w_branch_b, m_b_gate, m_w_out, m_mix_post_g, m_mem_pre_g, m_mem_kv_g, m_w_mq, m_w_mkv, m_w_mo, m_mem_post_g, m_ffn2_pre_g, m_ffn2_w_in, m_ffn2_w_down, m_ffn2_post_g, v_ffn1_pre_g, v_ffn1_w_in, v_ffn1_w_down, v_ffn1_post_g, v_mix_pre_g, v_w_in, v_hg_lb_logits, v_hg_norm_g, v_fox_f_bias, v_w_branch_a, v_w_branch_b, v_b_gate, v_w_out, v_mix_post_g, v_mem_pre_g, v_mem_kv_g, v_w_mq, v_w_mkv, v_w_mo, v_mem_post_g, v_ffn2_pre_g, v_ffn2_w_in, v_ffn2_w_down, v_ffn2_post_g):
    w = dict(ffn1_pre_g=ffn1_pre_g, ffn1_w_in=ffn1_w_in, ffn1_w_down=ffn1_w_down, ffn1_post_g=ffn1_post_g, mix_pre_g=mix_pre_g, w_in=w_in, hg_lb_logits=hg_lb_logits, hg_norm_g=hg_norm_g, fox_f_bias=fox_f_bias, w_branch_a=w_branch_a, w_branch_b=w_branch_b, b_gate=b_gate, w_out=w_out, mix_post_g=mix_post_g, mem_pre_g=mem_pre_g, mem_kv_g=mem_kv_g, w_mq=w_mq, w_mkv=w_mkv, w_mo=w_mo, mem_post_g=mem_post_g, ffn2_pre_g=ffn2_pre_g, ffn2_w_in=ffn2_w_in, ffn2_w_down=ffn2_w_down, ffn2_post_g=ffn2_post_g)
    m = dict(ffn1_pre_g=m_ffn1_pre_g, ffn1_w_in=m_ffn1_w_in, ffn1_w_down=m_ffn1_w_down, ffn1_post_g=m_ffn1_post_g, mix_pre_g=m_mix_pre_g, w_in=m_w_in, hg_lb_logits=m_hg_lb_logits, hg_norm_g=m_hg_norm_g, fox_f_bias=m_fox_f_bias, w_branch_a=m_w_branch_a, w_branch_b=m_w_branch_b, b_gate=m_b_gate, w_out=m_w_out, mix_post_g=m_mix_post_g, mem_pre_g=m_mem_pre_g, mem_kv_g=m_mem_kv_g, w_mq=m_w_mq, w_mkv=m_w_mkv, w_mo=m_w_mo, mem_post_g=m_mem_post_g, ffn2_pre_g=m_ffn2_pre_g, ffn2_w_in=m_ffn2_w_in, ffn2_w_down=m_ffn2_w_down, ffn2_post_g=m_ffn2_post_g)
    v = dict(ffn1_pre_g=v_ffn1_pre_g, ffn1_w_in=v_ffn1_w_in, ffn1_w_down=v_ffn1_w_down, ffn1_post_g=v_ffn1_post_g, mix_pre_g=v_mix_pre_g, w_in=v_w_in, hg_lb_logits=v_hg_lb_logits, hg_norm_g=v_hg_norm_g, fox_f_bias=v_fox_f_bias, w_branch_a=v_w_branch_a, w_branch_b=v_w_branch_b, b_gate=v_b_gate, w_out=v_w_out, mix_post_g=v_mix_post_g, mem_pre_g=v_mem_pre_g, mem_kv_g=v_mem_kv_g, w_mq=v_w_mq, w_mkv=v_w_mkv, w_mo=v_w_mo, mem_post_g=v_mem_post_g, ffn2_pre_g=v_ffn2_pre_g, ffn2_w_in=v_ffn2_w_in, ffn2_w_down=v_ffn2_w_down, ffn2_post_g=v_ffn2_post_g)
    sharded = [name for name, _, _ in SHARDED]
    shard_of = lambda d: {name: d[name][0] for name in sharded}

    W = _full_weights(_gather_weights(_pack(shard_of(w), BF16)))
    P = {name: w[name] for name in SMALL}

    sq, dx0, G = _local_step(x[0], mem[0], loss_target[0], W, P)
    loss = lax.psum(0.5 * jnp.sum(sq) / D_MODEL, ("x", "y", "c"))

    place = jnp.stack([2 * lax.axis_index("x") + lax.axis_index("y"), lax.axis_index("c")]).astype(jnp.int32)
    slabs = _grad_slabs(G)
    pbf, own = _pair_sum(slabs, _swap_halves(slabs), place)
    g_shard = _join_halves(_chip_sum(own, _scatter_partials(pbf)))
    g_out, d_out, m_out, v_out = _adamw(_pack(shard_of(w), F32), g_shard, _pack(shard_of(m), F32),
                                        _pack(shard_of(v), F32), name="adamw")
    gs_out, ds_out, ms_out, vs_out = _adamw(_pack_small(w), _gather_small(_pack_small(G)), _pack_small(m),
                                            _pack_small(v), name="adamw_small", tm=SMALL_ROWS)

    outs = [loss, dx0[None]]
    for big, small in ((g_out, gs_out), (d_out, ds_out), (m_out, ms_out), (v_out, vs_out)):
        vals = {name: a[None] for name, a in _unpack(big).items()}
        vals.update(_unpack_small(small))
        outs += [vals[name] for name in WEIGHT_ORDER]
    return tuple(outs)
```

```python
import functools
import math

import jax
import jax.numpy as jnp
from jax import lax
from jax.experimental import pallas as pl
from jax.experimental.pallas import tpu as pltpu

F32 = jnp.float32
BF16 = jnp.bfloat16

D_MODEL = 1024
D_FF = 2816
HEADS = 8
DH = 128
MEM_HEADS = 4
MEM_DH = 256
MEM_LEN = 256
EPS = 1e-6
SUB = 16
LANE = 128
SUBLANE = 8
VMEM_LIMIT = 56 * 1024 * 1024

ADAM_LR = 0.001
ADAM_B1 = 0.9
ADAM_B2 = 0.999
ADAM_EPS = 1e-08
ADAM_WD = 0.01
ADAM_STEP = 10

HIGHEST = lax.Precision.HIGHEST


def _params(*sem):
    return pltpu.CompilerParams(dimension_semantics=sem, vmem_limit_bytes=VMEM_LIMIT)


def _sigmoid(v):
    return 1.0 / (1.0 + jnp.exp(-v))


def _silu(v):
    return v * _sigmoid(v)


def _dsilu(v):
    s = _sigmoid(v)
    return s * (1.0 + v * (1.0 - s))


def _dot(a, b, dims):
    return lax.dot_general(a.astype(BF16), b.astype(BF16), (dims, ((), ())), preferred_element_type=F32)


NN = ((1,), (0,))
NT = ((1,), (1,))
TN = ((0,), (0,))


def _mm(pairs, mode, *, tm, tn, tk, out_dtypes, name, epilogue=None, tiles=(), b_koff=None):
    a0, b0 = pairs[0]
    if mode == "nn":
        (M, K), N = a0.shape, b0.shape[1]
    elif mode == "nt":
        (M, K), N = a0.shape, b0.shape[0]
    else:
        (K, M), N = a0.shape, b0.shape[1]
    tm, tn, tk = min(tm, M), min(tn, N), min(tk, K)
    assert M % tm == 0 and N % tn == 0 and K % tk == 0, (name, M, N, K, tm, tn, tk)
    nk = K // tk
    npair = len(pairs)
    koff = [0] * npair if b_koff is None else [o // tk for o in b_koff]
    if b_koff is not None:
        assert all(o % tk == 0 for o in b_koff)
    in_specs, args = [], []
    for p, (a, b) in enumerate(pairs):
        if mode == "nn":
            sa = pl.BlockSpec((tm, tk), lambda i, j, k: (i, k))
            sb = pl.BlockSpec((tk, tn), lambda i, j, k, o=koff[p]: (k + o, j))
            dims = NN
        elif mode == "nt":
            sa = pl.BlockSpec((tm, tk), lambda i, j, k: (i, k))
            sb = pl.BlockSpec((tn, tk), lambda i, j, k, o=koff[p]: (j, k + o))
            dims = NT
        else:
            sa = pl.BlockSpec((tk, tm), lambda i, j, k: (k, i))
            sb = pl.BlockSpec((tk, tn), lambda i, j, k, o=koff[p]: (k + o, j))
            dims = TN
        in_specs += [sa, sb]
        args += [a, b]
    for t in tiles:
        in_specs.append(pl.BlockSpec((tm, tn), lambda i, j, k: (i, j)))
        args.append(t)
    nt_ = len(tiles)
    nout = len(out_dtypes)

    def body(*refs):
        ab = refs[: 2 * npair]
        tl = refs[2 * npair: 2 * npair + nt_]
        outs = refs[2 * npair + nt_: 2 * npair + nt_ + nout]
        acc_ref = refs[-1] if nk > 1 else None

        def partial_sum():
            s = _dot(ab[0][...], ab[1][...], dims)
            for p in range(1, npair):
                s = s + _dot(ab[2 * p][...], ab[2 * p + 1][...], dims)
            return s

        def finish(acc):
            res = (acc,) if epilogue is None else epilogue(acc, *[t[...] for t in tl])
            for o, r in zip(outs, res):
                o[...] = r.astype(o.dtype)

        if nk == 1:
            finish(partial_sum())
        else:
            k = pl.program_id(2)

            @pl.when(k == 0)
            def _():
                acc_ref[...] = jnp.zeros_like(acc_ref)

            acc_ref[...] += partial_sum()

            @pl.when(k == nk - 1)
            def _():
                finish(acc_ref[...])

    out_shape = [jax.ShapeDtypeStruct((M, N), dt) for dt in out_dtypes]
    out_specs = [pl.BlockSpec((tm, tn), lambda i, j, k: (i, j)) for _ in out_dtypes]
    res = pl.pallas_call(
        body, name=name, grid=(M // tm, N // tn, nk), in_specs=in_specs, out_specs=out_specs, out_shape=out_shape,
        scratch_shapes=[pltpu.VMEM((tm, tn), F32)] if nk > 1 else [],
        compiler_params=_params("parallel", "parallel", "arbitrary"),
    )(*args)
    return res[0] if nout == 1 else res


def _col(arr, tm, width, cb):
    return pl.BlockSpec((tm, width), lambda i, cb=cb: (i, cb))


def _rms_fwd(x, g, *, out_dtype, name, mul=None, res=None, coeff=1.0, tm=512):
    T, D = x.shape
    tm = min(tm, T)
    args, in_specs = [x, g], [pl.BlockSpec((tm, D), lambda i: (i, 0)), pl.BlockSpec((1, D), lambda i: (0, 0))]
    if mul is not None:
        args.append(mul[0])
        in_specs.append(_col(mul[0], tm, D, mul[1]))
    if res is not None:
        args.append(res)
        in_specs.append(pl.BlockSpec((tm, D), lambda i: (i, 0)))

    def body(*refs):
        xv = refs[0][...].astype(F32)
        r = lax.rsqrt(jnp.mean(xv * xv, axis=-1, keepdims=True) + EPS)
        y = (xv * r) * refs[1][...]
        n = 2
        if mul is not None:
            y = y * _silu(refs[n][...])
            n += 1
        if res is not None:
            y = refs[n][...] + coeff * y
        refs[-1][...] = y.astype(out_dtype)

    return pl.pallas_call(
        body, name=name, grid=(T // tm,), in_specs=in_specs, out_specs=pl.BlockSpec((tm, D), lambda i: (i, 0)),
        out_shape=jax.ShapeDtypeStruct((T, D), out_dtype), compiler_params=_params("parallel"),
    )(*args)


def _fold8(v):
    tm, d = v.shape
    return v.reshape(tm // SUBLANE, SUBLANE, d).sum(axis=0)


def _rms_bwd(x, g, dy, *, name, coeff=1.0, add=None, mul=None, dx_dtype=F32, tm=512):
    T, D = x.shape
    tm = min(tm, T)
    row = pl.BlockSpec((tm, D), lambda i: (i, 0))
    args, in_specs = [x, g, dy], [row, pl.BlockSpec((1, D), lambda i: (0, 0)), row]
    if add is not None:
        args.append(add)
        in_specs.append(row)
    if mul is not None:
        args.append(mul[0])
        in_specs.append(_col(mul[0], tm, D, mul[1]))
    nin = len(args)

    def body(*refs):
        xv = refs[0][...].astype(F32)
        gv = refs[1][...]
        dyv = refs[2][...].astype(F32) * coeff
        r = lax.rsqrt(jnp.mean(xv * xv, axis=-1, keepdims=True) + EPS)
        nrm = xv * r
        n = 3
        addv = None
        if add is not None:
            addv = refs[n][...]
            n += 1
        if mul is not None:
            mv = refs[n][...]
            sm = _silu(mv)
            refs[nin + 2][...] = (dyv * nrm * gv * _dsilu(mv)).astype(refs[nin + 2].dtype)
            dyv = dyv * sm
        dn = dyv * gv
        dx = r * (dn - nrm * jnp.mean(dn * nrm, axis=-1, keepdims=True))
        if addv is not None:
            dx = dx + addv
        refs[nin][...] = dx.astype(dx_dtype)
        dg_ref = refs[nin + 1]

        @pl.when(pl.program_id(0) == 0)
        def _():
            dg_ref[...] = jnp.zeros_like(dg_ref)

        dg_ref[...] += _fold8(dyv * nrm)

    out_shape = [jax.ShapeDtypeStruct((T, D), dx_dtype), jax.ShapeDtypeStruct((SUBLANE, D), F32)]
    out_specs = [row, pl.BlockSpec((SUBLANE, D), lambda i: (0, 0))]
    if mul is not None:
        out_shape.append(jax.ShapeDtypeStruct((T, D), BF16))
        out_specs.append(row)
    return pl.pallas_call(
        body, name=name, grid=(T // tm,), in_specs=in_specs, out_specs=out_specs, out_shape=out_shape,
        compiler_params=_params("arbitrary"),
    )(*args)


def _ffn_in(h, wg, wu, *, name, tm=1024, tn=256):
    T, D = h.shape
    F = wg.shape[1]
    tm = min(tm, T)
    assert F % tn == 0

    def body(h_ref, wg_ref, wu_ref, a_ref, g_ref, u_ref):
        hv = h_ref[...]
        gt = _dot(hv, wg_ref[...], NN)
        up = _dot(hv, wu_ref[...], NN)
        a_ref[...] = (_silu(gt) * up).astype(BF16)
        g_ref[...] = gt.astype(BF16)
        u_ref[...] = up.astype(BF16)

    o = pl.BlockSpec((tm, tn), lambda i, j: (i, j))
    w = pl.BlockSpec((D, tn), lambda i, j: (0, j))
    return pl.pallas_call(
        body, name=name, grid=(T // tm, F // tn), in_specs=[pl.BlockSpec((tm, D), lambda i, j: (i, 0)), w, w],
        out_specs=[o, o, o], out_shape=[jax.ShapeDtypeStruct((T, F), BF16)] * 3,
        compiler_params=_params("parallel", "parallel"),
    )(h, wg, wu)


def _swiglu_bwd_epilogue(da, gt, up):
    gt = gt.astype(F32)
    up = up.astype(F32)
    return da * up * _dsilu(gt), da * _silu(gt)


GATE_CB = 7


def _gatemix_fwd(z, b_gate, ya, yb, *, name, tm=512):
    T, D = ya.shape
    tm = min(tm, T)
    row = pl.BlockSpec((tm, D), lambda i: (i, 0))

    def body(z0, z1, b0, b1, ya_ref, yb_ref, y_ref):
        g0 = _sigmoid(z0[...] + b0[...])
        g1 = _sigmoid(z1[...] + b1[...])
        y_ref[...] = (g0 * ya_ref[...] + g1 * yb_ref[...]).astype(y_ref.dtype)

    bs = lambda c: pl.BlockSpec((1, D), lambda i, c=c: (0, c))
    return pl.pallas_call(
        body, name=name, grid=(T // tm,),
        in_specs=[_col(z, tm, D, GATE_CB), _col(z, tm, D, GATE_CB + 1), bs(0), bs(1), row, row],
        out_specs=row, out_shape=jax.ShapeDtypeStruct((T, D), BF16), compiler_params=_params("parallel"),
    )(z, z, b_gate, b_gate, ya, yb)


def _gatemix_bwd(z, b_gate, ya, yb, dy, *, name, tm=512):
    T, D = ya.shape
    tm = min(tm, T)
    row = pl.BlockSpec((tm, D), lambda i: (i, 0))
    part = pl.BlockSpec((SUBLANE, D), lambda i: (0, 0))

    def body(z0, z1, b0, b1, ya_ref, yb_ref, dy_ref, dya, dyb, dz0, dz1, s0, s1):
        g0 = _sigmoid(z0[...] + b0[...])
        g1 = _sigmoid(z1[...] + b1[...])
        dyv = dy_ref[...]
        dya[...] = (dyv * g0).astype(BF16)
        dyb[...] = (dyv * g1).astype(BF16)
        d0 = dyv * ya_ref[...] * (g0 * (1.0 - g0))
        d1 = dyv * yb_ref[...] * (g1 * (1.0 - g1))
        dz0[...] = d0.astype(BF16)
        dz1[...] = d1.astype(BF16)

        @pl.when(pl.program_id(0) == 0)
        def _():
            s0[...] = jnp.zeros_like(s0)
            s1[...] = jnp.zeros_like(s1)

        s0[...] += _fold8(d0)
        s1[...] += _fold8(d1)

    bs = lambda c: pl.BlockSpec((1, D), lambda i, c=c: (0, c))
    act = jax.ShapeDtypeStruct((T, D), BF16)
    ps = jax.ShapeDtypeStruct((SUBLANE, D), F32)
    return pl.pallas_call(
        body, name=name, grid=(T // tm,),
        in_specs=[_col(z, tm, D, GATE_CB), _col(z, tm, D, GATE_CB + 1), bs(0), bs(1), row, row, row],
        out_specs=[row, row, row, row, part, part], out_shape=[act, act, act, act, ps, ps],
        compiler_params=_params("arbitrary"),
    )(z, z, b_gate, b_gate, ya, yb, dy)


def _loss_head(x, target, *, name, tm=512):
    T, D = x.shape
    tm = min(tm, T)
    row = pl.BlockSpec((tm, D), lambda i: (i, 0))

    def body(x_ref, t_ref, dx_ref, s_ref):
        e = x_ref[...] - t_ref[...]
        dx_ref[...] = e * (1.0 / D)

        @pl.when(pl.program_id(0) == 0)
        def _():
            s_ref[...] = jnp.zeros_like(s_ref)

        s_ref[...] += _fold8(e * e)

    return pl.pallas_call(
        body, name=name, grid=(T // tm,), in_specs=[row, row],
        out_specs=[row, pl.BlockSpec((SUBLANE, D), lambda i: (0, 0))],
        out_shape=[jax.ShapeDtypeStruct((T, D), F32), jax.ShapeDtypeStruct((SUBLANE, D), F32)],
        compiler_params=_params("arbitrary"),
    )(x, target)


def _tri(n, reverse):
    r = lax.broadcasted_iota(jnp.int32, (n, n), 0)
    c = lax.broadcasted_iota(jnp.int32, (n, n), 1)
    return jnp.where((c >= r) if reverse else (c <= r), 1.0, 0.0).astype(F32)


def _cumsum_t(xs, *, name, width, pre, reverse=False, rows=(), post=None, out_dtypes=(F32,), fold=None, tb=256):
    T = xs[0][0].shape[0]
    tb = min(tb, T)
    nb = T // tb
    tblk = (lambda i: nb - 1 - i) if reverse else (lambda i: i)
    args = [a for a, _ in xs] + [a for a, _ in rows]
    in_specs = [pl.BlockSpec((tb, width), lambda i, cb=cb: (tblk(i), cb)) for _, cb in xs]
    in_specs += [pl.BlockSpec((1, width), lambda i, cb=cb: (0, cb)) for _, cb in rows]
    nin, nout = len(args), len(out_dtypes)

    def body(*refs):
        vals = [r[...] for r in refs[:nin]]
        outs = refs[nin:nin + nout]
        carry = refs[-1]
        first = pl.program_id(0) == 0

        @pl.when(first)
        def _():
            carry[...] = jnp.zeros_like(carry)

        cum = jnp.dot(_tri(tb, reverse), pre(*vals), precision=HIGHEST, preferred_element_type=F32) + carry[...]
        carry[...] = cum[0:1, :] if reverse else cum[tb - 1:tb, :]
        res = (cum,) if post is None else post(cum, *vals)
        for o, r in zip(outs, res):
            o[...] = r.astype(o.dtype)
        if fold is not None:
            f_ref = refs[nin + nout]

            @pl.when(first)
            def _():
                f_ref[...] = jnp.zeros_like(f_ref)

            f_ref[...] += _fold8(fold(cum, *vals))

    tspec = pl.BlockSpec((tb, width), lambda i: (tblk(i), 0))
    out_shape = [jax.ShapeDtypeStruct((T, width), dt) for dt in out_dtypes]
    out_specs = [tspec] * nout
    if fold is not None:
        out_shape.append(jax.ShapeDtypeStruct((SUBLANE, width), F32))
        out_specs.append(pl.BlockSpec((SUBLANE, width), lambda i: (0, 0)))
    res = pl.pallas_call(
        body, name=name, grid=(nb,), in_specs=in_specs, out_specs=out_specs, out_shape=out_shape,
        scratch_shapes=[pltpu.VMEM((1, width), F32)], compiler_params=_params("arbitrary"),
    )(*args)
    return res[0] if len(res) == 1 else res


def _logsigmoid(v):
    return jnp.minimum(v, 0.0) - jnp.log(1.0 + jnp.exp(-jnp.abs(v)))


HG_TB = 256
HG_HB = 4
HG_W = HG_HB * DH
HG_GROUPS = HEADS // HG_HB
HG_Q_CB, HG_F_CB, HG_I_CB = 0, HG_GROUPS, 2 * HG_GROUPS
NEG = -1e30


def _scan16(x, rowid, reverse=False):
    for k in (1, 2, 4, 8):
        if reverse:
            x = x + jnp.where(rowid < SUB - k, pltpu.roll(x, SUB - k, 0), 0.0)
        else:
            x = x + jnp.where(rowid >= k, pltpu.roll(x, k, 0), 0.0)
    return x


def _hg_block(q_ref, f_ref, i_ref, lb_ref, rows, cols, rowid):
    lb = lb_ref[:, cols]
    qr = q_ref[rows, cols]
    sg = _sigmoid(f_ref[rows, cols])
    f = lb + (1.0 - lb) * sg
    b = _scan16(jnp.log(f), rowid)
    return _silu(qr), 1.0 - f, i_ref[rows, cols], b, qr, sg, f, lb


def _hg_specs(tb, tmap):
    return [pl.BlockSpec((tb, HG_W), lambda g, t: (tmap(t), HG_Q_CB + g)),
            pl.BlockSpec((tb, HG_W), lambda g, t: (tmap(t), HG_F_CB + g)),
            pl.BlockSpec((tb, HG_W), lambda g, t: (tmap(t), HG_I_CB + g)),
            pl.BlockSpec((1, HG_W), lambda g, t: (0, g))]


def _hgrn2_fwd(z, lb_row, *, name):
    T = z.shape[0]
    tb = min(HG_TB, T)
    nb, nsub = T // tb, tb // SUB

    def body(q_ref, f_ref, i_ref, lb_ref, o_ref, st_ref, state):
        @pl.when(pl.program_id(1) == 0)
        def _():
            state[...] = jnp.zeros_like(state)

        rowid = lax.broadcasted_iota(jnp.int32, (SUB, DH), 0)

        def step(c, carry):
            rows = pl.ds(pl.multiple_of(c * SUB, SUB), SUB)
            for hh in range(HG_HB):
                cols = slice(hh * DH, (hh + 1) * DH)
                q, k, iv, b = _hg_block(q_ref, f_ref, i_ref, lb_ref, rows, cols, rowid)[:4]
                bl = b[SUB - 1:SUB, :]
                sv = state[hh]
                st_ref[c, hh] = sv
                o = _dot(q * jnp.exp(b), sv, NT)
                for s in range(SUB):
                    e = jnp.exp(jnp.where(rowid >= s, b - b[s:s + 1, :], NEG))
                    a = jnp.sum(q * e * k[s:s + 1, :], axis=-1, keepdims=True)
                    o = o + a * iv[s:s + 1, :]
                o_ref[rows, cols] = o
                state[hh] = sv * jnp.exp(bl) + _dot(iv, k * jnp.exp(bl - b), TN)
            return carry

        lax.fori_loop(0, nsub, step, 0)

    return pl.pallas_call(
        body, name=name, grid=(HG_GROUPS, nb), in_specs=_hg_specs(tb, lambda t: t),
        out_specs=[pl.BlockSpec((tb, HG_W), lambda g, t: (t, g)),
                   pl.BlockSpec((nsub, HG_HB, DH, DH), lambda g, t: (t, g, 0, 0))],
        out_shape=[jax.ShapeDtypeStruct((T, D_MODEL), F32), jax.ShapeDtypeStruct((T // SUB, HEADS, DH, DH), F32)],
        scratch_shapes=[pltpu.VMEM((HG_HB, DH, DH), F32)], compiler_params=_params("parallel", "arbitrary"),
    )(z, z, z, lb_row)


def _hgrn2_bwd(z, lb_row, states, do, *, name):
    T = z.shape[0]
    tb = min(HG_TB, T)
    nb, nsub = T // tb, tb // SUB
    rev = lambda t: nb - 1 - t

    def body(q_ref, f_ref, i_ref, lb_ref, st_ref, do_ref, dq_ref, dfl_ref, di_ref, dlb_ref, dstate, later):
        @pl.when(pl.program_id(1) == 0)
        def _():
            dstate[...] = jnp.zeros_like(dstate)
            later[...] = jnp.zeros_like(later)
            dlb_ref[...] = jnp.zeros_like(dlb_ref)

        rowid = lax.broadcasted_iota(jnp.int32, (SUB, DH), 0)

        def step(cc, carry):
            c = nsub - 1 - cc
            rows = pl.ds(pl.multiple_of(c * SUB, SUB), SUB)
            for hh in range(HG_HB):
                cols = slice(hh * DH, (hh + 1) * DH)
                q, k, iv, b, qr, sg, f, lb = _hg_block(q_ref, f_ref, i_ref, lb_ref, rows, cols, rowid)
                bl = b[SUB - 1:SUB, :]
                eb, ebl = jnp.exp(b), jnp.exp(bl - b)
                sv, dsv = st_ref[c, hh], dstate[hh]
                dov = do_ref[rows, cols]
                dq = _dot(dov, sv, NN) * eb
                dk = _dot(iv, dsv, NN) * ebl
                di = _dot(k * ebl, dsv, NT)
                for s in range(SUB):
                    e = jnp.exp(jnp.where(rowid >= s, b - b[s:s + 1, :], NEG))
                    ks, isv = k[s:s + 1, :], iv[s:s + 1, :]
                    qe = q * e
                    a = jnp.sum(qe * ks, axis=-1, keepdims=True)
                    p = jnp.sum(dov * isv, axis=-1, keepdims=True)
                    dq = dq + p * (e * ks)
                    dks = jnp.sum(p * qe, axis=0, keepdims=True)
                    dis = jnp.sum(a * dov, axis=0, keepdims=True)
                    dk = dk + jnp.where(rowid == s, dks, 0.0)
                    di = di + jnp.where(rowid == s, dis, 0.0)
                dlogf = _scan16(q * dq - k * dk, rowid, reverse=True) + later[hh]
                df = dlogf / f - dk
                dlb_ref[:, cols] += jnp.sum(df * (1.0 - sg), axis=0, keepdims=True)
                dfl_ref[rows, cols] = (df * (1.0 - lb) * (sg * (1.0 - sg))).astype(BF16)
                dq_ref[rows, cols] = (dq * _dsilu(qr)).astype(BF16)
                di_ref[rows, cols] = di.astype(BF16)
                dnew = dsv * jnp.exp(bl) + _dot(dov, q * eb, TN)
                dstate[hh] = dnew
                later[hh] = jnp.sum(dnew * sv, axis=0, keepdims=True)
            return carry

        lax.fori_loop(0, nsub, step, 0)

    tile = pl.BlockSpec((tb, HG_W), lambda g, t: (rev(t), g))
    act = jax.ShapeDtypeStruct((T, D_MODEL), BF16)
    return pl.pallas_call(
        body, name=name, grid=(HG_GROUPS, nb),
        in_specs=_hg_specs(tb, rev) + [pl.BlockSpec((nsub, HG_HB, DH, DH), lambda g, t: (rev(t), g, 0, 0)), tile],
        out_specs=[tile, tile, tile, pl.BlockSpec((1, HG_W), lambda g, t: (0, g))],
        out_shape=[act, act, act, jax.ShapeDtypeStruct((1, D_MODEL), F32)],
        scratch_shapes=[pltpu.VMEM((HG_HB, DH, DH), F32), pltpu.VMEM((HG_HB, 1, DH), F32)],
        compiler_params=_params("parallel", "arbitrary"),
    )(z, z, z, lb_row, states, do)


FOX_Q_CB, FOX_K_CB, FOX_V_CB = 4 * HEADS, 5 * HEADS, 6 * HEADS
FOX_SCALE = 1.0 / math.sqrt(DH)


def _fox_tile(T):
    return 512 if T >= 2048 else 128


def _fox_pairs(nq, by_query):
    if by_query:
        pairs = [(i, j) for i in range(nq) for j in range(i + 1)]
    else:
        pairs = [(i, j) for j in range(nq) for i in range(j, nq)]
    return (jnp.asarray([p[0] for p in pairs], jnp.int32), jnp.asarray([p[1] for p in pairs], jnp.int32))


def _fox_scores(q, k, cc, cr, i, j, tq, masked):
    s = _dot(q, k, NT) * FOX_SCALE + (cc - cr)
    if masked:
        rpos = i * tq + lax.broadcasted_iota(jnp.int32, (tq, tq), 0)
        cpos = j * tq + lax.broadcasted_iota(jnp.int32, (tq, tq), 1)
        s = jnp.where(cpos <= rpos, s, NEG)
    return s


def _fox_fwd(z, c_col, c_row, *, name):
    T = z.shape[0]
    tq = _fox_tile(T)
    nq = T // tq

    qi, kj = _fox_pairs(nq, by_query=True)

    def body(qi_ref, kj_ref, q_ref, k_ref, v_ref, cc_ref, cr_ref, o_ref, lse_ref, m_scr, l_scr, acc):
        p_id = pl.program_id(1)
        i, j = qi_ref[p_id], kj_ref[p_id]

        @pl.when(j == 0)
        def _():
            m_scr[...] = jnp.full_like(m_scr, NEG)
            l_scr[...] = jnp.zeros_like(l_scr)
            acc[...] = jnp.zeros_like(acc)

        def update(masked):
            s = _fox_scores(q_ref[...], k_ref[...], cc_ref[...], cr_ref[...], i, j, tq, masked)
            m_old = m_scr[...]
            m_new = jnp.maximum(m_old, jnp.max(s, axis=-1, keepdims=True))
            alpha = jnp.exp(m_old - m_new)
            p = jnp.exp(s - m_new)
            l_scr[...] = alpha * l_scr[...] + jnp.sum(p, axis=-1, keepdims=True)
            acc[...] = alpha * acc[...] + _dot(p, v_ref[...], NN)
            m_scr[...] = m_new

        @pl.when(j < i)
        def _():
            update(False)

        @pl.when(j == i)
        def _():
            update(True)
            o_ref[...] = acc[...] / l_scr[...]
            lse_ref[...] = m_scr[...] + jnp.log(l_scr[...])

    qtile = lambda cb: pl.BlockSpec((tq, DH), lambda h, p, qi, kj, cb=cb: (qi[p], cb + h))
    ktile = lambda cb: pl.BlockSpec((tq, DH), lambda h, p, qi, kj, cb=cb: (kj[p], cb + h))
    qcol = pl.BlockSpec((None, tq, 1), lambda h, p, qi, kj: (h, qi[p], 0))
    return pl.pallas_call(
        body, name=name,
        grid_spec=pltpu.PrefetchScalarGridSpec(
            num_scalar_prefetch=2, grid=(HEADS, qi.shape[0]),
            in_specs=[qtile(FOX_Q_CB), ktile(FOX_K_CB), ktile(FOX_V_CB), qcol,
                      pl.BlockSpec((None, 1, tq), lambda h, p, qi, kj: (h, 0, kj[p]))],
            out_specs=[qtile(0), qcol],
            scratch_shapes=[pltpu.VMEM((tq, 1), F32), pltpu.VMEM((tq, 1), F32), pltpu.VMEM((tq, DH), F32)]),
        out_shape=[jax.ShapeDtypeStruct((T, D_MODEL), F32), jax.ShapeDtypeStruct((HEADS, T, 1), F32)],
        compiler_params=_params("parallel", "arbitrary"),
    )(qi, kj, z, z, z, c_col, c_row)


def _fox_bwd(z, c_col, c_row, o, lse, do, *, name):
    T = z.shape[0]
    tq = _fox_tile(T)
    nq = T // tq

    qi, kj = _fox_pairs(nq, by_query=False)

    def body(qi_ref, kj_ref, q_ref, k_ref, v_ref, cc_ref, cr_ref, o_ref, lse_ref, do_ref, dq_ref, dk_ref, dv_ref,
             dc_ref, dcq_ref, dk_acc, dv_acc, dc_acc):
        p_id = pl.program_id(1)
        i, j = qi_ref[p_id], kj_ref[p_id]

        @pl.when(p_id == 0)
        def _():
            dq_ref[...] = jnp.zeros_like(dq_ref)
            dcq_ref[...] = jnp.zeros_like(dcq_ref)

        def update(masked):
            q, k, dov = q_ref[...], k_ref[...], do_ref[...]
            s = _fox_scores(q, k, cc_ref[...], cr_ref[...], i, j, tq, masked)
            p = jnp.exp(s - lse_ref[...])
            dp = _dot(dov, v_ref[...], NT)
            delta = jnp.sum(dov * o_ref[...], axis=-1, keepdims=True)
            ds = p * (dp - delta)
            dv_new = _dot(p, dov, TN)
            dk_new = _dot(ds, q, TN) * FOX_SCALE
            dc_new = -jnp.sum(ds, axis=0, keepdims=True)
            rows = pl.ds(pl.multiple_of(i * tq, tq), tq)
            dq_ref[rows, :] += _dot(ds, k, NN) * FOX_SCALE
            dcq_ref[rows, :] += jnp.sum(ds, axis=-1, keepdims=True)
            return dk_new, dv_new, dc_new

        @pl.when(i == j)
        def _():
            dk_new, dv_new, dc_new = update(True)
            dk_acc[...] = dk_new
            dv_acc[...] = dv_new
            dc_acc[...] = dc_new

        @pl.when(i > j)
        def _():
            dk_new, dv_new, dc_new = update(False)
            dk_acc[...] += dk_new
            dv_acc[...] += dv_new
            dc_acc[...] += dc_new

        @pl.when(i == nq - 1)
        def _():
            dk_ref[...] = dk_acc[...].astype(BF16)
            dv_ref[...] = dv_acc[...].astype(BF16)
            dc_ref[...] = dc_acc[...]

    qtile = lambda cb: pl.BlockSpec((tq, DH), lambda h, p, qi, kj, cb=cb: (qi[p], cb + h))
    ktile = lambda cb: pl.BlockSpec((tq, DH), lambda h, p, qi, kj, cb=cb: (kj[p], cb + h))
    qcol = pl.BlockSpec((None, tq, 1), lambda h, p, qi, kj: (h, qi[p], 0))
    krow = pl.BlockSpec((None, 1, tq), lambda h, p, qi, kj: (h, 0, kj[p]))
    return pl.pallas_call(
        body, name=name,
        grid_spec=pltpu.PrefetchScalarGridSpec(
            num_scalar_prefetch=2, grid=(HEADS, qi.shape[0]),
            in_specs=[qtile(FOX_Q_CB), ktile(FOX_K_CB), ktile(FOX_V_CB), qcol, krow, qtile(0), qcol, qtile(0)],
            out_specs=[pl.BlockSpec((T, DH), lambda h, p, qi, kj: (0, h)), ktile(0), ktile(0), krow,
                       pl.BlockSpec((None, T, 1), lambda h, p, qi, kj: (h, 0, 0))],
            scratch_shapes=[pltpu.VMEM((tq, DH), F32), pltpu.VMEM((tq, DH), F32), pltpu.VMEM((1, tq), F32)]),
        out_shape=[jax.ShapeDtypeStruct((T, D_MODEL), F32), jax.ShapeDtypeStruct((T, D_MODEL), BF16),
                   jax.ShapeDtypeStruct((T, D_MODEL), BF16), jax.ShapeDtypeStruct((HEADS, 1, T), F32),
                   jax.ShapeDtypeStruct((HEADS, T, 1), F32)],
        compiler_params=_params("parallel", "arbitrary"),
    )(qi, kj, z, z, z, c_col, c_row, o, lse, do)


MEM_SCALE = 1.0 / math.sqrt(MEM_DH)


def _mem_probs(qh, kh):
    s = _dot(qh, kh, NT) * MEM_SCALE
    p = jnp.exp(s - jnp.max(s, axis=-1, keepdims=True))
    return p / jnp.sum(p, axis=-1, keepdims=True)


def _mem_fwd(q, kv, *, name, tq=512):
    T = q.shape[0]
    tq = min(tq, T)

    def body(q_ref, kv_ref, o_ref):
        for h in range(MEM_HEADS):
            cols = slice(h * MEM_DH, (h + 1) * MEM_DH)
            vcols = slice(D_MODEL + h * MEM_DH, D_MODEL + (h + 1) * MEM_DH)
            p = _mem_probs(q_ref[:, cols], kv_ref[:, cols])
            o_ref[:, cols] = _dot(p, kv_ref[:, vcols], NN).astype(o_ref.dtype)

    return pl.pallas_call(
        body, name=name, grid=(T // tq,),
        in_specs=[pl.BlockSpec((tq, D_MODEL), lambda i: (i, 0)), pl.BlockSpec((MEM_LEN, 2 * D_MODEL), lambda i: (0, 0))],
        out_specs=pl.BlockSpec((tq, D_MODEL), lambda i: (i, 0)), out_shape=jax.ShapeDtypeStruct((T, D_MODEL), BF16),
        compiler_params=_params("parallel"),
    )(q, kv)


def _mem_bwd(q, kv, do, *, name, tq=512):
    T = q.shape[0]
    tq = min(tq, T)

    def body(q_ref, kv_ref, do_ref, dq_ref, dkv_ref):
        @pl.when(pl.program_id(0) == 0)
        def _():
            dkv_ref[...] = jnp.zeros_like(dkv_ref)

        for h in range(MEM_HEADS):
            cols = slice(h * MEM_DH, (h + 1) * MEM_DH)
            vcols = slice(D_MODEL + h * MEM_DH, D_MODEL + (h + 1) * MEM_DH)
            qh, kh, doh = q_ref[:, cols], kv_ref[:, cols], do_ref[:, cols]
            p = _mem_probs(qh, kh)
            dp = _dot(doh, kv_ref[:, vcols], NT)
            ds = p * (dp - jnp.sum(p * dp, axis=-1, keepdims=True))
            dq_ref[:, cols] = (_dot(ds, kh, NN) * MEM_SCALE).astype(dq_ref.dtype)
            dkv_ref[:, cols] += _dot(ds, qh, TN) * MEM_SCALE
            dkv_ref[:, vcols] += _dot(p, doh, TN)

    row = pl.BlockSpec((tq, D_MODEL), lambda i: (i, 0))
    full = pl.BlockSpec((MEM_LEN, 2 * D_MODEL), lambda i: (0, 0))
    return pl.pallas_call(
        body, name=name, grid=(T // tq,), in_specs=[row, full, row], out_specs=[row, full],
        out_shape=[jax.ShapeDtypeStruct((T, D_MODEL), BF16), jax.ShapeDtypeStruct((MEM_LEN, 2 * D_MODEL), F32)],
        compiler_params=_params("arbitrary"),
    )(q, kv, do)


def _adamw(w, g, m, v, *, name, tm=256):
    R, C = w.shape
    tm = min(tm, R)
    assert R % tm == 0
    nsum = g.shape[0] if g.ndim == 3 else 0

    def body(w_ref, g_ref, m_ref, v_ref, go_ref, d_ref, mo_ref, vo_ref):
        if nsum:
            gv = g_ref[0]
            for n in range(1, nsum):
                gv = gv + g_ref[n]
        else:
            gv = g_ref[...]
        mv = ADAM_B1 * m_ref[...] + (1.0 - ADAM_B1) * gv
        vv = ADAM_B2 * v_ref[...] + (1.0 - ADAM_B2) * jnp.square(gv)
        m_hat = mv / (1.0 - ADAM_B1 ** ADAM_STEP)
        v_hat = vv / (1.0 - ADAM_B2 ** ADAM_STEP)
        d_ref[...] = -ADAM_LR * (m_hat / (jnp.sqrt(v_hat) + ADAM_EPS) + ADAM_WD * w_ref[...])
        go_ref[...] = gv
        mo_ref[...] = mv
        vo_ref[...] = vv

    row = pl.BlockSpec((tm, C), lambda i: (i, 0))
    gspec = pl.BlockSpec((nsum, tm, C), lambda i: (0, i, 0)) if nsum else row
    return pl.pallas_call(
        body, name=name, grid=(R // tm,), in_specs=[row, gspec, row, row], out_specs=[row] * 4,
        out_shape=[jax.ShapeDtypeStruct((R, C), F32)] * 4, compiler_params=_params("parallel"),
    )(w, g, m, v)


def _act_mm(a, w, name, out_dtype=F32):
    return _mm([(a, w)], "nn", tm=1024, tn=512, tk=w.shape[0], out_dtypes=[out_dtype], name=name)


def _act_mm_t(a, w, name, out_dtype=F32):
    return _mm([(a, w)], "nt", tm=1024, tn=512, tk=1024, out_dtypes=[out_dtype], name=name)


def _wgrad(a, dy, name, tm=1024):
    tn = D_MODEL if dy.shape[1] % D_MODEL == 0 else D_FF // 2
    return _mm([(a, dy)], "tn", tm=tm, tn=tn, tk=1024, out_dtypes=[F32], name=name)


def _colsum8(p):
    return jnp.sum(p, axis=0, keepdims=True)


def _ffn_fwd(x, pre_g, post_g, wg, wu, wd, tag):
    h = _rms_fwd(x, pre_g, out_dtype=BF16, name=tag + "_pre")
    act, gate, up = _ffn_in(h, wg, wu, name=tag + "_in")
    d = _mm([(act, wd)], "nn", tm=1024, tn=512, tk=D_FF, out_dtypes=[F32], name=tag + "_down")
    xo = _rms_fwd(d, post_g, out_dtype=F32, name=tag + "_post", res=x, coeff=0.5)
    return xo, (h, act, gate, up, d)


def _ffn_bwd(x, dxo, saved, pre_g, post_g, wg, wu, wd, tag):
    h, act, gate, up, d = saved
    dd, dg_post = _rms_bwd(d, post_g, dxo, name=tag + "_post_b", coeff=0.5, dx_dtype=BF16)
    dgate, dup = _mm([(dd, wd)], "nt", tm=1024, tn=256, tk=D_MODEL, out_dtypes=[BF16, BF16], name=tag + "_act_b",
                     epilogue=_swiglu_bwd_epilogue, tiles=(gate, up))
    dwd = _wgrad(act, dd, tag + "_dwd", tm=D_FF // 2)
    dh = _mm([(dgate, wg), (dup, wu)], "nt", tm=512, tn=512, tk=D_FF, out_dtypes=[F32], name=tag + "_in_b")
    dwg = _wgrad(h, dgate, tag + "_dwg")
    dwu = _wgrad(h, dup, tag + "_dwu")
    dx, dg_pre = _rms_bwd(x, pre_g, dh, name=tag + "_pre_b", add=dxo)
    return dx, dict(pre_g=_colsum8(dg_pre), post_g=_colsum8(dg_post), wg=dwg, wu=dwu, wd=dwd)


def _local_step(x, mem, target, W, P):
    T = x.shape[0]
    G = {}
    logits = P["hg_lb_logits"]
    lb = _sigmoid(logits[0] - logits[1])
    lb_row = lb.reshape(1, D_MODEL)
    fbias_row = jnp.pad(P["fox_f_bias"], ((0, 0), (0, LANE - HEADS)))

    x1, ffn1_saved = _ffn_fwd(x, P["ffn1_pre_g"], P["ffn1_post_g"], W["f1g"], W["f1u"], W["f1d"], "ffn1")
    h2 = _rms_fwd(x1, P["mix_pre_g"], out_dtype=BF16, name="mix_pre")
    z = _act_mm(h2, W["w_main"], "mix_in")
    zfb = _mm([(h2, W["w_fb"])], "nn", tm=1024, tn=LANE, tk=D_MODEL, out_dtypes=[F32], name="mix_in_fb")
    oa_pre, states = _hgrn2_fwd(z, lb_row, name="hgrn2_f")
    o_a = _rms_fwd(oa_pre, P["hg_norm_g"], out_dtype=BF16, name="hgrn2_post", mul=(z, 3))
    y_a = _act_mm(o_a, W["wa"], "branch_a")
    c = _cumsum_t([(zfb, 0)], name="fox_c", width=LANE, rows=[(fbias_row, 0)], pre=lambda v, r: _logsigmoid(v + r))
    ct = c[:, :HEADS].T
    c_col, c_row = ct[:, :, None], ct[:, None, :]
    o_b, lse = _fox_fwd(z, c_col, c_row, name="fox_f")
    y_b = _act_mm(o_b, W["wb"], "branch_b")
    y = _gatemix_fwd(z, P["b_gate"], y_a, y_b, name="gatemix")
    m = _act_mm(y, W["wo"], "mix_out")
    x2 = _rms_fwd(m, P["mix_post_g"], out_dtype=F32, name="mix_post", res=x1)
    h3 = _rms_fwd(x2, P["mem_pre_g"], out_dtype=BF16, name="mem_pre")
    mem_n = _rms_fwd(mem, P["mem_kv_g"], out_dtype=BF16, name="mem_kvn")
    qm = _act_mm(h3, W["wmq"], "mem_q")
    kv = _act_mm(mem_n, W["wmkv"], "mem_kv")
    om = _mem_fwd(qm, kv, name="mem_attn")
    mo = _act_mm(om, W["wmo"], "mem_o")
    x3 = _rms_fwd(mo, P["mem_post_g"], out_dtype=F32, name="mem_post", res=x2)
    x4, ffn2_saved = _ffn_fwd(x3, P["ffn2_pre_g"], P["ffn2_post_g"], W["f2g"], W["f2u"], W["f2d"], "ffn2")
    dx4, sq = _loss_head(x4, target, name="loss_head")

    dx3, g = _ffn_bwd(x3, dx4, ffn2_saved, P["ffn2_pre_g"], P["ffn2_post_g"], W["f2g"], W["f2u"], W["f2d"], "ffn2")
    G.update(ffn2_pre_g=g["pre_g"], ffn2_post_g=g["post_g"], f2g=g["wg"], f2u=g["wu"], f2d=g["wd"])

    dmo, dgp = _rms_bwd(mo, P["mem_post_g"], dx3, name="mem_post_b", dx_dtype=BF16)
    G["mem_post_g"] = _colsum8(dgp)
    dom = _act_mm_t(dmo, W["wmo"], "mem_o_b", BF16)
    G["wmo"] = _wgrad(om, dmo, "mem_o_w")
    dqm, dkv = _mem_bwd(qm, kv, dom, name="mem_attn_b")
    dh3 = _act_mm_t(dqm, W["wmq"], "mem_q_b")
    G["wmq"] = _wgrad(h3, dqm, "mem_q_w")
    G["wmkv"] = _mm([(mem_n, dkv)], "tn", tm=1024, tn=512, tk=MEM_LEN, out_dtypes=[F32], name="mem_kv_w")
    dmem_n = _mm([(dkv, W["wmkv"])], "nt", tm=MEM_LEN, tn=512, tk=2 * D_MODEL, out_dtypes=[F32], name="mem_kv_b")
    _, dgp = _rms_bwd(mem, P["mem_kv_g"], dmem_n, name="mem_kvn_b")
    G["mem_kv_g"] = _colsum8(dgp)
    dx2, dgp = _rms_bwd(x2, P["mem_pre_g"], dh3, name="mem_pre_b", add=dx3)
    G["mem_pre_g"] = _colsum8(dgp)

    dm, dgp = _rms_bwd(m, P["mix_post_g"], dx2, name="mix_post_b", dx_dtype=BF16)
    G["mix_post_g"] = _colsum8(dgp)
    dy = _act_mm_t(dm, W["wo"], "mix_out_b")
    G["wo"] = _wgrad(y, dm, "mix_out_w")
    dya, dyb, dz0, dz1, s0, s1 = _gatemix_bwd(z, P["b_gate"], y_a, y_b, dy, name="gatemix_b")
    G["b_gate"] = jnp.concatenate([_colsum8(s0), _colsum8(s1)], axis=1)
    do_a = _act_mm_t(dya, W["wa"], "branch_a_b")
    G["wa"] = _wgrad(o_a, dya, "branch_a_w")
    do_b = _act_mm_t(dyb, W["wb"], "branch_b_b")
    G["wb"] = _wgrad(o_b, dyb, "branch_b_w")
    doa_pre, dgp, dga = _rms_bwd(oa_pre, P["hg_norm_g"], do_a, name="hgrn2_post_b", mul=(z, 3))
    G["hg_norm_g"] = _colsum8(dgp)
    dq_a, dfl_a, di_a, dlb = _hgrn2_bwd(z, lb_row, states, doa_pre, name="hgrn2_b")
    dl0 = (dlb * lb_row * (1.0 - lb_row)).reshape(1, HEADS, DH)
    G["hg_lb_logits"] = jnp.concatenate([dl0, -dl0], axis=0)
    dq_b, dk_b, dv_b, dcr, dcq = _fox_bwd(z, c_col, c_row, o_b, lse, do_b, name="fox_b")
    dc_pad = jnp.pad((dcr[:, 0, :] + dcq[:, :, 0]).T, ((0, 0), (0, LANE - HEADS)))
    gate_b = lambda cum, dc, zf, r: cum * _sigmoid(-(zf + r))
    dfl_b, dfb = _cumsum_t([(dc_pad, 0), (zfb, 0)], name="fox_c_b", width=LANE, reverse=True, rows=[(fbias_row, 0)],
                           pre=lambda dc, zf, r: dc, post=lambda *a: (gate_b(*a),), fold=gate_b)
    G["fox_f_bias"] = _colsum8(dfb)[:, :HEADS]

    pieces = [dq_a, dfl_a, di_a, dga, dq_b, dk_b, dv_b, dz0, dz1]
    dh2 = _mm([(dfl_b, W["w_fb"])], "nt", tm=512, tn=D_MODEL, tk=LANE, out_dtypes=[F32], name="mix_in_fb_b")
    for lo, hi in ((0, 5), (5, 9)):
        dh2 = _mm([(p, W["w_main"]) for p in pieces[lo:hi]], "nt", tm=512, tn=D_MODEL, tk=D_MODEL, out_dtypes=[F32],
                  name=f"mix_in_b{lo}", b_koff=[n * D_MODEL for n in range(lo, hi)],
                  epilogue=lambda acc, t: (acc + t,), tiles=(dh2,))
    G["w_main"] = [_wgrad(h2, p, f"mix_in_w{n}") for n, p in enumerate(pieces)]
    G["w_fb"] = _mm([(h2, dfl_b)], "tn", tm=1024, tn=LANE, tk=512, out_dtypes=[F32], name="mix_in_fb_w")
    dx1, dgp = _rms_bwd(x1, P["mix_pre_g"], dh2, name="mix_pre_b", add=dx2)
    G["mix_pre_g"] = _colsum8(dgp)

    dx0, g = _ffn_bwd(x, dx1, ffn1_saved, P["ffn1_pre_g"], P["ffn1_post_g"], W["f1g"], W["f1u"], W["f1d"], "ffn1")
    G.update(ffn1_pre_g=g["pre_g"], ffn1_post_g=g["post_g"], f1g=g["wg"], f1u=g["wu"], f1d=g["wd"])
    return sq, dx0, G


N_CHIP = 4
N_DEV = 8
IN_COLS = 9224
FB_COL = 7 * D_MODEL
SHARDED = (
    ("ffn1_w_in", (D_MODEL, 2 * D_FF), 1), ("ffn1_w_down", (D_FF, D_MODEL), 0), ("w_in", (D_MODEL, IN_COLS), 1),
    ("w_branch_a", (D_MODEL, D_MODEL), 0), ("w_branch_b", (D_MODEL, D_MODEL), 0), ("w_out", (D_MODEL, D_MODEL), 0),
    ("w_mq", (D_MODEL, D_MODEL), 0), ("w_mkv", (D_MODEL, 2 * D_MODEL), 1), ("w_mo", (D_MODEL, D_MODEL), 0),
    ("ffn2_w_in", (D_MODEL, 2 * D_FF), 1), ("ffn2_w_down", (D_FF, D_MODEL), 0),
)
SMALL = ("ffn1_pre_g", "ffn1_post_g", "mix_pre_g", "hg_norm_g", "mix_post_g", "mem_pre_g", "mem_kv_g", "mem_post_g",
         "ffn2_pre_g", "ffn2_post_g", "b_gate", "hg_lb_logits", "fox_f_bias")
SMALL_SHAPES = dict(b_gate=(1, 2 * D_MODEL), hg_lb_logits=(2, HEADS, DH), fox_f_bias=(1, HEADS))
SMALL_ROWS = 16
WEIGHT_ORDER = ("ffn1_pre_g", "ffn1_w_in", "ffn1_w_down", "ffn1_post_g", "mix_pre_g", "w_in", "hg_lb_logits", "hg_norm_g",
                "fox_f_bias", "w_branch_a", "w_branch_b", "b_gate", "w_out", "mix_post_g", "mem_pre_g", "mem_kv_g", "w_mq",
                "w_mkv", "w_mo", "mem_post_g", "ffn2_pre_g", "ffn2_w_in", "ffn2_w_down", "ffn2_post_g")


def _shard_shape(shape, axis):
    s = list(shape)
    s[axis] //= N_CHIP
    return tuple(s)


PACK_USED = sum(math.prod(shape) // N_CHIP // D_MODEL for _, shape, _ in SHARDED)
PACK_ROWS = -(-PACK_USED // 256) * 256
HALF = PACK_ROWS // 2


def _pack(shards, dtype):
    rows = [shards[name].astype(dtype).reshape(-1, D_MODEL) for name, _, _ in SHARDED]
    rows.append(jnp.zeros((PACK_ROWS - PACK_USED, D_MODEL), dtype))
    return jnp.concatenate(rows, axis=0)


def _unpack(slab):
    out, r = {}, 0
    for name, shape, axis in SHARDED:
        ss = _shard_shape(shape, axis)
        n = math.prod(ss) // D_MODEL
        out[name] = slab[r:r + n].reshape(ss)
        r += n
    return out


def _pack_small(vals):
    rows = []
    for name in SMALL:
        v = vals[name].astype(F32).reshape(-1)
        rows.append(jnp.pad(v, (0, -v.shape[0] % D_MODEL)).reshape(-1, D_MODEL))
    rows = jnp.concatenate(rows, axis=0)
    return jnp.pad(rows, ((0, SMALL_ROWS - rows.shape[0]), (0, 0)))


def _unpack_small(slab):
    out, r = {}, 0
    for name in SMALL:
        shape = SMALL_SHAPES.get(name, (1, D_MODEL))
        size = math.prod(shape)
        n = -(-size // D_MODEL)
        out[name] = slab[r:r + n].reshape(-1)[:size].reshape(shape)
        r += n
    return out


ANY = pl.BlockSpec(memory_space=pl.ANY)
MESH = pl.DeviceIdType.MESH
CHIP_FLIPS = ((0, 1), (1, 0), (1, 1))


def _place():
    x, y, c = lax.axis_index("x"), lax.axis_index("y"), lax.axis_index("c")
    chips = [(x ^ fx, y ^ fy) for fx, fy in CHIP_FLIPS]
    return x, y, c, chips


def _remote(src, dst, sems, k, dev):
    return pltpu.make_async_remote_copy(src_ref=src, dst_ref=dst, send_sem=sems[0].at[k], recv_sem=sems[1].at[k],
                                        device_id=dev, device_id_type=MESH)


def _gather_weights(wpack):
    def body(w_ref, out_ref, send_sems, recv_sems):
        x, y, c, chips = _place()
        sems = (send_sems, recv_sems)
        mine, theirs = pl.ds(c * HALF, HALF), pl.ds((1 - c) * HALF, HALF)
        sent = [_remote(w_ref.at[mine], out_ref.at[k, mine], sems, k, (px, py, c)) for k, (px, py) in enumerate(chips)]
        for cp in sent:
            cp.start()
        passed = []
        for k, (px, py) in enumerate(chips):
            rows = out_ref.at[k, mine]
            _remote(rows, rows, sems, k, (px, py, c)).wait_recv()
            cp = _remote(rows, rows, sems, 3 + k, (x, y, 1 - c))
            cp.start()
            passed.append(cp)
        for k in range(3):
            rows = out_ref.at[k, theirs]
            _remote(rows, rows, sems, 3 + k, (x, y, 1 - c)).wait_recv()
        for cp in sent + passed:
            cp.wait_send()

    return pl.pallas_call(
        body, name="gather_weights", out_shape=jax.ShapeDtypeStruct((3, PACK_ROWS, D_MODEL), wpack.dtype),
        in_specs=[ANY], out_specs=ANY,
        scratch_shapes=[pltpu.SemaphoreType.DMA((6,)), pltpu.SemaphoreType.DMA((6,))],
    )(wpack)


def _swap_halves(g):
    def body(g_ref, out_ref, send_sems, recv_sems):
        x, y, c, _ = _place()
        sems = (send_sems, recv_sems)
        theirs = pl.ds((1 - c) * HALF, HALF)
        sent = [_remote(g_ref.at[j, theirs], out_ref.at[j], sems, j, (x, y, 1 - c)) for j in range(N_CHIP)]
        for cp in sent:
            cp.start()
        for cp in sent:
            cp.wait_recv()
        for cp in sent:
            cp.wait_send()

    return pl.pallas_call(
        body, name="swap_halves", out_shape=jax.ShapeDtypeStruct((N_CHIP, HALF, D_MODEL), g.dtype),
        in_specs=[ANY], out_specs=ANY,
        scratch_shapes=[pltpu.SemaphoreType.DMA((N_CHIP,)), pltpu.SemaphoreType.DMA((N_CHIP,))],
    )(g)


def _pair_sum(g, got, place, *, tm=128):
    nb = HALF // tm

    def body(s_ref, g_ref, a_ref, bf_ref, own_ref):
        v = g_ref[...] + a_ref[...]
        bf_ref[...] = v.astype(BF16)

        @pl.when(pl.program_id(1) == s_ref[0])
        def _():
            own_ref[...] = v

    return pl.pallas_call(
        body, name="pair_sum",
        grid_spec=pltpu.PrefetchScalarGridSpec(
            num_scalar_prefetch=1, grid=(nb, N_CHIP),
            in_specs=[pl.BlockSpec((None, tm, D_MODEL), lambda i, j, s: (j, s[1] * nb + i, 0)),
                      pl.BlockSpec((None, tm, D_MODEL), lambda i, j, s: (j, i, 0))],
            out_specs=[pl.BlockSpec((None, tm, D_MODEL), lambda i, j, s: (j, i, 0)),
                       pl.BlockSpec((tm, D_MODEL), lambda i, j, s: (i, 0))]),
        out_shape=[jax.ShapeDtypeStruct((N_CHIP, HALF, D_MODEL), BF16), jax.ShapeDtypeStruct((HALF, D_MODEL), F32)],
        compiler_params=_params("arbitrary", "arbitrary"),
    )(place, g, got)


def _scatter_partials(pbf):
    def body(p_ref, out_ref, send_sems, recv_sems):
        x, y, c, chips = _place()
        sems = (send_sems, recv_sems)
        sent = [_remote(p_ref.at[2 * px + py], out_ref.at[k], sems, k, (px, py, c)) for k, (px, py) in enumerate(chips)]
        for cp in sent:
            cp.start()
        for cp in sent:
            cp.wait_recv()
        for cp in sent:
            cp.wait_send()

    return pl.pallas_call(
        body, name="scatter_partials", out_shape=jax.ShapeDtypeStruct((3, HALF, D_MODEL), pbf.dtype),
        in_specs=[ANY], out_specs=ANY, scratch_shapes=[pltpu.SemaphoreType.DMA((3,)), pltpu.SemaphoreType.DMA((3,))],
    )(pbf)


def _chip_sum(own, got, *, tm=128):
    def body(o_ref, g_ref, r_ref):
        r_ref[...] = ((o_ref[...] + g_ref[0].astype(F32)) + g_ref[1].astype(F32)) + g_ref[2].astype(F32)

    row = pl.BlockSpec((tm, D_MODEL), lambda i: (i, 0))
    return pl.pallas_call(
        body, name="chip_sum", grid=(HALF // tm,),
        in_specs=[row, pl.BlockSpec((3, tm, D_MODEL), lambda i: (0, i, 0))], out_specs=row,
        out_shape=jax.ShapeDtypeStruct((HALF, D_MODEL), F32), compiler_params=_params("parallel"),
    )(own, got)


def _join_halves(r):
    def body(r_ref, out_ref, send_sems, recv_sems):
        x, y, c, _ = _place()
        cp = _remote(r_ref, out_ref, (send_sems, recv_sems), 0, (x, y, 1 - c))
        cp.start()
        cp.wait_recv()
        cp.wait_send()

    return pl.pallas_call(
        body, name="join_halves", out_shape=jax.ShapeDtypeStruct((HALF, D_MODEL), r.dtype),
        in_specs=[ANY], out_specs=ANY,
        scratch_shapes=[pltpu.SemaphoreType.DMA((1,)), pltpu.SemaphoreType.DMA((1,))],
    )(r)


def _gather_small(s):
    flips = [(fx, fy, fc) for fx in (0, 1) for fy in (0, 1) for fc in (0, 1)][1:]

    def body(s_ref, out_ref, send_sems, recv_sems, local_sem):
        x, y, c, _ = _place()
        sems = (send_sems, recv_sems)
        me = 4 * x + 2 * y + c
        local = pltpu.make_async_copy(s_ref, out_ref.at[me], local_sem)
        local.start()
        sent = [_remote(s_ref, out_ref.at[me], sems, k, (x ^ fx, y ^ fy, c ^ fc)) for k, (fx, fy, fc) in enumerate(flips)]
        for cp in sent:
            cp.start()
        for k, (fx, fy, fc) in enumerate(flips):
            peer = (x ^ fx, y ^ fy, c ^ fc)
            _remote(s_ref, out_ref.at[4 * peer[0] + 2 * peer[1] + peer[2]], sems, k, peer).wait_recv()
        for cp in sent:
            cp.wait_send()
        local.wait()

    return pl.pallas_call(
        body, name="gather_small", out_shape=jax.ShapeDtypeStruct((N_DEV, SMALL_ROWS, D_MODEL), s.dtype),
        in_specs=[ANY], out_specs=ANY,
        scratch_shapes=[pltpu.SemaphoreType.DMA((7,)), pltpu.SemaphoreType.DMA((7,)), pltpu.SemaphoreType.DMA],
    )(s)


def _full_weights(own, others, me):
    by_flip = jnp.concatenate([own[None], others], axis=0)
    per_chip = [_unpack(lax.dynamic_index_in_dim(by_flip, j ^ me, 0, keepdims=False)) for j in range(N_CHIP)]
    full = {name: jnp.concatenate([pc[name] for pc in per_chip], axis=axis) for name, _, axis in SHARDED}
    w_in = full["w_in"]
    return dict(
        f1g=full["ffn1_w_in"][:, :D_FF], f1u=full["ffn1_w_in"][:, D_FF:], f1d=full["ffn1_w_down"],
        f2g=full["ffn2_w_in"][:, :D_FF], f2u=full["ffn2_w_in"][:, D_FF:], f2d=full["ffn2_w_down"],
        w_main=jnp.concatenate([w_in[:, :FB_COL], w_in[:, FB_COL + HEADS:]], axis=1),
        w_fb=jnp.pad(w_in[:, FB_COL:FB_COL + HEADS], ((0, 0), (0, LANE - HEADS))),
        wa=full["w_branch_a"], wb=full["w_branch_b"], wo=full["w_out"],
        wmq=full["w_mq"], wmkv=full["w_mkv"], wmo=full["w_mo"],
    )


def _grad_slabs(G):
    main = jnp.concatenate(G["w_main"], axis=1)
    full = dict(
        ffn1_w_in=jnp.concatenate([G["f1g"], G["f1u"]], axis=1), ffn1_w_down=G["f1d"],
        ffn2_w_in=jnp.concatenate([G["f2g"], G["f2u"]], axis=1), ffn2_w_down=G["f2d"],
        w_in=jnp.concatenate([main[:, :FB_COL], G["w_fb"][:, :HEADS], main[:, FB_COL:]], axis=1),
        w_branch_a=G["wa"], w_branch_b=G["wb"], w_out=G["wo"], w_mq=G["wmq"], w_mkv=G["wmkv"], w_mo=G["wmo"],
    )
    slabs = []
    for j in range(N_CHIP):
        shards = {}
        for name, shape, axis in SHARDED:
            n = shape[axis] // N_CHIP
            shards[name] = lax.slice_in_dim(full[name], j * n, (j + 1) * n, axis=axis)
        slabs.append(_pack(shards, F32))
    return jnp.stack(slabs, axis=0)


def kernel(x, mem, ffn1_pre_g, ffn1_w_in, ffn1_w_down, ffn1_post_g, mix_pre_g, w_in, hg_lb_logits, hg_norm_g, fox_f_bias, w_branch_a, w_branch_b, b_gate, w_out, mix_post_g, mem_pre_g, mem_kv_g, w_mq, w_mkv, w_mo, mem_post_g, ffn2_pre_g, ffn2_w_in, ffn2_w_down, ffn2_post_g, loss_target, m_ffn1_pre_g, m_ffn1_w_in, m_ffn1_w_down, m_ffn1_post_g, m_mix_pre_g, m_w_in, m_hg_lb_logits, m_hg_norm_g, m_fox_f_bias, m_w_branch_a, m_w_branch_b, m_b_gate, m_w_out, m_mix_post_g, m_mem_pre_g, m_mem_kv_g, m_w_mq, m_w_mkv, m_w_mo, m_mem_post_g, m_ffn2_pre_g, m_ffn2_w_in, m_ffn2_w_down, m_ffn2_post_g, v_ffn1_pre_g, v_ffn1_w_in, v_ffn1_w_down, v_ffn1_post_g, v_mix_pre_g, v_w_in, v_hg_lb_logits, v_hg_norm_g, v_fox_f_bias, v_w_branch_a, v_w_branch_b, v_b_gate, v_w_out, v_mix_post_g, v_mem_pre_g, v_mem_kv_g, v_w_mq, v_w_mkv, v_w_mo, v_mem_post_g, v_ffn2_pre_g, v_ffn2_w_in, v_ffn2_w_down, v_ffn2_post_g):
    w = dict(ffn1_pre_g=ffn1_pre_g, ffn1_w_in=ffn1_w_in, ffn1_w_down=ffn1_w_down, ffn1_post_g=ffn1_post_g, mix_pre_g=mix_pre_g, w_in=w_in, hg_lb_logits=hg_lb_logits, hg_norm_g=hg_norm_g, fox_f_bias=fox_f_bias, w_branch_a=w_branch_a, w_branch_b=w_branch_b, b_gate=b_gate, w_out=w_out, mix_post_g=mix_post_g, mem_pre_g=mem_pre_g, mem_kv_g=mem_kv_g, w_mq=w_mq, w_mkv=w_mkv, w_mo=w_mo, mem_post_g=mem_post_g, ffn2_pre_g=ffn2_pre_g, ffn2_w_in=ffn2_w_in, ffn2_w_down=ffn2_w_down, ffn2_post_g=ffn2_post_g)
    m = dict(ffn1_pre_g=m_ffn1_pre_g, ffn1_w_in=m_ffn1_w_in, ffn1_w_down=m_ffn1_w_down, ffn1_post_g=m_ffn1_post_g, mix_pre_g=m_mix_pre_g, w_in=m_w_in, hg_lb_logits=m_hg_lb_logits, hg_norm_g=m_hg_norm_g, fox_f_bias=m_fox_f_bias, w_branch_a=m_w_branch_a, w_branch_b=m_w_branch_b, b_gate=m_b_gate, w_out=m_w_out, mix_post_g=m_mix_post_g, mem_pre_g=m_mem_pre_g, mem_kv_g=m_mem_kv_g, w_mq=m_w_mq, w_mkv=m_w_mkv, w_mo=m_w_mo, mem_post_g=m_mem_post_g, ffn2_pre_g=m_ffn2_pre_g, ffn2_w_in=m_ffn2_w_in, ffn2_w_down=m_ffn2_w_down, ffn2_post_g=m_ffn2_post_g)
    v = dict(ffn1_pre_g=v_ffn1_pre_g, ffn1_w_in=v_ffn1_w_in, ffn1_w_down=v_ffn1_w_down, ffn1_post_g=v_ffn1_post_g, mix_pre_g=v_mix_pre_g, w_in=v_w_in, hg_lb_logits=v_hg_lb_logits, hg_norm_g=v_hg_norm_g, fox_f_bias=v_fox_f_bias, w_branch_a=v_w_branch_a, w_branch_b=v_w_branch_b, b_gate=v_b_gate, w_out=v_w_out, mix_post_g=v_mix_post_g, mem_pre_g=v_mem_pre_g, mem_kv_g=v_mem_kv_g, w_mq=v_w_mq, w_mkv=v_w_mkv, w_mo=v_w_mo, mem_post_g=v_mem_post_g, ffn2_pre_g=v_ffn2_pre_g, ffn2_w_in=v_ffn2_w_in, ffn2_w_down=v_ffn2_w_down, ffn2_post_g=v_ffn2_post_g)
    sharded = [name for name, _, _ in SHARDED]
    shard_of = lambda d: {name: d[name][0] for name in sharded}

    me, core = 2 * lax.axis_index("x") + lax.axis_index("y"), lax.axis_index("c")
    w_own = _pack(shard_of(w), BF16)
    W = _full_weights(w_own, _gather_weights(w_own), me)
    P = {name: w[name] for name in SMALL}

    sq, dx0, G = _local_step(x[0], mem[0], loss_target[0], W, P)
    loss = lax.psum(0.5 * jnp.sum(sq) / D_MODEL, ("x", "y", "c"))

    place = jnp.stack([me, core]).astype(jnp.int32)
    slabs = _grad_slabs(G)
    pbf, own = _pair_sum(slabs, _swap_halves(slabs), place)
    g_mine = _chip_sum(own, _scatter_partials(pbf))
    g_theirs = _join_halves(g_mine)
    g_shard = jnp.where(core == 0, jnp.concatenate([g_mine, g_theirs]), jnp.concatenate([g_theirs, g_mine]))
    g_out, d_out, m_out, v_out = _adamw(_pack(shard_of(w), F32), g_shard, _pack(shard_of(m), F32),
                                        _pack(shard_of(v), F32), name="adamw")
    gs_out, ds_out, ms_out, vs_out = _adamw(_pack_small(w), _gather_small(_pack_small(G)), _pack_small(m),
                                            _pack_small(v), name="adamw_small", tm=SMALL_ROWS)

    outs = [loss, dx0[None]]
    for big, small in ((g_out, gs_out), (d_out, ds_out), (m_out, ms_out), (v_out, vs_out)):
        vals = {name: a[None] for name, a in _unpack(big).items()}
        vals.update(_unpack_small(small))
        outs += [vals[name] for name in WEIGHT_ORDER]
    return tuple(outs)
```

```python
import functools
import math

import jax
import jax.numpy as jnp
from jax import lax
from jax.experimental import pallas as pl
from jax.experimental.pallas import tpu as pltpu

F32 = jnp.float32
BF16 = jnp.bfloat16

D_MODEL = 1024
D_FF = 2816
HEADS = 8
DH = 128
MEM_HEADS = 4
MEM_DH = 256
MEM_LEN = 256
EPS = 1e-6
SUB = 16
LANE = 128
SUBLANE = 8
VMEM_LIMIT = 56 * 1024 * 1024

ADAM_LR = 0.001
ADAM_B1 = 0.9
ADAM_B2 = 0.999
ADAM_EPS = 1e-08
ADAM_WD = 0.01
ADAM_STEP = 10

HIGHEST = lax.Precision.HIGHEST


def _params(*sem):
    return pltpu.CompilerParams(dimension_semantics=sem, vmem_limit_bytes=VMEM_LIMIT)


def _sigmoid(v):
    return 1.0 / (1.0 + jnp.exp(-v))


def _silu(v):
    return v * _sigmoid(v)


def _dsilu(v):
    s = _sigmoid(v)
    return s * (1.0 + v * (1.0 - s))


def _dot(a, b, dims):
    return lax.dot_general(a.astype(BF16), b.astype(BF16), (dims, ((), ())), preferred_element_type=F32)


NN = ((1,), (0,))
NT = ((1,), (1,))
TN = ((0,), (0,))


def _mm(pairs, mode, *, tm, tn, tk, out_dtypes, name, epilogue=None, tiles=(), b_koff=None):
    a0, b0 = pairs[0]
    if mode == "nn":
        (M, K), N = a0.shape, b0.shape[1]
    elif mode == "nt":
        (M, K), N = a0.shape, b0.shape[0]
    else:
        (K, M), N = a0.shape, b0.shape[1]
    tm, tn, tk = min(tm, M), min(tn, N), min(tk, K)
    assert M % tm == 0 and N % tn == 0 and K % tk == 0, (name, M, N, K, tm, tn, tk)
    nk = K // tk
    npair = len(pairs)
    koff = [0] * npair if b_koff is None else [o // tk for o in b_koff]
    if b_koff is not None:
        assert all(o % tk == 0 for o in b_koff)
    in_specs, args = [], []
    for p, (a, b) in enumerate(pairs):
        if mode == "nn":
            sa = pl.BlockSpec((tm, tk), lambda i, j, k: (i, k))
            sb = pl.BlockSpec((tk, tn), lambda i, j, k, o=koff[p]: (k + o, j))
            dims = NN
        elif mode == "nt":
            sa = pl.BlockSpec((tm, tk), lambda i, j, k: (i, k))
            sb = pl.BlockSpec((tn, tk), lambda i, j, k, o=koff[p]: (j, k + o))
            dims = NT
        else:
            sa = pl.BlockSpec((tk, tm), lambda i, j, k: (k, i))
            sb = pl.BlockSpec((tk, tn), lambda i, j, k, o=koff[p]: (k + o, j))
            dims = TN
        in_specs += [sa, sb]
        args += [a, b]
    for t in tiles:
        in_specs.append(pl.BlockSpec((tm, tn), lambda i, j, k: (i, j)))
        args.append(t)
    nt_ = len(tiles)
    nout = len(out_dtypes)

    def body(*refs):
        ab = refs[: 2 * npair]
        tl = refs[2 * npair: 2 * npair + nt_]
        outs = refs[2 * npair + nt_: 2 * npair + nt_ + nout]
        acc_ref = refs[-1] if nk > 1 else None

        def partial_sum():
            s = _dot(ab[0][...], ab[1][...], dims)
            for p in range(1, npair):
                s = s + _dot(ab[2 * p][...], ab[2 * p + 1][...], dims)
            return s

        def finish(acc):
            res = (acc,) if epilogue is None else epilogue(acc, *[t[...] for t in tl])
            for o, r in zip(outs, res):
                o[...] = r.astype(o.dtype)

        if nk == 1:
            finish(partial_sum())
        else:
            k = pl.program_id(2)

            @pl.when(k == 0)
            def _():
                acc_ref[...] = jnp.zeros_like(acc_ref)

            acc_ref[...] += partial_sum()

            @pl.when(k == nk - 1)
            def _():
                finish(acc_ref[...])

    out_shape = [jax.ShapeDtypeStruct((M, N), dt) for dt in out_dtypes]
    out_specs = [pl.BlockSpec((tm, tn), lambda i, j, k: (i, j)) for _ in out_dtypes]
    res = pl.pallas_call(
        body, name=name, grid=(M // tm, N // tn, nk), in_specs=in_specs, out_specs=out_specs, out_shape=out_shape,
        scratch_shapes=[pltpu.VMEM((tm, tn), F32)] if nk > 1 else [],
        compiler_params=_params("parallel", "parallel", "arbitrary"),
    )(*args)
    return res[0] if nout == 1 else res


def _col(arr, tm, width, cb):
    return pl.BlockSpec((tm, width), lambda i, cb=cb: (i, cb))


def _rms_fwd(x, g, *, out_dtype, name, mul=None, res=None, coeff=1.0, tm=512):
    T, D = x.shape
    tm = min(tm, T)
    args, in_specs = [x, g], [pl.BlockSpec((tm, D), lambda i: (i, 0)), pl.BlockSpec((1, D), lambda i: (0, 0))]
    if mul is not None:
        args.append(mul[0])
        in_specs.append(_col(mul[0], tm, D, mul[1]))
    if res is not None:
        args.append(res)
        in_specs.append(pl.BlockSpec((tm, D), lambda i: (i, 0)))

    def body(*refs):
        xv = refs[0][...].astype(F32)
        r = lax.rsqrt(jnp.mean(xv * xv, axis=-1, keepdims=True) + EPS)
        y = (xv * r) * refs[1][...]
        n = 2
        if mul is not None:
            y = y * _silu(refs[n][...])
            n += 1
        if res is not None:
            y = refs[n][...] + coeff * y
        refs[-1][...] = y.astype(out_dtype)

    return pl.pallas_call(
        body, name=name, grid=(T // tm,), in_specs=in_specs, out_specs=pl.BlockSpec((tm, D), lambda i: (i, 0)),
        out_shape=jax.ShapeDtypeStruct((T, D), out_dtype), compiler_params=_params("parallel"),
    )(*args)


def _fold8(v):
    tm, d = v.shape
    return v.reshape(tm // SUBLANE, SUBLANE, d).sum(axis=0)


def _rms_bwd(x, g, dy, *, name, coeff=1.0, add=None, mul=None, dx_dtype=F32, tm=512):
    T, D = x.shape
    tm = min(tm, T)
    row = pl.BlockSpec((tm, D), lambda i: (i, 0))
    args, in_specs = [x, g, dy], [row, pl.BlockSpec((1, D), lambda i: (0, 0)), row]
    if add is not None:
        args.append(add)
        in_specs.append(row)
    if mul is not None:
        args.append(mul[0])
        in_specs.append(_col(mul[0], tm, D, mul[1]))
    nin = len(args)

    def body(*refs):
        xv = refs[0][...].astype(F32)
        gv = refs[1][...]
        dyv = refs[2][...].astype(F32) * coeff
        r = lax.rsqrt(jnp.mean(xv * xv, axis=-1, keepdims=True) + EPS)
        nrm = xv * r
        n = 3
        addv = None
        if add is not None:
            addv = refs[n][...]
            n += 1
        if mul is not None:
            mv = refs[n][...]
            sm = _silu(mv)
            refs[nin + 2][...] = (dyv * nrm * gv * _dsilu(mv)).astype(refs[nin + 2].dtype)
            dyv = dyv * sm
        dn = dyv * gv
        dx = r * (dn - nrm * jnp.mean(dn * nrm, axis=-1, keepdims=True))
        if addv is not None:
            dx = dx + addv
        refs[nin][...] = dx.astype(dx_dtype)
        dg_ref = refs[nin + 1]

        @pl.when(pl.program_id(0) == 0)
        def _():
            dg_ref[...] = jnp.zeros_like(dg_ref)

        dg_ref[...] += _fold8(dyv * nrm)

    out_shape = [jax.ShapeDtypeStruct((T, D), dx_dtype), jax.ShapeDtypeStruct((SUBLANE, D), F32)]
    out_specs = [row, pl.BlockSpec((SUBLANE, D), lambda i: (0, 0))]
    if mul is not None:
        out_shape.append(jax.ShapeDtypeStruct((T, D), BF16))
        out_specs.append(row)
    return pl.pallas_call(
        body, name=name, grid=(T // tm,), in_specs=in_specs, out_specs=out_specs, out_shape=out_shape,
        compiler_params=_params("arbitrary"),
    )(*args)


def _ffn_in(h, wg, wu, *, name, tm=1024, tn=256):
    T, D = h.shape
    F = wg.shape[1]
    tm = min(tm, T)
    assert F % tn == 0

    def body(h_ref, wg_ref, wu_ref, a_ref, g_ref, u_ref):
        hv = h_ref[...]
        gt = _dot(hv, wg_ref[...], NN)
        up = _dot(hv, wu_ref[...], NN)
        a_ref[...] = (_silu(gt) * up).astype(BF16)
        g_ref[...] = gt.astype(BF16)
        u_ref[...] = up.astype(BF16)

    o = pl.BlockSpec((tm, tn), lambda i, j: (i, j))
    w = pl.BlockSpec((D, tn), lambda i, j: (0, j))
    return pl.pallas_call(
        body, name=name, grid=(T // tm, F // tn), in_specs=[pl.BlockSpec((tm, D), lambda i, j: (i, 0)), w, w],
        out_specs=[o, o, o], out_shape=[jax.ShapeDtypeStruct((T, F), BF16)] * 3,
        compiler_params=_params("parallel", "parallel"),
    )(h, wg, wu)


def _swiglu_bwd_epilogue(da, gt, up):
    gt = gt.astype(F32)
    up = up.astype(F32)
    return da * up * _dsilu(gt), da * _silu(gt)


GATE_CB = 7


def _gatemix_fwd(z, b_gate, ya, yb, *, name, tm=512):
    T, D = ya.shape
    tm = min(tm, T)
    row = pl.BlockSpec((tm, D), lambda i: (i, 0))

    def body(z0, z1, b0, b1, ya_ref, yb_ref, y_ref):
        g0 = _sigmoid(z0[...] + b0[...])
        g1 = _sigmoid(z1[...] + b1[...])
        y_ref[...] = (g0 * ya_ref[...] + g1 * yb_ref[...]).astype(y_ref.dtype)

    bs = lambda c: pl.BlockSpec((1, D), lambda i, c=c: (0, c))
    return pl.pallas_call(
        body, name=name, grid=(T // tm,),
        in_specs=[_col(z, tm, D, GATE_CB), _col(z, tm, D, GATE_CB + 1), bs(0), bs(1), row, row],
        out_specs=row, out_shape=jax.ShapeDtypeStruct((T, D), BF16), compiler_params=_params("parallel"),
    )(z, z, b_gate, b_gate, ya, yb)


def _gatemix_bwd(z, b_gate, ya, yb, dy, *, name, tm=512):
    T, D = ya.shape
    tm = min(tm, T)
    row = pl.BlockSpec((tm, D), lambda i: (i, 0))
    part = pl.BlockSpec((SUBLANE, D), lambda i: (0, 0))

    def body(z0, z1, b0, b1, ya_ref, yb_ref, dy_ref, dya, dyb, dz0, dz1, s0, s1):
        g0 = _sigmoid(z0[...] + b0[...])
        g1 = _sigmoid(z1[...] + b1[...])
        dyv = dy_ref[...]
        dya[...] = (dyv * g0).astype(BF16)
        dyb[...] = (dyv * g1).astype(BF16)
        d0 = dyv * ya_ref[...] * (g0 * (1.0 - g0))
        d1 = dyv * yb_ref[...] * (g1 * (1.0 - g1))
        dz0[...] = d0.astype(BF16)
        dz1[...] = d1.astype(BF16)

        @pl.when(pl.program_id(0) == 0)
        def _():
            s0[...] = jnp.zeros_like(s0)
            s1[...] = jnp.zeros_like(s1)

        s0[...] += _fold8(d0)
        s1[...] += _fold8(d1)

    bs = lambda c: pl.BlockSpec((1, D), lambda i, c=c: (0, c))
    act = jax.ShapeDtypeStruct((T, D), BF16)
    ps = jax.ShapeDtypeStruct((SUBLANE, D), F32)
    return pl.pallas_call(
        body, name=name, grid=(T // tm,),
        in_specs=[_col(z, tm, D, GATE_CB), _col(z, tm, D, GATE_CB + 1), bs(0), bs(1), row, row, row],
        out_specs=[row, row, row, row, part, part], out_shape=[act, act, act, act, ps, ps],
        compiler_params=_params("arbitrary"),
    )(z, z, b_gate, b_gate, ya, yb, dy)


def _loss_head(x, target, *, name, tm=512):
    T, D = x.shape
    tm = min(tm, T)
    row = pl.BlockSpec((tm, D), lambda i: (i, 0))

    def body(x_ref, t_ref, dx_ref, s_ref):
        e = x_ref[...] - t_ref[...]
        dx_ref[...] = e * (1.0 / D)

        @pl.when(pl.program_id(0) == 0)
        def _():
            s_ref[...] = jnp.zeros_like(s_ref)

        s_ref[...] += _fold8(e * e)

    return pl.pallas_call(
        body, name=name, grid=(T // tm,), in_specs=[row, row],
        out_specs=[row, pl.BlockSpec((SUBLANE, D), lambda i: (0, 0))],
        out_shape=[jax.ShapeDtypeStruct((T, D), F32), jax.ShapeDtypeStruct((SUBLANE, D), F32)],
        compiler_params=_params("arbitrary"),
    )(x, target)


def _tri(n, reverse):
    r = lax.broadcasted_iota(jnp.int32, (n, n), 0)
    c = lax.broadcasted_iota(jnp.int32, (n, n), 1)
    return jnp.where((c >= r) if reverse else (c <= r), 1.0, 0.0).astype(F32)


def _cumsum_t(xs, *, name, width, pre, reverse=False, rows=(), post=None, out_dtypes=(F32,), fold=None, tb=256):
    T = xs[0][0].shape[0]
    tb = min(tb, T)
    nb = T // tb
    tblk = (lambda i: nb - 1 - i) if reverse else (lambda i: i)
    args = [a for a, _ in xs] + [a for a, _ in rows]
    in_specs = [pl.BlockSpec((tb, width), lambda i, cb=cb: (tblk(i), cb)) for _, cb in xs]
    in_specs += [pl.BlockSpec((1, width), lambda i, cb=cb: (0, cb)) for _, cb in rows]
    nin, nout = len(args), len(out_dtypes)

    def body(*refs):
        vals = [r[...] for r in refs[:nin]]
        outs = refs[nin:nin + nout]
        carry = refs[-1]
        first = pl.program_id(0) == 0

        @pl.when(first)
        def _():
            carry[...] = jnp.zeros_like(carry)

        cum = jnp.dot(_tri(tb, reverse), pre(*vals), precision=HIGHEST, preferred_element_type=F32) + carry[...]
        carry[...] = cum[0:1, :] if reverse else cum[tb - 1:tb, :]
        res = (cum,) if post is None else post(cum, *vals)
        for o, r in zip(outs, res):
            o[...] = r.astype(o.dtype)
        if fold is not None:
            f_ref = refs[nin + nout]

            @pl.when(first)
            def _():
                f_ref[...] = jnp.zeros_like(f_ref)

            f_ref[...] += _fold8(fold(cum, *vals))

    tspec = pl.BlockSpec((tb, width), lambda i: (tblk(i), 0))
    out_shape = [jax.ShapeDtypeStruct((T, width), dt) for dt in out_dtypes]
    out_specs = [tspec] * nout
    if fold is not None:
        out_shape.append(jax.ShapeDtypeStruct((SUBLANE, width), F32))
        out_specs.append(pl.BlockSpec((SUBLANE, width), lambda i: (0, 0)))
    res = pl.pallas_call(
        body, name=name, grid=(nb,), in_specs=in_specs, out_specs=out_specs, out_shape=out_shape,
        scratch_shapes=[pltpu.VMEM((1, width), F32)], compiler_params=_params("arbitrary"),
    )(*args)
    return res[0] if len(res) == 1 else res


def _logsigmoid(v):
    return jnp.minimum(v, 0.0) - jnp.log(1.0 + jnp.exp(-jnp.abs(v)))


HG_TB = 256
HG_HB = 4
HG_W = HG_HB * DH
HG_GROUPS = HEADS // HG_HB
HG_Q_CB, HG_F_CB, HG_I_CB = 0, HG_GROUPS, 2 * HG_GROUPS
NEG = -1e30


def _scan16(x, rowid, reverse=False):
    for k in (1, 2, 4, 8):
        if reverse:
            x = x + jnp.where(rowid < SUB - k, pltpu.roll(x, SUB - k, 0), 0.0)
        else:
            x = x + jnp.where(rowid >= k, pltpu.roll(x, k, 0), 0.0)
    return x


def _hg_block(q_ref, f_ref, i_ref, lb_ref, rows, cols, rowid):
    lb = lb_ref[:, cols]
    qr = q_ref[rows, cols]
    sg = _sigmoid(f_ref[rows, cols])
    f = lb + (1.0 - lb) * sg
    b = _scan16(jnp.log(f), rowid)
    return _silu(qr), 1.0 - f, i_ref[rows, cols], b, qr, sg, f, lb


def _hg_specs(tb, tmap):
    return [pl.BlockSpec((tb, HG_W), lambda g, t: (tmap(t), HG_Q_CB + g)),
            pl.BlockSpec((tb, HG_W), lambda g, t: (tmap(t), HG_F_CB + g)),
            pl.BlockSpec((tb, HG_W), lambda g, t: (tmap(t), HG_I_CB + g)),
            pl.BlockSpec((1, HG_W), lambda g, t: (0, g))]


def _hgrn2_fwd(z, lb_row, *, name):
    T = z.shape[0]
    tb = min(HG_TB, T)
    nb, nsub = T // tb, tb // SUB

    def body(q_ref, f_ref, i_ref, lb_ref, o_ref, st_ref, state):
        @pl.when(pl.program_id(1) == 0)
        def _():
            state[...] = jnp.zeros_like(state)

        rowid = lax.broadcasted_iota(jnp.int32, (SUB, DH), 0)

        def step(c, carry):
            rows = pl.ds(pl.multiple_of(c * SUB, SUB), SUB)
            for hh in range(HG_HB):
                cols = slice(hh * DH, (hh + 1) * DH)
                q, k, iv, b = _hg_block(q_ref, f_ref, i_ref, lb_ref, rows, cols, rowid)[:4]
                bl = b[SUB - 1:SUB, :]
                sv = state[hh]
                st_ref[c, hh] = sv
                o = _dot(q * jnp.exp(b), sv, NT)
                for s in range(SUB):
                    e = jnp.exp(jnp.where(rowid >= s, b - b[s:s + 1, :], NEG))
                    a = jnp.sum(q * e * k[s:s + 1, :], axis=-1, keepdims=True)
                    o = o + a * iv[s:s + 1, :]
                o_ref[rows, cols] = o
                state[hh] = sv * jnp.exp(bl) + _dot(iv, k * jnp.exp(bl - b), TN)
            return carry

        lax.fori_loop(0, nsub, step, 0)

    return pl.pallas_call(
        body, name=name, grid=(HG_GROUPS, nb), in_specs=_hg_specs(tb, lambda t: t),
        out_specs=[pl.BlockSpec((tb, HG_W), lambda g, t: (t, g)),
                   pl.BlockSpec((nsub, HG_HB, DH, DH), lambda g, t: (t, g, 0, 0))],
        out_shape=[jax.ShapeDtypeStruct((T, D_MODEL), F32), jax.ShapeDtypeStruct((T // SUB, HEADS, DH, DH), F32)],
        scratch_shapes=[pltpu.VMEM((HG_HB, DH, DH), F32)], compiler_params=_params("parallel", "arbitrary"),
    )(z, z, z, lb_row)


def _hgrn2_bwd(z, lb_row, states, do, *, name):
    T = z.shape[0]
    tb = min(HG_TB, T)
    nb, nsub = T // tb, tb // SUB
    rev = lambda t: nb - 1 - t

    def body(q_ref, f_ref, i_ref, lb_ref, st_ref, do_ref, dq_ref, dfl_ref, di_ref, dlb_ref, dstate, later):
        @pl.when(pl.program_id(1) == 0)
        def _():
            dstate[...] = jnp.zeros_like(dstate)
            later[...] = jnp.zeros_like(later)
            dlb_ref[...] = jnp.zeros_like(dlb_ref)

        rowid = lax.broadcasted_iota(jnp.int32, (SUB, DH), 0)

        def step(cc, carry):
            c = nsub - 1 - cc
            rows = pl.ds(pl.multiple_of(c * SUB, SUB), SUB)
            for hh in range(HG_HB):
                cols = slice(hh * DH, (hh + 1) * DH)
                q, k, iv, b, qr, sg, f, lb = _hg_block(q_ref, f_ref, i_ref, lb_ref, rows, cols, rowid)
                bl = b[SUB - 1:SUB, :]
                eb, ebl = jnp.exp(b), jnp.exp(bl - b)
                sv, dsv = st_ref[c, hh], dstate[hh]
                dov = do_ref[rows, cols]
                dq = _dot(dov, sv, NN) * eb
                dk = _dot(iv, dsv, NN) * ebl
                di = _dot(k * ebl, dsv, NT)
                for s in range(SUB):
                    e = jnp.exp(jnp.where(rowid >= s, b - b[s:s + 1, :], NEG))
                    ks, isv = k[s:s + 1, :], iv[s:s + 1, :]
                    qe = q * e
                    a = jnp.sum(qe * ks, axis=-1, keepdims=True)
                    p = jnp.sum(dov * isv, axis=-1, keepdims=True)
                    dq = dq + p * (e * ks)
                    dks = jnp.sum(p * qe, axis=0, keepdims=True)
                    dis = jnp.sum(a * dov, axis=0, keepdims=True)
                    dk = dk + jnp.where(rowid == s, dks, 0.0)
                    di = di + jnp.where(rowid == s, dis, 0.0)
                dlogf = _scan16(q * dq - k * dk, rowid, reverse=True) + later[hh]
                df = dlogf / f - dk
                dlb_ref[:, cols] += jnp.sum(df * (1.0 - sg), axis=0, keepdims=True)
                dfl_ref[rows, cols] = (df * (1.0 - lb) * (sg * (1.0 - sg))).astype(BF16)
                dq_ref[rows, cols] = (dq * _dsilu(qr)).astype(BF16)
                di_ref[rows, cols] = di.astype(BF16)
                dnew = dsv * jnp.exp(bl) + _dot(dov, q * eb, TN)
                dstate[hh] = dnew
                later[hh] = jnp.sum(dnew * sv, axis=0, keepdims=True)
            return carry

        lax.fori_loop(0, nsub, step, 0)

    tile = pl.BlockSpec((tb, HG_W), lambda g, t: (rev(t), g))
    act = jax.ShapeDtypeStruct((T, D_MODEL), BF16)
    return pl.pallas_call(
        body, name=name, grid=(HG_GROUPS, nb),
        in_specs=_hg_specs(tb, rev) + [pl.BlockSpec((nsub, HG_HB, DH, DH), lambda g, t: (rev(t), g, 0, 0)), tile],
        out_specs=[tile, tile, tile, pl.BlockSpec((1, HG_W), lambda g, t: (0, g))],
        out_shape=[act, act, act, jax.ShapeDtypeStruct((1, D_MODEL), F32)],
        scratch_shapes=[pltpu.VMEM((HG_HB, DH, DH), F32), pltpu.VMEM((HG_HB, 1, DH), F32)],
        compiler_params=_params("parallel", "arbitrary"),
    )(z, z, z, lb_row, states, do)


FOX_Q_CB, FOX_K_CB, FOX_V_CB = 4 * HEADS, 5 * HEADS, 6 * HEADS
FOX_SCALE = 1.0 / math.sqrt(DH)


def _fox_tile(T):
    return 512 if T >= 2048 else 128


def _fox_pairs(nq, by_query):
    if by_query:
        pairs = [(i, j) for i in range(nq) for j in range(i + 1)]
    else:
        pairs = [(i, j) for j in range(nq) for i in range(j, nq)]
    return (jnp.asarray([p[0] for p in pairs], jnp.int32), jnp.asarray([p[1] for p in pairs], jnp.int32))


LOG2E = 1.4426950408889634
FOX_RC = 64


def _fox_q2(q):
    return (q * (FOX_SCALE * LOG2E)).astype(BF16)


def _fox_fwd(z, c_col, c_row, *, name):
    T = z.shape[0]
    tq = _fox_tile(T)
    nq = T // tq
    rc = min(FOX_RC, tq)

    qi, kj = _fox_pairs(nq, by_query=True)

    def body(qi_ref, kj_ref, q_ref, k_ref, v_ref, cc_ref, cr_ref, o_ref, lse_ref, m_scr, l_scr, acc, a_scr, s_scr,
             p_scr):
        p_id = pl.program_id(1)
        i, j = qi_ref[p_id], kj_ref[p_id]

        @pl.when(j == 0)
        def _():
            m_scr[...] = jnp.full_like(m_scr, NEG)
            l_scr[...] = jnp.zeros_like(l_scr)
            acc[...] = jnp.zeros_like(acc)

        def update(masked):
            bias = cc_ref[0:1, :] - cr_ref[...]
            s_scr[...] = _dot(_fox_q2(q_ref[...]), k_ref[...], NT)
            for r in range(tq // rc):
                rows = slice(r * rc, (r + 1) * rc)
                t = s_scr[rows, :] + bias
                if masked:
                    t = jnp.where(lax.broadcasted_iota(jnp.int32, (rc, tq), 1)
                                  <= r * rc + lax.broadcasted_iota(jnp.int32, (rc, tq), 0), t, NEG)
                m_old = m_scr[rows, :]
                m_new = jnp.maximum(m_old, jnp.max(t, axis=-1, keepdims=True))
                alpha = jnp.exp2(m_old - m_new)
                p = jnp.exp2(t - jnp.tile(m_new, (1, tq // LANE)))
                l_scr[rows, :] = alpha * l_scr[rows, :] + jnp.sum(p, axis=-1, keepdims=True)
                a_scr[rows, :] = alpha
                p_scr[rows, :] = p.astype(BF16)
                m_scr[rows, :] = m_new
            acc[...] = a_scr[...] * acc[...] + _dot(p_scr[...], v_ref[...], NN)

        @pl.when(j < i)
        def _():
            update(False)

        @pl.when(j == i)
        def _():
            update(True)
            o_ref[...] = acc[...] / l_scr[...]
            lse_ref[...] = (m_scr[:, 0:1] + jnp.log2(l_scr[:, 0:1])) + (cc_ref[...] - cc_ref[0:1, :])

    qtile = lambda cb: pl.BlockSpec((tq, DH), lambda h, p, qi, kj, cb=cb: (qi[p], cb + h))
    ktile = lambda cb: pl.BlockSpec((tq, DH), lambda h, p, qi, kj, cb=cb: (kj[p], cb + h))
    qcol = pl.BlockSpec((None, tq, 1), lambda h, p, qi, kj: (h, qi[p], 0))
    return pl.pallas_call(
        body, name=name,
        grid_spec=pltpu.PrefetchScalarGridSpec(
            num_scalar_prefetch=2, grid=(HEADS, qi.shape[0]),
            in_specs=[qtile(FOX_Q_CB), ktile(FOX_K_CB), ktile(FOX_V_CB), qcol,
                      pl.BlockSpec((None, 1, tq), lambda h, p, qi, kj: (h, 0, kj[p]))],
            out_specs=[qtile(0), qcol],
            scratch_shapes=[pltpu.VMEM((tq, LANE), F32), pltpu.VMEM((tq, LANE), F32), pltpu.VMEM((tq, DH), F32),
                            pltpu.VMEM((tq, LANE), F32), pltpu.VMEM((tq, tq), F32), pltpu.VMEM((tq, tq), BF16)]),
        out_shape=[jax.ShapeDtypeStruct((T, D_MODEL), F32), jax.ShapeDtypeStruct((HEADS, T, 1), F32)],
        compiler_params=_params("parallel", "arbitrary"),
    )(qi, kj, z, z, z, c_col, c_row)


def _fox_bwd(z, c_col, c_row, o, lse, do, *, name):
    T = z.shape[0]
    tq = _fox_tile(T)
    nq = T // tq
    rc = min(FOX_RC, tq)

    qi, kj = _fox_pairs(nq, by_query=False)

    def body(qi_ref, kj_ref, q_ref, k_ref, v_ref, cc_ref, cr_ref, o_ref, lse_ref, do_ref, dq_ref, dk_ref, dv_ref,
             dc_ref, dcq_ref, dk_acc, dv_acc, dc_acc, s_scr, dp_scr, p_scr, ds_scr):
        p_id = pl.program_id(1)
        i, j = qi_ref[p_id], kj_ref[p_id]

        @pl.when(p_id == 0)
        def _():
            dq_ref[...] = jnp.zeros_like(dq_ref)
            dcq_ref[...] = jnp.zeros_like(dcq_ref)

        def update(masked):
            q2, k, dov = _fox_q2(q_ref[...]), k_ref[...], do_ref[...]
            s_scr[...] = _dot(q2, k, NT)
            dp_scr[...] = _dot(dov, v_ref[...], NT)
            crow = cr_ref[...]
            csum = jnp.zeros((SUBLANE, tq), F32)
            wide = lambda col: jnp.tile(jnp.broadcast_to(col, (rc, LANE)), (1, tq // LANE))
            for r in range(tq // rc):
                rows = slice(r * rc, (r + 1) * rc)
                t = (s_scr[rows, :] + wide(cc_ref[rows, :] - lse_ref[rows, :])) - crow
                if masked:
                    t = jnp.where(lax.broadcasted_iota(jnp.int32, (rc, tq), 1)
                                  <= r * rc + lax.broadcasted_iota(jnp.int32, (rc, tq), 0), t, NEG)
                p = jnp.exp2(t)
                delta = jnp.sum(do_ref[rows, :] * o_ref[rows, :], axis=-1, keepdims=True)
                ds = p * (dp_scr[rows, :] - wide(delta))
                p_scr[rows, :] = p.astype(BF16)
                ds_scr[rows, :] = ds.astype(BF16)
                grows = pl.ds(pl.multiple_of(i * tq + r * rc, rc), rc)
                dcq_ref[grows, :] += jnp.broadcast_to(jnp.sum(ds, axis=-1, keepdims=True), (rc, LANE))
                csum = csum + _fold8(ds)
            dsb = ds_scr[...]
            dv_new = _dot(p_scr[...], dov, TN)
            dk_new = _dot(dsb, q2, TN) * (1.0 / LOG2E)
            dc_new = -jnp.sum(csum, axis=0, keepdims=True)
            rows = pl.ds(pl.multiple_of(i * tq, tq), tq)
            dq_ref[rows, :] += _dot(dsb, k, NN) * FOX_SCALE
            return dk_new, dv_new, dc_new

        @pl.when(i == j)
        def _():
            dk_new, dv_new, dc_new = update(True)
            dk_acc[...] = dk_new
            dv_acc[...] = dv_new
            dc_acc[...] = dc_new

        @pl.when(i > j)
        def _():
            dk_new, dv_new, dc_new = update(False)
            dk_acc[...] += dk_new
            dv_acc[...] += dv_new
            dc_acc[...] += dc_new

        @pl.when(i == nq - 1)
        def _():
            dk_ref[...] = dk_acc[...].astype(BF16)
            dv_ref[...] = dv_acc[...].astype(BF16)
            dc_ref[...] = dc_acc[...]

    qtile = lambda cb: pl.BlockSpec((tq, DH), lambda h, p, qi, kj, cb=cb: (qi[p], cb + h))
    ktile = lambda cb: pl.BlockSpec((tq, DH), lambda h, p, qi, kj, cb=cb: (kj[p], cb + h))
    qcol = pl.BlockSpec((None, tq, 1), lambda h, p, qi, kj: (h, qi[p], 0))
    krow = pl.BlockSpec((None, 1, tq), lambda h, p, qi, kj: (h, 0, kj[p]))
    return pl.pallas_call(
        body, name=name,
        grid_spec=pltpu.PrefetchScalarGridSpec(
            num_scalar_prefetch=2, grid=(HEADS, qi.shape[0]),
            in_specs=[qtile(FOX_Q_CB), ktile(FOX_K_CB), ktile(FOX_V_CB), qcol, krow, qtile(0), qcol, qtile(0)],
            out_specs=[pl.BlockSpec((T, DH), lambda h, p, qi, kj: (0, h)), ktile(0), ktile(0), krow,
                       pl.BlockSpec((None, T, LANE), lambda h, p, qi, kj: (h, 0, 0))],
            scratch_shapes=[pltpu.VMEM((tq, DH), F32), pltpu.VMEM((tq, DH), F32), pltpu.VMEM((1, tq), F32),
                            pltpu.VMEM((tq, tq), F32), pltpu.VMEM((tq, tq), F32), pltpu.VMEM((tq, tq), BF16),
                            pltpu.VMEM((tq, tq), BF16)]),
        out_shape=[jax.ShapeDtypeStruct((T, D_MODEL), F32), jax.ShapeDtypeStruct((T, D_MODEL), BF16),
                   jax.ShapeDtypeStruct((T, D_MODEL), BF16), jax.ShapeDtypeStruct((HEADS, 1, T), F32),
                   jax.ShapeDtypeStruct((HEADS, T, LANE), F32)],
        compiler_params=_params("parallel", "arbitrary"),
    )(qi, kj, z, z, z, c_col, c_row, o, lse, do)


MEM_SCALE = 1.0 / math.sqrt(MEM_DH)


def _mem_probs(qh, kh):
    s = _dot(qh, kh, NT) * MEM_SCALE
    p = jnp.exp(s - jnp.max(s, axis=-1, keepdims=True))
    return p / jnp.sum(p, axis=-1, keepdims=True)


def _mem_fwd(q, kv, *, name, tq=512):
    T = q.shape[0]
    tq = min(tq, T)

    def body(q_ref, kv_ref, o_ref):
        for h in range(MEM_HEADS):
            cols = slice(h * MEM_DH, (h + 1) * MEM_DH)
            vcols = slice(D_MODEL + h * MEM_DH, D_MODEL + (h + 1) * MEM_DH)
            p = _mem_probs(q_ref[:, cols], kv_ref[:, cols])
            o_ref[:, cols] = _dot(p, kv_ref[:, vcols], NN).astype(o_ref.dtype)

    return pl.pallas_call(
        body, name=name, grid=(T // tq,),
        in_specs=[pl.BlockSpec((tq, D_MODEL), lambda i: (i, 0)), pl.BlockSpec((MEM_LEN, 2 * D_MODEL), lambda i: (0, 0))],
        out_specs=pl.BlockSpec((tq, D_MODEL), lambda i: (i, 0)), out_shape=jax.ShapeDtypeStruct((T, D_MODEL), BF16),
        compiler_params=_params("parallel"),
    )(q, kv)


def _mem_bwd(q, kv, do, *, name, tq=512):
    T = q.shape[0]
    tq = min(tq, T)

    def body(q_ref, kv_ref, do_ref, dq_ref, dkv_ref):
        @pl.when(pl.program_id(0) == 0)
        def _():
            dkv_ref[...] = jnp.zeros_like(dkv_ref)

        for h in range(MEM_HEADS):
            cols = slice(h * MEM_DH, (h + 1) * MEM_DH)
            vcols = slice(D_MODEL + h * MEM_DH, D_MODEL + (h + 1) * MEM_DH)
            qh, kh, doh = q_ref[:, cols], kv_ref[:, cols], do_ref[:, cols]
            p = _mem_probs(qh, kh)
            dp = _dot(doh, kv_ref[:, vcols], NT)
            ds = p * (dp - jnp.sum(p * dp, axis=-1, keepdims=True))
            dq_ref[:, cols] = (_dot(ds, kh, NN) * MEM_SCALE).astype(dq_ref.dtype)
            dkv_ref[:, cols] += _dot(ds, qh, TN) * MEM_SCALE
            dkv_ref[:, vcols] += _dot(p, doh, TN)

    row = pl.BlockSpec((tq, D_MODEL), lambda i: (i, 0))
    full = pl.BlockSpec((MEM_LEN, 2 * D_MODEL), lambda i: (0, 0))
    return pl.pallas_call(
        body, name=name, grid=(T // tq,), in_specs=[row, full, row], out_specs=[row, full],
        out_shape=[jax.ShapeDtypeStruct((T, D_MODEL), BF16), jax.ShapeDtypeStruct((MEM_LEN, 2 * D_MODEL), F32)],
        compiler_params=_params("arbitrary"),
    )(q, kv, do)


def _adamw(w, g, m, v, *, name, tm=256):
    R, C = w.shape
    tm = min(tm, R)
    assert R % tm == 0
    nsum = g.shape[0] if g.ndim == 3 else 0

    def body(w_ref, g_ref, m_ref, v_ref, go_ref, d_ref, mo_ref, vo_ref):
        if nsum:
            gv = g_ref[0]
            for n in range(1, nsum):
                gv = gv + g_ref[n]
        else:
            gv = g_ref[...]
        mv = ADAM_B1 * m_ref[...] + (1.0 - ADAM_B1) * gv
        vv = ADAM_B2 * v_ref[...] + (1.0 - ADAM_B2) * jnp.square(gv)
        m_hat = mv / (1.0 - ADAM_B1 ** ADAM_STEP)
        v_hat = vv / (1.0 - ADAM_B2 ** ADAM_STEP)
        d_ref[...] = -ADAM_LR * (m_hat / (jnp.sqrt(v_hat) + ADAM_EPS) + ADAM_WD * w_ref[...])
        go_ref[...] = gv
        mo_ref[...] = mv
        vo_ref[...] = vv

    row = pl.BlockSpec((tm, C), lambda i: (i, 0))
    gspec = pl.BlockSpec((nsum, tm, C), lambda i: (0, i, 0)) if nsum else row
    return pl.pallas_call(
        body, name=name, grid=(R // tm,), in_specs=[row, gspec, row, row], out_specs=[row] * 4,
        out_shape=[jax.ShapeDtypeStruct((R, C), F32)] * 4, compiler_params=_params("parallel"),
    )(w, g, m, v)


def _act_mm(a, w, name, out_dtype=F32):
    return _mm([(a, w)], "nn", tm=1024, tn=512, tk=w.shape[0], out_dtypes=[out_dtype], name=name)


def _act_mm_t(a, w, name, out_dtype=F32):
    return _mm([(a, w)], "nt", tm=1024, tn=512, tk=1024, out_dtypes=[out_dtype], name=name)


def _wgrad(a, dy, name, tm=1024):
    tn = D_MODEL if dy.shape[1] % D_MODEL == 0 else D_FF // 2
    return _mm([(a, dy)], "tn", tm=tm, tn=tn, tk=1024, out_dtypes=[F32], name=name)


def _colsum8(p):
    return jnp.sum(p, axis=0, keepdims=True)


def _ffn_fwd(x, pre_g, post_g, wg, wu, wd, tag):
    h = _rms_fwd(x, pre_g, out_dtype=BF16, name=tag + "_pre")
    act, gate, up = _ffn_in(h, wg, wu, name=tag + "_in")
    d = _mm([(act, wd)], "nn", tm=1024, tn=512, tk=D_FF, out_dtypes=[F32], name=tag + "_down")
    xo = _rms_fwd(d, post_g, out_dtype=F32, name=tag + "_post", res=x, coeff=0.5)
    return xo, (h, act, gate, up, d)


def _ffn_bwd(x, dxo, saved, pre_g, post_g, wg, wu, wd, tag):
    h, act, gate, up, d = saved
    dd, dg_post = _rms_bwd(d, post_g, dxo, name=tag + "_post_b", coeff=0.5, dx_dtype=BF16)
    dgate, dup = _mm([(dd, wd)], "nt", tm=1024, tn=256, tk=D_MODEL, out_dtypes=[BF16, BF16], name=tag + "_act_b",
                     epilogue=_swiglu_bwd_epilogue, tiles=(gate, up))
    dwd = _wgrad(act, dd, tag + "_dwd", tm=D_FF // 2)
    dh = _mm([(dgate, wg), (dup, wu)], "nt", tm=512, tn=512, tk=D_FF, out_dtypes=[F32], name=tag + "_in_b")
    dwg = _wgrad(h, dgate, tag + "_dwg")
    dwu = _wgrad(h, dup, tag + "_dwu")
    dx, dg_pre = _rms_bwd(x, pre_g, dh, name=tag + "_pre_b", add=dxo)
    return dx, dict(pre_g=_colsum8(dg_pre), post_g=_colsum8(dg_post), wg=dwg, wu=dwu, wd=dwd)


def _local_step(x, mem, target, W, P):
    T = x.shape[0]
    G = {}
    logits = P["hg_lb_logits"]
    lb = _sigmoid(logits[0] - logits[1])
    lb_row = lb.reshape(1, D_MODEL)
    fbias_row = jnp.pad(P["fox_f_bias"], ((0, 0), (0, LANE - HEADS)))

    x1, ffn1_saved = _ffn_fwd(x, P["ffn1_pre_g"], P["ffn1_post_g"], W["f1g"], W["f1u"], W["f1d"], "ffn1")
    h2 = _rms_fwd(x1, P["mix_pre_g"], out_dtype=BF16, name="mix_pre")
    z = _act_mm(h2, W["w_main"], "mix_in")
    zfb = _mm([(h2, W["w_fb"])], "nn", tm=1024, tn=LANE, tk=D_MODEL, out_dtypes=[F32], name="mix_in_fb")
    oa_pre, states = _hgrn2_fwd(z, lb_row, name="hgrn2_f")
    o_a = _rms_fwd(oa_pre, P["hg_norm_g"], out_dtype=BF16, name="hgrn2_post", mul=(z, 3))
    y_a = _act_mm(o_a, W["wa"], "branch_a")
    c = _cumsum_t([(zfb, 0)], name="fox_c", width=LANE, rows=[(fbias_row, 0)], pre=lambda v, r: _logsigmoid(v + r),
                  post=lambda cum, v, r: (cum * LOG2E,))
    ct = c[:, :HEADS].T
    c_col, c_row = ct[:, :, None], ct[:, None, :]
    o_b, lse = _fox_fwd(z, c_col, c_row, name="fox_f")
    y_b = _act_mm(o_b, W["wb"], "branch_b")
    y = _gatemix_fwd(z, P["b_gate"], y_a, y_b, name="gatemix")
    m = _act_mm(y, W["wo"], "mix_out")
    x2 = _rms_fwd(m, P["mix_post_g"], out_dtype=F32, name="mix_post", res=x1)
    h3 = _rms_fwd(x2, P["mem_pre_g"], out_dtype=BF16, name="mem_pre")
    mem_n = _rms_fwd(mem, P["mem_kv_g"], out_dtype=BF16, name="mem_kvn")
    qm = _act_mm(h3, W["wmq"], "mem_q")
    kv = _act_mm(mem_n, W["wmkv"], "mem_kv")
    om = _mem_fwd(qm, kv, name="mem_attn")
    mo = _act_mm(om, W["wmo"], "mem_o")
    x3 = _rms_fwd(mo, P["mem_post_g"], out_dtype=F32, name="mem_post", res=x2)
    x4, ffn2_saved = _ffn_fwd(x3, P["ffn2_pre_g"], P["ffn2_post_g"], W["f2g"], W["f2u"], W["f2d"], "ffn2")
    dx4, sq = _loss_head(x4, target, name="loss_head")

    dx3, g = _ffn_bwd(x3, dx4, ffn2_saved, P["ffn2_pre_g"], P["ffn2_post_g"], W["f2g"], W["f2u"], W["f2d"], "ffn2")
    G.update(ffn2_pre_g=g["pre_g"], ffn2_post_g=g["post_g"], f2g=g["wg"], f2u=g["wu"], f2d=g["wd"])

    dmo, dgp = _rms_bwd(mo, P["mem_post_g"], dx3, name="mem_post_b", dx_dtype=BF16)
    G["mem_post_g"] = _colsum8(dgp)
    dom = _act_mm_t(dmo, W["wmo"], "mem_o_b", BF16)
    G["wmo"] = _wgrad(om, dmo, "mem_o_w")
    dqm, dkv = _mem_bwd(qm, kv, dom, name="mem_attn_b")
    dh3 = _act_mm_t(dqm, W["wmq"], "mem_q_b")
    G["wmq"] = _wgrad(h3, dqm, "mem_q_w")
    G["wmkv"] = _mm([(mem_n, dkv)], "tn", tm=1024, tn=512, tk=MEM_LEN, out_dtypes=[F32], name="mem_kv_w")
    dmem_n = _mm([(dkv, W["wmkv"])], "nt", tm=MEM_LEN, tn=512, tk=2 * D_MODEL, out_dtypes=[F32], name="mem_kv_b")
    _, dgp = _rms_bwd(mem, P["mem_kv_g"], dmem_n, name="mem_kvn_b")
    G["mem_kv_g"] = _colsum8(dgp)
    dx2, dgp = _rms_bwd(x2, P["mem_pre_g"], dh3, name="mem_pre_b", add=dx3)
    G["mem_pre_g"] = _colsum8(dgp)

    dm, dgp = _rms_bwd(m, P["mix_post_g"], dx2, name="mix_post_b", dx_dtype=BF16)
    G["mix_post_g"] = _colsum8(dgp)
    dy = _act_mm_t(dm, W["wo"], "mix_out_b")
    G["wo"] = _wgrad(y, dm, "mix_out_w")
    dya, dyb, dz0, dz1, s0, s1 = _gatemix_bwd(z, P["b_gate"], y_a, y_b, dy, name="gatemix_b")
    G["b_gate"] = jnp.concatenate([_colsum8(s0), _colsum8(s1)], axis=1)
    do_a = _act_mm_t(dya, W["wa"], "branch_a_b")
    G["wa"] = _wgrad(o_a, dya, "branch_a_w")
    do_b = _act_mm_t(dyb, W["wb"], "branch_b_b")
    G["wb"] = _wgrad(o_b, dyb, "branch_b_w")
    doa_pre, dgp, dga = _rms_bwd(oa_pre, P["hg_norm_g"], do_a, name="hgrn2_post_b", mul=(z, 3))
    G["hg_norm_g"] = _colsum8(dgp)
    dq_a, dfl_a, di_a, dlb = _hgrn2_bwd(z, lb_row, states, doa_pre, name="hgrn2_b")
    dl0 = (dlb * lb_row * (1.0 - lb_row)).reshape(1, HEADS, DH)
    G["hg_lb_logits"] = jnp.concatenate([dl0, -dl0], axis=0)
    dq_b, dk_b, dv_b, dcr, dcq = _fox_bwd(z, c_col, c_row, o_b, lse, do_b, name="fox_b")
    dc_pad = jnp.pad((dcr[:, 0, :] + dcq[:, :, 0]).T, ((0, 0), (0, LANE - HEADS)))
    gate_b = lambda cum, dc, zf, r: cum * _sigmoid(-(zf + r))
    dfl_b, dfb = _cumsum_t([(dc_pad, 0), (zfb, 0)], name="fox_c_b", width=LANE, reverse=True, rows=[(fbias_row, 0)],
                           pre=lambda dc, zf, r: dc, post=lambda *a: (gate_b(*a),), fold=gate_b)
    G["fox_f_bias"] = _colsum8(dfb)[:, :HEADS]

    pieces = [dq_a, dfl_a, di_a, dga, dq_b, dk_b, dv_b, dz0, dz1]
    dh2 = _mm([(dfl_b, W["w_fb"])], "nt", tm=512, tn=D_MODEL, tk=LANE, out_dtypes=[F32], name="mix_in_fb_b")
    for lo, hi in ((0, 5), (5, 9)):
        dh2 = _mm([(p, W["w_main"]) for p in pieces[lo:hi]], "nt", tm=512, tn=D_MODEL, tk=D_MODEL, out_dtypes=[F32],
                  name=f"mix_in_b{lo}", b_koff=[n * D_MODEL for n in range(lo, hi)],
                  epilogue=lambda acc, t: (acc + t,), tiles=(dh2,))
    G["w_main"] = [_wgrad(h2, p, f"mix_in_w{n}") for n, p in enumerate(pieces)]
    G["w_fb"] = _mm([(h2, dfl_b)], "tn", tm=1024, tn=LANE, tk=512, out_dtypes=[F32], name="mix_in_fb_w")
    dx1, dgp = _rms_bwd(x1, P["mix_pre_g"], dh2, name="mix_pre_b", add=dx2)
    G["mix_pre_g"] = _colsum8(dgp)

    dx0, g = _ffn_bwd(x, dx1, ffn1_saved, P["ffn1_pre_g"], P["ffn1_post_g"], W["f1g"], W["f1u"], W["f1d"], "ffn1")
    G.update(ffn1_pre_g=g["pre_g"], ffn1_post_g=g["post_g"], f1g=g["wg"], f1u=g["wu"], f1d=g["wd"])
    return sq, dx0, G


N_CHIP = 4
N_DEV = 8
IN_COLS = 9224
FB_COL = 7 * D_MODEL
SHARDED = (
    ("ffn1_w_in", (D_MODEL, 2 * D_FF), 1), ("ffn1_w_down", (D_FF, D_MODEL), 0), ("w_in", (D_MODEL, IN_COLS), 1),
    ("w_branch_a", (D_MODEL, D_MODEL), 0), ("w_branch_b", (D_MODEL, D_MODEL), 0), ("w_out", (D_MODEL, D_MODEL), 0),
    ("w_mq", (D_MODEL, D_MODEL), 0), ("w_mkv", (D_MODEL, 2 * D_MODEL), 1), ("w_mo", (D_MODEL, D_MODEL), 0),
    ("ffn2_w_in", (D_MODEL, 2 * D_FF), 1), ("ffn2_w_down", (D_FF, D_MODEL), 0),
)
SMALL = ("ffn1_pre_g", "ffn1_post_g", "mix_pre_g", "hg_norm_g", "mix_post_g", "mem_pre_g", "mem_kv_g", "mem_post_g",
         "ffn2_pre_g", "ffn2_post_g", "b_gate", "hg_lb_logits", "fox_f_bias")
SMALL_SHAPES = dict(b_gate=(1, 2 * D_MODEL), hg_lb_logits=(2, HEADS, DH), fox_f_bias=(1, HEADS))
SMALL_ROWS = 16
WEIGHT_ORDER = ("ffn1_pre_g", "ffn1_w_in", "ffn1_w_down", "ffn1_post_g", "mix_pre_g", "w_in", "hg_lb_logits", "hg_norm_g",
                "fox_f_bias", "w_branch_a", "w_branch_b", "b_gate", "w_out", "mix_post_g", "mem_pre_g", "mem_kv_g", "w_mq",
                "w_mkv", "w_mo", "mem_post_g", "ffn2_pre_g", "ffn2_w_in", "ffn2_w_down", "ffn2_post_g")


def _layout(axis):
    out, at = [], 0
    for name, shape, ax in SHARDED:
        if ax == axis:
            n = shape[ax] // N_CHIP
            out.append((name, at, n))
            at += n if axis == 0 else -(-n // LANE) * LANE
    return tuple(out), at


ROW_LAYOUT, SLAB_ROWS = _layout(0)
COL_LAYOUT, SLAB_COLS = _layout(1)


def _pack(shards, dtype):
    rows = jnp.concatenate([shards[name].astype(dtype) for name, _, _ in ROW_LAYOUT], axis=0)
    cols = []
    for name, _, n in COL_LAYOUT:
        cols.append(jnp.pad(shards[name].astype(dtype), ((0, 0), (0, -n % LANE))))
    return rows, jnp.concatenate(cols, axis=1)


def _unpack(rows, cols):
    out = {name: rows[at:at + n] for name, at, n in ROW_LAYOUT}
    out.update({name: cols[:, at:at + n] for name, at, n in COL_LAYOUT})
    return out


def _pack_small(vals):
    rows = []
    for name in SMALL:
        v = vals[name].astype(F32).reshape(-1)
        rows.append(jnp.pad(v, (0, -v.shape[0] % D_MODEL)).reshape(-1, D_MODEL))
    rows = jnp.concatenate(rows, axis=0)
    return jnp.pad(rows, ((0, SMALL_ROWS - rows.shape[0]), (0, 0)))


def _unpack_small(slab):
    out, r = {}, 0
    for name in SMALL:
        shape = SMALL_SHAPES.get(name, (1, D_MODEL))
        size = math.prod(shape)
        n = -(-size // D_MODEL)
        out[name] = slab[r:r + n].reshape(-1)[:size].reshape(shape)
        r += n
    return out


ANY = pl.BlockSpec(memory_space=pl.ANY)
MESH = pl.DeviceIdType.MESH
CHIP_FLIPS = ((0, 1), (1, 0), (1, 1))


def _place():
    x, y, c = lax.axis_index("x"), lax.axis_index("y"), lax.axis_index("c")
    chips = [(x ^ fx, y ^ fy) for fx, fy in CHIP_FLIPS]
    return x, y, c, chips


def _remote(src, dst, sems, k, dev):
    return pltpu.make_async_remote_copy(src_ref=src, dst_ref=dst, send_sem=sems[0].at[k], recv_sem=sems[1].at[k],
                                        device_id=dev, device_id_type=MESH)


def _gather_weights(wpack, tag):
    half = wpack.shape[0] // 2

    def body(w_ref, out_ref, send_sems, recv_sems):
        x, y, c, chips = _place()
        sems = (send_sems, recv_sems)
        mine, theirs = pl.ds(c * half, half), pl.ds((1 - c) * half, half)
        sent = [_remote(w_ref.at[mine], out_ref.at[k, mine], sems, k, (px, py, c)) for k, (px, py) in enumerate(chips)]
        for cp in sent:
            cp.start()
        passed = []
        for k, (px, py) in enumerate(chips):
            rows = out_ref.at[k, mine]
            _remote(rows, rows, sems, k, (px, py, c)).wait_recv()
            cp = _remote(rows, rows, sems, 3 + k, (x, y, 1 - c))
            cp.start()
            passed.append(cp)
        for k in range(3):
            rows = out_ref.at[k, theirs]
            _remote(rows, rows, sems, 3 + k, (x, y, 1 - c)).wait_recv()
        for cp in sent + passed:
            cp.wait_send()

    return pl.pallas_call(
        body, name="gather_weights_" + tag, out_shape=jax.ShapeDtypeStruct((3,) + wpack.shape, wpack.dtype),
        in_specs=[ANY], out_specs=ANY,
        scratch_shapes=[pltpu.SemaphoreType.DMA((6,)), pltpu.SemaphoreType.DMA((6,))],
    )(wpack)


def _swap_halves(g, tag):
    half = g.shape[1] // 2

    def body(g_ref, out_ref, send_sems, recv_sems):
        x, y, c, _ = _place()
        sems = (send_sems, recv_sems)
        theirs = pl.ds((1 - c) * half, half)
        sent = [_remote(g_ref.at[j, theirs], out_ref.at[j], sems, j, (x, y, 1 - c)) for j in range(N_CHIP)]
        for cp in sent:
            cp.start()
        for cp in sent:
            cp.wait_recv()
        for cp in sent:
            cp.wait_send()

    return pl.pallas_call(
        body, name="swap_halves_" + tag, out_shape=jax.ShapeDtypeStruct((N_CHIP, half, g.shape[2]), g.dtype),
        in_specs=[ANY], out_specs=ANY,
        scratch_shapes=[pltpu.SemaphoreType.DMA((N_CHIP,)), pltpu.SemaphoreType.DMA((N_CHIP,))],
    )(g)


def _pair_sum(g, got, place, tag, *, tm):
    _, half, width = got.shape
    nb = half // tm

    def body(s_ref, g_ref, a_ref, bf_ref, own_ref):
        v = g_ref[...] + a_ref[...]
        bf_ref[...] = v.astype(BF16)

        @pl.when(pl.program_id(1) == s_ref[0])
        def _():
            own_ref[...] = v

    return pl.pallas_call(
        body, name="pair_sum_" + tag,
        grid_spec=pltpu.PrefetchScalarGridSpec(
            num_scalar_prefetch=1, grid=(nb, N_CHIP),
            in_specs=[pl.BlockSpec((None, tm, width), lambda i, j, s: (j, s[1] * nb + i, 0)),
                      pl.BlockSpec((None, tm, width), lambda i, j, s: (j, i, 0))],
            out_specs=[pl.BlockSpec((None, tm, width), lambda i, j, s: (j, i, 0)),
                       pl.BlockSpec((tm, width), lambda i, j, s: (i, 0))]),
        out_shape=[jax.ShapeDtypeStruct((N_CHIP, half, width), BF16), jax.ShapeDtypeStruct((half, width), F32)],
        compiler_params=_params("arbitrary", "arbitrary"),
    )(place, g, got)


def _scatter_partials(pbf, tag):
    def body(p_ref, out_ref, send_sems, recv_sems):
        x, y, c, chips = _place()
        sems = (send_sems, recv_sems)
        sent = [_remote(p_ref.at[2 * px + py], out_ref.at[k], sems, k, (px, py, c)) for k, (px, py) in enumerate(chips)]
        for cp in sent:
            cp.start()
        for cp in sent:
            cp.wait_recv()
        for cp in sent:
            cp.wait_send()

    return pl.pallas_call(
        body, name="scatter_partials_" + tag, out_shape=jax.ShapeDtypeStruct((3,) + pbf.shape[1:], pbf.dtype),
        in_specs=[ANY], out_specs=ANY, scratch_shapes=[pltpu.SemaphoreType.DMA((3,)), pltpu.SemaphoreType.DMA((3,))],
    )(pbf)


def _chip_sum(own, got, tag, *, tm):
    half, width = own.shape

    def body(o_ref, g_ref, r_ref):
        r_ref[...] = ((o_ref[...] + g_ref[0].astype(F32)) + g_ref[1].astype(F32)) + g_ref[2].astype(F32)

    row = pl.BlockSpec((tm, width), lambda i: (i, 0))
    return pl.pallas_call(
        body, name="chip_sum_" + tag, grid=(half // tm,),
        in_specs=[row, pl.BlockSpec((3, tm, width), lambda i: (0, i, 0))], out_specs=row,
        out_shape=jax.ShapeDtypeStruct((half, width), F32), compiler_params=_params("parallel"),
    )(own, got)


def _join_halves(r, tag):
    def body(r_ref, out_ref, send_sems, recv_sems):
        x, y, c, _ = _place()
        cp = _remote(r_ref, out_ref, (send_sems, recv_sems), 0, (x, y, 1 - c))
        cp.start()
        cp.wait_recv()
        cp.wait_send()

    return pl.pallas_call(
        body, name="join_halves_" + tag, out_shape=jax.ShapeDtypeStruct(r.shape, r.dtype),
        in_specs=[ANY], out_specs=ANY,
        scratch_shapes=[pltpu.SemaphoreType.DMA((1,)), pltpu.SemaphoreType.DMA((1,))],
    )(r)


def _reduce_scatter(slabs, place, core, tag, tm):
    pbf, own = _pair_sum(slabs, _swap_halves(slabs, tag), place, tag, tm=tm)
    mine = _chip_sum(own, _scatter_partials(pbf, tag), tag, tm=tm)
    theirs = _join_halves(mine, tag)
    return jnp.where(core == 0, jnp.concatenate([mine, theirs]), jnp.concatenate([theirs, mine]))


def _gather_small(s):
    flips = [(fx, fy, fc) for fx in (0, 1) for fy in (0, 1) for fc in (0, 1)][1:]

    def body(s_ref, out_ref, send_sems, recv_sems, local_sem):
        x, y, c, _ = _place()
        sems = (send_sems, recv_sems)
        me = 4 * x + 2 * y + c
        local = pltpu.make_async_copy(s_ref, out_ref.at[me], local_sem)
        local.start()
        sent = [_remote(s_ref, out_ref.at[me], sems, k, (x ^ fx, y ^ fy, c ^ fc)) for k, (fx, fy, fc) in enumerate(flips)]
        for cp in sent:
            cp.start()
        for k, (fx, fy, fc) in enumerate(flips):
            peer = (x ^ fx, y ^ fy, c ^ fc)
            _remote(s_ref, out_ref.at[4 * peer[0] + 2 * peer[1] + peer[2]], sems, k, peer).wait_recv()
        for cp in sent:
            cp.wait_send()
        local.wait()

    return pl.pallas_call(
        body, name="gather_small", out_shape=jax.ShapeDtypeStruct((N_DEV, SMALL_ROWS, D_MODEL), s.dtype),
        in_specs=[ANY], out_specs=ANY,
        scratch_shapes=[pltpu.SemaphoreType.DMA((7,)), pltpu.SemaphoreType.DMA((7,)), pltpu.SemaphoreType.DMA],
    )(s)


def _full_weights(own, others, me):
    by_flip = [jnp.concatenate([o[None], t], axis=0) for o, t in zip(own, others)]
    per_chip = [_unpack(*[lax.dynamic_index_in_dim(s, j ^ me, 0, keepdims=False) for s in by_flip])
                for j in range(N_CHIP)]
    full = {name: jnp.concatenate([pc[name] for pc in per_chip], axis=axis) for name, _, axis in SHARDED}
    w_in = full["w_in"]
    return dict(
        f1g=full["ffn1_w_in"][:, :D_FF], f1u=full["ffn1_w_in"][:, D_FF:], f1d=full["ffn1_w_down"],
        f2g=full["ffn2_w_in"][:, :D_FF], f2u=full["ffn2_w_in"][:, D_FF:], f2d=full["ffn2_w_down"],
        w_main=jnp.concatenate([w_in[:, :FB_COL], w_in[:, FB_COL + HEADS:]], axis=1),
        w_fb=jnp.pad(w_in[:, FB_COL:FB_COL + HEADS], ((0, 0), (0, LANE - HEADS))),
        wa=full["w_branch_a"], wb=full["w_branch_b"], wo=full["w_out"],
        wmq=full["w_mq"], wmkv=full["w_mkv"], wmo=full["w_mo"],
    )


def _grad_slabs(G):
    main = jnp.concatenate(G["w_main"], axis=1)
    full = dict(
        ffn1_w_in=jnp.concatenate([G["f1g"], G["f1u"]], axis=1), ffn1_w_down=G["f1d"],
        ffn2_w_in=jnp.concatenate([G["f2g"], G["f2u"]], axis=1), ffn2_w_down=G["f2d"],
        w_in=jnp.concatenate([main[:, :FB_COL], G["w_fb"][:, :HEADS], main[:, FB_COL:]], axis=1),
        w_branch_a=G["wa"], w_branch_b=G["wb"], w_out=G["wo"], w_mq=G["wmq"], w_mkv=G["wmkv"], w_mo=G["wmo"],
    )
    rows, cols = [], []
    for j in range(N_CHIP):
        shards = {}
        for name, shape, axis in SHARDED:
            n = shape[axis] // N_CHIP
            shards[name] = lax.slice_in_dim(full[name], j * n, (j + 1) * n, axis=axis)
        r, c = _pack(shards, F32)
        rows.append(r)
        cols.append(c)
    return jnp.stack(rows, axis=0), jnp.stack(cols, axis=0)


def kernel(x, mem, ffn1_pre_g, ffn1_w_in, ffn1_w_down, ffn1_post_g, mix_pre_g, w_in, hg_lb_logits, hg_norm_g, fox_f_bias, w_branch_a, w_branch_b, b_gate, w_out, mix_post_g, mem_pre_g, mem_kv_g, w_mq, w_mkv, w_mo, mem_post_g, ffn2_pre_g, ffn2_w_in, ffn2_w_down, ffn2_post_g, loss_target, m_ffn1_pre_g, m_ffn1_w_in, m_ffn1_w_down, m_ffn1_post_g, m_mix_pre_g, m_w_in, m_hg_lb_logits, m_hg_norm_g, m_fox_f_bias, m_w_branch_a, m_w_branch_b, m_b_gate, m_w_out, m_mix_post_g, m_mem_pre_g, m_mem_kv_g, m_w_mq, m_w_mkv, m_w_mo, m_mem_post_g, m_ffn2_pre_g, m_ffn2_w_in, m_ffn2_w_down, m_ffn2_post_g, v_ffn1_pre_g, v_ffn1_w_in, v_ffn1_w_down, v_ffn1_post_g, v_mix_pre_g, v_w_in, v_hg_lb_logits, v_hg_norm_g, v_fox_f_bias, v_w_branch_a, v_w_branch_b, v_b_gate, v_w_out, v_mix_post_g, v_mem_pre_g, v_mem_kv_g, v_w_mq, v_w_mkv, v_w_mo, v_mem_post_g, v_ffn2_pre_g, v_ffn2_w_in, v_ffn2_w_down, v_ffn2_post_g):
    w = dict(ffn1_pre_g=ffn1_pre_g, ffn1_w_in=ffn1_w_in, ffn1_w_down=ffn1_w_down, ffn1_post_g=ffn1_post_g, mix_pre_g=mix_pre_g, w_in=w_in, hg_lb_logits=hg_lb_logits, hg_norm_g=hg_norm_g, fox_f_bias=fox_f_bias, w_branch_a=w_branch_a, w_branch_b=w_branch_b, b_gate=b_gate, w_out=w_out, mix_post_g=mix_post_g, mem_pre_g=mem_pre_g, mem_kv_g=mem_kv_g, w_mq=w_mq, w_mkv=w_mkv, w_mo=w_mo, mem_post_g=mem_post_g, ffn2_pre_g=ffn2_pre_g, ffn2_w_in=ffn2_w_in, ffn2_w_down=ffn2_w_down, ffn2_post_g=ffn2_post_g)
    m = dict(ffn1_pre_g=m_ffn1_pre_g, ffn1_w_in=m_ffn1_w_in, ffn1_w_down=m_ffn1_w_down, ffn1_post_g=m_ffn1_post_g, mix_pre_g=m_mix_pre_g, w_in=m_w_in, hg_lb_logits=m_hg_lb_logits, hg_norm_g=m_hg_norm_g, fox_f_bias=m_fox_f_bias, w_branch_a=m_w_branch_a, w_branch_b=m_w_branch_b, b_gate=m_b_gate, w_out=m_w_out, mix_post_g=m_mix_post_g, mem_pre_g=m_mem_pre_g, mem_kv_g=m_mem_kv_g, w_mq=m_w_mq, w_mkv=m_w_mkv, w_mo=m_w_mo, mem_post_g=m_mem_post_g, ffn2_pre_g=m_ffn2_pre_g, ffn2_w_in=m_ffn2_w_in, ffn2_w_down=m_ffn2_w_down, ffn2_post_g=m_ffn2_post_g)
    v = dict(ffn1_pre_g=v_ffn1_pre_g, ffn1_w_in=v_ffn1_w_in, ffn1_w_down=v_ffn1_w_down, ffn1_post_g=v_ffn1_post_g, mix_pre_g=v_mix_pre_g, w_in=v_w_in, hg_lb_logits=v_hg_lb_logits, hg_norm_g=v_hg_norm_g, fox_f_bias=v_fox_f_bias, w_branch_a=v_w_branch_a, w_branch_b=v_w_branch_b, b_gate=v_b_gate, w_out=v_w_out, mix_post_g=v_mix_post_g, mem_pre_g=v_mem_pre_g, mem_kv_g=v_mem_kv_g, w_mq=v_w_mq, w_mkv=v_w_mkv, w_mo=v_w_mo, mem_post_g=v_mem_post_g, ffn2_pre_g=v_ffn2_pre_g, ffn2_w_in=v_ffn2_w_in, ffn2_w_down=v_ffn2_w_down, ffn2_post_g=v_ffn2_post_g)
    sharded = [name for name, _, _ in SHARDED]
    shard_of = lambda d: {name: d[name][0] for name in sharded}

    me, core = 2 * lax.axis_index("x") + lax.axis_index("y"), lax.axis_index("c")
    w_rows, w_cols = _pack(shard_of(w), BF16)
    W = _full_weights((w_rows, w_cols), (_gather_weights(w_rows, "rows"), _gather_weights(w_cols, "cols")), me)
    P = {name: w[name] for name in SMALL}

    sq, dx0, G = _local_step(x[0], mem[0], loss_target[0], W, P)
    loss = lax.psum(0.5 * jnp.sum(sq) / D_MODEL, ("x", "y", "c"))

    place = jnp.stack([me, core]).astype(jnp.int32)
    g_rows, g_cols = _grad_slabs(G)
    g_shards = _unpack(_reduce_scatter(g_rows, place, core, "rows", SLAB_ROWS // 8),
                       _reduce_scatter(g_cols, place, core, "cols", D_MODEL // 8))
    big = {}
    for name, shape, axis in SHARDED:
        rows = shape[0] // (N_CHIP if axis == 0 else 1)
        big[name] = _adamw(w[name][0], g_shards[name], m[name][0], v[name][0], name="adamw_" + name, tm=rows // 8)
    small = _adamw(_pack_small(w), _gather_small(_pack_small(G)), _pack_small(m), _pack_small(v), name="adamw_small",
                   tm=SMALL_ROWS)

    outs = [loss, dx0[None]]
    for n in range(4):
        vals = {name: res[n][None] for name, res in big.items()}
        vals.update(_unpack_small(small[n]))
        outs += [vals[name] for name in WEIGHT_ORDER]
    return tuple(outs)
```

```python
import functools
import math

import jax
import jax.numpy as jnp
from jax import lax
from jax.experimental import pallas as pl
from jax.experimental.pallas import tpu as pltpu

F32 = jnp.float32
BF16 = jnp.bfloat16

D_MODEL = 1024
D_FF = 2816
HEADS = 8
DH = 128
MEM_HEADS = 4
MEM_DH = 256
MEM_LEN = 256
EPS = 1e-6
SUB = 16
LANE = 128
SUBLANE = 8
VMEM_LIMIT = 56 * 1024 * 1024

ADAM_LR = 0.001
ADAM_B1 = 0.9
ADAM_B2 = 0.999
ADAM_EPS = 1e-08
ADAM_WD = 0.01
ADAM_STEP = 10

HIGHEST = lax.Precision.HIGHEST


def _params(*sem):
    return pltpu.CompilerParams(dimension_semantics=sem, vmem_limit_bytes=VMEM_LIMIT)


def _sigmoid(v):
    return 1.0 / (1.0 + jnp.exp(-v))


def _silu(v):
    return v * _sigmoid(v)


def _dsilu(v):
    s = _sigmoid(v)
    return s * (1.0 + v * (1.0 - s))


def _dot(a, b, dims):
    return lax.dot_general(a.astype(BF16), b.astype(BF16), (dims, ((), ())), preferred_element_type=F32)


NN = ((1,), (0,))
NT = ((1,), (1,))
TN = ((0,), (0,))


def _mm(pairs, mode, *, tm, tn, tk, out_dtypes, name, epilogue=None, tiles=(), b_koff=None):
    a0, b0 = pairs[0]
    if mode == "nn":
        (M, K), N = a0.shape, b0.shape[1]
    elif mode == "nt":
        (M, K), N = a0.shape, b0.shape[0]
    else:
        (K, M), N = a0.shape, b0.shape[1]
    tm, tn, tk = min(tm, M), min(tn, N), min(tk, K)
    assert M % tm == 0 and N % tn == 0 and K % tk == 0, (name, M, N, K, tm, tn, tk)
    nk = K // tk
    npair = len(pairs)
    koff = [0] * npair if b_koff is None else [o // tk for o in b_koff]
    if b_koff is not None:
        assert all(o % tk == 0 for o in b_koff)
    in_specs, args = [], []
    for p, (a, b) in enumerate(pairs):
        if mode == "nn":
            sa = pl.BlockSpec((tm, tk), lambda i, j, k: (i, k))
            sb = pl.BlockSpec((tk, tn), lambda i, j, k, o=koff[p]: (k + o, j))
            dims = NN
        elif mode == "nt":
            sa = pl.BlockSpec((tm, tk), lambda i, j, k: (i, k))
            sb = pl.BlockSpec((tn, tk), lambda i, j, k, o=koff[p]: (j, k + o))
            dims = NT
        else:
            sa = pl.BlockSpec((tk, tm), lambda i, j, k: (k, i))
            sb = pl.BlockSpec((tk, tn), lambda i, j, k, o=koff[p]: (k + o, j))
            dims = TN
        in_specs += [sa, sb]
        args += [a, b]
    for t in tiles:
        in_specs.append(pl.BlockSpec((tm, tn), lambda i, j, k: (i, j)))
        args.append(t)
    nt_ = len(tiles)
    nout = len(out_dtypes)

    def body(*refs):
        ab = refs[: 2 * npair]
        tl = refs[2 * npair: 2 * npair + nt_]
        outs = refs[2 * npair + nt_: 2 * npair + nt_ + nout]
        acc_ref = refs[-1] if nk > 1 else None

        def partial_sum():
            s = _dot(ab[0][...], ab[1][...], dims)
            for p in range(1, npair):
                s = s + _dot(ab[2 * p][...], ab[2 * p + 1][...], dims)
            return s

        def finish(acc):
            res = (acc,) if epilogue is None else epilogue(acc, *[t[...] for t in tl])
            for o, r in zip(outs, res):
                o[...] = r.astype(o.dtype)

        if nk == 1:
            finish(partial_sum())
        else:
            k = pl.program_id(2)

            @pl.when(k == 0)
            def _():
                acc_ref[...] = jnp.zeros_like(acc_ref)

            acc_ref[...] += partial_sum()

            @pl.when(k == nk - 1)
            def _():
                finish(acc_ref[...])

    out_shape = [jax.ShapeDtypeStruct((M, N), dt) for dt in out_dtypes]
    out_specs = [pl.BlockSpec((tm, tn), lambda i, j, k: (i, j)) for _ in out_dtypes]
    res = pl.pallas_call(
        body, name=name, grid=(M // tm, N // tn, nk), in_specs=in_specs, out_specs=out_specs, out_shape=out_shape,
        scratch_shapes=[pltpu.VMEM((tm, tn), F32)] if nk > 1 else [],
        compiler_params=_params("parallel", "parallel", "arbitrary"),
    )(*args)
    return res[0] if nout == 1 else res


def _col(arr, tm, width, cb):
    return pl.BlockSpec((tm, width), lambda i, cb=cb: (i, cb))


def _rms_fwd(x, g, *, out_dtype, name, mul=None, res=None, coeff=1.0, tm=512):
    T, D = x.shape
    tm = min(tm, T)
    args, in_specs = [x, g], [pl.BlockSpec((tm, D), lambda i: (i, 0)), pl.BlockSpec((1, D), lambda i: (0, 0))]
    if mul is not None:
        args.append(mul[0])
        in_specs.append(_col(mul[0], tm, D, mul[1]))
    if res is not None:
        args.append(res)
        in_specs.append(pl.BlockSpec((tm, D), lambda i: (i, 0)))

    def body(*refs):
        xv = refs[0][...].astype(F32)
        r = lax.rsqrt(jnp.mean(xv * xv, axis=-1, keepdims=True) + EPS)
        y = (xv * r) * refs[1][...]
        n = 2
        if mul is not None:
            y = y * _silu(refs[n][...])
            n += 1
        if res is not None:
            y = refs[n][...] + coeff * y
        refs[-1][...] = y.astype(out_dtype)

    return pl.pallas_call(
        body, name=name, grid=(T // tm,), in_specs=in_specs, out_specs=pl.BlockSpec((tm, D), lambda i: (i, 0)),
        out_shape=jax.ShapeDtypeStruct((T, D), out_dtype), compiler_params=_params("parallel"),
    )(*args)


def _fold8(v):
    tm, d = v.shape
    return v.reshape(tm // SUBLANE, SUBLANE, d).sum(axis=0)


def _rms_bwd(x, g, dy, *, name, coeff=1.0, add=None, mul=None, dx_dtype=F32, tm=512):
    T, D = x.shape
    tm = min(tm, T)
    row = pl.BlockSpec((tm, D), lambda i: (i, 0))
    args, in_specs = [x, g, dy], [row, pl.BlockSpec((1, D), lambda i: (0, 0)), row]
    if add is not None:
        args.append(add)
        in_specs.append(row)
    if mul is not None:
        args.append(mul[0])
        in_specs.append(_col(mul[0], tm, D, mul[1]))
    nin = len(args)

    def body(*refs):
        xv = refs[0][...].astype(F32)
        gv = refs[1][...]
        dyv = refs[2][...].astype(F32) * coeff
        r = lax.rsqrt(jnp.mean(xv * xv, axis=-1, keepdims=True) + EPS)
        nrm = xv * r
        n = 3
        addv = None
        if add is not None:
            addv = refs[n][...]
            n += 1
        if mul is not None:
            mv = refs[n][...]
            sm = _silu(mv)
            refs[nin + 2][...] = (dyv * nrm * gv * _dsilu(mv)).astype(refs[nin + 2].dtype)
            dyv = dyv * sm
        dn = dyv * gv
        dx = r * (dn - nrm * jnp.mean(dn * nrm, axis=-1, keepdims=True))
        if addv is not None:
            dx = dx + addv
        refs[nin][...] = dx.astype(dx_dtype)
        dg_ref = refs[nin + 1]

        @pl.when(pl.program_id(0) == 0)
        def _():
            dg_ref[...] = jnp.zeros_like(dg_ref)

        dg_ref[...] += _fold8(dyv * nrm)

    out_shape = [jax.ShapeDtypeStruct((T, D), dx_dtype), jax.ShapeDtypeStruct((SUBLANE, D), F32)]
    out_specs = [row, pl.BlockSpec((SUBLANE, D), lambda i: (0, 0))]
    if mul is not None:
        out_shape.append(jax.ShapeDtypeStruct((T, D), BF16))
        out_specs.append(row)
    return pl.pallas_call(
        body, name=name, grid=(T // tm,), in_specs=in_specs, out_specs=out_specs, out_shape=out_shape,
        compiler_params=_params("arbitrary"),
    )(*args)


def _ffn_in(h, wg, wu, *, name, tm=1024, tn=256):
    T, D = h.shape
    F = wg.shape[1]
    tm = min(tm, T)
    assert F % tn == 0

    def body(h_ref, wg_ref, wu_ref, a_ref, g_ref, u_ref):
        hv = h_ref[...]
        gt = _dot(hv, wg_ref[...], NN)
        up = _dot(hv, wu_ref[...], NN)
        a_ref[...] = (_silu(gt) * up).astype(BF16)
        g_ref[...] = gt.astype(BF16)
        u_ref[...] = up.astype(BF16)

    o = pl.BlockSpec((tm, tn), lambda i, j: (i, j))
    w = pl.BlockSpec((D, tn), lambda i, j: (0, j))
    return pl.pallas_call(
        body, name=name, grid=(T // tm, F // tn), in_specs=[pl.BlockSpec((tm, D), lambda i, j: (i, 0)), w, w],
        out_specs=[o, o, o], out_shape=[jax.ShapeDtypeStruct((T, F), BF16)] * 3,
        compiler_params=_params("parallel", "parallel"),
    )(h, wg, wu)


def _swiglu_bwd_epilogue(da, gt, up):
    gt = gt.astype(F32)
    up = up.astype(F32)
    return da * up * _dsilu(gt), da * _silu(gt)


GATE_CB = 7


def _gatemix_fwd(z, b_gate, ya, yb, *, name, tm=512):
    T, D = ya.shape
    tm = min(tm, T)
    row = pl.BlockSpec((tm, D), lambda i: (i, 0))

    def body(z0, z1, b0, b1, ya_ref, yb_ref, y_ref):
        g0 = _sigmoid(z0[...] + b0[...])
        g1 = _sigmoid(z1[...] + b1[...])
        y_ref[...] = (g0 * ya_ref[...] + g1 * yb_ref[...]).astype(y_ref.dtype)

    bs = lambda c: pl.BlockSpec((1, D), lambda i, c=c: (0, c))
    return pl.pallas_call(
        body, name=name, grid=(T // tm,),
        in_specs=[_col(z, tm, D, GATE_CB), _col(z, tm, D, GATE_CB + 1), bs(0), bs(1), row, row],
        out_specs=row, out_shape=jax.ShapeDtypeStruct((T, D), BF16), compiler_params=_params("parallel"),
    )(z, z, b_gate, b_gate, ya, yb)


def _gatemix_bwd(z, b_gate, ya, yb, dy, *, name, tm=512):
    T, D = ya.shape
    tm = min(tm, T)
    row = pl.BlockSpec((tm, D), lambda i: (i, 0))
    part = pl.BlockSpec((SUBLANE, D), lambda i: (0, 0))

    def body(z0, z1, b0, b1, ya_ref, yb_ref, dy_ref, dya, dyb, dz0, dz1, s0, s1):
        g0 = _sigmoid(z0[...] + b0[...])
        g1 = _sigmoid(z1[...] + b1[...])
        dyv = dy_ref[...]
        dya[...] = (dyv * g0).astype(BF16)
        dyb[...] = (dyv * g1).astype(BF16)
        d0 = dyv * ya_ref[...] * (g0 * (1.0 - g0))
        d1 = dyv * yb_ref[...] * (g1 * (1.0 - g1))
        dz0[...] = d0.astype(BF16)
        dz1[...] = d1.astype(BF16)

        @pl.when(pl.program_id(0) == 0)
        def _():
            s0[...] = jnp.zeros_like(s0)
            s1[...] = jnp.zeros_like(s1)

        s0[...] += _fold8(d0)
        s1[...] += _fold8(d1)

    bs = lambda c: pl.BlockSpec((1, D), lambda i, c=c: (0, c))
    act = jax.ShapeDtypeStruct((T, D), BF16)
    ps = jax.ShapeDtypeStruct((SUBLANE, D), F32)
    return pl.pallas_call(
        body, name=name, grid=(T // tm,),
        in_specs=[_col(z, tm, D, GATE_CB), _col(z, tm, D, GATE_CB + 1), bs(0), bs(1), row, row, row],
        out_specs=[row, row, row, row, part, part], out_shape=[act, act, act, act, ps, ps],
        compiler_params=_params("arbitrary"),
    )(z, z, b_gate, b_gate, ya, yb, dy)


def _loss_head(x, target, *, name, tm=512):
    T, D = x.shape
    tm = min(tm, T)
    row = pl.BlockSpec((tm, D), lambda i: (i, 0))

    def body(x_ref, t_ref, dx_ref, s_ref):
        e = x_ref[...] - t_ref[...]
        dx_ref[...] = e * (1.0 / D)

        @pl.when(pl.program_id(0) == 0)
        def _():
            s_ref[...] = jnp.zeros_like(s_ref)

        s_ref[...] += _fold8(e * e)

    return pl.pallas_call(
        body, name=name, grid=(T // tm,), in_specs=[row, row],
        out_specs=[row, pl.BlockSpec((SUBLANE, D), lambda i: (0, 0))],
        out_shape=[jax.ShapeDtypeStruct((T, D), F32), jax.ShapeDtypeStruct((SUBLANE, D), F32)],
        compiler_params=_params("arbitrary"),
    )(x, target)


def _tri(n, reverse):
    r = lax.broadcasted_iota(jnp.int32, (n, n), 0)
    c = lax.broadcasted_iota(jnp.int32, (n, n), 1)
    return jnp.where((c >= r) if reverse else (c <= r), 1.0, 0.0).astype(F32)


def _cumsum_t(xs, *, name, width, pre, reverse=False, rows=(), post=None, out_dtypes=(F32,), fold=None, tb=256):
    T = xs[0][0].shape[0]
    tb = min(tb, T)
    nb = T // tb
    tblk = (lambda i: nb - 1 - i) if reverse else (lambda i: i)
    args = [a for a, _ in xs] + [a for a, _ in rows]
    in_specs = [pl.BlockSpec((tb, width), lambda i, cb=cb: (tblk(i), cb)) for _, cb in xs]
    in_specs += [pl.BlockSpec((1, width), lambda i, cb=cb: (0, cb)) for _, cb in rows]
    nin, nout = len(args), len(out_dtypes)

    def body(*refs):
        vals = [r[...] for r in refs[:nin]]
        outs = refs[nin:nin + nout]
        carry = refs[-1]
        first = pl.program_id(0) == 0

        @pl.when(first)
        def _():
            carry[...] = jnp.zeros_like(carry)

        cum = jnp.dot(_tri(tb, reverse), pre(*vals), precision=HIGHEST, preferred_element_type=F32) + carry[...]
        carry[...] = cum[0:1, :] if reverse else cum[tb - 1:tb, :]
        res = (cum,) if post is None else post(cum, *vals)
        for o, r in zip(outs, res):
            o[...] = r.astype(o.dtype)
        if fold is not None:
            f_ref = refs[nin + nout]

            @pl.when(first)
            def _():
                f_ref[...] = jnp.zeros_like(f_ref)

            f_ref[...] += _fold8(fold(cum, *vals))

    tspec = pl.BlockSpec((tb, width), lambda i: (tblk(i), 0))
    out_shape = [jax.ShapeDtypeStruct((T, width), dt) for dt in out_dtypes]
    out_specs = [tspec] * nout
    if fold is not None:
        out_shape.append(jax.ShapeDtypeStruct((SUBLANE, width), F32))
        out_specs.append(pl.BlockSpec((SUBLANE, width), lambda i: (0, 0)))
    res = pl.pallas_call(
        body, name=name, grid=(nb,), in_specs=in_specs, out_specs=out_specs, out_shape=out_shape,
        scratch_shapes=[pltpu.VMEM((1, width), F32)], compiler_params=_params("arbitrary"),
    )(*args)
    return res[0] if len(res) == 1 else res


def _logsigmoid(v):
    return jnp.minimum(v, 0.0) - jnp.log(1.0 + jnp.exp(-jnp.abs(v)))


HG_TB = 256
HG_HB = 4
HG_W = HG_HB * DH
HG_GROUPS = HEADS // HG_HB
HG_Q_CB, HG_F_CB, HG_I_CB = 0, HG_GROUPS, 2 * HG_GROUPS
NEG = -1e30


def _scan16(x, rowid, reverse=False):
    for k in [1 << n for n in range(SUB.bit_length() - 1)]:
        if reverse:
            x = x + jnp.where(rowid < SUB - k, pltpu.roll(x, SUB - k, 0), 0.0)
        else:
            x = x + jnp.where(rowid >= k, pltpu.roll(x, k, 0), 0.0)
    return x


def _hg_block(q_ref, f_ref, i_ref, lb_ref, rows, cols, rowid):
    lb = lb_ref[:, cols]
    qr = q_ref[rows, cols]
    sg = _sigmoid(f_ref[rows, cols])
    f = lb + (1.0 - lb) * sg
    b = _scan16(jnp.log(f), rowid)
    return _silu(qr), 1.0 - f, i_ref[rows, cols], b, qr, sg, f, lb


def _hg_specs(tb, tmap):
    return [pl.BlockSpec((tb, HG_W), lambda g, t: (tmap(t), HG_Q_CB + g)),
            pl.BlockSpec((tb, HG_W), lambda g, t: (tmap(t), HG_F_CB + g)),
            pl.BlockSpec((tb, HG_W), lambda g, t: (tmap(t), HG_I_CB + g)),
            pl.BlockSpec((1, HG_W), lambda g, t: (0, g))]


def _hgrn2_fwd(z, lb_row, *, name):
    T = z.shape[0]
    tb = min(HG_TB, T)
    nb, nsub = T // tb, tb // SUB

    def body(q_ref, f_ref, i_ref, lb_ref, o_ref, st_ref, state):
        @pl.when(pl.program_id(1) == 0)
        def _():
            state[...] = jnp.zeros_like(state)

        rowid = lax.broadcasted_iota(jnp.int32, (SUB, DH), 0)

        def step(c, carry):
            rows = pl.ds(pl.multiple_of(c * SUB, SUB), SUB)
            for hh in range(HG_HB):
                cols = slice(hh * DH, (hh + 1) * DH)
                q, k, iv, b = _hg_block(q_ref, f_ref, i_ref, lb_ref, rows, cols, rowid)[:4]
                bl = b[SUB - 1:SUB, :]
                sv = state[hh]
                st_ref[c, hh] = sv
                o = _dot(q * jnp.exp(b), sv, NT)
                for s in range(SUB):
                    e = jnp.exp(jnp.where(rowid >= s, b - b[s:s + 1, :], NEG))
                    a = jnp.sum(q * e * k[s:s + 1, :], axis=-1, keepdims=True)
                    o = o + a * iv[s:s + 1, :]
                o_ref[rows, cols] = o
                state[hh] = sv * jnp.exp(bl) + _dot(iv, k * jnp.exp(bl - b), TN)
            return carry

        lax.fori_loop(0, nsub, step, 0)

    return pl.pallas_call(
        body, name=name, grid=(HG_GROUPS, nb), in_specs=_hg_specs(tb, lambda t: t),
        out_specs=[pl.BlockSpec((tb, HG_W), lambda g, t: (t, g)),
                   pl.BlockSpec((nsub, HG_HB, DH, DH), lambda g, t: (t, g, 0, 0))],
        out_shape=[jax.ShapeDtypeStruct((T, D_MODEL), F32), jax.ShapeDtypeStruct((T // SUB, HEADS, DH, DH), F32)],
        scratch_shapes=[pltpu.VMEM((HG_HB, DH, DH), F32)], compiler_params=_params("parallel", "arbitrary"),
    )(z, z, z, lb_row)


def _hgrn2_bwd(z, lb_row, states, do, *, name):
    T = z.shape[0]
    tb = min(HG_TB, T)
    nb, nsub = T // tb, tb // SUB
    rev = lambda t: nb - 1 - t

    def body(q_ref, f_ref, i_ref, lb_ref, st_ref, do_ref, dq_ref, dfl_ref, di_ref, dlb_ref, dstate, later):
        @pl.when(pl.program_id(1) == 0)
        def _():
            dstate[...] = jnp.zeros_like(dstate)
            later[...] = jnp.zeros_like(later)
            dlb_ref[...] = jnp.zeros_like(dlb_ref)

        rowid = lax.broadcasted_iota(jnp.int32, (SUB, DH), 0)

        def step(cc, carry):
            c = nsub - 1 - cc
            rows = pl.ds(pl.multiple_of(c * SUB, SUB), SUB)
            for hh in range(HG_HB):
                cols = slice(hh * DH, (hh + 1) * DH)
                q, k, iv, b, qr, sg, f, lb = _hg_block(q_ref, f_ref, i_ref, lb_ref, rows, cols, rowid)
                bl = b[SUB - 1:SUB, :]
                eb, ebl = jnp.exp(b), jnp.exp(bl - b)
                sv, dsv = st_ref[c, hh], dstate[hh]
                dov = do_ref[rows, cols]
                dq = _dot(dov, sv, NN) * eb
                dk = _dot(iv, dsv, NN) * ebl
                di = _dot(k * ebl, dsv, NT)
                for s in range(SUB):
                    e = jnp.exp(jnp.where(rowid >= s, b - b[s:s + 1, :], NEG))
                    ks, isv = k[s:s + 1, :], iv[s:s + 1, :]
                    qe = q * e
                    a = jnp.sum(qe * ks, axis=-1, keepdims=True)
                    p = jnp.sum(dov * isv, axis=-1, keepdims=True)
                    dq = dq + p * (e * ks)
                    dks = jnp.sum(p * qe, axis=0, keepdims=True)
                    dis = jnp.sum(a * dov, axis=0, keepdims=True)
                    dk = dk + jnp.where(rowid == s, dks, 0.0)
                    di = di + jnp.where(rowid == s, dis, 0.0)
                dlogf = _scan16(q * dq - k * dk, rowid, reverse=True) + later[hh]
                df = dlogf / f - dk
                dlb_ref[:, cols] += jnp.sum(df * (1.0 - sg), axis=0, keepdims=True)
                dfl_ref[rows, cols] = (df * (1.0 - lb) * (sg * (1.0 - sg))).astype(BF16)
                dq_ref[rows, cols] = (dq * _dsilu(qr)).astype(BF16)
                di_ref[rows, cols] = di.astype(BF16)
                dnew = dsv * jnp.exp(bl) + _dot(dov, q * eb, TN)
                dstate[hh] = dnew
                later[hh] = jnp.sum(dnew * sv, axis=0, keepdims=True)
            return carry

        lax.fori_loop(0, nsub, step, 0)

    tile = pl.BlockSpec((tb, HG_W), lambda g, t: (rev(t), g))
    act = jax.ShapeDtypeStruct((T, D_MODEL), BF16)
    return pl.pallas_call(
        body, name=name, grid=(HG_GROUPS, nb),
        in_specs=_hg_specs(tb, rev) + [pl.BlockSpec((nsub, HG_HB, DH, DH), lambda g, t: (rev(t), g, 0, 0)), tile],
        out_specs=[tile, tile, tile, pl.BlockSpec((1, HG_W), lambda g, t: (0, g))],
        out_shape=[act, act, act, jax.ShapeDtypeStruct((1, D_MODEL), F32)],
        scratch_shapes=[pltpu.VMEM((HG_HB, DH, DH), F32), pltpu.VMEM((HG_HB, 1, DH), F32)],
        compiler_params=_params("parallel", "arbitrary"),
    )(z, z, z, lb_row, states, do)


FOX_Q_CB, FOX_K_CB, FOX_V_CB = 4 * HEADS, 5 * HEADS, 6 * HEADS
FOX_SCALE = 1.0 / math.sqrt(DH)


def _fox_tile(T):
    return 512 if T >= 2048 else 128


def _fox_pairs(nq, by_query):
    if by_query:
        pairs = [(i, j) for i in range(nq) for j in range(i + 1)]
    else:
        pairs = [(i, j) for j in range(nq) for i in range(j, nq)]
    return (jnp.asarray([p[0] for p in pairs], jnp.int32), jnp.asarray([p[1] for p in pairs], jnp.int32))


LOG2E = 1.4426950408889634
FOX_RC = 64


def _fox_q2(q):
    return (q * (FOX_SCALE * LOG2E)).astype(BF16)


FOX_ZERO = -200.0


def _fox_norms(z, *, name):
    T = z.shape[0]
    tq = _fox_tile(T)
    nq = T // tq

    def body(q_ref, k_ref, nq_ref, nk_ref):
        head_of_col = lax.broadcasted_iota(jnp.int32, (D_MODEL, LANE), 0) // DH
        pick = jnp.where(head_of_col == lax.broadcasted_iota(jnp.int32, (D_MODEL, LANE), 1), 1.0, 0.0).astype(BF16)

        def tile_max(v):
            v = v.astype(F32)
            sq = _dot(v * v, pick, NN)
            return jnp.broadcast_to(jnp.max(jnp.sqrt(sq), axis=0, keepdims=True), (SUBLANE, LANE))

        nq_ref[...] = tile_max(_fox_q2(q_ref[...]))
        nk_ref[...] = tile_max(k_ref[...].astype(BF16))

    out = jax.ShapeDtypeStruct((nq * SUBLANE, LANE), F32)
    spec = pl.BlockSpec((SUBLANE, LANE), lambda i: (i, 0))
    a, b = pl.pallas_call(
        body, name=name, grid=(nq,),
        in_specs=[pl.BlockSpec((tq, D_MODEL), lambda i: (i, FOX_Q_CB // HEADS)),
                  pl.BlockSpec((tq, D_MODEL), lambda i: (i, FOX_K_CB // HEADS))],
        out_specs=[spec, spec], out_shape=[out, out], compiler_params=_params("parallel"),
    )(z, z)
    return a[::SUBLANE, :HEADS], b[::SUBLANE, :HEADS]


def _fox_schedule(norm_q, norm_k, ct, tq):
    nq = ct.shape[1] // tq
    first, last = ct[:, ::tq], ct[:, tq - 1::tq]
    nqh, nkh = norm_q.T * 1.05, norm_k.T * 1.05
    bound = nqh[:, :, None] * (nkh[:, None, :] + nkh[:, :, None]) + first[:, :, None] - last[:, None, :]
    tri = jnp.arange(nq)[:, None] > jnp.arange(nq)[None, :]
    drop = (bound < FOX_ZERO) & tri[None]
    lo = jnp.argmin(drop, axis=2).astype(jnp.int32)
    dropped = jnp.arange(nq)[None, None, :] < lo[:, :, None]
    qf, kf = _fox_pairs(nq, by_query=True)
    qb, kb = _fox_pairs(nq, by_query=False)
    fetch_k = jnp.maximum(kf[None, :], lo[:, qf])
    kept_q = jnp.where(dropped | ~(tri | jnp.eye(nq, dtype=bool))[None], -1, jnp.arange(nq)[None, :, None])
    last_kept = lax.cummax(kept_q, axis=1)
    fetch_q = last_kept[:, qb, kb]
    i32 = lambda a: a.astype(jnp.int32)
    return i32(fetch_k), i32(dropped[:, qf, kf]), i32(fetch_q), i32(dropped[:, qb, kb])


def _fox_fwd(z, c_col, c_row, fetch_k, skip, *, name):
    T = z.shape[0]
    tq = _fox_tile(T)
    nq = T // tq
    rc = min(FOX_RC, tq)

    qi, kj = _fox_pairs(nq, by_query=True)

    def body(qi_ref, kj_ref, fk_ref, skip_ref, q_ref, k_ref, v_ref, cc_ref, cr_ref, o_ref, lse_ref, m_scr, l_scr, acc,
             a_scr, s_scr, p_scr):
        p_id = pl.program_id(1)
        i, j = qi_ref[p_id], kj_ref[p_id]
        live = skip_ref[pl.program_id(0), p_id] == 0

        @pl.when(j == 0)
        def _():
            m_scr[...] = jnp.full_like(m_scr, NEG)
            l_scr[...] = jnp.zeros_like(l_scr)
            acc[...] = jnp.zeros_like(acc)

        def update(masked):
            bias = cc_ref[0:1, :] - cr_ref[...]
            s_scr[...] = _dot(_fox_q2(q_ref[...]), k_ref[...], NT)
            for r in range(tq // rc):
                rows = slice(r * rc, (r + 1) * rc)
                t = s_scr[rows, :] + bias
                if masked:
                    t = jnp.where(lax.broadcasted_iota(jnp.int32, (rc, tq), 1)
                                  <= r * rc + lax.broadcasted_iota(jnp.int32, (rc, tq), 0), t, NEG)
                m_old = m_scr[rows, :]
                m_new = jnp.maximum(m_old, jnp.max(t, axis=-1, keepdims=True))
                alpha = jnp.exp2(m_old - m_new)
                p = jnp.exp2(t - jnp.tile(m_new, (1, tq // LANE)))
                l_scr[rows, :] = alpha * l_scr[rows, :] + jnp.sum(p, axis=-1, keepdims=True)
                a_scr[rows, :] = alpha
                p_scr[rows, :] = p.astype(BF16)
                m_scr[rows, :] = m_new
            acc[...] = a_scr[...] * acc[...] + _dot(p_scr[...], v_ref[...], NN)

        @pl.when((j < i) & live)
        def _():
            update(False)

        @pl.when(j == i)
        def _():
            update(True)
            o_ref[...] = acc[...] / l_scr[...]
            lse_ref[...] = (m_scr[:, 0:1] + jnp.log2(l_scr[:, 0:1])) + (cc_ref[...] - cc_ref[0:1, :])

    qtile = lambda cb: pl.BlockSpec((tq, DH), lambda h, p, qi, kj, fk, sk, cb=cb: (qi[p], cb + h))
    ktile = lambda cb: pl.BlockSpec((tq, DH), lambda h, p, qi, kj, fk, sk, cb=cb: (fk[h, p], cb + h))
    qcol = pl.BlockSpec((None, tq, 1), lambda h, p, qi, kj, fk, sk: (h, qi[p], 0))
    return pl.pallas_call(
        body, name=name,
        grid_spec=pltpu.PrefetchScalarGridSpec(
            num_scalar_prefetch=4, grid=(HEADS, qi.shape[0]),
            in_specs=[qtile(FOX_Q_CB), ktile(FOX_K_CB), ktile(FOX_V_CB), qcol,
                      pl.BlockSpec((None, 1, tq), lambda h, p, qi, kj, fk, sk: (h, 0, fk[h, p]))],
            out_specs=[qtile(0), qcol],
            scratch_shapes=[pltpu.VMEM((tq, LANE), F32), pltpu.VMEM((tq, LANE), F32), pltpu.VMEM((tq, DH), F32),
                            pltpu.VMEM((tq, LANE), F32), pltpu.VMEM((tq, tq), F32), pltpu.VMEM((tq, tq), BF16)]),
        out_shape=[jax.ShapeDtypeStruct((T, D_MODEL), F32), jax.ShapeDtypeStruct((HEADS, T, 1), F32)],
        compiler_params=_params("parallel", "arbitrary"),
    )(qi, kj, fetch_k, skip, z, z, z, c_col, c_row)


def _fox_bwd(z, c_col, c_row, o, lse, do, fetch_q, skip, *, name):
    T = z.shape[0]
    tq = _fox_tile(T)
    nq = T // tq
    rc = min(FOX_RC, tq)

    qi, kj = _fox_pairs(nq, by_query=False)

    def body(qi_ref, kj_ref, fq_ref, skip_ref, q_ref, k_ref, v_ref, cc_ref, cr_ref, o_ref, lse_ref, do_ref, dq_ref,
             dk_ref, dv_ref, dc_ref, dcq_ref, dk_acc, dv_acc, dc_acc, s_scr, dp_scr, p_scr, ds_scr, dcq_scr):
        p_id = pl.program_id(1)
        i, j = qi_ref[p_id], kj_ref[p_id]
        live = skip_ref[pl.program_id(0), p_id] == 0

        @pl.when(p_id == 0)
        def _():
            dq_ref[...] = jnp.zeros_like(dq_ref)
            dcq_scr[...] = jnp.zeros_like(dcq_scr)

        def update(masked):
            q2, k, dov = _fox_q2(q_ref[...]), k_ref[...], do_ref[...]
            s_scr[...] = _dot(q2, k, NT)
            dp_scr[...] = _dot(dov, v_ref[...], NT)
            crow = cr_ref[...]
            csum = jnp.zeros((SUBLANE, tq), F32)
            wide = lambda col: jnp.tile(jnp.broadcast_to(col, (rc, LANE)), (1, tq // LANE))
            for r in range(tq // rc):
                rows = slice(r * rc, (r + 1) * rc)
                t = (s_scr[rows, :] + wide(cc_ref[rows, :] - lse_ref[rows, :])) - crow
                if masked:
                    t = jnp.where(lax.broadcasted_iota(jnp.int32, (rc, tq), 1)
                                  <= r * rc + lax.broadcasted_iota(jnp.int32, (rc, tq), 0), t, NEG)
                p = jnp.exp2(t)
                delta = jnp.sum(do_ref[rows, :] * o_ref[rows, :], axis=-1, keepdims=True)
                ds = p * (dp_scr[rows, :] - wide(delta))
                p_scr[rows, :] = p.astype(BF16)
                ds_scr[rows, :] = ds.astype(BF16)
                grows = pl.ds(pl.multiple_of(i * tq + r * rc, rc), rc)
                dcq_scr[grows, :] += jnp.broadcast_to(jnp.sum(ds, axis=-1, keepdims=True), (rc, LANE))
                csum = csum + _fold8(ds)
            dsb = ds_scr[...]
            dv_new = _dot(p_scr[...], dov, TN)
            dk_new = _dot(dsb, q2, TN) * (1.0 / LOG2E)
            dc_new = -jnp.sum(csum, axis=0, keepdims=True)
            rows = pl.ds(pl.multiple_of(i * tq, tq), tq)
            dq_ref[rows, :] += _dot(dsb, k, NN) * FOX_SCALE
            return dk_new, dv_new, dc_new

        @pl.when(i == j)
        def _():
            dk_new, dv_new, dc_new = update(True)
            dk_acc[...] = dk_new
            dv_acc[...] = dv_new
            dc_acc[...] = dc_new

        @pl.when((i > j) & live)
        def _():
            dk_new, dv_new, dc_new = update(False)
            dk_acc[...] += dk_new
            dv_acc[...] += dv_new
            dc_acc[...] += dc_new

        @pl.when(i == nq - 1)
        def _():
            dk_ref[...] = dk_acc[...].astype(BF16)
            dv_ref[...] = dv_acc[...].astype(BF16)
            dc_ref[...] = dc_acc[...]

        @pl.when(p_id == qi.shape[0] - 1)
        def _():
            dcq_ref[...] = dcq_scr[:, 0:1]

    qtile = lambda cb: pl.BlockSpec((tq, DH), lambda h, p, qi, kj, fq, sk, cb=cb: (fq[h, p], cb + h))
    ktile = lambda cb: pl.BlockSpec((tq, DH), lambda h, p, qi, kj, fq, sk, cb=cb: (kj[p], cb + h))
    qcol = pl.BlockSpec((None, tq, 1), lambda h, p, qi, kj, fq, sk: (h, fq[h, p], 0))
    krow = pl.BlockSpec((None, 1, tq), lambda h, p, qi, kj, fq, sk: (h, 0, kj[p]))
    return pl.pallas_call(
        body, name=name,
        grid_spec=pltpu.PrefetchScalarGridSpec(
            num_scalar_prefetch=4, grid=(HEADS, qi.shape[0]),
            in_specs=[qtile(FOX_Q_CB), ktile(FOX_K_CB), ktile(FOX_V_CB), qcol, krow, qtile(0), qcol, qtile(0)],
            out_specs=[pl.BlockSpec((T, DH), lambda h, p, qi, kj, fq, sk: (0, h)), ktile(0), ktile(0), krow,
                       pl.BlockSpec((None, T, 1), lambda h, p, qi, kj, fq, sk: (h, 0, 0))],
            scratch_shapes=[pltpu.VMEM((tq, DH), F32), pltpu.VMEM((tq, DH), F32), pltpu.VMEM((1, tq), F32),
                            pltpu.VMEM((tq, tq), F32), pltpu.VMEM((tq, tq), F32), pltpu.VMEM((tq, tq), BF16),
                            pltpu.VMEM((tq, tq), BF16), pltpu.VMEM((T, LANE), F32)]),
        out_shape=[jax.ShapeDtypeStruct((T, D_MODEL), F32), jax.ShapeDtypeStruct((T, D_MODEL), BF16),
                   jax.ShapeDtypeStruct((T, D_MODEL), BF16), jax.ShapeDtypeStruct((HEADS, 1, T), F32),
                   jax.ShapeDtypeStruct((HEADS, T, 1), F32)],
        compiler_params=_params("parallel", "arbitrary"),
    )(qi, kj, fetch_q, skip, z, z, z, c_col, c_row, o, lse, do)


MEM_SCALE = 1.0 / math.sqrt(MEM_DH)


def _mem_probs(qh, kh):
    s = _dot(qh, kh, NT) * MEM_SCALE
    p = jnp.exp(s - jnp.max(s, axis=-1, keepdims=True))
    return p / jnp.sum(p, axis=-1, keepdims=True)


def _mem_fwd(q, kv, *, name, tq=512):
    T = q.shape[0]
    tq = min(tq, T)

    def body(q_ref, kv_ref, o_ref):
        for h in range(MEM_HEADS):
            cols = slice(h * MEM_DH, (h + 1) * MEM_DH)
            vcols = slice(D_MODEL + h * MEM_DH, D_MODEL + (h + 1) * MEM_DH)
            p = _mem_probs(q_ref[:, cols], kv_ref[:, cols])
            o_ref[:, cols] = _dot(p, kv_ref[:, vcols], NN).astype(o_ref.dtype)

    return pl.pallas_call(
        body, name=name, grid=(T // tq,),
        in_specs=[pl.BlockSpec((tq, D_MODEL), lambda i: (i, 0)), pl.BlockSpec((MEM_LEN, 2 * D_MODEL), lambda i: (0, 0))],
        out_specs=pl.BlockSpec((tq, D_MODEL), lambda i: (i, 0)), out_shape=jax.ShapeDtypeStruct((T, D_MODEL), BF16),
        compiler_params=_params("parallel"),
    )(q, kv)


def _mem_bwd(q, kv, do, *, name, tq=512):
    T = q.shape[0]
    tq = min(tq, T)

    def body(q_ref, kv_ref, do_ref, dq_ref, dkv_ref):
        @pl.when(pl.program_id(0) == 0)
        def _():
            dkv_ref[...] = jnp.zeros_like(dkv_ref)

        for h in range(MEM_HEADS):
            cols = slice(h * MEM_DH, (h + 1) * MEM_DH)
            vcols = slice(D_MODEL + h * MEM_DH, D_MODEL + (h + 1) * MEM_DH)
            qh, kh, doh = q_ref[:, cols], kv_ref[:, cols], do_ref[:, cols]
            p = _mem_probs(qh, kh)
            dp = _dot(doh, kv_ref[:, vcols], NT)
            ds = p * (dp - jnp.sum(p * dp, axis=-1, keepdims=True))
            dq_ref[:, cols] = (_dot(ds, kh, NN) * MEM_SCALE).astype(dq_ref.dtype)
            dkv_ref[:, cols] += _dot(ds, qh, TN) * MEM_SCALE
            dkv_ref[:, vcols] += _dot(p, doh, TN)

    row = pl.BlockSpec((tq, D_MODEL), lambda i: (i, 0))
    full = pl.BlockSpec((MEM_LEN, 2 * D_MODEL), lambda i: (0, 0))
    return pl.pallas_call(
        body, name=name, grid=(T // tq,), in_specs=[row, full, row], out_specs=[row, full],
        out_shape=[jax.ShapeDtypeStruct((T, D_MODEL), BF16), jax.ShapeDtypeStruct((MEM_LEN, 2 * D_MODEL), F32)],
        compiler_params=_params("arbitrary"),
    )(q, kv, do)


def _adamw(w, g, m, v, *, name, tm=256):
    R, C = w.shape
    tm = min(tm, R)
    assert R % tm == 0
    nsum = g.shape[0] if g.ndim == 3 else 0

    def body(w_ref, g_ref, m_ref, v_ref, go_ref, d_ref, mo_ref, vo_ref):
        if nsum:
            gv = g_ref[0]
            for n in range(1, nsum):
                gv = gv + g_ref[n]
        else:
            gv = g_ref[...]
        mv = ADAM_B1 * m_ref[...] + (1.0 - ADAM_B1) * gv
        vv = ADAM_B2 * v_ref[...] + (1.0 - ADAM_B2) * jnp.square(gv)
        m_hat = mv / (1.0 - ADAM_B1 ** ADAM_STEP)
        v_hat = vv / (1.0 - ADAM_B2 ** ADAM_STEP)
        d_ref[...] = -ADAM_LR * (m_hat / (jnp.sqrt(v_hat) + ADAM_EPS) + ADAM_WD * w_ref[...])
        go_ref[...] = gv
        mo_ref[...] = mv
        vo_ref[...] = vv

    row = pl.BlockSpec((tm, C), lambda i: (i, 0))
    gspec = pl.BlockSpec((nsum, tm, C), lambda i: (0, i, 0)) if nsum else row
    return pl.pallas_call(
        body, name=name, grid=(R // tm,), in_specs=[row, gspec, row, row], out_specs=[row] * 4,
        out_shape=[jax.ShapeDtypeStruct((R, C), F32)] * 4, compiler_params=_params("parallel"),
    )(w, g, m, v)


def _act_mm(a, w, name, out_dtype=F32):
    return _mm([(a, w)], "nn", tm=1024, tn=512, tk=w.shape[0], out_dtypes=[out_dtype], name=name)


def _act_mm_t(a, w, name, out_dtype=F32):
    return _mm([(a, w)], "nt", tm=1024, tn=512, tk=1024, out_dtypes=[out_dtype], name=name)


def _wgrad(a, dy, name, tm=1024):
    tn = D_MODEL if dy.shape[1] % D_MODEL == 0 else D_FF // 2
    return _mm([(a, dy)], "tn", tm=tm, tn=tn, tk=1024, out_dtypes=[F32], name=name)


def _colsum8(p):
    return jnp.sum(p, axis=0, keepdims=True)


def _ffn_fwd(x, pre_g, post_g, wg, wu, wd, tag):
    h = _rms_fwd(x, pre_g, out_dtype=BF16, name=tag + "_pre")
    act, gate, up = _ffn_in(h, wg, wu, name=tag + "_in")
    d = _mm([(act, wd)], "nn", tm=1024, tn=512, tk=D_FF, out_dtypes=[F32], name=tag + "_down")
    xo = _rms_fwd(d, post_g, out_dtype=F32, name=tag + "_post", res=x, coeff=0.5)
    return xo, (h, act, gate, up, d)


def _ffn_bwd(x, dxo, saved, pre_g, post_g, wg, wu, wd, tag):
    h, act, gate, up, d = saved
    dd, dg_post = _rms_bwd(d, post_g, dxo, name=tag + "_post_b", coeff=0.5, dx_dtype=BF16)
    dgate, dup = _mm([(dd, wd)], "nt", tm=1024, tn=256, tk=D_MODEL, out_dtypes=[BF16, BF16], name=tag + "_act_b",
                     epilogue=_swiglu_bwd_epilogue, tiles=(gate, up))
    dwd = _wgrad(act, dd, tag + "_dwd", tm=D_FF // 2)
    dh = _mm([(dgate, wg), (dup, wu)], "nt", tm=512, tn=512, tk=D_FF, out_dtypes=[F32], name=tag + "_in_b")
    dwg = _wgrad(h, dgate, tag + "_dwg")
    dwu = _wgrad(h, dup, tag + "_dwu")
    dx, dg_pre = _rms_bwd(x, pre_g, dh, name=tag + "_pre_b", add=dxo)
    return dx, dict(pre_g=_colsum8(dg_pre), post_g=_colsum8(dg_post), wg=dwg, wu=dwu, wd=dwd)


def _local_step(x, mem, target, W, P):
    T = x.shape[0]
    G = {}
    logits = P["hg_lb_logits"]
    lb = _sigmoid(logits[0] - logits[1])
    lb_row = lb.reshape(1, D_MODEL)
    fbias_row = jnp.pad(P["fox_f_bias"], ((0, 0), (0, LANE - HEADS)))

    x1, ffn1_saved = _ffn_fwd(x, P["ffn1_pre_g"], P["ffn1_post_g"], W["f1g"], W["f1u"], W["f1d"], "ffn1")
    h2 = _rms_fwd(x1, P["mix_pre_g"], out_dtype=BF16, name="mix_pre")
    z = _act_mm(h2, W["w_main"], "mix_in")
    zfb = _mm([(h2, W["w_fb"])], "nn", tm=1024, tn=LANE, tk=D_MODEL, out_dtypes=[F32], name="mix_in_fb")
    oa_pre, states = _hgrn2_fwd(z, lb_row, name="hgrn2_f")
    o_a = _rms_fwd(oa_pre, P["hg_norm_g"], out_dtype=BF16, name="hgrn2_post", mul=(z, 3))
    y_a = _act_mm(o_a, W["wa"], "branch_a")
    c = _cumsum_t([(zfb, 0)], name="fox_c", width=LANE, rows=[(fbias_row, 0)], pre=lambda v, r: _logsigmoid(v + r),
                  post=lambda cum, v, r: (cum * LOG2E,))
    ct = c[:, :HEADS].T
    c_col, c_row = ct[:, :, None], ct[:, None, :]
    fetch_k, skip_f, fetch_q, skip_b = _fox_schedule(*_fox_norms(z, name="fox_norms"), ct, _fox_tile(T))
    o_b, lse = _fox_fwd(z, c_col, c_row, fetch_k, skip_f, name="fox_f")
    y_b = _act_mm(o_b, W["wb"], "branch_b")
    y = _gatemix_fwd(z, P["b_gate"], y_a, y_b, name="gatemix")
    m = _act_mm(y, W["wo"], "mix_out")
    x2 = _rms_fwd(m, P["mix_post_g"], out_dtype=F32, name="mix_post", res=x1)
    h3 = _rms_fwd(x2, P["mem_pre_g"], out_dtype=BF16, name="mem_pre")
    mem_n = _rms_fwd(mem, P["mem_kv_g"], out_dtype=BF16, name="mem_kvn")
    qm = _act_mm(h3, W["wmq"], "mem_q")
    kv = _act_mm(mem_n, W["wmkv"], "mem_kv")
    om = _mem_fwd(qm, kv, name="mem_attn")
    mo = _act_mm(om, W["wmo"], "mem_o")
    x3 = _rms_fwd(mo, P["mem_post_g"], out_dtype=F32, name="mem_post", res=x2)
    x4, ffn2_saved = _ffn_fwd(x3, P["ffn2_pre_g"], P["ffn2_post_g"], W["f2g"], W["f2u"], W["f2d"], "ffn2")
    dx4, sq = _loss_head(x4, target, name="loss_head")

    dx3, g = _ffn_bwd(x3, dx4, ffn2_saved, P["ffn2_pre_g"], P["ffn2_post_g"], W["f2g"], W["f2u"], W["f2d"], "ffn2")
    G.update(ffn2_pre_g=g["pre_g"], ffn2_post_g=g["post_g"], f2g=g["wg"], f2u=g["wu"], f2d=g["wd"])

    dmo, dgp = _rms_bwd(mo, P["mem_post_g"], dx3, name="mem_post_b", dx_dtype=BF16)
    G["mem_post_g"] = _colsum8(dgp)
    dom = _act_mm_t(dmo, W["wmo"], "mem_o_b", BF16)
    G["wmo"] = _wgrad(om, dmo, "mem_o_w")
    dqm, dkv = _mem_bwd(qm, kv, dom, name="mem_attn_b")
    dh3 = _act_mm_t(dqm, W["wmq"], "mem_q_b")
    G["wmq"] = _wgrad(h3, dqm, "mem_q_w")
    G["wmkv"] = _mm([(mem_n, dkv)], "tn", tm=1024, tn=512, tk=MEM_LEN, out_dtypes=[F32], name="mem_kv_w")
    dmem_n = _mm([(dkv, W["wmkv"])], "nt", tm=MEM_LEN, tn=512, tk=2 * D_MODEL, out_dtypes=[F32], name="mem_kv_b")
    _, dgp = _rms_bwd(mem, P["mem_kv_g"], dmem_n, name="mem_kvn_b")
    G["mem_kv_g"] = _colsum8(dgp)
    dx2, dgp = _rms_bwd(x2, P["mem_pre_g"], dh3, name="mem_pre_b", add=dx3)
    G["mem_pre_g"] = _colsum8(dgp)

    dm, dgp = _rms_bwd(m, P["mix_post_g"], dx2, name="mix_post_b", dx_dtype=BF16)
    G["mix_post_g"] = _colsum8(dgp)
    dy = _act_mm_t(dm, W["wo"], "mix_out_b")
    G["wo"] = _wgrad(y, dm, "mix_out_w")
    dya, dyb, dz0, dz1, s0, s1 = _gatemix_bwd(z, P["b_gate"], y_a, y_b, dy, name="gatemix_b")
    G["b_gate"] = jnp.concatenate([_colsum8(s0), _colsum8(s1)], axis=1)
    do_a = _act_mm_t(dya, W["wa"], "branch_a_b")
    G["wa"] = _wgrad(o_a, dya, "branch_a_w")
    do_b = _act_mm_t(dyb, W["wb"], "branch_b_b")
    G["wb"] = _wgrad(o_b, dyb, "branch_b_w")
    doa_pre, dgp, dga = _rms_bwd(oa_pre, P["hg_norm_g"], do_a, name="hgrn2_post_b", mul=(z, 3))
    G["hg_norm_g"] = _colsum8(dgp)
    dq_a, dfl_a, di_a, dlb = _hgrn2_bwd(z, lb_row, states, doa_pre, name="hgrn2_b")
    dl0 = (dlb * lb_row * (1.0 - lb_row)).reshape(1, HEADS, DH)
    G["hg_lb_logits"] = jnp.concatenate([dl0, -dl0], axis=0)
    dq_b, dk_b, dv_b, dcr, dcq = _fox_bwd(z, c_col, c_row, o_b, lse, do_b, fetch_q, skip_b, name="fox_b")
    dc_pad = jnp.pad((dcr[:, 0, :] + dcq[:, :, 0]).T, ((0, 0), (0, LANE - HEADS)))
    gate_b = lambda cum, dc, zf, r: cum * _sigmoid(-(zf + r))
    dfl_b, dfb = _cumsum_t([(dc_pad, 0), (zfb, 0)], name="fox_c_b", width=LANE, reverse=True, rows=[(fbias_row, 0)],
                           pre=lambda dc, zf, r: dc, post=lambda *a: (gate_b(*a),), fold=gate_b)
    G["fox_f_bias"] = _colsum8(dfb)[:, :HEADS]

    pieces = [dq_a, dfl_a, di_a, dga, dq_b, dk_b, dv_b, dz0, dz1]
    dh2 = _mm([(dfl_b, W["w_fb"])], "nt", tm=512, tn=D_MODEL, tk=LANE, out_dtypes=[F32], name="mix_in_fb_b")
    for lo, hi in ((0, 5), (5, 9)):
        dh2 = _mm([(p, W["w_main"]) for p in pieces[lo:hi]], "nt", tm=512, tn=D_MODEL, tk=D_MODEL, out_dtypes=[F32],
                  name=f"mix_in_b{lo}", b_koff=[n * D_MODEL for n in range(lo, hi)],
                  epilogue=lambda acc, t: (acc + t,), tiles=(dh2,))
    G["w_main"] = [_wgrad(h2, p, f"mix_in_w{n}") for n, p in enumerate(pieces)]
    G["w_fb"] = _mm([(h2, dfl_b)], "tn", tm=1024, tn=LANE, tk=512, out_dtypes=[F32], name="mix_in_fb_w")
    dx1, dgp = _rms_bwd(x1, P["mix_pre_g"], dh2, name="mix_pre_b", add=dx2)
    G["mix_pre_g"] = _colsum8(dgp)

    dx0, g = _ffn_bwd(x, dx1, ffn1_saved, P["ffn1_pre_g"], P["ffn1_post_g"], W["f1g"], W["f1u"], W["f1d"], "ffn1")
    G.update(ffn1_pre_g=g["pre_g"], ffn1_post_g=g["post_g"], f1g=g["wg"], f1u=g["wu"], f1d=g["wd"])
    return sq, dx0, G


N_CHIP = 4
N_DEV = 8
IN_COLS = 9224
FB_COL = 7 * D_MODEL
SHARDED = (
    ("ffn1_w_in", (D_MODEL, 2 * D_FF), 1), ("ffn1_w_down", (D_FF, D_MODEL), 0), ("w_in", (D_MODEL, IN_COLS), 1),
    ("w_branch_a", (D_MODEL, D_MODEL), 0), ("w_branch_b", (D_MODEL, D_MODEL), 0), ("w_out", (D_MODEL, D_MODEL), 0),
    ("w_mq", (D_MODEL, D_MODEL), 0), ("w_mkv", (D_MODEL, 2 * D_MODEL), 1), ("w_mo", (D_MODEL, D_MODEL), 0),
    ("ffn2_w_in", (D_MODEL, 2 * D_FF), 1), ("ffn2_w_down", (D_FF, D_MODEL), 0),
)
SMALL = ("ffn1_pre_g", "ffn1_post_g", "mix_pre_g", "hg_norm_g", "mix_post_g", "mem_pre_g", "mem_kv_g", "mem_post_g",
         "ffn2_pre_g", "ffn2_post_g", "b_gate", "hg_lb_logits", "fox_f_bias")
SMALL_SHAPES = dict(b_gate=(1, 2 * D_MODEL), hg_lb_logits=(2, HEADS, DH), fox_f_bias=(1, HEADS))
SMALL_ROWS = 16
WEIGHT_ORDER = ("ffn1_pre_g", "ffn1_w_in", "ffn1_w_down", "ffn1_post_g", "mix_pre_g", "w_in", "hg_lb_logits", "hg_norm_g",
                "fox_f_bias", "w_branch_a", "w_branch_b", "b_gate", "w_out", "mix_post_g", "mem_pre_g", "mem_kv_g", "w_mq",
                "w_mkv", "w_mo", "mem_post_g", "ffn2_pre_g", "ffn2_w_in", "ffn2_w_down", "ffn2_post_g")


def _layout(axis):
    out, at = [], 0
    for name, shape, ax in SHARDED:
        if ax == axis:
            n = shape[ax] // N_CHIP
            out.append((name, at, n))
            at += n if axis == 0 else -(-n // LANE) * LANE
    return tuple(out), at


ROW_LAYOUT, SLAB_ROWS = _layout(0)
COL_LAYOUT, SLAB_COLS = _layout(1)


def _pack(shards, dtype):
    rows = jnp.concatenate([shards[name].astype(dtype) for name, _, _ in ROW_LAYOUT], axis=0)
    cols = []
    for name, _, n in COL_LAYOUT:
        cols.append(jnp.pad(shards[name].astype(dtype), ((0, 0), (0, -n % LANE))))
    return rows, jnp.concatenate(cols, axis=1)


def _unpack(rows, cols):
    out = {name: rows[at:at + n] for name, at, n in ROW_LAYOUT}
    out.update({name: cols[:, at:at + n] for name, at, n in COL_LAYOUT})
    return out


def _pack_small(vals):
    rows = []
    for name in SMALL:
        v = vals[name].astype(F32).reshape(-1)
        rows.append(jnp.pad(v, (0, -v.shape[0] % D_MODEL)).reshape(-1, D_MODEL))
    rows = jnp.concatenate(rows, axis=0)
    return jnp.pad(rows, ((0, SMALL_ROWS - rows.shape[0]), (0, 0)))


def _unpack_small(slab):
    out, r = {}, 0
    for name in SMALL:
        shape = SMALL_SHAPES.get(name, (1, D_MODEL))
        size = math.prod(shape)
        n = -(-size // D_MODEL)
        out[name] = slab[r:r + n].reshape(-1)[:size].reshape(shape)
        r += n
    return out


ANY = pl.BlockSpec(memory_space=pl.ANY)
MESH = pl.DeviceIdType.MESH
CHIP_FLIPS = ((0, 1), (1, 0), (1, 1))


def _place():
    x, y, c = lax.axis_index("x"), lax.axis_index("y"), lax.axis_index("c")
    chips = [(x ^ fx, y ^ fy) for fx, fy in CHIP_FLIPS]
    return x, y, c, chips


def _remote(src, dst, sems, k, dev):
    return pltpu.make_async_remote_copy(src_ref=src, dst_ref=dst, send_sem=sems[0].at[k], recv_sem=sems[1].at[k],
                                        device_id=dev, device_id_type=MESH)


def _gather_weights(wpack, tag):
    half = wpack.shape[0] // 2

    def body(w_ref, out_ref, send_sems, recv_sems):
        x, y, c, chips = _place()
        sems = (send_sems, recv_sems)
        mine, theirs = pl.ds(c * half, half), pl.ds((1 - c) * half, half)
        sent = [_remote(w_ref.at[mine], out_ref.at[k, mine], sems, k, (px, py, c)) for k, (px, py) in enumerate(chips)]
        for cp in sent:
            cp.start()
        passed = []
        for k, (px, py) in enumerate(chips):
            rows = out_ref.at[k, mine]
            _remote(rows, rows, sems, k, (px, py, c)).wait_recv()
            cp = _remote(rows, rows, sems, 3 + k, (x, y, 1 - c))
            cp.start()
            passed.append(cp)
        for k in range(3):
            rows = out_ref.at[k, theirs]
            _remote(rows, rows, sems, 3 + k, (x, y, 1 - c)).wait_recv()
        for cp in sent + passed:
            cp.wait_send()

    return pl.pallas_call(
        body, name="gather_weights_" + tag, out_shape=jax.ShapeDtypeStruct((3,) + wpack.shape, wpack.dtype),
        in_specs=[ANY], out_specs=ANY,
        scratch_shapes=[pltpu.SemaphoreType.DMA((6,)), pltpu.SemaphoreType.DMA((6,))],
    )(wpack)


def _swap_halves(g, tag):
    half = g.shape[1] // 2

    def body(g_ref, out_ref, send_sems, recv_sems):
        x, y, c, _ = _place()
        sems = (send_sems, recv_sems)
        theirs = pl.ds((1 - c) * half, half)
        sent = [_remote(g_ref.at[j, theirs], out_ref.at[j], sems, j, (x, y, 1 - c)) for j in range(N_CHIP)]
        for cp in sent:
            cp.start()
        for cp in sent:
            cp.wait_recv()
        for cp in sent:
            cp.wait_send()

    return pl.pallas_call(
        body, name="swap_halves_" + tag, out_shape=jax.ShapeDtypeStruct((N_CHIP, half, g.shape[2]), g.dtype),
        in_specs=[ANY], out_specs=ANY,
        scratch_shapes=[pltpu.SemaphoreType.DMA((N_CHIP,)), pltpu.SemaphoreType.DMA((N_CHIP,))],
    )(g)


def _pair_sum(g, got, place, tag, *, tm):
    _, half, width = got.shape
    nb = half // tm

    def body(s_ref, g_ref, a_ref, bf_ref, own_ref):
        v = g_ref[...] + a_ref[...]
        bf_ref[...] = v.astype(BF16)

        @pl.when(pl.program_id(1) == s_ref[0])
        def _():
            own_ref[...] = v

    return pl.pallas_call(
        body, name="pair_sum_" + tag,
        grid_spec=pltpu.PrefetchScalarGridSpec(
            num_scalar_prefetch=1, grid=(nb, N_CHIP),
            in_specs=[pl.BlockSpec((None, tm, width), lambda i, j, s: (j, s[1] * nb + i, 0)),
                      pl.BlockSpec((None, tm, width), lambda i, j, s: (j, i, 0))],
            out_specs=[pl.BlockSpec((None, tm, width), lambda i, j, s: (j, i, 0)),
                       pl.BlockSpec((tm, width), lambda i, j, s: (i, 0))]),
        out_shape=[jax.ShapeDtypeStruct((N_CHIP, half, width), BF16), jax.ShapeDtypeStruct((half, width), F32)],
        compiler_params=_params("arbitrary", "arbitrary"),
    )(place, g, got)


def _scatter_partials(pbf, tag):
    def body(p_ref, out_ref, send_sems, recv_sems):
        x, y, c, chips = _place()
        sems = (send_sems, recv_sems)
        sent = [_remote(p_ref.at[2 * px + py], out_ref.at[k], sems, k, (px, py, c)) for k, (px, py) in enumerate(chips)]
        for cp in sent:
            cp.start()
        for cp in sent:
            cp.wait_recv()
        for cp in sent:
            cp.wait_send()

    return pl.pallas_call(
        body, name="scatter_partials_" + tag, out_shape=jax.ShapeDtypeStruct((3,) + pbf.shape[1:], pbf.dtype),
        in_specs=[ANY], out_specs=ANY, scratch_shapes=[pltpu.SemaphoreType.DMA((3,)), pltpu.SemaphoreType.DMA((3,))],
    )(pbf)


def _chip_sum(own, got, tag, *, tm):
    half, width = own.shape

    def body(o_ref, g_ref, r_ref):
        r_ref[...] = ((o_ref[...] + g_ref[0].astype(F32)) + g_ref[1].astype(F32)) + g_ref[2].astype(F32)

    row = pl.BlockSpec((tm, width), lambda i: (i, 0))
    return pl.pallas_call(
        body, name="chip_sum_" + tag, grid=(half // tm,),
        in_specs=[row, pl.BlockSpec((3, tm, width), lambda i: (0, i, 0))], out_specs=row,
        out_shape=jax.ShapeDtypeStruct((half, width), F32), compiler_params=_params("parallel"),
    )(own, got)


def _join_halves(r, tag):
    def body(r_ref, out_ref, send_sems, recv_sems):
        x, y, c, _ = _place()
        cp = _remote(r_ref, out_ref, (send_sems, recv_sems), 0, (x, y, 1 - c))
        cp.start()
        cp.wait_recv()
        cp.wait_send()

    return pl.pallas_call(
        body, name="join_halves_" + tag, out_shape=jax.ShapeDtypeStruct(r.shape, r.dtype),
        in_specs=[ANY], out_specs=ANY,
        scratch_shapes=[pltpu.SemaphoreType.DMA((1,)), pltpu.SemaphoreType.DMA((1,))],
    )(r)


def _reduce_scatter(slabs, place, core, tag, tm):
    pbf, own = _pair_sum(slabs, _swap_halves(slabs, tag), place, tag, tm=tm)
    mine = _chip_sum(own, _scatter_partials(pbf, tag), tag, tm=tm)
    theirs = _join_halves(mine, tag)
    return jnp.where(core == 0, jnp.concatenate([mine, theirs]), jnp.concatenate([theirs, mine]))


def _gather_small(s):
    flips = [(fx, fy, fc) for fx in (0, 1) for fy in (0, 1) for fc in (0, 1)][1:]

    def body(s_ref, out_ref, send_sems, recv_sems, local_sem):
        x, y, c, _ = _place()
        sems = (send_sems, recv_sems)
        me = 4 * x + 2 * y + c
        local = pltpu.make_async_copy(s_ref, out_ref.at[me], local_sem)
        local.start()
        sent = [_remote(s_ref, out_ref.at[me], sems, k, (x ^ fx, y ^ fy, c ^ fc)) for k, (fx, fy, fc) in enumerate(flips)]
        for cp in sent:
            cp.start()
        for k, (fx, fy, fc) in enumerate(flips):
            peer = (x ^ fx, y ^ fy, c ^ fc)
            _remote(s_ref, out_ref.at[4 * peer[0] + 2 * peer[1] + peer[2]], sems, k, peer).wait_recv()
        for cp in sent:
            cp.wait_send()
        local.wait()

    return pl.pallas_call(
        body, name="gather_small", out_shape=jax.ShapeDtypeStruct((N_DEV, SMALL_ROWS, D_MODEL), s.dtype),
        in_specs=[ANY], out_specs=ANY,
        scratch_shapes=[pltpu.SemaphoreType.DMA((7,)), pltpu.SemaphoreType.DMA((7,)), pltpu.SemaphoreType.DMA],
    )(s)


def _full_weights(own, others, me):
    by_flip = [jnp.concatenate([o[None], t], axis=0) for o, t in zip(own, others)]
    per_chip = [_unpack(*[lax.dynamic_index_in_dim(s, j ^ me, 0, keepdims=False) for s in by_flip])
                for j in range(N_CHIP)]
    full = {name: jnp.concatenate([pc[name] for pc in per_chip], axis=axis) for name, _, axis in SHARDED}
    w_in = full["w_in"]
    return dict(
        f1g=full["ffn1_w_in"][:, :D_FF], f1u=full["ffn1_w_in"][:, D_FF:], f1d=full["ffn1_w_down"],
        f2g=full["ffn2_w_in"][:, :D_FF], f2u=full["ffn2_w_in"][:, D_FF:], f2d=full["ffn2_w_down"],
        w_main=jnp.concatenate([w_in[:, :FB_COL], w_in[:, FB_COL + HEADS:]], axis=1),
        w_fb=jnp.pad(w_in[:, FB_COL:FB_COL + HEADS], ((0, 0), (0, LANE - HEADS))),
        wa=full["w_branch_a"], wb=full["w_branch_b"], wo=full["w_out"],
        wmq=full["w_mq"], wmkv=full["w_mkv"], wmo=full["w_mo"],
    )


def _grad_slabs(G):
    main = jnp.concatenate(G["w_main"], axis=1)
    full = dict(
        ffn1_w_in=jnp.concatenate([G["f1g"], G["f1u"]], axis=1), ffn1_w_down=G["f1d"],
        ffn2_w_in=jnp.concatenate([G["f2g"], G["f2u"]], axis=1), ffn2_w_down=G["f2d"],
        w_in=jnp.concatenate([main[:, :FB_COL], G["w_fb"][:, :HEADS], main[:, FB_COL:]], axis=1),
        w_branch_a=G["wa"], w_branch_b=G["wb"], w_out=G["wo"], w_mq=G["wmq"], w_mkv=G["wmkv"], w_mo=G["wmo"],
    )
    rows, cols = [], []
    for j in range(N_CHIP):
        shards = {}
        for name, shape, axis in SHARDED:
            n = shape[axis] // N_CHIP
            shards[name] = lax.slice_in_dim(full[name], j * n, (j + 1) * n, axis=axis)
        r, c = _pack(shards, F32)
        rows.append(r)
        cols.append(c)
    return jnp.stack(rows, axis=0), jnp.stack(cols, axis=0)


def kernel(x, mem, ffn1_pre_g, ffn1_w_in, ffn1_w_down, ffn1_post_g, mix_pre_g, w_in, hg_lb_logits, hg_norm_g, fox_f_bias, w_branch_a, w_branch_b, b_gate, w_out, mix_post_g, mem_pre_g, mem_kv_g, w_mq, w_mkv, w_mo, mem_post_g, ffn2_pre_g, ffn2_w_in, ffn2_w_down, ffn2_post_g, loss_target, m_ffn1_pre_g, m_ffn1_w_in, m_ffn1_w_down, m_ffn1_post_g, m_mix_pre_g, m_w_in, m_hg_lb_logits, m_hg_norm_g, m_fox_f_bias, m_w_branch_a, m_w_branch_b, m_b_gate, m_w_out, m_mix_post_g, m_mem_pre_g, m_mem_kv_g, m_w_mq, m_w_mkv, m_w_mo, m_mem_post_g, m_ffn2_pre_g, m_ffn2_w_in, m_ffn2_w_down, m_ffn2_post_g, v_ffn1_pre_g, v_ffn1_w_in, v_ffn1_w_down, v_ffn1_post_g, v_mix_pre_g, v_w_in, v_hg_lb_logits, v_hg_norm_g, v_fox_f_bias, v_w_branch_a, v_w_branch_b, v_b_gate, v_w_out, v_mix_post_g, v_mem_pre_g, v_mem_kv_g, v_w_mq, v_w_mkv, v_w_mo, v_mem_post_g, v_ffn2_pre_g, v_ffn2_w_in, v_ffn2_w_down, v_ffn2_post_g):
    w = dict(ffn1_pre_g=ffn1_pre_g, ffn1_w_in=ffn1_w_in, ffn1_w_down=ffn1_w_down, ffn1_post_g=ffn1_post_g, mix_pre_g=mix_pre_g, w_in=w_in, hg_lb_logits=hg_lb_logits, hg_norm_g=hg_norm_g, fox_f_bias=fox_f_bias, w_branch_a=w_branch_a, w_branch_b=w_branch_b, b_gate=b_gate, w_out=w_out, mix_post_g=mix_post_g, mem_pre_g=mem_pre_g, mem_kv_g=mem_kv_g, w_mq=w_mq, w_mkv=w_mkv, w_mo=w_mo, mem_post_g=mem_post_g, ffn2_pre_g=ffn2_pre_g, ffn2_w_in=ffn2_w_in, ffn2_w_down=ffn2_w_down, ffn2_post_g=ffn2_post_g)
    m = dict(ffn1_pre_g=m_ffn1_pre_g, ffn1_w_in=m_ffn1_w_in, ffn1_w_down=m_ffn1_w_down, ffn1_post_g=m_ffn1_post_g, mix_pre_g=m_mix_pre_g, w_in=m_w_in, hg_lb_logits=m_hg_lb_logits, hg_norm_g=m_hg_norm_g, fox_f_bias=m_fox_f_bias, w_branch_a=m_w_branch_a, w_branch_b=m_w_branch_b, b_gate=m_b_gate, w_out=m_w_out, mix_post_g=m_mix_post_g, mem_pre_g=m_mem_pre_g, mem_kv_g=m_mem_kv_g, w_mq=m_w_mq, w_mkv=m_w_mkv, w_mo=m_w_mo, mem_post_g=m_mem_post_g, ffn2_pre_g=m_ffn2_pre_g, ffn2_w_in=m_ffn2_w_in, ffn2_w_down=m_ffn2_w_down, ffn2_post_g=m_ffn2_post_g)
    v = dict(ffn1_pre_g=v_ffn1_pre_g, ffn1_w_in=v_ffn1_w_in, ffn1_w_down=v_ffn1_w_down, ffn1_post_g=v_ffn1_post_g, mix_pre_g=v_mix_pre_g, w_in=v_w_in, hg_lb_logits=v_hg_lb_logits, hg_norm_g=v_hg_norm_g, fox_f_bias=v_fox_f_bias, w_branch_a=v_w_branch_a, w_branch_b=v_w_branch_b, b_gate=v_b_gate, w_out=v_w_out, mix_post_g=v_mix_post_g, mem_pre_g=v_mem_pre_g, mem_kv_g=v_mem_kv_g, w_mq=v_w_mq, w_mkv=v_w_mkv, w_mo=v_w_mo, mem_post_g=v_mem_post_g, ffn2_pre_g=v_ffn2_pre_g, ffn2_w_in=v_ffn2_w_in, ffn2_w_down=v_ffn2_w_down, ffn2_post_g=v_ffn2_post_g)
    sharded = [name for name, _, _ in SHARDED]
    shard_of = lambda d: {name: d[name][0] for name in sharded}

    me, core = 2 * lax.axis_index("x") + lax.axis_index("y"), lax.axis_index("c")
    w_rows, w_cols = _pack(shard_of(w), BF16)
    W = _full_weights((w_rows, w_cols), (_gather_weights(w_rows, "rows"), _gather_weights(w_cols, "cols")), me)
    P = {name: w[name] for name in SMALL}

    sq, dx0, G = _local_step(x[0], mem[0], loss_target[0], W, P)
    loss = lax.psum(0.5 * jnp.sum(sq) / D_MODEL, ("x", "y", "c"))

    place = jnp.stack([me, core]).astype(jnp.int32)
    g_rows, g_cols = _grad_slabs(G)
    g_shards = _unpack(_reduce_scatter(g_rows, place, core, "rows", SLAB_ROWS // 8),
                       _reduce_scatter(g_cols, place, core, "cols", D_MODEL // 8))
    big = {}
    for name, shape, axis in SHARDED:
        rows = shape[0] // (N_CHIP if axis == 0 else 1)
        big[name] = _adamw(w[name][0], g_shards[name], m[name][0], v[name][0], name="adamw_" + name, tm=rows // 8)
    small = _adamw(_pack_small(w), _gather_small(_pack_small(G)), _pack_small(m), _pack_small(v), name="adamw_small",
                   tm=SMALL_ROWS)

    outs = [loss, dx0[None]]
    for n in range(4):
        vals = {name: res[n][None] for name, res in big.items()}
        vals.update(_unpack_small(small[n]))
        outs += [vals[name] for name in WEIGHT_ORDER]
    return tuple(outs)
```

```python
import functools
import math

import jax
import jax.numpy as jnp
from jax import lax
from jax.experimental import pallas as pl
from jax.experimental.pallas import tpu as pltpu

F32 = jnp.float32
BF16 = jnp.bfloat16

D_MODEL = 1024
D_FF = 2816
HEADS = 8
DH = 128
MEM_HEADS = 4
MEM_DH = 256
MEM_LEN = 256
EPS = 1e-6
SUB = 16
LANE = 128
SUBLANE = 8
VMEM_LIMIT = 56 * 1024 * 1024

ADAM_LR = 0.001
ADAM_B1 = 0.9
ADAM_B2 = 0.999
ADAM_EPS = 1e-08
ADAM_WD = 0.01
ADAM_STEP = 10

HIGHEST = lax.Precision.HIGHEST


def _params(*sem):
    return pltpu.CompilerParams(dimension_semantics=sem, vmem_limit_bytes=VMEM_LIMIT)


def _sigmoid(v):
    return 1.0 / (1.0 + jnp.exp(-v))


def _silu(v):
    return v * _sigmoid(v)


def _dsilu(v):
    s = _sigmoid(v)
    return s * (1.0 + v * (1.0 - s))


def _dot(a, b, dims):
    return lax.dot_general(a.astype(BF16), b.astype(BF16), (dims, ((), ())), preferred_element_type=F32)


NN = ((1,), (0,))
NT = ((1,), (1,))
TN = ((0,), (0,))


ANY = pl.BlockSpec(memory_space=pl.ANY)


class _Side:
    def __init__(self, arrays, out_shapes, nsem, start, wait, aliases=None):
        self.arrays, self.out_shapes, self.nsem = list(arrays), list(out_shapes), nsem
        self.start, self.wait, self.aliases = start, wait, dict(aliases or {})

    def plumb(self, n_in, n_out):
        return dict(args=self.arrays, in_specs=[ANY] * len(self.arrays), out_specs=[ANY] * len(self.out_shapes),
                    scratch=[pltpu.SemaphoreType.DMA((self.nsem,)), pltpu.SemaphoreType.DMA((self.nsem,))],
                    aliases={n_in + i: n_out + o for i, o in self.aliases.items()})

    def run_at_ends(self, ins, outs, sems, first, last, compute):
        @pl.when(first)
        def _():
            self.start(ins, outs, sems)

        compute()

        @pl.when(last)
        def _():
            self.wait(ins, outs, sems)


def _grid_ends(grid):
    first = functools.reduce(lambda a, b: a & b, [pl.program_id(d) == 0 for d in range(len(grid))])
    last = functools.reduce(lambda a, b: a & b, [pl.program_id(d) == grid[d] - 1 for d in range(len(grid))])
    return first, last


def _mm(pairs, mode, *, tm, tn, tk, out_dtypes, name, epilogue=None, tiles=(), b_koff=None, side=None):
    a0, b0 = pairs[0]
    if mode == "nn":
        (M, K), N = a0.shape, b0.shape[1]
    elif mode == "nt":
        (M, K), N = a0.shape, b0.shape[0]
    else:
        (K, M), N = a0.shape, b0.shape[1]
    tm, tn, tk = min(tm, M), min(tn, N), min(tk, K)
    assert M % tm == 0 and N % tn == 0 and K % tk == 0, (name, M, N, K, tm, tn, tk)
    nk = K // tk
    npair = len(pairs)
    koff = [0] * npair if b_koff is None else [o // tk for o in b_koff]
    if b_koff is not None:
        assert all(o % tk == 0 for o in b_koff)
    in_specs, args = [], []
    for p, (a, b) in enumerate(pairs):
        if mode == "nn":
            sa = pl.BlockSpec((tm, tk), lambda i, j, k: (i, k))
            sb = pl.BlockSpec((tk, tn), lambda i, j, k, o=koff[p]: (k + o, j))
            dims = NN
        elif mode == "nt":
            sa = pl.BlockSpec((tm, tk), lambda i, j, k: (i, k))
            sb = pl.BlockSpec((tn, tk), lambda i, j, k, o=koff[p]: (j, k + o))
            dims = NT
        else:
            sa = pl.BlockSpec((tk, tm), lambda i, j, k: (k, i))
            sb = pl.BlockSpec((tk, tn), lambda i, j, k, o=koff[p]: (k + o, j))
            dims = TN
        in_specs += [sa, sb]
        args += [a, b]
    for t in tiles:
        in_specs.append(pl.BlockSpec((tm, tn), lambda i, j, k: (i, j)))
        args.append(t)
    nt_ = len(tiles)
    nout = len(out_dtypes)
    nin = len(args)
    grid = (M // tm, N // tn, nk)
    plumb = side.plumb(nin, nout) if side is not None else None
    ns_in, ns_out = (len(side.arrays), len(side.out_shapes)) if side is not None else (0, 0)

    def body(*refs):
        ab = refs[: 2 * npair]
        tl = refs[2 * npair: nin]
        outs = refs[nin + ns_in: nin + ns_in + nout]
        scratch = refs[nin + ns_in + nout + ns_out:]
        acc_ref = scratch[0] if nk > 1 else None
        if side is None:
            compute(ab, tl, outs, acc_ref)
        else:
            first, last = _grid_ends(grid)
            side.run_at_ends(refs[nin: nin + ns_in], refs[nin + ns_in + nout: nin + ns_in + nout + ns_out],
                             scratch[-2:], first, last, lambda: compute(ab, tl, outs, acc_ref))

    def compute(ab, tl, outs, acc_ref):
        def partial_sum():
            s = _dot(ab[0][...], ab[1][...], dims)
            for p in range(1, npair):
                s = s + _dot(ab[2 * p][...], ab[2 * p + 1][...], dims)
            return s

        def finish(acc):
            res = (acc,) if epilogue is None else epilogue(acc, *[t[...] for t in tl])
            for o, r in zip(outs, res):
                o[...] = r.astype(o.dtype)

        if nk == 1:
            finish(partial_sum())
        else:
            k = pl.program_id(2)

            @pl.when(k == 0)
            def _():
                acc_ref[...] = jnp.zeros_like(acc_ref)

            acc_ref[...] += partial_sum()

            @pl.when(k == nk - 1)
            def _():
                finish(acc_ref[...])

    out_shape = [jax.ShapeDtypeStruct((M, N), dt) for dt in out_dtypes]
    out_specs = [pl.BlockSpec((tm, tn), lambda i, j, k: (i, j)) for _ in out_dtypes]
    scratch = [pltpu.VMEM((tm, tn), F32)] if nk > 1 else []
    if side is None:
        res = pl.pallas_call(
            body, name=name, grid=grid, in_specs=in_specs, out_specs=out_specs, out_shape=out_shape,
            scratch_shapes=scratch, compiler_params=_params("parallel", "parallel", "arbitrary"),
        )(*args)
        return res[0] if nout == 1 else res
    res = pl.pallas_call(
        body, name=name, grid=grid, in_specs=in_specs + plumb["in_specs"], out_specs=out_specs + plumb["out_specs"],
        out_shape=out_shape + side.out_shapes, scratch_shapes=scratch + plumb["scratch"],
        input_output_aliases=plumb["aliases"], compiler_params=_params("arbitrary", "arbitrary", "arbitrary"),
    )(*args, *plumb["args"])
    return res[:nout], res[nout:]


def _col(arr, tm, width, cb):
    return pl.BlockSpec((tm, width), lambda i, cb=cb: (i, cb))


def _rms_fwd(x, g, *, out_dtype, name, mul=None, res=None, coeff=1.0, tm=512):
    T, D = x.shape
    tm = min(tm, T)
    args, in_specs = [x, g], [pl.BlockSpec((tm, D), lambda i: (i, 0)), pl.BlockSpec((1, D), lambda i: (0, 0))]
    if mul is not None:
        args.append(mul[0])
        in_specs.append(_col(mul[0], tm, D, mul[1]))
    if res is not None:
        args.append(res)
        in_specs.append(pl.BlockSpec((tm, D), lambda i: (i, 0)))

    def body(*refs):
        xv = refs[0][...].astype(F32)
        r = lax.rsqrt(jnp.mean(xv * xv, axis=-1, keepdims=True) + EPS)
        y = (xv * r) * refs[1][...]
        n = 2
        if mul is not None:
            y = y * _silu(refs[n][...])
            n += 1
        if res is not None:
            y = refs[n][...] + coeff * y
        refs[-1][...] = y.astype(out_dtype)

    return pl.pallas_call(
        body, name=name, grid=(T // tm,), in_specs=in_specs, out_specs=pl.BlockSpec((tm, D), lambda i: (i, 0)),
        out_shape=jax.ShapeDtypeStruct((T, D), out_dtype), compiler_params=_params("parallel"),
    )(*args)


def _fold8(v):
    tm, d = v.shape
    return v.reshape(tm // SUBLANE, SUBLANE, d).sum(axis=0)


def _rms_bwd(x, g, dy, *, name, coeff=1.0, add=None, mul=None, dx_dtype=F32, tm=512):
    T, D = x.shape
    tm = min(tm, T)
    row = pl.BlockSpec((tm, D), lambda i: (i, 0))
    args, in_specs = [x, g, dy], [row, pl.BlockSpec((1, D), lambda i: (0, 0)), row]
    if add is not None:
        args.append(add)
        in_specs.append(row)
    if mul is not None:
        args.append(mul[0])
        in_specs.append(_col(mul[0], tm, D, mul[1]))
    nin = len(args)

    def body(*refs):
        xv = refs[0][...].astype(F32)
        gv = refs[1][...]
        dyv = refs[2][...].astype(F32) * coeff
        r = lax.rsqrt(jnp.mean(xv * xv, axis=-1, keepdims=True) + EPS)
        nrm = xv * r
        n = 3
        addv = None
        if add is not None:
            addv = refs[n][...]
            n += 1
        if mul is not None:
            mv = refs[n][...]
            sm = _silu(mv)
            refs[nin + 2][...] = (dyv * nrm * gv * _dsilu(mv)).astype(refs[nin + 2].dtype)
            dyv = dyv * sm
        dn = dyv * gv
        dx = r * (dn - nrm * jnp.mean(dn * nrm, axis=-1, keepdims=True))
        if addv is not None:
            dx = dx + addv
        refs[nin][...] = dx.astype(dx_dtype)
        dg_ref = refs[nin + 1]

        @pl.when(pl.program_id(0) == 0)
        def _():
            dg_ref[...] = jnp.zeros_like(dg_ref)

        dg_ref[...] += _fold8(dyv * nrm)

    out_shape = [jax.ShapeDtypeStruct((T, D), dx_dtype), jax.ShapeDtypeStruct((SUBLANE, D), F32)]
    out_specs = [row, pl.BlockSpec((SUBLANE, D), lambda i: (0, 0))]
    if mul is not None:
        out_shape.append(jax.ShapeDtypeStruct((T, D), BF16))
        out_specs.append(row)
    return pl.pallas_call(
        body, name=name, grid=(T // tm,), in_specs=in_specs, out_specs=out_specs, out_shape=out_shape,
        compiler_params=_params("arbitrary"),
    )(*args)


def _ffn_in(h, wg, wu, *, name, tm=1024, tn=256, side=None):
    T, D = h.shape
    F = wg.shape[1]
    tm = min(tm, T)
    assert F % tn == 0
    grid = (T // tm, F // tn)
    ns_in, ns_out = (len(side.arrays), len(side.out_shapes)) if side is not None else (0, 0)

    def compute(h_ref, wg_ref, wu_ref, a_ref, g_ref, u_ref):
        hv = h_ref[...]
        gt = _dot(hv, wg_ref[...], NN)
        up = _dot(hv, wu_ref[...], NN)
        a_ref[...] = (_silu(gt) * up).astype(BF16)
        g_ref[...] = gt.astype(BF16)
        u_ref[...] = up.astype(BF16)

    def body(*refs):
        if side is None:
            compute(*refs)
        else:
            outs0 = 3 + ns_in
            first, last = _grid_ends(grid)
            side.run_at_ends(refs[3:outs0], refs[outs0 + 3: outs0 + 3 + ns_out], refs[-2:], first, last,
                             lambda: compute(*refs[:3], *refs[outs0: outs0 + 3]))

    o = pl.BlockSpec((tm, tn), lambda i, j: (i, j))
    w = pl.BlockSpec((D, tn), lambda i, j: (0, j))
    in_specs = [pl.BlockSpec((tm, D), lambda i, j: (i, 0)), w, w]
    out_shape = [jax.ShapeDtypeStruct((T, F), BF16)] * 3
    if side is None:
        return pl.pallas_call(body, name=name, grid=grid, in_specs=in_specs, out_specs=[o, o, o], out_shape=out_shape,
                              compiler_params=_params("parallel", "parallel"))(h, wg, wu)
    plumb = side.plumb(3, 3)
    res = pl.pallas_call(
        body, name=name, grid=grid, in_specs=in_specs + plumb["in_specs"], out_specs=[o, o, o] + plumb["out_specs"],
        out_shape=out_shape + side.out_shapes, scratch_shapes=plumb["scratch"], input_output_aliases=plumb["aliases"],
        compiler_params=_params("arbitrary", "arbitrary"),
    )(h, wg, wu, *plumb["args"])
    return res[:3], res[3:]


def _swiglu_bwd_epilogue(da, gt, up):
    gt = gt.astype(F32)
    up = up.astype(F32)
    return da * up * _dsilu(gt), da * _silu(gt)


GATE_CB = 7


def _gatemix_fwd(z, b_gate, ya, yb, *, name, tm=512):
    T, D = ya.shape
    tm = min(tm, T)
    row = pl.BlockSpec((tm, D), lambda i: (i, 0))

    def body(z0, z1, b0, b1, ya_ref, yb_ref, y_ref):
        g0 = _sigmoid(z0[...] + b0[...])
        g1 = _sigmoid(z1[...] + b1[...])
        y_ref[...] = (g0 * ya_ref[...] + g1 * yb_ref[...]).astype(y_ref.dtype)

    bs = lambda c: pl.BlockSpec((1, D), lambda i, c=c: (0, c))
    return pl.pallas_call(
        body, name=name, grid=(T // tm,),
        in_specs=[_col(z, tm, D, GATE_CB), _col(z, tm, D, GATE_CB + 1), bs(0), bs(1), row, row],
        out_specs=row, out_shape=jax.ShapeDtypeStruct((T, D), BF16), compiler_params=_params("parallel"),
    )(z, z, b_gate, b_gate, ya, yb)


def _gatemix_bwd(z, b_gate, ya, yb, dy, *, name, tm=512):
    T, D = ya.shape
    tm = min(tm, T)
    row = pl.BlockSpec((tm, D), lambda i: (i, 0))
    part = pl.BlockSpec((SUBLANE, D), lambda i: (0, 0))

    def body(z0, z1, b0, b1, ya_ref, yb_ref, dy_ref, dya, dyb, dz0, dz1, s0, s1):
        g0 = _sigmoid(z0[...] + b0[...])
        g1 = _sigmoid(z1[...] + b1[...])
        dyv = dy_ref[...]
        dya[...] = (dyv * g0).astype(BF16)
        dyb[...] = (dyv * g1).astype(BF16)
        d0 = dyv * ya_ref[...] * (g0 * (1.0 - g0))
        d1 = dyv * yb_ref[...] * (g1 * (1.0 - g1))
        dz0[...] = d0.astype(BF16)
        dz1[...] = d1.astype(BF16)

        @pl.when(pl.program_id(0) == 0)
        def _():
            s0[...] = jnp.zeros_like(s0)
            s1[...] = jnp.zeros_like(s1)

        s0[...] += _fold8(d0)
        s1[...] += _fold8(d1)

    bs = lambda c: pl.BlockSpec((1, D), lambda i, c=c: (0, c))
    act = jax.ShapeDtypeStruct((T, D), BF16)
    ps = jax.ShapeDtypeStruct((SUBLANE, D), F32)
    return pl.pallas_call(
        body, name=name, grid=(T // tm,),
        in_specs=[_col(z, tm, D, GATE_CB), _col(z, tm, D, GATE_CB + 1), bs(0), bs(1), row, row, row],
        out_specs=[row, row, row, row, part, part], out_shape=[act, act, act, act, ps, ps],
        compiler_params=_params("arbitrary"),
    )(z, z, b_gate, b_gate, ya, yb, dy)


def _loss_head(x, target, *, name, tm=512):
    T, D = x.shape
    tm = min(tm, T)
    row = pl.BlockSpec((tm, D), lambda i: (i, 0))

    def body(x_ref, t_ref, dx_ref, s_ref):
        e = x_ref[...] - t_ref[...]
        dx_ref[...] = e * (1.0 / D)

        @pl.when(pl.program_id(0) == 0)
        def _():
            s_ref[...] = jnp.zeros_like(s_ref)

        s_ref[...] += _fold8(e * e)

    return pl.pallas_call(
        body, name=name, grid=(T // tm,), in_specs=[row, row],
        out_specs=[row, pl.BlockSpec((SUBLANE, D), lambda i: (0, 0))],
        out_shape=[jax.ShapeDtypeStruct((T, D), F32), jax.ShapeDtypeStruct((SUBLANE, D), F32)],
        compiler_params=_params("arbitrary"),
    )(x, target)


def _tri(n, reverse):
    r = lax.broadcasted_iota(jnp.int32, (n, n), 0)
    c = lax.broadcasted_iota(jnp.int32, (n, n), 1)
    return jnp.where((c >= r) if reverse else (c <= r), 1.0, 0.0).astype(F32)


def _cumsum_t(xs, *, name, width, pre, reverse=False, rows=(), post=None, out_dtypes=(F32,), fold=None, tb=256):
    T = xs[0][0].shape[0]
    tb = min(tb, T)
    nb = T // tb
    tblk = (lambda i: nb - 1 - i) if reverse else (lambda i: i)
    args = [a for a, _ in xs] + [a for a, _ in rows]
    in_specs = [pl.BlockSpec((tb, width), lambda i, cb=cb: (tblk(i), cb)) for _, cb in xs]
    in_specs += [pl.BlockSpec((1, width), lambda i, cb=cb: (0, cb)) for _, cb in rows]
    nin, nout = len(args), len(out_dtypes)

    def body(*refs):
        vals = [r[...] for r in refs[:nin]]
        outs = refs[nin:nin + nout]
        carry = refs[-1]
        first = pl.program_id(0) == 0

        @pl.when(first)
        def _():
            carry[...] = jnp.zeros_like(carry)

        cum = jnp.dot(_tri(tb, reverse), pre(*vals), precision=HIGHEST, preferred_element_type=F32) + carry[...]
        carry[...] = cum[0:1, :] if reverse else cum[tb - 1:tb, :]
        res = (cum,) if post is None else post(cum, *vals)
        for o, r in zip(outs, res):
            o[...] = r.astype(o.dtype)
        if fold is not None:
            f_ref = refs[nin + nout]

            @pl.when(first)
            def _():
                f_ref[...] = jnp.zeros_like(f_ref)

            f_ref[...] += _fold8(fold(cum, *vals))

    tspec = pl.BlockSpec((tb, width), lambda i: (tblk(i), 0))
    out_shape = [jax.ShapeDtypeStruct((T, width), dt) for dt in out_dtypes]
    out_specs = [tspec] * nout
    if fold is not None:
        out_shape.append(jax.ShapeDtypeStruct((SUBLANE, width), F32))
        out_specs.append(pl.BlockSpec((SUBLANE, width), lambda i: (0, 0)))
    res = pl.pallas_call(
        body, name=name, grid=(nb,), in_specs=in_specs, out_specs=out_specs, out_shape=out_shape,
        scratch_shapes=[pltpu.VMEM((1, width), F32)], compiler_params=_params("arbitrary"),
    )(*args)
    return res[0] if len(res) == 1 else res


def _logsigmoid(v):
    return jnp.minimum(v, 0.0) - jnp.log(1.0 + jnp.exp(-jnp.abs(v)))


HG_TB = 256
HG_HB = 4
HG_W = HG_HB * DH
HG_GROUPS = HEADS // HG_HB
HG_Q_CB, HG_F_CB, HG_I_CB = 0, HG_GROUPS, 2 * HG_GROUPS
NEG = -1e30


def _scan16(x, rowid, reverse=False):
    for k in [1 << n for n in range(SUB.bit_length() - 1)]:
        if reverse:
            x = x + jnp.where(rowid < SUB - k, pltpu.roll(x, SUB - k, 0), 0.0)
        else:
            x = x + jnp.where(rowid >= k, pltpu.roll(x, k, 0), 0.0)
    return x


def _hg_block(q_ref, f_ref, i_ref, lb_ref, rows, cols, rowid):
    lb = lb_ref[:, cols]
    qr = q_ref[rows, cols]
    sg = _sigmoid(f_ref[rows, cols])
    f = lb + (1.0 - lb) * sg
    b = _scan16(jnp.log(f), rowid)
    return _silu(qr), 1.0 - f, i_ref[rows, cols], b, qr, sg, f, lb


def _hg_specs(tb, tmap):
    return [pl.BlockSpec((tb, HG_W), lambda g, t: (tmap(t), HG_Q_CB + g)),
            pl.BlockSpec((tb, HG_W), lambda g, t: (tmap(t), HG_F_CB + g)),
            pl.BlockSpec((tb, HG_W), lambda g, t: (tmap(t), HG_I_CB + g)),
            pl.BlockSpec((1, HG_W), lambda g, t: (0, g))]


def _hgrn2_fwd(z, lb_row, *, name):
    T = z.shape[0]
    tb = min(HG_TB, T)
    nb, nsub = T // tb, tb // SUB

    def body(q_ref, f_ref, i_ref, lb_ref, o_ref, st_ref, state):
        @pl.when(pl.program_id(1) == 0)
        def _():
            state[...] = jnp.zeros_like(state)

        rowid = lax.broadcasted_iota(jnp.int32, (SUB, DH), 0)

        def step(c, carry):
            rows = pl.ds(pl.multiple_of(c * SUB, SUB), SUB)
            for hh in range(HG_HB):
                cols = slice(hh * DH, (hh + 1) * DH)
                q, k, iv, b = _hg_block(q_ref, f_ref, i_ref, lb_ref, rows, cols, rowid)[:4]
                bl = b[SUB - 1:SUB, :]
                sv = state[hh]
                st_ref[c, hh] = sv
                o = _dot(q * jnp.exp(b), sv, NT)
                for s in range(SUB):
                    e = jnp.exp(jnp.where(rowid >= s, b - b[s:s + 1, :], NEG))
                    a = jnp.sum(q * e * k[s:s + 1, :], axis=-1, keepdims=True)
                    o = o + a * iv[s:s + 1, :]
                o_ref[rows, cols] = o
                state[hh] = sv * jnp.exp(bl) + _dot(iv, k * jnp.exp(bl - b), TN)
            return carry

        lax.fori_loop(0, nsub, step, 0)

    return pl.pallas_call(
        body, name=name, grid=(HG_GROUPS, nb), in_specs=_hg_specs(tb, lambda t: t),
        out_specs=[pl.BlockSpec((tb, HG_W), lambda g, t: (t, g)),
                   pl.BlockSpec((nsub, HG_HB, DH, DH), lambda g, t: (t, g, 0, 0))],
        out_shape=[jax.ShapeDtypeStruct((T, D_MODEL), F32), jax.ShapeDtypeStruct((T // SUB, HEADS, DH, DH), F32)],
        scratch_shapes=[pltpu.VMEM((HG_HB, DH, DH), F32)], compiler_params=_params("parallel", "arbitrary"),
    )(z, z, z, lb_row)


def _hgrn2_bwd(z, lb_row, states, do, *, name):
    T = z.shape[0]
    tb = min(HG_TB, T)
    nb, nsub = T // tb, tb // SUB
    rev = lambda t: nb - 1 - t

    def body(q_ref, f_ref, i_ref, lb_ref, st_ref, do_ref, dq_ref, dfl_ref, di_ref, dlb_ref, dstate, later):
        @pl.when(pl.program_id(1) == 0)
        def _():
            dstate[...] = jnp.zeros_like(dstate)
            later[...] = jnp.zeros_like(later)
            dlb_ref[...] = jnp.zeros_like(dlb_ref)

        rowid = lax.broadcasted_iota(jnp.int32, (SUB, DH), 0)

        def step(cc, carry):
            c = nsub - 1 - cc
            rows = pl.ds(pl.multiple_of(c * SUB, SUB), SUB)
            for hh in range(HG_HB):
                cols = slice(hh * DH, (hh + 1) * DH)
                q, k, iv, b, qr, sg, f, lb = _hg_block(q_ref, f_ref, i_ref, lb_ref, rows, cols, rowid)
                bl = b[SUB - 1:SUB, :]
                eb, ebl = jnp.exp(b), jnp.exp(bl - b)
                sv, dsv = st_ref[c, hh], dstate[hh]
                dov = do_ref[rows, cols]
                dq = _dot(dov, sv, NN) * eb
                dk = _dot(iv, dsv, NN) * ebl
                di = _dot(k * ebl, dsv, NT)
                for s in range(SUB):
                    e = jnp.exp(jnp.where(rowid >= s, b - b[s:s + 1, :], NEG))
                    ks, isv = k[s:s + 1, :], iv[s:s + 1, :]
                    qe = q * e
                    a = jnp.sum(qe * ks, axis=-1, keepdims=True)
                    p = jnp.sum(dov * isv, axis=-1, keepdims=True)
                    dq = dq + p * (e * ks)
                    dks = jnp.sum(p * qe, axis=0, keepdims=True)
                    dis = jnp.sum(a * dov, axis=0, keepdims=True)
                    dk = dk + jnp.where(rowid == s, dks, 0.0)
                    di = di + jnp.where(rowid == s, dis, 0.0)
                dlogf = _scan16(q * dq - k * dk, rowid, reverse=True) + later[hh]
                df = dlogf / f - dk
                dlb_ref[:, cols] += jnp.sum(df * (1.0 - sg), axis=0, keepdims=True)
                dfl_ref[rows, cols] = (df * (1.0 - lb) * (sg * (1.0 - sg))).astype(BF16)
                dq_ref[rows, cols] = (dq * _dsilu(qr)).astype(BF16)
                di_ref[rows, cols] = di.astype(BF16)
                dnew = dsv * jnp.exp(bl) + _dot(dov, q * eb, TN)
                dstate[hh] = dnew
                later[hh] = jnp.sum(dnew * sv, axis=0, keepdims=True)
            return carry

        lax.fori_loop(0, nsub, step, 0)

    tile = pl.BlockSpec((tb, HG_W), lambda g, t: (rev(t), g))
    act = jax.ShapeDtypeStruct((T, D_MODEL), BF16)
    return pl.pallas_call(
        body, name=name, grid=(HG_GROUPS, nb),
        in_specs=_hg_specs(tb, rev) + [pl.BlockSpec((nsub, HG_HB, DH, DH), lambda g, t: (rev(t), g, 0, 0)), tile],
        out_specs=[tile, tile, tile, pl.BlockSpec((1, HG_W), lambda g, t: (0, g))],
        out_shape=[act, act, act, jax.ShapeDtypeStruct((1, D_MODEL), F32)],
        scratch_shapes=[pltpu.VMEM((HG_HB, DH, DH), F32), pltpu.VMEM((HG_HB, 1, DH), F32)],
        compiler_params=_params("parallel", "arbitrary"),
    )(z, z, z, lb_row, states, do)


FOX_Q_CB, FOX_K_CB, FOX_V_CB = 4 * HEADS, 5 * HEADS, 6 * HEADS
FOX_SCALE = 1.0 / math.sqrt(DH)


def _fox_tile(T):
    return 512 if T >= 2048 else 128


def _fox_pairs(nq, by_query):
    if by_query:
        pairs = [(i, j) for i in range(nq) for j in range(i + 1)]
    else:
        pairs = [(i, j) for j in range(nq) for i in range(j, nq)]
    return (jnp.asarray([p[0] for p in pairs], jnp.int32), jnp.asarray([p[1] for p in pairs], jnp.int32))


LOG2E = 1.4426950408889634
FOX_RC = 64


def _fox_q2(q):
    return (q * (FOX_SCALE * LOG2E)).astype(BF16)


FOX_ZERO = -200.0


def _fox_norms(z, *, name):
    T = z.shape[0]
    tq = _fox_tile(T)
    nq = T // tq

    def body(q_ref, k_ref, nq_ref, nk_ref):
        head_of_col = lax.broadcasted_iota(jnp.int32, (D_MODEL, LANE), 0) // DH
        pick = jnp.where(head_of_col == lax.broadcasted_iota(jnp.int32, (D_MODEL, LANE), 1), 1.0, 0.0).astype(BF16)

        def tile_max(v):
            v = v.astype(F32)
            sq = _dot(v * v, pick, NN)
            return jnp.broadcast_to(jnp.max(jnp.sqrt(sq), axis=0, keepdims=True), (SUBLANE, LANE))

        nq_ref[...] = tile_max(_fox_q2(q_ref[...]))
        nk_ref[...] = tile_max(k_ref[...].astype(BF16))

    out = jax.ShapeDtypeStruct((nq * SUBLANE, LANE), F32)
    spec = pl.BlockSpec((SUBLANE, LANE), lambda i: (i, 0))
    a, b = pl.pallas_call(
        body, name=name, grid=(nq,),
        in_specs=[pl.BlockSpec((tq, D_MODEL), lambda i: (i, FOX_Q_CB // HEADS)),
                  pl.BlockSpec((tq, D_MODEL), lambda i: (i, FOX_K_CB // HEADS))],
        out_specs=[spec, spec], out_shape=[out, out], compiler_params=_params("parallel"),
    )(z, z)
    return a[::SUBLANE, :HEADS], b[::SUBLANE, :HEADS]


def _fox_schedule(norm_q, norm_k, ct, tq):
    nq = ct.shape[1] // tq
    first, last = ct[:, ::tq], ct[:, tq - 1::tq]
    nqh, nkh = norm_q.T * 1.05, norm_k.T * 1.05
    bound = nqh[:, :, None] * (nkh[:, None, :] + nkh[:, :, None]) + first[:, :, None] - last[:, None, :]
    tri = jnp.arange(nq)[:, None] > jnp.arange(nq)[None, :]
    drop = (bound < FOX_ZERO) & tri[None]
    lo = jnp.argmin(drop, axis=2).astype(jnp.int32)
    dropped = jnp.arange(nq)[None, None, :] < lo[:, :, None]
    qf, kf = _fox_pairs(nq, by_query=True)
    qb, kb = _fox_pairs(nq, by_query=False)
    fetch_k = jnp.maximum(kf[None, :], lo[:, qf])
    kept_q = jnp.where(dropped | ~(tri | jnp.eye(nq, dtype=bool))[None], -1, jnp.arange(nq)[None, :, None])
    last_kept = lax.cummax(kept_q, axis=1)
    fetch_q = last_kept[:, qb, kb]
    i32 = lambda a: a.astype(jnp.int32)
    return i32(fetch_k), i32(dropped[:, qf, kf]), i32(fetch_q), i32(dropped[:, qb, kb])


def _fox_fwd(z, c_col, c_row, fetch_k, skip, *, name):
    T = z.shape[0]
    tq = _fox_tile(T)
    nq = T // tq
    rc = min(FOX_RC, tq)

    qi, kj = _fox_pairs(nq, by_query=True)

    def body(qi_ref, kj_ref, fk_ref, skip_ref, q_ref, k_ref, v_ref, cc_ref, cr_ref, o_ref, lse_ref, m_scr, l_scr, acc,
             a_scr, s_scr, p_scr):
        p_id = pl.program_id(1)
        i, j = qi_ref[p_id], kj_ref[p_id]
        live = skip_ref[pl.program_id(0), p_id] == 0

        @pl.when(j == 0)
        def _():
            m_scr[...] = jnp.full_like(m_scr, NEG)
            l_scr[...] = jnp.zeros_like(l_scr)
            acc[...] = jnp.zeros_like(acc)

        def update(masked):
            bias = cc_ref[0:1, :] - cr_ref[...]
            s_scr[...] = _dot(_fox_q2(q_ref[...]), k_ref[...], NT)
            for r in range(tq // rc):
                rows = slice(r * rc, (r + 1) * rc)
                t = s_scr[rows, :] + bias
                if masked:
                    t = jnp.where(lax.broadcasted_iota(jnp.int32, (rc, tq), 1)
                                  <= r * rc + lax.broadcasted_iota(jnp.int32, (rc, tq), 0), t, NEG)
                m_old = m_scr[rows, :]
                m_new = jnp.maximum(m_old, jnp.max(t, axis=-1, keepdims=True))
                alpha = jnp.exp2(m_old - m_new)
                p = jnp.exp2(t - jnp.tile(m_new, (1, tq // LANE)))
                l_scr[rows, :] = alpha * l_scr[rows, :] + jnp.sum(p, axis=-1, keepdims=True)
                a_scr[rows, :] = alpha
                p_scr[rows, :] = p.astype(BF16)
                m_scr[rows, :] = m_new
            acc[...] = a_scr[...] * acc[...] + _dot(p_scr[...], v_ref[...], NN)

        @pl.when((j < i) & live)
        def _():
            update(False)

        @pl.when(j == i)
        def _():
            update(True)
            o_ref[...] = acc[...] / l_scr[...]
            lse_ref[...] = (m_scr[:, 0:1] + jnp.log2(l_scr[:, 0:1])) + (cc_ref[...] - cc_ref[0:1, :])

    qtile = lambda cb: pl.BlockSpec((tq, DH), lambda h, p, qi, kj, fk, sk, cb=cb: (qi[p], cb + h))
    ktile = lambda cb: pl.BlockSpec((tq, DH), lambda h, p, qi, kj, fk, sk, cb=cb: (fk[h, p], cb + h))
    qcol = pl.BlockSpec((None, tq, 1), lambda h, p, qi, kj, fk, sk: (h, qi[p], 0))
    return pl.pallas_call(
        body, name=name,
        grid_spec=pltpu.PrefetchScalarGridSpec(
            num_scalar_prefetch=4, grid=(HEADS, qi.shape[0]),
            in_specs=[qtile(FOX_Q_CB), ktile(FOX_K_CB), ktile(FOX_V_CB), qcol,
                      pl.BlockSpec((None, 1, tq), lambda h, p, qi, kj, fk, sk: (h, 0, fk[h, p]))],
            out_specs=[qtile(0), qcol],
            scratch_shapes=[pltpu.VMEM((tq, LANE), F32), pltpu.VMEM((tq, LANE), F32), pltpu.VMEM((tq, DH), F32),
                            pltpu.VMEM((tq, LANE), F32), pltpu.VMEM((tq, tq), F32), pltpu.VMEM((tq, tq), BF16)]),
        out_shape=[jax.ShapeDtypeStruct((T, D_MODEL), F32), jax.ShapeDtypeStruct((HEADS, T, 1), F32)],
        compiler_params=_params("parallel", "arbitrary"),
    )(qi, kj, fetch_k, skip, z, z, z, c_col, c_row)


def _fox_bwd(z, c_col, c_row, o, lse, do, fetch_q, skip, *, name):
    T = z.shape[0]
    tq = _fox_tile(T)
    nq = T // tq
    rc = min(FOX_RC, tq)

    qi, kj = _fox_pairs(nq, by_query=False)

    def body(qi_ref, kj_ref, fq_ref, skip_ref, q_ref, k_ref, v_ref, cc_ref, cr_ref, o_ref, lse_ref, do_ref, dq_ref,
             dk_ref, dv_ref, dc_ref, dcq_ref, dk_acc, dv_acc, dc_acc, s_scr, dp_scr, p_scr, ds_scr, dcq_scr):
        p_id = pl.program_id(1)
        i, j = qi_ref[p_id], kj_ref[p_id]
        live = skip_ref[pl.program_id(0), p_id] == 0

        @pl.when(p_id == 0)
        def _():
            dq_ref[...] = jnp.zeros_like(dq_ref)
            dcq_scr[...] = jnp.zeros_like(dcq_scr)

        def update(masked):
            q2, k, dov = _fox_q2(q_ref[...]), k_ref[...], do_ref[...]
            s_scr[...] = _dot(q2, k, NT)
            dp_scr[...] = _dot(dov, v_ref[...], NT)
            crow = cr_ref[...]
            csum = jnp.zeros((SUBLANE, tq), F32)
            wide = lambda col: jnp.tile(jnp.broadcast_to(col, (rc, LANE)), (1, tq // LANE))
            for r in range(tq // rc):
                rows = slice(r * rc, (r + 1) * rc)
                t = (s_scr[rows, :] + wide(cc_ref[rows, :] - lse_ref[rows, :])) - crow
                if masked:
                    t = jnp.where(lax.broadcasted_iota(jnp.int32, (rc, tq), 1)
                                  <= r * rc + lax.broadcasted_iota(jnp.int32, (rc, tq), 0), t, NEG)
                p = jnp.exp2(t)
                delta = jnp.sum(do_ref[rows, :] * o_ref[rows, :], axis=-1, keepdims=True)
                ds = p * (dp_scr[rows, :] - wide(delta))
                p_scr[rows, :] = p.astype(BF16)
                ds_scr[rows, :] = ds.astype(BF16)
                grows = pl.ds(pl.multiple_of(i * tq + r * rc, rc), rc)
                dcq_scr[grows, :] += jnp.broadcast_to(jnp.sum(ds, axis=-1, keepdims=True), (rc, LANE))
                csum = csum + _fold8(ds)
            dsb = ds_scr[...]
            dv_new = _dot(p_scr[...], dov, TN)
            dk_new = _dot(dsb, q2, TN) * (1.0 / LOG2E)
            dc_new = -jnp.sum(csum, axis=0, keepdims=True)
            rows = pl.ds(pl.multiple_of(i * tq, tq), tq)
            dq_ref[rows, :] += _dot(dsb, k, NN) * FOX_SCALE
            return dk_new, dv_new, dc_new

        @pl.when(i == j)
        def _():
            dk_new, dv_new, dc_new = update(True)
            dk_acc[...] = dk_new
            dv_acc[...] = dv_new
            dc_acc[...] = dc_new

        @pl.when((i > j) & live)
        def _():
            dk_new, dv_new, dc_new = update(False)
            dk_acc[...] += dk_new
            dv_acc[...] += dv_new
            dc_acc[...] += dc_new

        @pl.when(i == nq - 1)
        def _():
            dk_ref[...] = dk_acc[...].astype(BF16)
            dv_ref[...] = dv_acc[...].astype(BF16)
            dc_ref[...] = dc_acc[...]

        @pl.when(p_id == qi.shape[0] - 1)
        def _():
            dcq_ref[...] = dcq_scr[:, 0:1]

    qtile = lambda cb: pl.BlockSpec((tq, DH), lambda h, p, qi, kj, fq, sk, cb=cb: (fq[h, p], cb + h))
    ktile = lambda cb: pl.BlockSpec((tq, DH), lambda h, p, qi, kj, fq, sk, cb=cb: (kj[p], cb + h))
    qcol = pl.BlockSpec((None, tq, 1), lambda h, p, qi, kj, fq, sk: (h, fq[h, p], 0))
    krow = pl.BlockSpec((None, 1, tq), lambda h, p, qi, kj, fq, sk: (h, 0, kj[p]))
    return pl.pallas_call(
        body, name=name,
        grid_spec=pltpu.PrefetchScalarGridSpec(
            num_scalar_prefetch=4, grid=(HEADS, qi.shape[0]),
            in_specs=[qtile(FOX_Q_CB), ktile(FOX_K_CB), ktile(FOX_V_CB), qcol, krow, qtile(0), qcol, qtile(0)],
            out_specs=[pl.BlockSpec((T, DH), lambda h, p, qi, kj, fq, sk: (0, h)), ktile(0), ktile(0), krow,
                       pl.BlockSpec((None, T, 1), lambda h, p, qi, kj, fq, sk: (h, 0, 0))],
            scratch_shapes=[pltpu.VMEM((tq, DH), F32), pltpu.VMEM((tq, DH), F32), pltpu.VMEM((1, tq), F32),
                            pltpu.VMEM((tq, tq), F32), pltpu.VMEM((tq, tq), F32), pltpu.VMEM((tq, tq), BF16),
                            pltpu.VMEM((tq, tq), BF16), pltpu.VMEM((T, LANE), F32)]),
        out_shape=[jax.ShapeDtypeStruct((T, D_MODEL), F32), jax.ShapeDtypeStruct((T, D_MODEL), BF16),
                   jax.ShapeDtypeStruct((T, D_MODEL), BF16), jax.ShapeDtypeStruct((HEADS, 1, T), F32),
                   jax.ShapeDtypeStruct((HEADS, T, 1), F32)],
        compiler_params=_params("parallel", "arbitrary"),
    )(qi, kj, fetch_q, skip, z, z, z, c_col, c_row, o, lse, do)


MEM_SCALE = 1.0 / math.sqrt(MEM_DH)


def _mem_probs(qh, kh):
    s = _dot(qh, kh, NT) * MEM_SCALE
    p = jnp.exp(s - jnp.max(s, axis=-1, keepdims=True))
    return p / jnp.sum(p, axis=-1, keepdims=True)


def _mem_fwd(q, kv, *, name, tq=512):
    T = q.shape[0]
    tq = min(tq, T)

    def body(q_ref, kv_ref, o_ref):
        for h in range(MEM_HEADS):
            cols = slice(h * MEM_DH, (h + 1) * MEM_DH)
            vcols = slice(D_MODEL + h * MEM_DH, D_MODEL + (h + 1) * MEM_DH)
            p = _mem_probs(q_ref[:, cols], kv_ref[:, cols])
            o_ref[:, cols] = _dot(p, kv_ref[:, vcols], NN).astype(o_ref.dtype)

    return pl.pallas_call(
        body, name=name, grid=(T // tq,),
        in_specs=[pl.BlockSpec((tq, D_MODEL), lambda i: (i, 0)), pl.BlockSpec((MEM_LEN, 2 * D_MODEL), lambda i: (0, 0))],
        out_specs=pl.BlockSpec((tq, D_MODEL), lambda i: (i, 0)), out_shape=jax.ShapeDtypeStruct((T, D_MODEL), BF16),
        compiler_params=_params("parallel"),
    )(q, kv)


def _mem_bwd(q, kv, do, *, name, tq=512):
    T = q.shape[0]
    tq = min(tq, T)

    def body(q_ref, kv_ref, do_ref, dq_ref, dkv_ref):
        @pl.when(pl.program_id(0) == 0)
        def _():
            dkv_ref[...] = jnp.zeros_like(dkv_ref)

        for h in range(MEM_HEADS):
            cols = slice(h * MEM_DH, (h + 1) * MEM_DH)
            vcols = slice(D_MODEL + h * MEM_DH, D_MODEL + (h + 1) * MEM_DH)
            qh, kh, doh = q_ref[:, cols], kv_ref[:, cols], do_ref[:, cols]
            p = _mem_probs(qh, kh)
            dp = _dot(doh, kv_ref[:, vcols], NT)
            ds = p * (dp - jnp.sum(p * dp, axis=-1, keepdims=True))
            dq_ref[:, cols] = (_dot(ds, kh, NN) * MEM_SCALE).astype(dq_ref.dtype)
            dkv_ref[:, cols] += _dot(ds, qh, TN) * MEM_SCALE
            dkv_ref[:, vcols] += _dot(p, doh, TN)

    row = pl.BlockSpec((tq, D_MODEL), lambda i: (i, 0))
    full = pl.BlockSpec((MEM_LEN, 2 * D_MODEL), lambda i: (0, 0))
    return pl.pallas_call(
        body, name=name, grid=(T // tq,), in_specs=[row, full, row], out_specs=[row, full],
        out_shape=[jax.ShapeDtypeStruct((T, D_MODEL), BF16), jax.ShapeDtypeStruct((MEM_LEN, 2 * D_MODEL), F32)],
        compiler_params=_params("arbitrary"),
    )(q, kv, do)


def _adamw(w, g, m, v, *, name, tm=256):
    R, C = w.shape
    tm = min(tm, R)
    assert R % tm == 0
    nsum = g.shape[0] if g.ndim == 3 else 0

    def body(w_ref, g_ref, m_ref, v_ref, go_ref, d_ref, mo_ref, vo_ref):
        if nsum:
            gv = g_ref[0]
            for n in range(1, nsum):
                gv = gv + g_ref[n]
        else:
            gv = g_ref[...]
        mv = ADAM_B1 * m_ref[...] + (1.0 - ADAM_B1) * gv
        vv = ADAM_B2 * v_ref[...] + (1.0 - ADAM_B2) * jnp.square(gv)
        m_hat = mv / (1.0 - ADAM_B1 ** ADAM_STEP)
        v_hat = vv / (1.0 - ADAM_B2 ** ADAM_STEP)
        d_ref[...] = -ADAM_LR * (m_hat / (jnp.sqrt(v_hat) + ADAM_EPS) + ADAM_WD * w_ref[...])
        go_ref[...] = gv
        mo_ref[...] = mv
        vo_ref[...] = vv

    row = pl.BlockSpec((tm, C), lambda i: (i, 0))
    gspec = pl.BlockSpec((nsum, tm, C), lambda i: (0, i, 0)) if nsum else row
    return pl.pallas_call(
        body, name=name, grid=(R // tm,), in_specs=[row, gspec, row, row], out_specs=[row] * 4,
        out_shape=[jax.ShapeDtypeStruct((R, C), F32)] * 4, compiler_params=_params("parallel"),
    )(w, g, m, v)


def _act_mm(a, w, name, out_dtype=F32, side=None):
    res = _mm([(a, w)], "nn", tm=1024, tn=512, tk=w.shape[0], out_dtypes=[out_dtype], name=name, side=side)
    return res if side is None else (res[0][0], res[1])


def _act_mm_t(a, w, name, out_dtype=F32):
    return _mm([(a, w)], "nt", tm=1024, tn=512, tk=1024, out_dtypes=[out_dtype], name=name)


def _wgrad(a, dy, name, tm=1024):
    tn = D_MODEL if dy.shape[1] % D_MODEL == 0 else D_FF // 2
    return _mm([(a, dy)], "tn", tm=tm, tn=tn, tk=1024, out_dtypes=[F32], name=name)


def _colsum8(p):
    return jnp.sum(p, axis=0, keepdims=True)


def _ffn_fwd(x, pre_g, post_g, wg, wu, wd, tag, gather_beside=None):
    h = _rms_fwd(x, pre_g, out_dtype=BF16, name=tag + "_pre")
    down = functools.partial(_mm, mode="nn", tm=1024, tn=512, tk=D_FF, out_dtypes=[F32], name=tag + "_down")
    gathered = None
    if gather_beside is None:
        act, gate, up = _ffn_in(h, wg, wu, name=tag + "_in")
        d = down([(act, wd)])
    else:
        (act, gate, up), landed = _ffn_in(h, wg, wu, name=tag + "_in", side=_x_gather(gather_beside))
        (d,), gathered = down([(act, wd)], side=_x_forward(landed))
    xo = _rms_fwd(d, post_g, out_dtype=F32, name=tag + "_post", res=x, coeff=0.5)
    return xo, (h, act, gate, up, d), gathered


def _ffn_bwd(x, dxo, saved, pre_g, post_g, wg, wu, wd, tag, reduce_beside=None):
    h, act, gate, up, d = saved
    dd, dg_post = _rms_bwd(d, post_g, dxo, name=tag + "_post_b", coeff=0.5, dx_dtype=BF16)
    act_b = functools.partial(_mm, [(dd, wd)], "nt", tm=1024, tn=256, tk=D_MODEL, out_dtypes=[BF16, BF16],
                              name=tag + "_act_b", epilogue=_swiglu_bwd_epilogue, tiles=(gate, up))
    in_b = lambda dgate, dup, **kw: _mm([(dgate, wg), (dup, wu)], "nt", tm=512, tn=512, tk=D_FF, out_dtypes=[F32],
                                        name=tag + "_in_b", **kw)
    reduced = None
    if reduce_beside is None:
        dgate, dup = act_b()
        dh = in_b(dgate, dup)
    else:
        grads, place, core = reduce_beside
        (dgate, dup), swapped = act_b(side=_x_swap(grads))
        pbf, own = _pair_sums(grads, swapped, place, "early")
        (dh,), scattered = in_b(dgate, dup, side=_x_scatter(pbf))
        reduced = _finish_reduce(own, scattered, core, "early")
    dwd = _wgrad(act, dd, tag + "_dwd", tm=D_FF // 2)
    dwg = _wgrad(h, dgate, tag + "_dwg")
    dwu = _wgrad(h, dup, tag + "_dwu")
    dx, dg_pre = _rms_bwd(x, pre_g, dh, name=tag + "_pre_b", add=dxo)
    return dx, dict(pre_g=_colsum8(dg_pre), post_g=_colsum8(dg_post), wg=dwg, wu=dwu, wd=dwd), reduced


def _local_step(x, mem, target, P, own, me, place, core):
    T = x.shape[0]
    G = {}
    logits = P["hg_lb_logits"]
    lb = _sigmoid(logits[0] - logits[1])
    lb_row = lb.reshape(1, D_MODEL)
    fbias_row = jnp.pad(P["fox_f_bias"], ((0, 0), (0, LANE - HEADS)))
    arrived = lambda group, others: _local_names(_assemble(group, own[group], others, me))

    W = arrived(GROUP_FFN1, _run(_x_forward(_run(_x_gather(own[GROUP_FFN1]), "gather_ffn1")), "forward_ffn1"))
    x1, ffn1_saved, others = _ffn_fwd(x, P["ffn1_pre_g"], P["ffn1_post_g"], W["f1g"], W["f1u"], W["f1d"], "ffn1",
                                      gather_beside=own[GROUP_MID])
    W.update(arrived(GROUP_MID, others))
    h2 = _rms_fwd(x1, P["mix_pre_g"], out_dtype=BF16, name="mix_pre")
    z, landed = _act_mm(h2, W["w_main"], "mix_in", side=_x_gather(own[GROUP_FFN2]))
    zfb = _mm([(h2, W["w_fb"])], "nn", tm=1024, tn=LANE, tk=D_MODEL, out_dtypes=[F32], name="mix_in_fb")
    oa_pre, states = _hgrn2_fwd(z, lb_row, name="hgrn2_f")
    o_a = _rms_fwd(oa_pre, P["hg_norm_g"], out_dtype=BF16, name="hgrn2_post", mul=(z, 3))
    y_a, others = _act_mm(o_a, W["wa"], "branch_a", side=_x_forward(landed))
    W.update(arrived(GROUP_FFN2, others))
    c = _cumsum_t([(zfb, 0)], name="fox_c", width=LANE, rows=[(fbias_row, 0)], pre=lambda v, r: _logsigmoid(v + r),
                  post=lambda cum, v, r: (cum * LOG2E,))
    ct = c[:, :HEADS].T
    c_col, c_row = ct[:, :, None], ct[:, None, :]
    fetch_k, skip_f, fetch_q, skip_b = _fox_schedule(*_fox_norms(z, name="fox_norms"), ct, _fox_tile(T))
    o_b, lse = _fox_fwd(z, c_col, c_row, fetch_k, skip_f, name="fox_f")
    y_b = _act_mm(o_b, W["wb"], "branch_b")
    y = _gatemix_fwd(z, P["b_gate"], y_a, y_b, name="gatemix")
    m = _act_mm(y, W["wo"], "mix_out")
    x2 = _rms_fwd(m, P["mix_post_g"], out_dtype=F32, name="mix_post", res=x1)
    h3 = _rms_fwd(x2, P["mem_pre_g"], out_dtype=BF16, name="mem_pre")
    mem_n = _rms_fwd(mem, P["mem_kv_g"], out_dtype=BF16, name="mem_kvn")
    qm = _act_mm(h3, W["wmq"], "mem_q")
    kv = _act_mm(mem_n, W["wmkv"], "mem_kv")
    om = _mem_fwd(qm, kv, name="mem_attn")
    mo = _act_mm(om, W["wmo"], "mem_o")
    x3 = _rms_fwd(mo, P["mem_post_g"], out_dtype=F32, name="mem_post", res=x2)
    x4, ffn2_saved, _ = _ffn_fwd(x3, P["ffn2_pre_g"], P["ffn2_post_g"], W["f2g"], W["f2u"], W["f2d"], "ffn2")
    dx4, sq = _loss_head(x4, target, name="loss_head")

    dx3, g, _ = _ffn_bwd(x3, dx4, ffn2_saved, P["ffn2_pre_g"], P["ffn2_post_g"], W["f2g"], W["f2u"], W["f2d"], "ffn2")
    G.update(ffn2_pre_g=g["pre_g"], ffn2_post_g=g["post_g"], f2g=g["wg"], f2u=g["wu"], f2d=g["wd"])

    dmo, dgp = _rms_bwd(mo, P["mem_post_g"], dx3, name="mem_post_b", dx_dtype=BF16)
    G["mem_post_g"] = _colsum8(dgp)
    dom = _act_mm_t(dmo, W["wmo"], "mem_o_b", BF16)
    G["wmo"] = _wgrad(om, dmo, "mem_o_w")
    dqm, dkv = _mem_bwd(qm, kv, dom, name="mem_attn_b")
    dh3 = _act_mm_t(dqm, W["wmq"], "mem_q_b")
    G["wmq"] = _wgrad(h3, dqm, "mem_q_w")
    G["wmkv"] = _mm([(mem_n, dkv)], "tn", tm=1024, tn=512, tk=MEM_LEN, out_dtypes=[F32], name="mem_kv_w")
    dmem_n = _mm([(dkv, W["wmkv"])], "nt", tm=MEM_LEN, tn=512, tk=2 * D_MODEL, out_dtypes=[F32], name="mem_kv_b")
    _, dgp = _rms_bwd(mem, P["mem_kv_g"], dmem_n, name="mem_kvn_b")
    G["mem_kv_g"] = _colsum8(dgp)
    dx2, dgp = _rms_bwd(x2, P["mem_pre_g"], dh3, name="mem_pre_b", add=dx3)
    G["mem_pre_g"] = _colsum8(dgp)

    dm, dgp = _rms_bwd(m, P["mix_post_g"], dx2, name="mix_post_b", dx_dtype=BF16)
    G["mix_post_g"] = _colsum8(dgp)
    dy = _act_mm_t(dm, W["wo"], "mix_out_b")
    G["wo"] = _wgrad(y, dm, "mix_out_w")
    dya, dyb, dz0, dz1, s0, s1 = _gatemix_bwd(z, P["b_gate"], y_a, y_b, dy, name="gatemix_b")
    G["b_gate"] = jnp.concatenate([_colsum8(s0), _colsum8(s1)], axis=1)
    do_a = _act_mm_t(dya, W["wa"], "branch_a_b")
    G["wa"] = _wgrad(o_a, dya, "branch_a_w")
    do_b = _act_mm_t(dyb, W["wb"], "branch_b_b")
    G["wb"] = _wgrad(o_b, dyb, "branch_b_w")
    doa_pre, dgp, dga = _rms_bwd(oa_pre, P["hg_norm_g"], do_a, name="hgrn2_post_b", mul=(z, 3))
    G["hg_norm_g"] = _colsum8(dgp)
    dq_a, dfl_a, di_a, dlb = _hgrn2_bwd(z, lb_row, states, doa_pre, name="hgrn2_b")
    dl0 = (dlb * lb_row * (1.0 - lb_row)).reshape(1, HEADS, DH)
    G["hg_lb_logits"] = jnp.concatenate([dl0, -dl0], axis=0)
    dq_b, dk_b, dv_b, dcr, dcq = _fox_bwd(z, c_col, c_row, o_b, lse, do_b, fetch_q, skip_b, name="fox_b")
    dc_pad = jnp.pad((dcr[:, 0, :] + dcq[:, :, 0]).T, ((0, 0), (0, LANE - HEADS)))
    gate_b = lambda cum, dc, zf, r: cum * _sigmoid(-(zf + r))
    dfl_b, dfb = _cumsum_t([(dc_pad, 0), (zfb, 0)], name="fox_c_b", width=LANE, reverse=True, rows=[(fbias_row, 0)],
                           pre=lambda dc, zf, r: dc, post=lambda *a: (gate_b(*a),), fold=gate_b)
    G["fox_f_bias"] = _colsum8(dfb)[:, :HEADS]

    pieces = [dq_a, dfl_a, di_a, dga, dq_b, dk_b, dv_b, dz0, dz1]
    dh2 = _mm([(dfl_b, W["w_fb"])], "nt", tm=512, tn=D_MODEL, tk=LANE, out_dtypes=[F32], name="mix_in_fb_b")
    for lo, hi in ((0, 5), (5, 9)):
        dh2 = _mm([(p, W["w_main"]) for p in pieces[lo:hi]], "nt", tm=512, tn=D_MODEL, tk=D_MODEL, out_dtypes=[F32],
                  name=f"mix_in_b{lo}", b_koff=[n * D_MODEL for n in range(lo, hi)],
                  epilogue=lambda acc, t: (acc + t,), tiles=(dh2,))
    G["w_main"] = [_wgrad(h2, p, f"mix_in_w{n}") for n, p in enumerate(pieces)]
    G["w_fb"] = _mm([(h2, dfl_b)], "tn", tm=1024, tn=LANE, tk=512, out_dtypes=[F32], name="mix_in_fb_w")
    dx1, dgp = _rms_bwd(x1, P["mix_pre_g"], dh2, name="mix_pre_b", add=dx2)
    G["mix_pre_g"] = _colsum8(dgp)

    dx0, g, reduced = _ffn_bwd(x, dx1, ffn1_saved, P["ffn1_pre_g"], P["ffn1_post_g"], W["f1g"], W["f1u"], W["f1d"],
                               "ffn1", reduce_beside=(_grad_slabs(G, GROUP_MID + GROUP_FFN2), place, core))
    G.update(ffn1_pre_g=g["pre_g"], ffn1_post_g=g["post_g"], f1g=g["wg"], f1u=g["wu"], f1d=g["wd"])
    return sq, dx0, G, reduced


N_CHIP = 4
N_DEV = 8
IN_COLS = 9224
FB_COL = 7 * D_MODEL
SHARDED = (
    ("ffn1_w_in", (D_MODEL, 2 * D_FF), 1), ("ffn1_w_down", (D_FF, D_MODEL), 0), ("w_in", (D_MODEL, IN_COLS), 1),
    ("w_branch_a", (D_MODEL, D_MODEL), 0), ("w_branch_b", (D_MODEL, D_MODEL), 0), ("w_out", (D_MODEL, D_MODEL), 0),
    ("w_mq", (D_MODEL, D_MODEL), 0), ("w_mkv", (D_MODEL, 2 * D_MODEL), 1), ("w_mo", (D_MODEL, D_MODEL), 0),
    ("ffn2_w_in", (D_MODEL, 2 * D_FF), 1), ("ffn2_w_down", (D_FF, D_MODEL), 0),
)
SMALL = ("ffn1_pre_g", "ffn1_post_g", "mix_pre_g", "hg_norm_g", "mix_post_g", "mem_pre_g", "mem_kv_g", "mem_post_g",
         "ffn2_pre_g", "ffn2_post_g", "b_gate", "hg_lb_logits", "fox_f_bias")
SMALL_SHAPES = dict(b_gate=(1, 2 * D_MODEL), hg_lb_logits=(2, HEADS, DH), fox_f_bias=(1, HEADS))
SMALL_ROWS = 16
WEIGHT_ORDER = ("ffn1_pre_g", "ffn1_w_in", "ffn1_w_down", "ffn1_post_g", "mix_pre_g", "w_in", "hg_lb_logits", "hg_norm_g",
                "fox_f_bias", "w_branch_a", "w_branch_b", "b_gate", "w_out", "mix_post_g", "mem_pre_g", "mem_kv_g", "w_mq",
                "w_mkv", "w_mo", "mem_post_g", "ffn2_pre_g", "ffn2_w_in", "ffn2_w_down", "ffn2_post_g")


GROUP_FFN1 = ("ffn1_w_in", "ffn1_w_down")
GROUP_MID = ("w_in", "w_branch_a", "w_branch_b", "w_out", "w_mq", "w_mkv", "w_mo")
GROUP_FFN2 = ("ffn2_w_in", "ffn2_w_down")


def _layout(names, axis):
    out, at = [], 0
    for name, shape, ax in SHARDED:
        if ax == axis and name in names:
            n = shape[ax] // N_CHIP
            out.append((name, at, n))
            at += n if axis == 0 else -(-n // LANE) * LANE
    return out


def _pack(shards, names, dtype):
    rows = jnp.concatenate([shards[name].astype(dtype) for name, _, _ in _layout(names, 0)], axis=0)
    cols = [jnp.pad(shards[name].astype(dtype), ((0, 0), (0, -n % LANE))) for name, _, n in _layout(names, 1)]
    return [rows, jnp.concatenate(cols, axis=1)]


def _unpack(slabs, names):
    rows, cols = slabs
    out = {name: rows[at:at + n] for name, at, n in _layout(names, 0)}
    out.update({name: cols[:, at:at + n] for name, at, n in _layout(names, 1)})
    return out


def _pack_small(vals):
    rows = []
    for name in SMALL:
        v = vals[name].astype(F32).reshape(-1)
        rows.append(jnp.pad(v, (0, -v.shape[0] % D_MODEL)).reshape(-1, D_MODEL))
    rows = jnp.concatenate(rows, axis=0)
    return jnp.pad(rows, ((0, SMALL_ROWS - rows.shape[0]), (0, 0)))


def _unpack_small(slab):
    out, r = {}, 0
    for name in SMALL:
        shape = SMALL_SHAPES.get(name, (1, D_MODEL))
        size = math.prod(shape)
        n = -(-size // D_MODEL)
        out[name] = slab[r:r + n].reshape(-1)[:size].reshape(shape)
        r += n
    return out


MESH = pl.DeviceIdType.MESH
CHIP_FLIPS = ((0, 1), (1, 0), (1, 1))


def _place():
    x, y, c = lax.axis_index("x"), lax.axis_index("y"), lax.axis_index("c")
    chips = [(x ^ fx, y ^ fy) for fx, fy in CHIP_FLIPS]
    return x, y, c, chips


def _remote(src, dst, sems, k, dev):
    return pltpu.make_async_remote_copy(src_ref=src, dst_ref=dst, send_sem=sems[0].at[k], recv_sem=sems[1].at[k],
                                        device_id=dev, device_id_type=MESH)


def _exchange(copies, arrays, out_shapes, aliases=None):
    def start(ins, outs, sems):
        for sent, _ in copies(ins, outs, sems):
            sent.start()

    def wait(ins, outs, sems):
        pairs = copies(ins, outs, sems)
        for _, got in pairs:
            got.wait_recv()
        for sent, _ in pairs:
            sent.wait_send()

    return _Side(arrays, out_shapes, copies.count, start, wait, aliases)


def _counted(count):
    def mark(fn):
        fn.count = count
        return fn
    return mark


def _slab_halves(c, rows):
    half = rows // 2
    return pl.ds(c * half, half), pl.ds((1 - c) * half, half)


def _x_gather(slabs):
    @_counted(3 * len(slabs))
    def copies(ins, outs, sems):
        x, y, c, chips = _place()
        res = []
        for s, slab in enumerate(slabs):
            mine, _ = _slab_halves(c, slab.shape[0])
            for k, (px, py) in enumerate(chips):
                res.append((_remote(ins[s].at[mine], outs[s].at[k, mine], sems, 3 * s + k, (px, py, c)),) * 2)
        return res

    return _exchange(copies, slabs, [jax.ShapeDtypeStruct((3,) + s.shape, s.dtype) for s in slabs])


def _x_forward(gathered):
    @_counted(3 * len(gathered))
    def copies(ins, outs, sems):
        x, y, c, _ = _place()
        res = []
        for s, buf in enumerate(gathered):
            mine, theirs = _slab_halves(c, buf.shape[1])
            for k in range(3):
                res.append((_remote(ins[s].at[k, mine], outs[s].at[k, mine], sems, 3 * s + k, (x, y, 1 - c)),
                            _remote(ins[s].at[k, theirs], outs[s].at[k, theirs], sems, 3 * s + k, (x, y, 1 - c))))
        return res

    return _exchange(copies, gathered, [jax.ShapeDtypeStruct(g.shape, g.dtype) for g in gathered],
                     aliases={s: s for s in range(len(gathered))})


def _x_swap(grads):
    @_counted(N_CHIP * len(grads))
    def copies(ins, outs, sems):
        x, y, c, _ = _place()
        res = []
        for s, g in enumerate(grads):
            _, theirs = _slab_halves(c, g.shape[1])
            for j in range(N_CHIP):
                res.append((_remote(ins[s].at[j, theirs], outs[s].at[j], sems, N_CHIP * s + j, (x, y, 1 - c)),) * 2)
        return res

    return _exchange(copies, grads, [jax.ShapeDtypeStruct((N_CHIP, g.shape[1] // 2, g.shape[2]), g.dtype) for g in grads])


def _x_scatter(partials):
    @_counted(3 * len(partials))
    def copies(ins, outs, sems):
        x, y, c, chips = _place()
        res = []
        for s in range(len(partials)):
            for k, (px, py) in enumerate(chips):
                res.append((_remote(ins[s].at[2 * px + py], outs[s].at[k], sems, 3 * s + k, (px, py, c)),) * 2)
        return res

    return _exchange(copies, partials, [jax.ShapeDtypeStruct((3,) + p.shape[1:], p.dtype) for p in partials])


def _x_join(halves):
    @_counted(len(halves))
    def copies(ins, outs, sems):
        x, y, c, _ = _place()
        return [(_remote(ins[s], outs[s], sems, s, (x, y, 1 - c)),) * 2 for s in range(len(halves))]

    return _exchange(copies, halves, [jax.ShapeDtypeStruct(h.shape, h.dtype) for h in halves])


def _run(side, name):
    n_in, n_out = len(side.arrays), len(side.out_shapes)

    def body(*refs):
        ins, outs, sems = refs[:n_in], refs[n_in:n_in + n_out], refs[-2:]
        side.start(ins, outs, sems)
        side.wait(ins, outs, sems)

    plumb = side.plumb(0, 0)
    return pl.pallas_call(
        body, name=name, in_specs=plumb["in_specs"], out_specs=plumb["out_specs"], out_shape=side.out_shapes,
        scratch_shapes=plumb["scratch"], input_output_aliases=plumb["aliases"],
    )(*side.arrays)


def _pair_sum(g, got, place, tag, *, tm):
    _, half, width = got.shape
    nb = half // tm

    def body(s_ref, g_ref, a_ref, bf_ref, own_ref):
        v = g_ref[...] + a_ref[...]
        bf_ref[...] = v.astype(BF16)

        @pl.when(pl.program_id(1) == s_ref[0])
        def _():
            own_ref[...] = v

    return pl.pallas_call(
        body, name="pair_sum_" + tag,
        grid_spec=pltpu.PrefetchScalarGridSpec(
            num_scalar_prefetch=1, grid=(nb, N_CHIP),
            in_specs=[pl.BlockSpec((None, tm, width), lambda i, j, s: (j, s[1] * nb + i, 0)),
                      pl.BlockSpec((None, tm, width), lambda i, j, s: (j, i, 0))],
            out_specs=[pl.BlockSpec((None, tm, width), lambda i, j, s: (j, i, 0)),
                       pl.BlockSpec((tm, width), lambda i, j, s: (i, 0))]),
        out_shape=[jax.ShapeDtypeStruct((N_CHIP, half, width), BF16), jax.ShapeDtypeStruct((half, width), F32)],
        compiler_params=_params("arbitrary", "arbitrary"),
    )(place, g, got)


def _chip_sum(own, got, tag, *, tm):
    half, width = own.shape

    def body(o_ref, g_ref, r_ref):
        r_ref[...] = ((o_ref[...] + g_ref[0].astype(F32)) + g_ref[1].astype(F32)) + g_ref[2].astype(F32)

    row = pl.BlockSpec((tm, width), lambda i: (i, 0))
    return pl.pallas_call(
        body, name="chip_sum_" + tag, grid=(half // tm,),
        in_specs=[row, pl.BlockSpec((3, tm, width), lambda i: (0, i, 0))], out_specs=row,
        out_shape=jax.ShapeDtypeStruct((half, width), F32), compiler_params=_params("parallel"),
    )(own, got)


def _sum_tiles(slabs):
    return [slabs[0].shape[1] // 4, D_MODEL // 8]


def _pair_sums(grads, swapped, place, tag):
    res = [_pair_sum(g, s, place, f"{tag}_{n}", tm=tm) for n, (g, s, tm) in enumerate(zip(grads, swapped, _sum_tiles(grads)))]
    return [r[0] for r in res], [r[1] for r in res]


def _finish_reduce(own, scattered, core, tag):
    mine = [_chip_sum(o, s, f"{tag}_{n}", tm=o.shape[0] // 2) for n, (o, s) in enumerate(zip(own, scattered))]
    theirs = _run(_x_join(mine), "join_halves_" + tag)
    return [lax.dynamic_update_slice(jnp.concatenate([a, a]), b, ((1 - core) * a.shape[0], 0))
            for a, b in zip(mine, theirs)]


def _reduce_scatter(grads, place, core, tag):
    pbf, own = _pair_sums(grads, _run(_x_swap(grads), "swap_halves_" + tag), place, tag)
    return _finish_reduce(own, _run(_x_scatter(pbf), "scatter_partials_" + tag), core, tag)


def _gather_small(s):
    flips = [(fx, fy, fc) for fx in (0, 1) for fy in (0, 1) for fc in (0, 1)][1:]

    def body(s_ref, out_ref, send_sems, recv_sems, local_sem):
        x, y, c, _ = _place()
        sems = (send_sems, recv_sems)
        me = 4 * x + 2 * y + c
        local = pltpu.make_async_copy(s_ref, out_ref.at[me], local_sem)
        local.start()
        sent = [_remote(s_ref, out_ref.at[me], sems, k, (x ^ fx, y ^ fy, c ^ fc)) for k, (fx, fy, fc) in enumerate(flips)]
        for cp in sent:
            cp.start()
        for k, (fx, fy, fc) in enumerate(flips):
            peer = (x ^ fx, y ^ fy, c ^ fc)
            _remote(s_ref, out_ref.at[4 * peer[0] + 2 * peer[1] + peer[2]], sems, k, peer).wait_recv()
        for cp in sent:
            cp.wait_send()
        local.wait()

    return pl.pallas_call(
        body, name="gather_small", out_shape=jax.ShapeDtypeStruct((N_DEV, SMALL_ROWS, D_MODEL), s.dtype),
        in_specs=[ANY], out_specs=ANY,
        scratch_shapes=[pltpu.SemaphoreType.DMA((7,)), pltpu.SemaphoreType.DMA((7,)), pltpu.SemaphoreType.DMA],
    )(s)


LOCAL_NAMES = dict(ffn1_w_down="f1d", ffn2_w_down="f2d", w_branch_a="wa", w_branch_b="wb", w_out="wo", w_mq="wmq",
                   w_mkv="wmkv", w_mo="wmo")


def _assemble(names, own, others, me):
    by_flip = [jnp.concatenate([o[None], t], axis=0) for o, t in zip(own, others)]
    per_chip = [_unpack([lax.dynamic_index_in_dim(s, j ^ me, 0, keepdims=False) for s in by_flip], names)
                for j in range(N_CHIP)]
    return {name: jnp.concatenate([pc[name] for pc in per_chip], axis=axis)
            for name, _, axis in SHARDED if name in names}


def _local_names(full):
    out = {LOCAL_NAMES[name]: a for name, a in full.items() if name in LOCAL_NAMES}
    for name, key in (("ffn1_w_in", "f1"), ("ffn2_w_in", "f2")):
        if name in full:
            out[key + "g"], out[key + "u"] = full[name][:, :D_FF], full[name][:, D_FF:]
    if "w_in" in full:
        w_in = full["w_in"]
        out["w_main"] = jnp.concatenate([w_in[:, :FB_COL], w_in[:, FB_COL + HEADS:]], axis=1)
        out["w_fb"] = jnp.pad(w_in[:, FB_COL:FB_COL + HEADS], ((0, 0), (0, LANE - HEADS)))
    return out


def _grad_slabs(G, names):
    full = {name: G[key] for name, key in LOCAL_NAMES.items() if name in names}
    for name, key in (("ffn1_w_in", "f1"), ("ffn2_w_in", "f2")):
        if name in names:
            full[name] = jnp.concatenate([G[key + "g"], G[key + "u"]], axis=1)
    if "w_in" in names:
        main = jnp.concatenate(G["w_main"], axis=1)
        full["w_in"] = jnp.concatenate([main[:, :FB_COL], G["w_fb"][:, :HEADS], main[:, FB_COL:]], axis=1)
    rows, cols = [], []
    for j in range(N_CHIP):
        shards = {}
        for name, shape, axis in SHARDED:
            if name in names:
                n = shape[axis] // N_CHIP
                shards[name] = lax.slice_in_dim(full[name], j * n, (j + 1) * n, axis=axis)
        r, c = _pack(shards, names, F32)
        rows.append(r)
        cols.append(c)
    return [jnp.stack(rows, axis=0), jnp.stack(cols, axis=0)]


def kernel(x, mem, ffn1_pre_g, ffn1_w_in, ffn1_w_down, ffn1_post_g, mix_pre_g, w_in, hg_lb_logits, hg_norm_g, fox_f_bias, w_branch_a, w_branch_b, b_gate, w_out, mix_post_g, mem_pre_g, mem_kv_g, w_mq, w_mkv, w_mo, mem_post_g, ffn2_pre_g, ffn2_w_in, ffn2_w_down, ffn2_post_g, loss_target, m_ffn1_pre_g, m_ffn1_w_in, m_ffn1_w_down, m_ffn1_post_g, m_mix_pre_g, m_w_in, m_hg_lb_logits, m_hg_norm_g, m_fox_f_bias, m_w_branch_a, m_w_branch_b, m_b_gate, m_w_out, m_mix_post_g, m_mem_pre_g, m_mem_kv_g, m_w_mq, m_w_mkv, m_w_mo, m_mem_post_g, m_ffn2_pre_g, m_ffn2_w_in, m_ffn2_w_down, m_ffn2_post_g, v_ffn1_pre_g, v_ffn1_w_in, v_ffn1_w_down, v_ffn1_post_g, v_mix_pre_g, v_w_in, v_hg_lb_logits, v_hg_norm_g, v_fox_f_bias, v_w_branch_a, v_w_branch_b, v_b_gate, v_w_out, v_mix_post_g, v_mem_pre_g, v_mem_kv_g, v_w_mq, v_w_mkv, v_w_mo, v_mem_post_g, v_ffn2_pre_g, v_ffn2_w_in, v_ffn2_w_down, v_ffn2_post_g):
    w = dict(ffn1_pre_g=ffn1_pre_g, ffn1_w_in=ffn1_w_in, ffn1_w_down=ffn1_w_down, ffn1_post_g=ffn1_post_g, mix_pre_g=mix_pre_g, w_in=w_in, hg_lb_logits=hg_lb_logits, hg_norm_g=hg_norm_g, fox_f_bias=fox_f_bias, w_branch_a=w_branch_a, w_branch_b=w_branch_b, b_gate=b_gate, w_out=w_out, mix_post_g=mix_post_g, mem_pre_g=mem_pre_g, mem_kv_g=mem_kv_g, w_mq=w_mq, w_mkv=w_mkv, w_mo=w_mo, mem_post_g=mem_post_g, ffn2_pre_g=ffn2_pre_g, ffn2_w_in=ffn2_w_in, ffn2_w_down=ffn2_w_down, ffn2_post_g=ffn2_post_g)
    m = dict(ffn1_pre_g=m_ffn1_pre_g, ffn1_w_in=m_ffn1_w_in, ffn1_w_down=m_ffn1_w_down, ffn1_post_g=m_ffn1_post_g, mix_pre_g=m_mix_pre_g, w_in=m_w_in, hg_lb_logits=m_hg_lb_logits, hg_norm_g=m_hg_norm_g, fox_f_bias=m_fox_f_bias, w_branch_a=m_w_branch_a, w_branch_b=m_w_branch_b, b_gate=m_b_gate, w_out=m_w_out, mix_post_g=m_mix_post_g, mem_pre_g=m_mem_pre_g, mem_kv_g=m_mem_kv_g, w_mq=m_w_mq, w_mkv=m_w_mkv, w_mo=m_w_mo, mem_post_g=m_mem_post_g, ffn2_pre_g=m_ffn2_pre_g, ffn2_w_in=m_ffn2_w_in, ffn2_w_down=m_ffn2_w_down, ffn2_post_g=m_ffn2_post_g)
    v = dict(ffn1_pre_g=v_ffn1_pre_g, ffn1_w_in=v_ffn1_w_in, ffn1_w_down=v_ffn1_w_down, ffn1_post_g=v_ffn1_post_g, mix_pre_g=v_mix_pre_g, w_in=v_w_in, hg_lb_logits=v_hg_lb_logits, hg_norm_g=v_hg_norm_g, fox_f_bias=v_fox_f_bias, w_branch_a=v_w_branch_a, w_branch_b=v_w_branch_b, b_gate=v_b_gate, w_out=v_w_out, mix_post_g=v_mix_post_g, mem_pre_g=v_mem_pre_g, mem_kv_g=v_mem_kv_g, w_mq=v_w_mq, w_mkv=v_w_mkv, w_mo=v_w_mo, mem_post_g=v_mem_post_g, ffn2_pre_g=v_ffn2_pre_g, ffn2_w_in=v_ffn2_w_in, ffn2_w_down=v_ffn2_w_down, ffn2_post_g=v_ffn2_post_g)
    sharded = [name for name, _, _ in SHARDED]
    shard_of = lambda d: {name: d[name][0] for name in sharded}

    me, core = 2 * lax.axis_index("x") + lax.axis_index("y"), lax.axis_index("c")
    place = jnp.stack([me, core]).astype(jnp.int32)
    own = {group: _pack(shard_of(w), group, BF16) for group in (GROUP_FFN1, GROUP_MID, GROUP_FFN2)}
    P = {name: w[name] for name in SMALL}

    sq, dx0, G, reduced = _local_step(x[0], mem[0], loss_target[0], P, own, me, place, core)
    loss = lax.psum(0.5 * jnp.sum(sq) / D_MODEL, ("x", "y", "c"))

    g_shards = _unpack(reduced, GROUP_MID + GROUP_FFN2)
    g_shards.update(_unpack(_reduce_scatter(_grad_slabs(G, GROUP_FFN1), place, core, "late"), GROUP_FFN1))
    big = {}
    for name, shape, axis in SHARDED:
        rows = shape[0] // (N_CHIP if axis == 0 else 1)
        big[name] = _adamw(w[name][0], g_shards[name], m[name][0], v[name][0], name="adamw_" + name, tm=rows // 8)
    small = _adamw(_pack_small(w), _gather_small(_pack_small(G)), _pack_small(m), _pack_small(v), name="adamw_small",
                   tm=SMALL_ROWS)

    outs = [loss, dx0[None]]
    for n in range(4):
        vals = {name: res[n][None] for name, res in big.items()}
        vals.update(_unpack_small(small[n]))
        outs += [vals[name] for name in WEIGHT_ORDER]
    return tuple(outs)
```

```python
import functools
import math

import jax
import jax.numpy as jnp
from jax import lax
from jax.experimental import pallas as pl
from jax.experimental.pallas import tpu as pltpu

F32 = jnp.float32
BF16 = jnp.bfloat16

D_MODEL = 1024
D_FF = 2816
HEADS = 8
DH = 128
MEM_HEADS = 4
MEM_DH = 256
MEM_LEN = 256
EPS = 1e-6
SUB = 16
LANE = 128
SUBLANE = 8
VMEM_LIMIT = 56 * 1024 * 1024

ADAM_LR = 0.001
ADAM_B1 = 0.9
ADAM_B2 = 0.999
ADAM_EPS = 1e-08
ADAM_WD = 0.01
ADAM_STEP = 10

HIGHEST = lax.Precision.HIGHEST


def _params(*sem):
    return pltpu.CompilerParams(dimension_semantics=sem, vmem_limit_bytes=VMEM_LIMIT)


def _sigmoid(v):
    return 0.5 * jnp.tanh(0.5 * v) + 0.5


def _silu(v):
    return v * _sigmoid(v)


def _dsilu(v):
    s = _sigmoid(v)
    return s * (1.0 + v * (1.0 - s))


def _dot(a, b, dims):
    return lax.dot_general(a.astype(BF16), b.astype(BF16), (dims, ((), ())), preferred_element_type=F32)


NN = ((1,), (0,))
NT = ((1,), (1,))
TN = ((0,), (0,))


ANY = pl.BlockSpec(memory_space=pl.ANY)


class _Side:
    def __init__(self, arrays, out_shapes, nsem, start, wait, aliases=None):
        self.arrays, self.out_shapes, self.nsem = list(arrays), list(out_shapes), nsem
        self.start, self.wait, self.aliases = start, wait, dict(aliases or {})

    def plumb(self, n_in, n_out):
        return dict(args=self.arrays, in_specs=[ANY] * len(self.arrays), out_specs=[ANY] * len(self.out_shapes),
                    scratch=[pltpu.SemaphoreType.DMA((self.nsem,)), pltpu.SemaphoreType.DMA((self.nsem,))],
                    aliases={n_in + i: n_out + o for i, o in self.aliases.items()})

    def run_at_ends(self, ins, outs, sems, first, last, compute):
        @pl.when(first)
        def _():
            self.start(ins, outs, sems)

        compute()

        @pl.when(last)
        def _():
            self.wait(ins, outs, sems)


def _grid_ends(grid):
    first = functools.reduce(lambda a, b: a & b, [pl.program_id(d) == 0 for d in range(len(grid))])
    last = functools.reduce(lambda a, b: a & b, [pl.program_id(d) == grid[d] - 1 for d in range(len(grid))])
    return first, last


def _mm(pairs, mode, *, tm, tn, tk, out_dtypes, name, epilogue=None, tiles=(), b_koff=None, side=None):
    a0, b0 = pairs[0]
    if mode == "nn":
        (M, K), N = a0.shape, b0.shape[1]
    elif mode == "nt":
        (M, K), N = a0.shape, b0.shape[0]
    else:
        (K, M), N = a0.shape, b0.shape[1]
    tm, tn, tk = min(tm, M), min(tn, N), min(tk, K)
    assert M % tm == 0 and N % tn == 0 and K % tk == 0, (name, M, N, K, tm, tn, tk)
    nk = K // tk
    npair = len(pairs)
    koff = [0] * npair if b_koff is None else [o // tk for o in b_koff]
    if b_koff is not None:
        assert all(o % tk == 0 for o in b_koff)
    in_specs, args = [], []
    for p, (a, b) in enumerate(pairs):
        if mode == "nn":
            sa = pl.BlockSpec((tm, tk), lambda i, j, k: (i, k))
            sb = pl.BlockSpec((tk, tn), lambda i, j, k, o=koff[p]: (k + o, j))
            dims = NN
        elif mode == "nt":
            sa = pl.BlockSpec((tm, tk), lambda i, j, k: (i, k))
            sb = pl.BlockSpec((tn, tk), lambda i, j, k, o=koff[p]: (j, k + o))
            dims = NT
        else:
            sa = pl.BlockSpec((tk, tm), lambda i, j, k: (k, i))
            sb = pl.BlockSpec((tk, tn), lambda i, j, k, o=koff[p]: (k + o, j))
            dims = TN
        in_specs += [sa, sb]
        args += [a, b]
    for t in tiles:
        in_specs.append(pl.BlockSpec((tm, tn), lambda i, j, k: (i, j)))
        args.append(t)
    nt_ = len(tiles)
    nout = len(out_dtypes)
    nin = len(args)
    grid = (M // tm, N // tn, nk)
    plumb = side.plumb(nin, nout) if side is not None else None
    ns_in, ns_out = (len(side.arrays), len(side.out_shapes)) if side is not None else (0, 0)

    def body(*refs):
        ab = refs[: 2 * npair]
        tl = refs[2 * npair: nin]
        outs = refs[nin + ns_in: nin + ns_in + nout]
        scratch = refs[nin + ns_in + nout + ns_out:]
        acc_ref = scratch[0] if nk > 1 else None
        if side is None:
            compute(ab, tl, outs, acc_ref)
        else:
            first, last = _grid_ends(grid)
            side.run_at_ends(refs[nin: nin + ns_in], refs[nin + ns_in + nout: nin + ns_in + nout + ns_out],
                             scratch[-2:], first, last, lambda: compute(ab, tl, outs, acc_ref))

    def compute(ab, tl, outs, acc_ref):
        def partial_sum():
            s = _dot(ab[0][...], ab[1][...], dims)
            for p in range(1, npair):
                s = s + _dot(ab[2 * p][...], ab[2 * p + 1][...], dims)
            return s

        def finish(acc):
            res = (acc,) if epilogue is None else epilogue(acc, *[t[...] for t in tl])
            for o, r in zip(outs, res):
                o[...] = r.astype(o.dtype)

        if nk == 1:
            finish(partial_sum())
        else:
            k = pl.program_id(2)

            @pl.when(k == 0)
            def _():
                acc_ref[...] = jnp.zeros_like(acc_ref)

            acc_ref[...] += partial_sum()

            @pl.when(k == nk - 1)
            def _():
                finish(acc_ref[...])

    out_shape = [jax.ShapeDtypeStruct((M, N), dt) for dt in out_dtypes]
    out_specs = [pl.BlockSpec((tm, tn), lambda i, j, k: (i, j)) for _ in out_dtypes]
    scratch = [pltpu.VMEM((tm, tn), F32)] if nk > 1 else []
    if side is None:
        res = pl.pallas_call(
            body, name=name, grid=grid, in_specs=in_specs, out_specs=out_specs, out_shape=out_shape,
            scratch_shapes=scratch, compiler_params=_params("parallel", "parallel", "arbitrary"),
        )(*args)
        return res[0] if nout == 1 else res
    res = pl.pallas_call(
        body, name=name, grid=grid, in_specs=in_specs + plumb["in_specs"], out_specs=out_specs + plumb["out_specs"],
        out_shape=out_shape + side.out_shapes, scratch_shapes=scratch + plumb["scratch"],
        input_output_aliases=plumb["aliases"], compiler_params=_params("arbitrary", "arbitrary", "arbitrary"),
    )(*args, *plumb["args"])
    return res[:nout], res[nout:]


def _col(arr, tm, width, cb):
    return pl.BlockSpec((tm, width), lambda i, cb=cb: (i, cb))


def _rms_fwd(x, g, *, out_dtype, name, mul=None, res=None, coeff=1.0, tm=512):
    T, D = x.shape
    tm = min(tm, T)
    args, in_specs = [x, g], [pl.BlockSpec((tm, D), lambda i: (i, 0)), pl.BlockSpec((1, D), lambda i: (0, 0))]
    if mul is not None:
        args.append(mul[0])
        in_specs.append(_col(mul[0], tm, D, mul[1]))
    if res is not None:
        args.append(res)
        in_specs.append(pl.BlockSpec((tm, D), lambda i: (i, 0)))

    def body(*refs):
        xv = refs[0][...].astype(F32)
        r = lax.rsqrt(jnp.mean(xv * xv, axis=-1, keepdims=True) + EPS)
        y = (xv * r) * refs[1][...]
        n = 2
        if mul is not None:
            y = y * _silu(refs[n][...])
            n += 1
        if res is not None:
            y = refs[n][...] + coeff * y
        refs[-1][...] = y.astype(out_dtype)

    return pl.pallas_call(
        body, name=name, grid=(T // tm,), in_specs=in_specs, out_specs=pl.BlockSpec((tm, D), lambda i: (i, 0)),
        out_shape=jax.ShapeDtypeStruct((T, D), out_dtype), compiler_params=_params("parallel"),
    )(*args)


def _fold8(v):
    tm, d = v.shape
    return v.reshape(tm // SUBLANE, SUBLANE, d).sum(axis=0)


def _rms_bwd(x, g, dy, *, name, coeff=1.0, add=None, mul=None, dx_dtype=F32, tm=512):
    T, D = x.shape
    tm = min(tm, T)
    row = pl.BlockSpec((tm, D), lambda i: (i, 0))
    args, in_specs = [x, g, dy], [row, pl.BlockSpec((1, D), lambda i: (0, 0)), row]
    if add is not None:
        args.append(add)
        in_specs.append(row)
    if mul is not None:
        args.append(mul[0])
        in_specs.append(_col(mul[0], tm, D, mul[1]))
    nin = len(args)

    def body(*refs):
        xv = refs[0][...].astype(F32)
        gv = refs[1][...]
        dyv = refs[2][...].astype(F32) * coeff
        r = lax.rsqrt(jnp.mean(xv * xv, axis=-1, keepdims=True) + EPS)
        nrm = xv * r
        n = 3
        addv = None
        if add is not None:
            addv = refs[n][...]
            n += 1
        if mul is not None:
            mv = refs[n][...]
            sm = _silu(mv)
            refs[nin + 2][...] = (dyv * nrm * gv * _dsilu(mv)).astype(refs[nin + 2].dtype)
            dyv = dyv * sm
        dn = dyv * gv
        dx = r * (dn - nrm * jnp.mean(dn * nrm, axis=-1, keepdims=True))
        if addv is not None:
            dx = dx + addv
        refs[nin][...] = dx.astype(dx_dtype)
        dg_ref = refs[nin + 1]

        @pl.when(pl.program_id(0) == 0)
        def _():
            dg_ref[...] = jnp.zeros_like(dg_ref)

        dg_ref[...] += _fold8(dyv * nrm)

    out_shape = [jax.ShapeDtypeStruct((T, D), dx_dtype), jax.ShapeDtypeStruct((SUBLANE, D), F32)]
    out_specs = [row, pl.BlockSpec((SUBLANE, D), lambda i: (0, 0))]
    if mul is not None:
        out_shape.append(jax.ShapeDtypeStruct((T, D), BF16))
        out_specs.append(row)
    return pl.pallas_call(
        body, name=name, grid=(T // tm,), in_specs=in_specs, out_specs=out_specs, out_shape=out_shape,
        compiler_params=_params("arbitrary"),
    )(*args)


def _ffn_in(h, wg, wu, *, name, tm=1024, tn=256, side=None):
    T, D = h.shape
    F = wg.shape[1]
    tm = min(tm, T)
    assert F % tn == 0
    grid = (T // tm, F // tn)
    ns_in, ns_out = (len(side.arrays), len(side.out_shapes)) if side is not None else (0, 0)

    def compute(h_ref, wg_ref, wu_ref, a_ref, g_ref, u_ref):
        hv = h_ref[...]
        gt = _dot(hv, wg_ref[...], NN)
        up = _dot(hv, wu_ref[...], NN)
        a_ref[...] = (_silu(gt) * up).astype(BF16)
        g_ref[...] = gt.astype(BF16)
        u_ref[...] = up.astype(BF16)

    def body(*refs):
        if side is None:
            compute(*refs)
        else:
            outs0 = 3 + ns_in
            first, last = _grid_ends(grid)
            side.run_at_ends(refs[3:outs0], refs[outs0 + 3: outs0 + 3 + ns_out], refs[-2:], first, last,
                             lambda: compute(*refs[:3], *refs[outs0: outs0 + 3]))

    o = pl.BlockSpec((tm, tn), lambda i, j: (i, j))
    w = pl.BlockSpec((D, tn), lambda i, j: (0, j))
    in_specs = [pl.BlockSpec((tm, D), lambda i, j: (i, 0)), w, w]
    out_shape = [jax.ShapeDtypeStruct((T, F), BF16)] * 3
    if side is None:
        return pl.pallas_call(body, name=name, grid=grid, in_specs=in_specs, out_specs=[o, o, o], out_shape=out_shape,
                              compiler_params=_params("parallel", "parallel"))(h, wg, wu)
    plumb = side.plumb(3, 3)
    res = pl.pallas_call(
        body, name=name, grid=grid, in_specs=in_specs + plumb["in_specs"], out_specs=[o, o, o] + plumb["out_specs"],
        out_shape=out_shape + side.out_shapes, scratch_shapes=plumb["scratch"], input_output_aliases=plumb["aliases"],
        compiler_params=_params("arbitrary", "arbitrary"),
    )(h, wg, wu, *plumb["args"])
    return res[:3], res[3:]


def _swiglu_bwd_epilogue(da, gt, up):
    gt = gt.astype(F32)
    up = up.astype(F32)
    return da * up * _dsilu(gt), da * _silu(gt)


GATE_CB = 7


def _gatemix_fwd(z, b_gate, ya, yb, *, name, tm=512):
    T, D = ya.shape
    tm = min(tm, T)
    row = pl.BlockSpec((tm, D), lambda i: (i, 0))

    def body(z0, z1, b0, b1, ya_ref, yb_ref, y_ref):
        g0 = _sigmoid(z0[...] + b0[...])
        g1 = _sigmoid(z1[...] + b1[...])
        y_ref[...] = (g0 * ya_ref[...] + g1 * yb_ref[...]).astype(y_ref.dtype)

    bs = lambda c: pl.BlockSpec((1, D), lambda i, c=c: (0, c))
    return pl.pallas_call(
        body, name=name, grid=(T // tm,),
        in_specs=[_col(z, tm, D, GATE_CB), _col(z, tm, D, GATE_CB + 1), bs(0), bs(1), row, row],
        out_specs=row, out_shape=jax.ShapeDtypeStruct((T, D), BF16), compiler_params=_params("parallel"),
    )(z, z, b_gate, b_gate, ya, yb)


def _gatemix_bwd(z, b_gate, ya, yb, dy, *, name, tm=512):
    T, D = ya.shape
    tm = min(tm, T)
    row = pl.BlockSpec((tm, D), lambda i: (i, 0))
    part = pl.BlockSpec((SUBLANE, D), lambda i: (0, 0))

    def body(z0, z1, b0, b1, ya_ref, yb_ref, dy_ref, dya, dyb, dz0, dz1, s0, s1):
        g0 = _sigmoid(z0[...] + b0[...])
        g1 = _sigmoid(z1[...] + b1[...])
        dyv = dy_ref[...]
        dya[...] = (dyv * g0).astype(BF16)
        dyb[...] = (dyv * g1).astype(BF16)
        d0 = dyv * ya_ref[...] * (g0 * (1.0 - g0))
        d1 = dyv * yb_ref[...] * (g1 * (1.0 - g1))
        dz0[...] = d0.astype(BF16)
        dz1[...] = d1.astype(BF16)

        @pl.when(pl.program_id(0) == 0)
        def _():
            s0[...] = jnp.zeros_like(s0)
            s1[...] = jnp.zeros_like(s1)

        s0[...] += _fold8(d0)
        s1[...] += _fold8(d1)

    bs = lambda c: pl.BlockSpec((1, D), lambda i, c=c: (0, c))
    act = jax.ShapeDtypeStruct((T, D), BF16)
    ps = jax.ShapeDtypeStruct((SUBLANE, D), F32)
    return pl.pallas_call(
        body, name=name, grid=(T // tm,),
        in_specs=[_col(z, tm, D, GATE_CB), _col(z, tm, D, GATE_CB + 1), bs(0), bs(1), row, row, row],
        out_specs=[row, row, row, row, part, part], out_shape=[act, act, act, act, ps, ps],
        compiler_params=_params("arbitrary"),
    )(z, z, b_gate, b_gate, ya, yb, dy)


def _loss_head(x, target, *, name, tm=512):
    T, D = x.shape
    tm = min(tm, T)
    row = pl.BlockSpec((tm, D), lambda i: (i, 0))

    def body(x_ref, t_ref, dx_ref, s_ref):
        e = x_ref[...] - t_ref[...]
        dx_ref[...] = e * (1.0 / D)

        @pl.when(pl.program_id(0) == 0)
        def _():
            s_ref[...] = jnp.zeros_like(s_ref)

        s_ref[...] += _fold8(e * e)

    return pl.pallas_call(
        body, name=name, grid=(T // tm,), in_specs=[row, row],
        out_specs=[row, pl.BlockSpec((SUBLANE, D), lambda i: (0, 0))],
        out_shape=[jax.ShapeDtypeStruct((T, D), F32), jax.ShapeDtypeStruct((SUBLANE, D), F32)],
        compiler_params=_params("arbitrary"),
    )(x, target)


def _tri(n, reverse):
    r = lax.broadcasted_iota(jnp.int32, (n, n), 0)
    c = lax.broadcasted_iota(jnp.int32, (n, n), 1)
    return jnp.where((c >= r) if reverse else (c <= r), 1.0, 0.0).astype(F32)


def _cumsum_t(xs, *, name, width, pre, reverse=False, rows=(), post=None, out_dtypes=(F32,), fold=None, tb=256):
    T = xs[0][0].shape[0]
    tb = min(tb, T)
    nb = T // tb
    tblk = (lambda i: nb - 1 - i) if reverse else (lambda i: i)
    args = [a for a, _ in xs] + [a for a, _ in rows]
    in_specs = [pl.BlockSpec((tb, width), lambda i, cb=cb: (tblk(i), cb)) for _, cb in xs]
    in_specs += [pl.BlockSpec((1, width), lambda i, cb=cb: (0, cb)) for _, cb in rows]
    nin, nout = len(args), len(out_dtypes)

    def body(*refs):
        vals = [r[...] for r in refs[:nin]]
        outs = refs[nin:nin + nout]
        carry = refs[-1]
        first = pl.program_id(0) == 0

        @pl.when(first)
        def _():
            carry[...] = jnp.zeros_like(carry)

        cum = jnp.dot(_tri(tb, reverse), pre(*vals), precision=HIGHEST, preferred_element_type=F32) + carry[...]
        carry[...] = cum[0:1, :] if reverse else cum[tb - 1:tb, :]
        res = (cum,) if post is None else post(cum, *vals)
        for o, r in zip(outs, res):
            o[...] = r.astype(o.dtype)
        if fold is not None:
            f_ref = refs[nin + nout]

            @pl.when(first)
            def _():
                f_ref[...] = jnp.zeros_like(f_ref)

            f_ref[...] += _fold8(fold(cum, *vals))

    tspec = pl.BlockSpec((tb, width), lambda i: (tblk(i), 0))
    out_shape = [jax.ShapeDtypeStruct((T, width), dt) for dt in out_dtypes]
    out_specs = [tspec] * nout
    if fold is not None:
        out_shape.append(jax.ShapeDtypeStruct((SUBLANE, width), F32))
        out_specs.append(pl.BlockSpec((SUBLANE, width), lambda i: (0, 0)))
    res = pl.pallas_call(
        body, name=name, grid=(nb,), in_specs=in_specs, out_specs=out_specs, out_shape=out_shape,
        scratch_shapes=[pltpu.VMEM((1, width), F32)], compiler_params=_params("arbitrary"),
    )(*args)
    return res[0] if len(res) == 1 else res


def _logsigmoid(v):
    return jnp.minimum(v, 0.0) - jnp.log(1.0 + jnp.exp(-jnp.abs(v)))


HG_TB = 256
HG_HB = 4
HG_W = HG_HB * DH
HG_GROUPS = HEADS // HG_HB
HG_Q_CB, HG_F_CB, HG_I_CB = 0, HG_GROUPS, 2 * HG_GROUPS
NEG = -1e30


def _scan16(x, rowid, reverse=False):
    for k in [1 << n for n in range(SUB.bit_length() - 1)]:
        if reverse:
            x = x + jnp.where(rowid < SUB - k, pltpu.roll(x, SUB - k, 0), 0.0)
        else:
            x = x + jnp.where(rowid >= k, pltpu.roll(x, k, 0), 0.0)
    return x


def _hg_block(q_ref, f_ref, i_ref, lb_ref, rows, cols, rowid):
    lb = lb_ref[:, cols]
    qr = q_ref[rows, cols]
    sg = _sigmoid(f_ref[rows, cols])
    f = lb + (1.0 - lb) * sg
    b = _scan16(jnp.log(f), rowid)
    return _silu(qr), 1.0 - f, i_ref[rows, cols], b, qr, sg, f, lb


def _hg_specs(tb, tmap):
    return [pl.BlockSpec((tb, HG_W), lambda g, t: (tmap(t), HG_Q_CB + g)),
            pl.BlockSpec((tb, HG_W), lambda g, t: (tmap(t), HG_F_CB + g)),
            pl.BlockSpec((tb, HG_W), lambda g, t: (tmap(t), HG_I_CB + g)),
            pl.BlockSpec((1, HG_W), lambda g, t: (0, g))]


def _hgrn2_fwd(z, lb_row, *, name):
    T = z.shape[0]
    tb = min(HG_TB, T)
    nb, nsub = T // tb, tb // SUB

    def body(q_ref, f_ref, i_ref, lb_ref, o_ref, st_ref, state):
        @pl.when(pl.program_id(1) == 0)
        def _():
            state[...] = jnp.zeros_like(state)

        rowid = lax.broadcasted_iota(jnp.int32, (SUB, DH), 0)

        def step(c, carry):
            rows = pl.ds(pl.multiple_of(c * SUB, SUB), SUB)
            for hh in range(HG_HB):
                cols = slice(hh * DH, (hh + 1) * DH)
                q, k, iv, b = _hg_block(q_ref, f_ref, i_ref, lb_ref, rows, cols, rowid)[:4]
                bl = b[SUB - 1:SUB, :]
                sv = state[hh]
                st_ref[c, hh] = sv
                o = _dot(q * jnp.exp(b), sv, NT)
                for s in range(SUB):
                    e = jnp.exp(jnp.where(rowid >= s, b - b[s:s + 1, :], NEG))
                    a = jnp.sum(q * e * k[s:s + 1, :], axis=-1, keepdims=True)
                    o = o + a * iv[s:s + 1, :]
                o_ref[rows, cols] = o
                state[hh] = sv * jnp.exp(bl) + _dot(iv, k * jnp.exp(bl - b), TN)
            return carry

        lax.fori_loop(0, nsub, step, 0)

    return pl.pallas_call(
        body, name=name, grid=(HG_GROUPS, nb), in_specs=_hg_specs(tb, lambda t: t),
        out_specs=[pl.BlockSpec((tb, HG_W), lambda g, t: (t, g)),
                   pl.BlockSpec((nsub, HG_HB, DH, DH), lambda g, t: (t, g, 0, 0))],
        out_shape=[jax.ShapeDtypeStruct((T, D_MODEL), F32), jax.ShapeDtypeStruct((T // SUB, HEADS, DH, DH), F32)],
        scratch_shapes=[pltpu.VMEM((HG_HB, DH, DH), F32)], compiler_params=_params("parallel", "arbitrary"),
    )(z, z, z, lb_row)


def _hgrn2_bwd(z, lb_row, states, do, *, name):
    T = z.shape[0]
    tb = min(HG_TB, T)
    nb, nsub = T // tb, tb // SUB
    rev = lambda t: nb - 1 - t

    def body(q_ref, f_ref, i_ref, lb_ref, st_ref, do_ref, dq_ref, dfl_ref, di_ref, dlb_ref, dstate, later):
        @pl.when(pl.program_id(1) == 0)
        def _():
            dstate[...] = jnp.zeros_like(dstate)
            later[...] = jnp.zeros_like(later)
            dlb_ref[...] = jnp.zeros_like(dlb_ref)

        rowid = lax.broadcasted_iota(jnp.int32, (SUB, DH), 0)

        def step(cc, carry):
            c = nsub - 1 - cc
            rows = pl.ds(pl.multiple_of(c * SUB, SUB), SUB)
            for hh in range(HG_HB):
                cols = slice(hh * DH, (hh + 1) * DH)
                q, k, iv, b, qr, sg, f, lb = _hg_block(q_ref, f_ref, i_ref, lb_ref, rows, cols, rowid)
                bl = b[SUB - 1:SUB, :]
                eb, ebl = jnp.exp(b), jnp.exp(bl - b)
                sv, dsv = st_ref[c, hh], dstate[hh]
                dov = do_ref[rows, cols]
                dq = _dot(dov, sv, NN) * eb
                dk = _dot(iv, dsv, NN) * ebl
                di = _dot(k * ebl, dsv, NT)
                for s in range(SUB):
                    e = jnp.exp(jnp.where(rowid >= s, b - b[s:s + 1, :], NEG))
                    ks, isv = k[s:s + 1, :], iv[s:s + 1, :]
                    qe = q * e
                    a = jnp.sum(qe * ks, axis=-1, keepdims=True)
                    p = jnp.sum(dov * isv, axis=-1, keepdims=True)
                    dq = dq + p * (e * ks)
                    dks = jnp.sum(p * qe, axis=0, keepdims=True)
                    dis = jnp.sum(a * dov, axis=0, keepdims=True)
                    dk = dk + jnp.where(rowid == s, dks, 0.0)
                    di = di + jnp.where(rowid == s, dis, 0.0)
                dlogf = _scan16(q * dq - k * dk, rowid, reverse=True) + later[hh]
                df = dlogf / f - dk
                dlb_ref[:, cols] += jnp.sum(df * (1.0 - sg), axis=0, keepdims=True)
                dfl_ref[rows, cols] = (df * (1.0 - lb) * (sg * (1.0 - sg))).astype(BF16)
                dq_ref[rows, cols] = (dq * _dsilu(qr)).astype(BF16)
                di_ref[rows, cols] = di.astype(BF16)
                dnew = dsv * jnp.exp(bl) + _dot(dov, q * eb, TN)
                dstate[hh] = dnew
                later[hh] = jnp.sum(dnew * sv, axis=0, keepdims=True)
            return carry

        lax.fori_loop(0, nsub, step, 0)

    tile = pl.BlockSpec((tb, HG_W), lambda g, t: (rev(t), g))
    act = jax.ShapeDtypeStruct((T, D_MODEL), BF16)
    return pl.pallas_call(
        body, name=name, grid=(HG_GROUPS, nb),
        in_specs=_hg_specs(tb, rev) + [pl.BlockSpec((nsub, HG_HB, DH, DH), lambda g, t: (rev(t), g, 0, 0)), tile],
        out_specs=[tile, tile, tile, pl.BlockSpec((1, HG_W), lambda g, t: (0, g))],
        out_shape=[act, act, act, jax.ShapeDtypeStruct((1, D_MODEL), F32)],
        scratch_shapes=[pltpu.VMEM((HG_HB, DH, DH), F32), pltpu.VMEM((HG_HB, 1, DH), F32)],
        compiler_params=_params("parallel", "arbitrary"),
    )(z, z, z, lb_row, states, do)


FOX_Q_CB, FOX_K_CB, FOX_V_CB = 4 * HEADS, 5 * HEADS, 6 * HEADS
FOX_SCALE = 1.0 / math.sqrt(DH)


def _fox_tile(T):
    return 512 if T >= 2048 else 128


def _fox_pairs(nq, by_query):
    if by_query:
        pairs = [(i, j) for i in range(nq) for j in range(i + 1)]
    else:
        pairs = [(i, j) for j in range(nq) for i in range(j, nq)]
    return (jnp.asarray([p[0] for p in pairs], jnp.int32), jnp.asarray([p[1] for p in pairs], jnp.int32))


LOG2E = 1.4426950408889634
FOX_RC = 64
FOX_HB = 2


def _fox_q2(q):
    return (q * (FOX_SCALE * LOG2E)).astype(BF16)


FOX_ZERO = -200.0


def _fox_norms(z, *, name):
    T = z.shape[0]
    tq = _fox_tile(T)
    nq = T // tq

    def body(q_ref, k_ref, nq_ref, nk_ref):
        head_of_col = lax.broadcasted_iota(jnp.int32, (D_MODEL, LANE), 0) // DH
        pick = jnp.where(head_of_col == lax.broadcasted_iota(jnp.int32, (D_MODEL, LANE), 1), 1.0, 0.0).astype(BF16)

        def tile_max(v):
            v = v.astype(F32)
            sq = _dot(v * v, pick, NN)
            return jnp.broadcast_to(jnp.max(jnp.sqrt(sq), axis=0, keepdims=True), (SUBLANE, LANE))

        nq_ref[...] = tile_max(_fox_q2(q_ref[...]))
        nk_ref[...] = tile_max(k_ref[...].astype(BF16))

    out = jax.ShapeDtypeStruct((nq * SUBLANE, LANE), F32)
    spec = pl.BlockSpec((SUBLANE, LANE), lambda i: (i, 0))
    a, b = pl.pallas_call(
        body, name=name, grid=(nq,),
        in_specs=[pl.BlockSpec((tq, D_MODEL), lambda i: (i, FOX_Q_CB // HEADS)),
                  pl.BlockSpec((tq, D_MODEL), lambda i: (i, FOX_K_CB // HEADS))],
        out_specs=[spec, spec], out_shape=[out, out], compiler_params=_params("parallel"),
    )(z, z)
    return a[::SUBLANE, :HEADS], b[::SUBLANE, :HEADS]


def _fox_schedule(norm_q, norm_k, ct, tq):
    nq = ct.shape[1] // tq
    first, last = ct[:, ::tq], ct[:, tq - 1::tq]
    nqh, nkh = norm_q.T * 1.05, norm_k.T * 1.05
    bound = nqh[:, :, None] * (nkh[:, None, :] + nkh[:, :, None]) + first[:, :, None] - last[:, None, :]
    tri = jnp.arange(nq)[:, None] > jnp.arange(nq)[None, :]
    drop = (bound < FOX_ZERO) & tri[None]
    lo = jnp.argmin(drop, axis=2).astype(jnp.int32)
    dropped = jnp.arange(nq)[None, None, :] < lo[:, :, None]
    lo_g = jnp.min(lo.reshape(HEADS // FOX_HB, FOX_HB, nq), axis=1)
    dropped_g = jnp.arange(nq)[None, None, :] < lo_g[:, :, None]
    qf, kf = _fox_pairs(nq, by_query=True)
    qb, kb = _fox_pairs(nq, by_query=False)
    fetch_k = jnp.maximum(kf[None, :], lo_g[:, qf])
    kept_q = jnp.where(dropped_g | ~(tri | jnp.eye(nq, dtype=bool))[None], -1, jnp.arange(nq)[None, :, None])
    last_kept = lax.cummax(kept_q, axis=1)
    fetch_q = last_kept[:, qb, kb]
    i32 = lambda a: a.astype(jnp.int32)
    return i32(fetch_k), i32(dropped[:, qf, kf]), i32(fetch_q), i32(dropped[:, qb, kb])


def _fox_fwd(z, c_col, c_row, fetch_k, skip, *, name):
    T = z.shape[0]
    tq = _fox_tile(T)
    nq = T // tq
    rc = min(FOX_RC, tq)

    qi, kj = _fox_pairs(nq, by_query=True)

    def body(qi_ref, kj_ref, fk_ref, skip_ref, q_ref, k_ref, v_ref, cc_ref, cr_ref, o_ref, lse_ref, m_scr, l_scr, acc,
             a_scr, s_scr, p_scr):
        p_id = pl.program_id(1)
        i, j = qi_ref[p_id], kj_ref[p_id]

        @pl.when(j == 0)
        def _():
            m_scr[...] = jnp.full_like(m_scr, NEG)
            l_scr[...] = jnp.zeros_like(l_scr)
            acc[...] = jnp.zeros_like(acc)

        def update(hh, masked):
            cols = slice(hh * DH, (hh + 1) * DH)
            bias = cc_ref[hh, 0:1, :] - cr_ref[hh]
            s_scr[hh] = _dot(_fox_q2(q_ref[:, cols]), k_ref[:, cols], NT)
            for r in range(tq // rc):
                rows = slice(r * rc, (r + 1) * rc)
                t = s_scr[hh, rows, :] + bias
                if masked:
                    t = jnp.where(lax.broadcasted_iota(jnp.int32, (rc, tq), 1)
                                  <= r * rc + lax.broadcasted_iota(jnp.int32, (rc, tq), 0), t, NEG)
                m_old = m_scr[hh, rows, :]
                m_new = jnp.maximum(m_old, jnp.max(t, axis=-1, keepdims=True))
                alpha = jnp.exp2(m_old - m_new)
                p = jnp.exp2(t - jnp.tile(m_new, (1, tq // LANE)))
                l_scr[hh, rows, :] = alpha * l_scr[hh, rows, :] + jnp.sum(p, axis=-1, keepdims=True)
                a_scr[hh, rows, :] = alpha
                p_scr[hh, rows, :] = p.astype(BF16)
                m_scr[hh, rows, :] = m_new
            acc[hh] = a_scr[hh] * acc[hh] + _dot(p_scr[hh], v_ref[:, cols], NN)

        for hh in range(FOX_HB):
            live = skip_ref[pl.program_id(0) * FOX_HB + hh, p_id] == 0

            @pl.when((j < i) & live)
            def _():
                update(hh, False)

            @pl.when(j == i)
            def _():
                update(hh, True)
                o_ref[:, hh * DH:(hh + 1) * DH] = acc[hh] / l_scr[hh]
                lse_ref[hh] = (m_scr[hh, :, 0:1] + jnp.log2(l_scr[hh, :, 0:1])) + (cc_ref[hh] - cc_ref[hh, 0:1, :])

    wide = FOX_HB * DH
    qtile = lambda cb: pl.BlockSpec((tq, wide), lambda g, p, qi, kj, fk, sk, cb=cb: (qi[p], cb // FOX_HB + g))
    ktile = lambda cb: pl.BlockSpec((tq, wide), lambda g, p, qi, kj, fk, sk, cb=cb: (fk[g, p], cb // FOX_HB + g))
    qcol = pl.BlockSpec((FOX_HB, tq, 1), lambda g, p, qi, kj, fk, sk: (g, qi[p], 0))
    stat = pltpu.VMEM((FOX_HB, tq, LANE), F32)
    return pl.pallas_call(
        body, name=name,
        grid_spec=pltpu.PrefetchScalarGridSpec(
            num_scalar_prefetch=4, grid=(HEADS // FOX_HB, qi.shape[0]),
            in_specs=[qtile(FOX_Q_CB), ktile(FOX_K_CB), ktile(FOX_V_CB), qcol,
                      pl.BlockSpec((FOX_HB, 1, tq), lambda g, p, qi, kj, fk, sk: (g, 0, fk[g, p]))],
            out_specs=[qtile(0), qcol],
            scratch_shapes=[stat, stat, pltpu.VMEM((FOX_HB, tq, DH), F32), stat,
                            pltpu.VMEM((FOX_HB, tq, tq), F32), pltpu.VMEM((FOX_HB, tq, tq), BF16)]),
        out_shape=[jax.ShapeDtypeStruct((T, D_MODEL), F32), jax.ShapeDtypeStruct((HEADS, T, 1), F32)],
        compiler_params=_params("parallel", "arbitrary"),
    )(qi, kj, fetch_k, skip, z, z, z, c_col, c_row)


def _fox_bwd(z, c_col, c_row, o, lse, do, fetch_q, skip, *, name):
    T = z.shape[0]
    tq = _fox_tile(T)
    nq = T // tq
    rc = min(FOX_RC, tq)

    qi, kj = _fox_pairs(nq, by_query=False)

    def body(qi_ref, kj_ref, fq_ref, skip_ref, q_ref, k_ref, v_ref, cc_ref, cr_ref, o_ref, lse_ref, do_ref, dq_ref,
             dk_ref, dv_ref, dc_ref, dcq_ref, dk_acc, dv_acc, dc_acc, s_scr, dp_scr, p_scr, ds_scr, dcq_scr):
        p_id = pl.program_id(1)
        i, j = qi_ref[p_id], kj_ref[p_id]

        @pl.when(p_id == 0)
        def _():
            dq_ref[...] = jnp.zeros_like(dq_ref)
            dcq_scr[...] = jnp.zeros_like(dcq_scr)

        def update(hh, masked):
            cols = slice(hh * DH, (hh + 1) * DH)
            q2, k, dov = _fox_q2(q_ref[:, cols]), k_ref[:, cols], do_ref[:, cols]
            s_scr[hh] = _dot(q2, k, NT)
            dp_scr[hh] = _dot(dov, v_ref[:, cols], NT)
            crow = cr_ref[hh]
            csum = jnp.zeros((SUBLANE, tq), F32)
            wide = lambda col: jnp.tile(jnp.broadcast_to(col, (rc, LANE)), (1, tq // LANE))
            for r in range(tq // rc):
                rows = slice(r * rc, (r + 1) * rc)
                t = (s_scr[hh, rows, :] + wide(cc_ref[hh, rows, :] - lse_ref[hh, rows, :])) - crow
                if masked:
                    t = jnp.where(lax.broadcasted_iota(jnp.int32, (rc, tq), 1)
                                  <= r * rc + lax.broadcasted_iota(jnp.int32, (rc, tq), 0), t, NEG)
                p = jnp.exp2(t)
                delta = jnp.sum(do_ref[rows, cols] * o_ref[rows, cols], axis=-1, keepdims=True)
                ds = p * (dp_scr[hh, rows, :] - wide(delta))
                p_scr[hh, rows, :] = p.astype(BF16)
                ds_scr[hh, rows, :] = ds.astype(BF16)
                grows = pl.ds(pl.multiple_of(i * tq + r * rc, rc), rc)
                dcq_scr[hh, grows, :] += jnp.broadcast_to(jnp.sum(ds, axis=-1, keepdims=True), (rc, LANE))
                csum = csum + _fold8(ds)
            dsb = ds_scr[hh]
            dv_new = _dot(p_scr[hh], dov, TN)
            dk_new = _dot(dsb, q2, TN) * (1.0 / LOG2E)
            dc_new = -jnp.sum(csum, axis=0, keepdims=True)
            rows = pl.ds(pl.multiple_of(i * tq, tq), tq)
            dq_ref[rows, cols] += _dot(dsb, k, NN) * FOX_SCALE
            return dk_new, dv_new, dc_new

        for hh in range(FOX_HB):
            live = skip_ref[pl.program_id(0) * FOX_HB + hh, p_id] == 0

            @pl.when(i == j)
            def _():
                dk_new, dv_new, dc_new = update(hh, True)
                dk_acc[hh] = dk_new
                dv_acc[hh] = dv_new
                dc_acc[hh] = dc_new

            @pl.when((i > j) & live)
            def _():
                dk_new, dv_new, dc_new = update(hh, False)
                dk_acc[hh] += dk_new
                dv_acc[hh] += dv_new
                dc_acc[hh] += dc_new

            @pl.when(i == nq - 1)
            def _():
                dk_ref[:, hh * DH:(hh + 1) * DH] = dk_acc[hh].astype(BF16)
                dv_ref[:, hh * DH:(hh + 1) * DH] = dv_acc[hh].astype(BF16)
                dc_ref[hh] = dc_acc[hh]

            @pl.when(p_id == qi.shape[0] - 1)
            def _():
                for r in range(nq):
                    rows = slice(r * tq, (r + 1) * tq)
                    dcq_ref[hh, :, rows] = jnp.transpose(dcq_scr[hh, rows, :])[0:1, :]

    wide_cols = FOX_HB * DH
    n_groups = HEADS // FOX_HB
    qtile = lambda cb: pl.BlockSpec((tq, wide_cols), lambda g, p, qi, kj, fq, sk, cb=cb: (fq[g, p], cb // FOX_HB + g))
    ktile = lambda cb: pl.BlockSpec((tq, wide_cols), lambda g, p, qi, kj, fq, sk, cb=cb: (kj[p], cb // FOX_HB + g))
    qcol = pl.BlockSpec((FOX_HB, tq, 1), lambda g, p, qi, kj, fq, sk: (g, fq[g, p], 0))
    krow = pl.BlockSpec((FOX_HB, 1, tq), lambda g, p, qi, kj, fq, sk: (g, 0, kj[p]))
    tile_f32 = pltpu.VMEM((FOX_HB, tq, tq), F32)
    tile_bf16 = pltpu.VMEM((FOX_HB, tq, tq), BF16)
    return pl.pallas_call(
        body, name=name,
        grid_spec=pltpu.PrefetchScalarGridSpec(
            num_scalar_prefetch=4, grid=(n_groups, qi.shape[0]),
            in_specs=[qtile(FOX_Q_CB), ktile(FOX_K_CB), ktile(FOX_V_CB), qcol, krow, qtile(0), qcol, qtile(0)],
            out_specs=[pl.BlockSpec((T, wide_cols), lambda g, p, qi, kj, fq, sk: (0, g)), ktile(0), ktile(0), krow,
                       pl.BlockSpec((FOX_HB, 1, T), lambda g, p, qi, kj, fq, sk: (g, 0, 0))],
            scratch_shapes=[pltpu.VMEM((FOX_HB, tq, DH), F32), pltpu.VMEM((FOX_HB, tq, DH), F32),
                            pltpu.VMEM((FOX_HB, 1, tq), F32), tile_f32, tile_f32, tile_bf16, tile_bf16,
                            pltpu.VMEM((FOX_HB, T, LANE), F32)]),
        out_shape=[jax.ShapeDtypeStruct((T, D_MODEL), F32), jax.ShapeDtypeStruct((T, D_MODEL), BF16),
                   jax.ShapeDtypeStruct((T, D_MODEL), BF16), jax.ShapeDtypeStruct((HEADS, 1, T), F32),
                   jax.ShapeDtypeStruct((HEADS, 1, T), F32)],
        compiler_params=_params("parallel", "arbitrary"),
    )(qi, kj, fetch_q, skip, z, z, z, c_col, c_row, o, lse, do)


MEM_SCALE = 1.0 / math.sqrt(MEM_DH)


def _mem_probs(qh, kh):
    s = _dot(qh, kh, NT) * MEM_SCALE
    p = jnp.exp(s - jnp.max(s, axis=-1, keepdims=True))
    return p / jnp.sum(p, axis=-1, keepdims=True)


def _mem_fwd(q, kv, *, name, tq=512):
    T = q.shape[0]
    tq = min(tq, T)

    def body(q_ref, kv_ref, o_ref):
        for h in range(MEM_HEADS):
            cols = slice(h * MEM_DH, (h + 1) * MEM_DH)
            vcols = slice(D_MODEL + h * MEM_DH, D_MODEL + (h + 1) * MEM_DH)
            p = _mem_probs(q_ref[:, cols], kv_ref[:, cols])
            o_ref[:, cols] = _dot(p, kv_ref[:, vcols], NN).astype(o_ref.dtype)

    return pl.pallas_call(
        body, name=name, grid=(T // tq,),
        in_specs=[pl.BlockSpec((tq, D_MODEL), lambda i: (i, 0)), pl.BlockSpec((MEM_LEN, 2 * D_MODEL), lambda i: (0, 0))],
        out_specs=pl.BlockSpec((tq, D_MODEL), lambda i: (i, 0)), out_shape=jax.ShapeDtypeStruct((T, D_MODEL), BF16),
        compiler_params=_params("parallel"),
    )(q, kv)


def _mem_bwd(q, kv, do, *, name, tq=512):
    T = q.shape[0]
    tq = min(tq, T)

    def body(q_ref, kv_ref, do_ref, dq_ref, dkv_ref):
        @pl.when(pl.program_id(0) == 0)
        def _():
            dkv_ref[...] = jnp.zeros_like(dkv_ref)

        for h in range(MEM_HEADS):
            cols = slice(h * MEM_DH, (h + 1) * MEM_DH)
            vcols = slice(D_MODEL + h * MEM_DH, D_MODEL + (h + 1) * MEM_DH)
            qh, kh, doh = q_ref[:, cols], kv_ref[:, cols], do_ref[:, cols]
            p = _mem_probs(qh, kh)
            dp = _dot(doh, kv_ref[:, vcols], NT)
            ds = p * (dp - jnp.sum(p * dp, axis=-1, keepdims=True))
            dq_ref[:, cols] = (_dot(ds, kh, NN) * MEM_SCALE).astype(dq_ref.dtype)
            dkv_ref[:, cols] += _dot(ds, qh, TN) * MEM_SCALE
            dkv_ref[:, vcols] += _dot(p, doh, TN)

    row = pl.BlockSpec((tq, D_MODEL), lambda i: (i, 0))
    full = pl.BlockSpec((MEM_LEN, 2 * D_MODEL), lambda i: (0, 0))
    return pl.pallas_call(
        body, name=name, grid=(T // tq,), in_specs=[row, full, row], out_specs=[row, full],
        out_shape=[jax.ShapeDtypeStruct((T, D_MODEL), BF16), jax.ShapeDtypeStruct((MEM_LEN, 2 * D_MODEL), F32)],
        compiler_params=_params("arbitrary"),
    )(q, kv, do)


def _adamw(w, g, m, v, *, name, tm=256):
    R, C = w.shape
    tm = min(tm, R)
    assert R % tm == 0
    nsum = g.shape[0] if g.ndim == 3 else 0

    def body(w_ref, g_ref, m_ref, v_ref, go_ref, d_ref, mo_ref, vo_ref):
        if nsum:
            gv = g_ref[0]
            for n in range(1, nsum):
                gv = gv + g_ref[n]
        else:
            gv = g_ref[...]
        mv = ADAM_B1 * m_ref[...] + (1.0 - ADAM_B1) * gv
        vv = ADAM_B2 * v_ref[...] + (1.0 - ADAM_B2) * jnp.square(gv)
        m_hat = mv / (1.0 - ADAM_B1 ** ADAM_STEP)
        v_hat = vv / (1.0 - ADAM_B2 ** ADAM_STEP)
        d_ref[...] = -ADAM_LR * (m_hat / (jnp.sqrt(v_hat) + ADAM_EPS) + ADAM_WD * w_ref[...])
        go_ref[...] = gv
        mo_ref[...] = mv
        vo_ref[...] = vv

    row = pl.BlockSpec((tm, C), lambda i: (i, 0))
    gspec = pl.BlockSpec((nsum, tm, C), lambda i: (0, i, 0)) if nsum else row
    return pl.pallas_call(
        body, name=name, grid=(R // tm,), in_specs=[row, gspec, row, row], out_specs=[row] * 4,
        out_shape=[jax.ShapeDtypeStruct((R, C), F32)] * 4, compiler_params=_params("parallel"),
    )(w, g, m, v)


def _act_mm(a, w, name, out_dtype=F32, side=None):
    res = _mm([(a, w)], "nn", tm=1024, tn=512, tk=w.shape[0], out_dtypes=[out_dtype], name=name, side=side)
    return res if side is None else (res[0][0], res[1])


def _act_mm_t(a, w, name, out_dtype=F32):
    return _mm([(a, w)], "nt", tm=1024, tn=512, tk=1024, out_dtypes=[out_dtype], name=name)


def _wgrad(a, dy, name, tm=1024):
    tn = D_MODEL if dy.shape[1] % D_MODEL == 0 else D_FF // 2
    return _mm([(a, dy)], "tn", tm=tm, tn=tn, tk=1024, out_dtypes=[F32], name=name)


def _colsum8(p):
    return jnp.sum(p, axis=0, keepdims=True)


def _ffn_fwd(x, pre_g, post_g, wg, wu, wd, tag, gather_beside=None):
    h = _rms_fwd(x, pre_g, out_dtype=BF16, name=tag + "_pre")
    down = functools.partial(_mm, mode="nn", tm=1024, tn=512, tk=D_FF, out_dtypes=[F32], name=tag + "_down")
    gathered = None
    if gather_beside is None:
        act, gate, up = _ffn_in(h, wg, wu, name=tag + "_in")
        d = down([(act, wd)])
    else:
        (act, gate, up), landed = _ffn_in(h, wg, wu, name=tag + "_in", side=_x_gather(gather_beside))
        (d,), gathered = down([(act, wd)], side=_x_forward(landed))
    xo = _rms_fwd(d, post_g, out_dtype=F32, name=tag + "_post", res=x, coeff=0.5)
    return xo, (h, act, gate, up, d), gathered


def _ffn_bwd(x, dxo, saved, pre_g, post_g, wg, wu, wd, tag, reduce_beside=None):
    h, act, gate, up, d = saved
    dd, dg_post = _rms_bwd(d, post_g, dxo, name=tag + "_post_b", coeff=0.5, dx_dtype=BF16)
    act_b = functools.partial(_mm, [(dd, wd)], "nt", tm=1024, tn=256, tk=D_MODEL, out_dtypes=[BF16, BF16],
                              name=tag + "_act_b", epilogue=_swiglu_bwd_epilogue, tiles=(gate, up))
    in_b = lambda dgate, dup, **kw: _mm([(dgate, wg), (dup, wu)], "nt", tm=512, tn=512, tk=D_FF, out_dtypes=[F32],
                                        name=tag + "_in_b", **kw)
    reduced = None
    if reduce_beside is None:
        dgate, dup = act_b()
        dh = in_b(dgate, dup)
    else:
        grads, place, core = reduce_beside
        (dgate, dup), swapped = act_b(side=_x_swap(grads))
        pbf, own = _pair_sums(grads, swapped, place, "early")
        (dh,), scattered = in_b(dgate, dup, side=_x_scatter(pbf))
        reduced = _finish_reduce(own, scattered, core, "early")
    dwd = _wgrad(act, dd, tag + "_dwd", tm=D_FF // 2)
    dwg = _wgrad(h, dgate, tag + "_dwg")
    dwu = _wgrad(h, dup, tag + "_dwu")
    dx, dg_pre = _rms_bwd(x, pre_g, dh, name=tag + "_pre_b", add=dxo)
    return dx, dict(pre_g=_colsum8(dg_pre), post_g=_colsum8(dg_post), wg=dwg, wu=dwu, wd=dwd), reduced


def _local_step(x, mem, target, P, own, me, place, core):
    T = x.shape[0]
    G = {}
    logits = P["hg_lb_logits"]
    lb = _sigmoid(logits[0] - logits[1])
    lb_row = lb.reshape(1, D_MODEL)
    fbias_row = jnp.pad(P["fox_f_bias"], ((0, 0), (0, LANE - HEADS)))
    arrived = lambda group, others: _local_names(_assemble(group, own[group], others, me))

    W = arrived(GROUP_FFN1, _run(_x_forward(_run(_x_gather(own[GROUP_FFN1]), "gather_ffn1")), "forward_ffn1"))
    x1, ffn1_saved, others = _ffn_fwd(x, P["ffn1_pre_g"], P["ffn1_post_g"], W["f1g"], W["f1u"], W["f1d"], "ffn1",
                                      gather_beside=own[GROUP_MID])
    W.update(arrived(GROUP_MID, others))
    h2 = _rms_fwd(x1, P["mix_pre_g"], out_dtype=BF16, name="mix_pre")
    z, landed = _act_mm(h2, W["w_main"], "mix_in", side=_x_gather(own[GROUP_FFN2]))
    zfb = _mm([(h2, W["w_fb"])], "nn", tm=1024, tn=LANE, tk=D_MODEL, out_dtypes=[F32], name="mix_in_fb")
    oa_pre, states = _hgrn2_fwd(z, lb_row, name="hgrn2_f")
    o_a = _rms_fwd(oa_pre, P["hg_norm_g"], out_dtype=BF16, name="hgrn2_post", mul=(z, 3))
    y_a, others = _act_mm(o_a, W["wa"], "branch_a", side=_x_forward(landed))
    W.update(arrived(GROUP_FFN2, others))
    c = _cumsum_t([(zfb, 0)], name="fox_c", width=LANE, rows=[(fbias_row, 0)], pre=lambda v, r: _logsigmoid(v + r),
                  post=lambda cum, v, r: (cum * LOG2E,))
    ct = c[:, :HEADS].T
    c_col, c_row = ct[:, :, None], ct[:, None, :]
    fetch_k, skip_f, fetch_q, skip_b = _fox_schedule(*_fox_norms(z, name="fox_norms"), ct, _fox_tile(T))
    o_b, lse = _fox_fwd(z, c_col, c_row, fetch_k, skip_f, name="fox_f")
    y_b = _act_mm(o_b, W["wb"], "branch_b")
    y = _gatemix_fwd(z, P["b_gate"], y_a, y_b, name="gatemix")
    m = _act_mm(y, W["wo"], "mix_out")
    x2 = _rms_fwd(m, P["mix_post_g"], out_dtype=F32, name="mix_post", res=x1)
    h3 = _rms_fwd(x2, P["mem_pre_g"], out_dtype=BF16, name="mem_pre")
    mem_n = _rms_fwd(mem, P["mem_kv_g"], out_dtype=BF16, name="mem_kvn")
    qm = _act_mm(h3, W["wmq"], "mem_q")
    kv = _act_mm(mem_n, W["wmkv"], "mem_kv")
    om = _mem_fwd(qm, kv, name="mem_attn")
    mo = _act_mm(om, W["wmo"], "mem_o")
    x3 = _rms_fwd(mo, P["mem_post_g"], out_dtype=F32, name="mem_post", res=x2)
    x4, ffn2_saved, _ = _ffn_fwd(x3, P["ffn2_pre_g"], P["ffn2_post_g"], W["f2g"], W["f2u"], W["f2d"], "ffn2")
    dx4, sq = _loss_head(x4, target, name="loss_head")

    dx3, g, _ = _ffn_bwd(x3, dx4, ffn2_saved, P["ffn2_pre_g"], P["ffn2_post_g"], W["f2g"], W["f2u"], W["f2d"], "ffn2")
    G.update(ffn2_pre_g=g["pre_g"], ffn2_post_g=g["post_g"], f2g=g["wg"], f2u=g["wu"], f2d=g["wd"])

    dmo, dgp = _rms_bwd(mo, P["mem_post_g"], dx3, name="mem_post_b", dx_dtype=BF16)
    G["mem_post_g"] = _colsum8(dgp)
    dom = _act_mm_t(dmo, W["wmo"], "mem_o_b", BF16)
    G["wmo"] = _wgrad(om, dmo, "mem_o_w")
    dqm, dkv = _mem_bwd(qm, kv, dom, name="mem_attn_b")
    dh3 = _act_mm_t(dqm, W["wmq"], "mem_q_b")
    G["wmq"] = _wgrad(h3, dqm, "mem_q_w")
    G["wmkv"] = _mm([(mem_n, dkv)], "tn", tm=1024, tn=512, tk=MEM_LEN, out_dtypes=[F32], name="mem_kv_w")
    dmem_n = _mm([(dkv, W["wmkv"])], "nt", tm=MEM_LEN, tn=512, tk=2 * D_MODEL, out_dtypes=[F32], name="mem_kv_b")
    _, dgp = _rms_bwd(mem, P["mem_kv_g"], dmem_n, name="mem_kvn_b")
    G["mem_kv_g"] = _colsum8(dgp)
    dx2, dgp = _rms_bwd(x2, P["mem_pre_g"], dh3, name="mem_pre_b", add=dx3)
    G["mem_pre_g"] = _colsum8(dgp)

    dm, dgp = _rms_bwd(m, P["mix_post_g"], dx2, name="mix_post_b", dx_dtype=BF16)
    G["mix_post_g"] = _colsum8(dgp)
    dy = _act_mm_t(dm, W["wo"], "mix_out_b")
    G["wo"] = _wgrad(y, dm, "mix_out_w")
    dya, dyb, dz0, dz1, s0, s1 = _gatemix_bwd(z, P["b_gate"], y_a, y_b, dy, name="gatemix_b")
    G["b_gate"] = jnp.concatenate([_colsum8(s0), _colsum8(s1)], axis=1)
    do_a = _act_mm_t(dya, W["wa"], "branch_a_b")
    G["wa"] = _wgrad(o_a, dya, "branch_a_w")
    do_b = _act_mm_t(dyb, W["wb"], "branch_b_b")
    G["wb"] = _wgrad(o_b, dyb, "branch_b_w")
    doa_pre, dgp, dga = _rms_bwd(oa_pre, P["hg_norm_g"], do_a, name="hgrn2_post_b", mul=(z, 3))
    G["hg_norm_g"] = _colsum8(dgp)
    dq_a, dfl_a, di_a, dlb = _hgrn2_bwd(z, lb_row, states, doa_pre, name="hgrn2_b")
    dl0 = (dlb * lb_row * (1.0 - lb_row)).reshape(1, HEADS, DH)
    G["hg_lb_logits"] = jnp.concatenate([dl0, -dl0], axis=0)
    dq_b, dk_b, dv_b, dcr, dcq = _fox_bwd(z, c_col, c_row, o_b, lse, do_b, fetch_q, skip_b, name="fox_b")
    dc_pad = jnp.pad((dcr[:, 0, :] + dcq[:, 0, :]).T, ((0, 0), (0, LANE - HEADS)))
    gate_b = lambda cum, dc, zf, r: cum * _sigmoid(-(zf + r))
    dfl_b, dfb = _cumsum_t([(dc_pad, 0), (zfb, 0)], name="fox_c_b", width=LANE, reverse=True, rows=[(fbias_row, 0)],
                           pre=lambda dc, zf, r: dc, post=lambda *a: (gate_b(*a),), fold=gate_b)
    G["fox_f_bias"] = _colsum8(dfb)[:, :HEADS]

    pieces = [dq_a, dfl_a, di_a, dga, dq_b, dk_b, dv_b, dz0, dz1]
    dh2 = _mm([(dfl_b, W["w_fb"])], "nt", tm=512, tn=D_MODEL, tk=LANE, out_dtypes=[F32], name="mix_in_fb_b")
    for lo, hi in ((0, 5), (5, 9)):
        dh2 = _mm([(p, W["w_main"]) for p in pieces[lo:hi]], "nt", tm=512, tn=D_MODEL, tk=D_MODEL, out_dtypes=[F32],
                  name=f"mix_in_b{lo}", b_koff=[n * D_MODEL for n in range(lo, hi)],
                  epilogue=lambda acc, t: (acc + t,), tiles=(dh2,))
    G["w_main"] = [_wgrad(h2, p, f"mix_in_w{n}") for n, p in enumerate(pieces)]
    G["w_fb"] = _mm([(h2, dfl_b)], "tn", tm=1024, tn=LANE, tk=512, out_dtypes=[F32], name="mix_in_fb_w")
    dx1, dgp = _rms_bwd(x1, P["mix_pre_g"], dh2, name="mix_pre_b", add=dx2)
    G["mix_pre_g"] = _colsum8(dgp)

    dx0, g, reduced = _ffn_bwd(x, dx1, ffn1_saved, P["ffn1_pre_g"], P["ffn1_post_g"], W["f1g"], W["f1u"], W["f1d"],
                               "ffn1", reduce_beside=(_grad_slabs(G, GROUP_MID + GROUP_FFN2), place, core))
    G.update(ffn1_pre_g=g["pre_g"], ffn1_post_g=g["post_g"], f1g=g["wg"], f1u=g["wu"], f1d=g["wd"])
    return sq, dx0, G, reduced


N_CHIP = 4
N_DEV = 8
IN_COLS = 9224
FB_COL = 7 * D_MODEL
SHARDED = (
    ("ffn1_w_in", (D_MODEL, 2 * D_FF), 1), ("ffn1_w_down", (D_FF, D_MODEL), 0), ("w_in", (D_MODEL, IN_COLS), 1),
    ("w_branch_a", (D_MODEL, D_MODEL), 0), ("w_branch_b", (D_MODEL, D_MODEL), 0), ("w_out", (D_MODEL, D_MODEL), 0),
    ("w_mq", (D_MODEL, D_MODEL), 0), ("w_mkv", (D_MODEL, 2 * D_MODEL), 1), ("w_mo", (D_MODEL, D_MODEL), 0),
    ("ffn2_w_in", (D_MODEL, 2 * D_FF), 1), ("ffn2_w_down", (D_FF, D_MODEL), 0),
)
SMALL = ("ffn1_pre_g", "ffn1_post_g", "mix_pre_g", "hg_norm_g", "mix_post_g", "mem_pre_g", "mem_kv_g", "mem_post_g",
         "ffn2_pre_g", "ffn2_post_g", "b_gate", "hg_lb_logits", "fox_f_bias")
SMALL_SHAPES = dict(b_gate=(1, 2 * D_MODEL), hg_lb_logits=(2, HEADS, DH), fox_f_bias=(1, HEADS))
SMALL_ROWS = 16
WEIGHT_ORDER = ("ffn1_pre_g", "ffn1_w_in", "ffn1_w_down", "ffn1_post_g", "mix_pre_g", "w_in", "hg_lb_logits", "hg_norm_g",
                "fox_f_bias", "w_branch_a", "w_branch_b", "b_gate", "w_out", "mix_post_g", "mem_pre_g", "mem_kv_g", "w_mq",
                "w_mkv", "w_mo", "mem_post_g", "ffn2_pre_g", "ffn2_w_in", "ffn2_w_down", "ffn2_post_g")


GROUP_FFN1 = ("ffn1_w_in", "ffn1_w_down")
GROUP_MID = ("w_in", "w_branch_a", "w_branch_b", "w_out", "w_mq", "w_mkv", "w_mo")
GROUP_FFN2 = ("ffn2_w_in", "ffn2_w_down")


def _layout(names, axis):
    out, at = [], 0
    for name, shape, ax in SHARDED:
        if ax == axis and name in names:
            n = shape[ax] // N_CHIP
            out.append((name, at, n))
            at += n if axis == 0 else -(-n // LANE) * LANE
    return out


def _pack(shards, names, dtype):
    rows = jnp.concatenate([shards[name].astype(dtype) for name, _, _ in _layout(names, 0)], axis=0)
    cols = [jnp.pad(shards[name].astype(dtype), ((0, 0), (0, -n % LANE))) for name, _, n in _layout(names, 1)]
    return [rows, jnp.concatenate(cols, axis=1)]


def _unpack(slabs, names):
    rows, cols = slabs
    out = {name: rows[at:at + n] for name, at, n in _layout(names, 0)}
    out.update({name: cols[:, at:at + n] for name, at, n in _layout(names, 1)})
    return out


def _pack_small(vals):
    rows = []
    for name in SMALL:
        v = vals[name].astype(F32).reshape(-1)
        rows.append(jnp.pad(v, (0, -v.shape[0] % D_MODEL)).reshape(-1, D_MODEL))
    rows = jnp.concatenate(rows, axis=0)
    return jnp.pad(rows, ((0, SMALL_ROWS - rows.shape[0]), (0, 0)))


def _unpack_small(slab):
    out, r = {}, 0
    for name in SMALL:
        shape = SMALL_SHAPES.get(name, (1, D_MODEL))
        size = math.prod(shape)
        n = -(-size // D_MODEL)
        out[name] = slab[r:r + n].reshape(-1)[:size].reshape(shape)
        r += n
    return out


MESH = pl.DeviceIdType.MESH
CHIP_FLIPS = ((0, 1), (1, 0), (1, 1))


def _place():
    x, y, c = lax.axis_index("x"), lax.axis_index("y"), lax.axis_index("c")
    chips = [(x ^ fx, y ^ fy) for fx, fy in CHIP_FLIPS]
    return x, y, c, chips


def _remote(src, dst, sems, k, dev):
    return pltpu.make_async_remote_copy(src_ref=src, dst_ref=dst, send_sem=sems[0].at[k], recv_sem=sems[1].at[k],
                                        device_id=dev, device_id_type=MESH)


def _exchange(copies, arrays, out_shapes, aliases=None):
    def start(ins, outs, sems):
        for sent, _ in copies(ins, outs, sems):
            sent.start()

    def wait(ins, outs, sems):
        pairs = copies(ins, outs, sems)
        for _, got in pairs:
            got.wait_recv()
        for sent, _ in pairs:
            sent.wait_send()

    return _Side(arrays, out_shapes, copies.count, start, wait, aliases)


def _counted(count):
    def mark(fn):
        fn.count = count
        return fn
    return mark


def _slab_halves(c, rows):
    half = rows // 2
    return pl.ds(c * half, half), pl.ds((1 - c) * half, half)


def _x_gather(slabs):
    @_counted(3 * len(slabs))
    def copies(ins, outs, sems):
        x, y, c, chips = _place()
        res = []
        for s, slab in enumerate(slabs):
            mine, _ = _slab_halves(c, slab.shape[0])
            for k, (px, py) in enumerate(chips):
                res.append((_remote(ins[s].at[mine], outs[s].at[k, mine], sems, 3 * s + k, (px, py, c)),) * 2)
        return res

    return _exchange(copies, slabs, [jax.ShapeDtypeStruct((3,) + s.shape, s.dtype) for s in slabs])


def _x_forward(gathered):
    @_counted(3 * len(gathered))
    def copies(ins, outs, sems):
        x, y, c, _ = _place()
        res = []
        for s, buf in enumerate(gathered):
            mine, theirs = _slab_halves(c, buf.shape[1])
            for k in range(3):
                res.append((_remote(ins[s].at[k, mine], outs[s].at[k, mine], sems, 3 * s + k, (x, y, 1 - c)),
                            _remote(ins[s].at[k, theirs], outs[s].at[k, theirs], sems, 3 * s + k, (x, y, 1 - c))))
        return res

    return _exchange(copies, gathered, [jax.ShapeDtypeStruct(g.shape, g.dtype) for g in gathered],
                     aliases={s: s for s in range(len(gathered))})


def _x_swap(grads):
    @_counted(N_CHIP * len(grads))
    def copies(ins, outs, sems):
        x, y, c, _ = _place()
        res = []
        for s, g in enumerate(grads):
            _, theirs = _slab_halves(c, g.shape[1])
            for j in range(N_CHIP):
                res.append((_remote(ins[s].at[j, theirs], outs[s].at[j], sems, N_CHIP * s + j, (x, y, 1 - c)),) * 2)
        return res

    return _exchange(copies, grads, [jax.ShapeDtypeStruct((N_CHIP, g.shape[1] // 2, g.shape[2]), g.dtype) for g in grads])


def _x_scatter(partials):
    @_counted(3 * len(partials))
    def copies(ins, outs, sems):
        x, y, c, chips = _place()
        res = []
        for s in range(len(partials)):
            for k, (px, py) in enumerate(chips):
                res.append((_remote(ins[s].at[2 * px + py], outs[s].at[k], sems, 3 * s + k, (px, py, c)),) * 2)
        return res

    return _exchange(copies, partials, [jax.ShapeDtypeStruct((3,) + p.shape[1:], p.dtype) for p in partials])


def _x_join(halves):
    @_counted(len(halves))
    def copies(ins, outs, sems):
        x, y, c, _ = _place()
        return [(_remote(ins[s], outs[s], sems, s, (x, y, 1 - c)),) * 2 for s in range(len(halves))]

    return _exchange(copies, halves, [jax.ShapeDtypeStruct(h.shape, h.dtype) for h in halves])


def _run(side, name):
    n_in, n_out = len(side.arrays), len(side.out_shapes)

    def body(*refs):
        ins, outs, sems = refs[:n_in], refs[n_in:n_in + n_out], refs[-2:]
        side.start(ins, outs, sems)
        side.wait(ins, outs, sems)

    plumb = side.plumb(0, 0)
    return pl.pallas_call(
        body, name=name, in_specs=plumb["in_specs"], out_specs=plumb["out_specs"], out_shape=side.out_shapes,
        scratch_shapes=plumb["scratch"], input_output_aliases=plumb["aliases"],
    )(*side.arrays)


def _pair_sum(g, got, place, tag, *, tm):
    _, half, width = got.shape
    nb = half // tm

    def body(s_ref, g_ref, a_ref, bf_ref, own_ref):
        v = g_ref[...] + a_ref[...]
        bf_ref[...] = v.astype(BF16)

        @pl.when(pl.program_id(1) == s_ref[0])
        def _():
            own_ref[...] = v

    return pl.pallas_call(
        body, name="pair_sum_" + tag,
        grid_spec=pltpu.PrefetchScalarGridSpec(
            num_scalar_prefetch=1, grid=(nb, N_CHIP),
            in_specs=[pl.BlockSpec((None, tm, width), lambda i, j, s: (j, s[1] * nb + i, 0)),
                      pl.BlockSpec((None, tm, width), lambda i, j, s: (j, i, 0))],
            out_specs=[pl.BlockSpec((None, tm, width), lambda i, j, s: (j, i, 0)),
                       pl.BlockSpec((tm, width), lambda i, j, s: (i, 0))]),
        out_shape=[jax.ShapeDtypeStruct((N_CHIP, half, width), BF16), jax.ShapeDtypeStruct((half, width), F32)],
        compiler_params=_params("arbitrary", "arbitrary"),
    )(place, g, got)


def _chip_sum(own, got, tag, *, tm):
    half, width = own.shape

    def body(o_ref, g_ref, r_ref):
        r_ref[...] = ((o_ref[...] + g_ref[0].astype(F32)) + g_ref[1].astype(F32)) + g_ref[2].astype(F32)

    row = pl.BlockSpec((tm, width), lambda i: (i, 0))
    return pl.pallas_call(
        body, name="chip_sum_" + tag, grid=(half // tm,),
        in_specs=[row, pl.BlockSpec((3, tm, width), lambda i: (0, i, 0))], out_specs=row,
        out_shape=jax.ShapeDtypeStruct((half, width), F32), compiler_params=_params("parallel"),
    )(own, got)


def _sum_tiles(slabs):
    return [slabs[0].shape[1] // 4, D_MODEL // 8]


def _pair_sums(grads, swapped, place, tag):
    res = [_pair_sum(g, s, place, f"{tag}_{n}", tm=tm) for n, (g, s, tm) in enumerate(zip(grads, swapped, _sum_tiles(grads)))]
    return [r[0] for r in res], [r[1] for r in res]


def _finish_reduce(own, scattered, core, tag):
    mine = [_chip_sum(o, s, f"{tag}_{n}", tm=o.shape[0] // 2) for n, (o, s) in enumerate(zip(own, scattered))]
    theirs = _run(_x_join(mine), "join_halves_" + tag)
    return [lax.dynamic_update_slice(jnp.concatenate([a, a]), b, ((1 - core) * a.shape[0], 0))
            for a, b in zip(mine, theirs)]


def _reduce_scatter(grads, place, core, tag):
    pbf, own = _pair_sums(grads, _run(_x_swap(grads), "swap_halves_" + tag), place, tag)
    return _finish_reduce(own, _run(_x_scatter(pbf), "scatter_partials_" + tag), core, tag)


def _gather_small(s):
    flips = [(fx, fy, fc) for fx in (0, 1) for fy in (0, 1) for fc in (0, 1)][1:]

    def body(s_ref, out_ref, send_sems, recv_sems, local_sem):
        x, y, c, _ = _place()
        sems = (send_sems, recv_sems)
        me = 4 * x + 2 * y + c
        local = pltpu.make_async_copy(s_ref, out_ref.at[me], local_sem)
        local.start()
        sent = [_remote(s_ref, out_ref.at[me], sems, k, (x ^ fx, y ^ fy, c ^ fc)) for k, (fx, fy, fc) in enumerate(flips)]
        for cp in sent:
            cp.start()
        for k, (fx, fy, fc) in enumerate(flips):
            peer = (x ^ fx, y ^ fy, c ^ fc)
            _remote(s_ref, out_ref.at[4 * peer[0] + 2 * peer[1] + peer[2]], sems, k, peer).wait_recv()
        for cp in sent:
            cp.wait_send()
        local.wait()

    return pl.pallas_call(
        body, name="gather_small", out_shape=jax.ShapeDtypeStruct((N_DEV, SMALL_ROWS, D_MODEL), s.dtype),
        in_specs=[ANY], out_specs=ANY,
        scratch_shapes=[pltpu.SemaphoreType.DMA((7,)), pltpu.SemaphoreType.DMA((7,)), pltpu.SemaphoreType.DMA],
    )(s)


LOCAL_NAMES = dict(ffn1_w_down="f1d", ffn2_w_down="f2d", w_branch_a="wa", w_branch_b="wb", w_out="wo", w_mq="wmq",
                   w_mkv="wmkv", w_mo="wmo")


def _assemble(names, own, others, me):
    by_flip = [jnp.concatenate([o[None], t], axis=0) for o, t in zip(own, others)]
    per_chip = [_unpack([lax.dynamic_index_in_dim(s, j ^ me, 0, keepdims=False) for s in by_flip], names)
                for j in range(N_CHIP)]
    return {name: jnp.concatenate([pc[name] for pc in per_chip], axis=axis)
            for name, _, axis in SHARDED if name in names}


def _local_names(full):
    out = {LOCAL_NAMES[name]: a for name, a in full.items() if name in LOCAL_NAMES}
    for name, key in (("ffn1_w_in", "f1"), ("ffn2_w_in", "f2")):
        if name in full:
            out[key + "g"], out[key + "u"] = full[name][:, :D_FF], full[name][:, D_FF:]
    if "w_in" in full:
        w_in = full["w_in"]
        out["w_main"] = jnp.concatenate([w_in[:, :FB_COL], w_in[:, FB_COL + HEADS:]], axis=1)
        out["w_fb"] = jnp.pad(w_in[:, FB_COL:FB_COL + HEADS], ((0, 0), (0, LANE - HEADS)))
    return out


def _grad_slabs(G, names):
    full = {name: G[key] for name, key in LOCAL_NAMES.items() if name in names}
    for name, key in (("ffn1_w_in", "f1"), ("ffn2_w_in", "f2")):
        if name in names:
            full[name] = jnp.concatenate([G[key + "g"], G[key + "u"]], axis=1)
    if "w_in" in names:
        main = jnp.concatenate(G["w_main"], axis=1)
        full["w_in"] = jnp.concatenate([main[:, :FB_COL], G["w_fb"][:, :HEADS], main[:, FB_COL:]], axis=1)
    rows, cols = [], []
    for j in range(N_CHIP):
        shards = {}
        for name, shape, axis in SHARDED:
            if name in names:
                n = shape[axis] // N_CHIP
                shards[name] = lax.slice_in_dim(full[name], j * n, (j + 1) * n, axis=axis)
        r, c = _pack(shards, names, F32)
        rows.append(r)
        cols.append(c)
    return [jnp.stack(rows, axis=0), jnp.stack(cols, axis=0)]


def kernel(x, mem, ffn1_pre_g, ffn1_w_in, ffn1_w_down, ffn1_post_g, mix_pre_g, w_in, hg_lb_logits, hg_norm_g, fox_f_bias, w_branch_a, w_branch_b, b_gate, w_out, mix_post_g, mem_pre_g, mem_kv_g, w_mq, w_mkv, w_mo, mem_post_g, ffn2_pre_g, ffn2_w_in, ffn2_w_down, ffn2_post_g, loss_target, m_ffn1_pre_g, m_ffn1_w_in, m_ffn1_w_down, m_ffn1_post_g, m_mix_pre_g, m_w_in, m_hg_lb_logits, m_hg_norm_g, m_fox_f_bias, m_w_branch_a, m_w_branch_b, m_b_gate, m_w_out, m_mix_post_g, m_mem_pre_g, m_mem_kv_g, m_w_mq, m_w_mkv, m_w_mo, m_mem_post_g, m_ffn2_pre_g, m_ffn2_w_in, m_ffn2_w_down, m_ffn2_post_g, v_ffn1_pre_g, v_ffn1_w_in, v_ffn1_w_down, v_ffn1_post_g, v_mix_pre_g, v_w_in, v_hg_lb_logits, v_hg_norm_g, v_fox_f_bias, v_w_branch_a, v_w_branch_b, v_b_gate, v_w_out, v_mix_post_g, v_mem_pre_g, v_mem_kv_g, v_w_mq, v_w_mkv, v_w_mo, v_mem_post_g, v_ffn2_pre_g, v_ffn2_w_in, v_ffn2_w_down, v_ffn2_post_g):
    w = dict(ffn1_pre_g=ffn1_pre_g, ffn1_w_in=ffn1_w_in, ffn1_w_down=ffn1_w_down, ffn1_post_g=ffn1_post_g, mix_pre_g=mix_pre_g, w_in=w_in, hg_lb_logits=hg_lb_logits, hg_norm_g=hg_norm_g, fox_f_bias=fox_f_bias, w_branch_a=w_branch_a, w_branch_b=w_branch_b, b_gate=b_gate, w_out=w_out, mix_post_g=mix_post_g, mem_pre_g=mem_pre_g, mem_kv_g=mem_kv_g, w_mq=w_mq, w_mkv=w_mkv, w_mo=w_mo, mem_post_g=mem_post_g, ffn2_pre_g=ffn2_pre_g, ffn2_w_in=ffn2_w_in, ffn2_w_down=ffn2_w_down, ffn2_post_g=ffn2_post_g)
    m = dict(ffn1_pre_g=m_ffn1_pre_g, ffn1_w_in=m_ffn1_w_in, ffn1_w_down=m_ffn1_w_down, ffn1_post_g=m_ffn1_post_g, mix_pre_g=m_mix_pre_g, w_in=m_w_in, hg_lb_logits=m_hg_lb_logits, hg_norm_g=m_hg_norm_g, fox_f_bias=m_fox_f_bias, w_branch_a=m_w_branch_a, w_branch_b=m_w_branch_b, b_gate=m_b_gate, w_out=m_w_out, mix_post_g=m_mix_post_g, mem_pre_g=m_mem_pre_g, mem_kv_g=m_mem_kv_g, w_mq=m_w_mq, w_mkv=m_w_mkv, w_mo=m_w_mo, mem_post_g=m_mem_post_g, ffn2_pre_g=m_ffn2_pre_g, ffn2_w_in=m_ffn2_w_in, ffn2_w_down=m_ffn2_w_down, ffn2_post_g=m_ffn2_post_g)
    v = dict(ffn1_pre_g=v_ffn1_pre_g, ffn1_w_in=v_ffn1_w_in, ffn1_w_down=v_ffn1_w_down, ffn1_post_g=v_ffn1_post_g, mix_pre_g=v_mix_pre_g, w_in=v_w_in, hg_lb_logits=v_hg_lb_logits, hg_norm_g=v_hg_norm_g, fox_f_bias=v_fox_f_bias, w_branch_a=v_w_branch_a, w_branch_b=v_w_branch_b, b_gate=v_b_gate, w_out=v_w_out, mix_post_g=v_mix_post_g, mem_pre_g=v_mem_pre_g, mem_kv_g=v_mem_kv_g, w_mq=v_w_mq, w_mkv=v_w_mkv, w_mo=v_w_mo, mem_post_g=v_mem_post_g, ffn2_pre_g=v_ffn2_pre_g, ffn2_w_in=v_ffn2_w_in, ffn2_w_down=v_ffn2_w_down, ffn2_post_g=v_ffn2_post_g)
    sharded = [name for name, _, _ in SHARDED]
    shard_of = lambda d: {name: d[name][0] for name in sharded}

    me, core = 2 * lax.axis_index("x") + lax.axis_index("y"), lax.axis_index("c")
    place = jnp.stack([me, core]).astype(jnp.int32)
    own = {group: _pack(shard_of(w), group, BF16) for group in (GROUP_FFN1, GROUP_MID, GROUP_FFN2)}
    P = {name: w[name] for name in SMALL}

    sq, dx0, G, reduced = _local_step(x[0], mem[0], loss_target[0], P, own, me, place, core)
    loss = lax.psum(0.5 * jnp.sum(sq) / D_MODEL, ("x", "y", "c"))

    g_shards = _unpack(reduced, GROUP_MID + GROUP_FFN2)
    g_shards.update(_unpack(_reduce_scatter(_grad_slabs(G, GROUP_FFN1), place, core, "late"), GROUP_FFN1))
    big = {}
    for name, shape, axis in SHARDED:
        rows = shape[0] // (N_CHIP if axis == 0 else 1)
        big[name] = _adamw(w[name][0], g_shards[name], m[name][0], v[name][0], name="adamw_" + name, tm=rows // 8)
    small = _adamw(_pack_small(w), _gather_small(_pack_small(G)), _pack_small(m), _pack_small(v), name="adamw_small",
                   tm=SMALL_ROWS)

    outs = [loss, dx0[None]]
    for n in range(4):
        vals = {name: res[n][None] for name, res in big.items()}
        vals.update(_unpack_small(small[n]))
        outs += [vals[name] for name in WEIGHT_ORDER]
    return tuple(outs)
```

```python
import functools
import math

import jax
import jax.numpy as jnp
from jax import lax
from jax.experimental import pallas as pl
from jax.experimental.pallas import tpu as pltpu

F32 = jnp.float32
BF16 = jnp.bfloat16

D_MODEL = 1024
D_FF = 2816
HEADS = 8
DH = 128
MEM_HEADS = 4
MEM_DH = 256
MEM_LEN = 256
EPS = 1e-6
SUB = 16
LANE = 128
SUBLANE = 8
VMEM_LIMIT = 56 * 1024 * 1024

ADAM_LR = 0.001
ADAM_B1 = 0.9
ADAM_B2 = 0.999
ADAM_EPS = 1e-08
ADAM_WD = 0.01
ADAM_STEP = 10

HIGHEST = lax.Precision.HIGHEST


def _params(*sem):
    return pltpu.CompilerParams(dimension_semantics=sem, vmem_limit_bytes=VMEM_LIMIT)


def _sigmoid(v):
    return 0.5 * jnp.tanh(0.5 * v) + 0.5


def _silu(v):
    return v * _sigmoid(v)


def _dsilu(v):
    s = _sigmoid(v)
    return s * (1.0 + v * (1.0 - s))


def _dot(a, b, dims):
    return lax.dot_general(a.astype(BF16), b.astype(BF16), (dims, ((), ())), preferred_element_type=F32)


NN = ((1,), (0,))
NT = ((1,), (1,))
TN = ((0,), (0,))


ANY = pl.BlockSpec(memory_space=pl.ANY)


class _Side:
    def __init__(self, arrays, out_shapes, nsem, start, wait, aliases=None):
        self.arrays, self.out_shapes, self.nsem = list(arrays), list(out_shapes), nsem
        self.start, self.wait, self.aliases = start, wait, dict(aliases or {})

    def plumb(self, n_in, n_out):
        return dict(args=self.arrays, in_specs=[ANY] * len(self.arrays), out_specs=[ANY] * len(self.out_shapes),
                    scratch=[pltpu.SemaphoreType.DMA((self.nsem,)), pltpu.SemaphoreType.DMA((self.nsem,))],
                    aliases={n_in + i: n_out + o for i, o in self.aliases.items()})

    def run_at_ends(self, ins, outs, sems, first, last, compute):
        @pl.when(first)
        def _():
            self.start(ins, outs, sems)

        compute()

        @pl.when(last)
        def _():
            self.wait(ins, outs, sems)


def _grid_ends(grid):
    first = functools.reduce(lambda a, b: a & b, [pl.program_id(d) == 0 for d in range(len(grid))])
    last = functools.reduce(lambda a, b: a & b, [pl.program_id(d) == grid[d] - 1 for d in range(len(grid))])
    return first, last


def _mm(pairs, mode, *, tm, tn, tk, out_dtypes, name, epilogue=None, tiles=(), b_koff=None, side=None):
    a0, b0 = pairs[0]
    if mode == "nn":
        (M, K), N = a0.shape, b0.shape[1]
    elif mode == "nt":
        (M, K), N = a0.shape, b0.shape[0]
    else:
        (K, M), N = a0.shape, b0.shape[1]
    tm, tn, tk = min(tm, M), min(tn, N), min(tk, K)
    assert M % tm == 0 and N % tn == 0 and K % tk == 0, (name, M, N, K, tm, tn, tk)
    nk = K // tk
    npair = len(pairs)
    koff = [0] * npair if b_koff is None else [o // tk for o in b_koff]
    if b_koff is not None:
        assert all(o % tk == 0 for o in b_koff)
    in_specs, args = [], []
    for p, (a, b) in enumerate(pairs):
        if mode == "nn":
            sa = pl.BlockSpec((tm, tk), lambda i, j, k: (i, k))
            sb = pl.BlockSpec((tk, tn), lambda i, j, k, o=koff[p]: (k + o, j))
            dims = NN
        elif mode == "nt":
            sa = pl.BlockSpec((tm, tk), lambda i, j, k: (i, k))
            sb = pl.BlockSpec((tn, tk), lambda i, j, k, o=koff[p]: (j, k + o))
            dims = NT
        else:
            sa = pl.BlockSpec((tk, tm), lambda i, j, k: (k, i))
            sb = pl.BlockSpec((tk, tn), lambda i, j, k, o=koff[p]: (k + o, j))
            dims = TN
        in_specs += [sa, sb]
        args += [a, b]
    for t in tiles:
        in_specs.append(pl.BlockSpec((tm, tn), lambda i, j, k: (i, j)))
        args.append(t)
    nt_ = len(tiles)
    nout = len(out_dtypes)
    nin = len(args)
    grid = (M // tm, N // tn, nk)
    plumb = side.plumb(nin, nout) if side is not None else None
    ns_in, ns_out = (len(side.arrays), len(side.out_shapes)) if side is not None else (0, 0)

    def body(*refs):
        ab = refs[: 2 * npair]
        tl = refs[2 * npair: nin]
        outs = refs[nin + ns_in: nin + ns_in + nout]
        scratch = refs[nin + ns_in + nout + ns_out:]
        acc_ref = scratch[0] if nk > 1 else None
        if side is None:
            compute(ab, tl, outs, acc_ref)
        else:
            first, last = _grid_ends(grid)
            side.run_at_ends(refs[nin: nin + ns_in], refs[nin + ns_in + nout: nin + ns_in + nout + ns_out],
                             scratch[-2:], first, last, lambda: compute(ab, tl, outs, acc_ref))

    def compute(ab, tl, outs, acc_ref):
        def partial_sum():
            s = _dot(ab[0][...], ab[1][...], dims)
            for p in range(1, npair):
                s = s + _dot(ab[2 * p][...], ab[2 * p + 1][...], dims)
            return s

        def finish(acc):
            res = (acc,) if epilogue is None else epilogue(acc, *[t[...] for t in tl])
            for o, r in zip(outs, res):
                o[...] = r.astype(o.dtype)

        if nk == 1:
            finish(partial_sum())
        else:
            k = pl.program_id(2)

            @pl.when(k == 0)
            def _():
                acc_ref[...] = jnp.zeros_like(acc_ref)

            acc_ref[...] += partial_sum()

            @pl.when(k == nk - 1)
            def _():
                finish(acc_ref[...])

    out_shape = [jax.ShapeDtypeStruct((M, N), dt) for dt in out_dtypes]
    out_specs = [pl.BlockSpec((tm, tn), lambda i, j, k: (i, j)) for _ in out_dtypes]
    scratch = [pltpu.VMEM((tm, tn), F32)] if nk > 1 else []
    if side is None:
        res = pl.pallas_call(
            body, name=name, grid=grid, in_specs=in_specs, out_specs=out_specs, out_shape=out_shape,
            scratch_shapes=scratch, compiler_params=_params("parallel", "parallel", "arbitrary"),
        )(*args)
        return res[0] if nout == 1 else res
    res = pl.pallas_call(
        body, name=name, grid=grid, in_specs=in_specs + plumb["in_specs"], out_specs=out_specs + plumb["out_specs"],
        out_shape=out_shape + side.out_shapes, scratch_shapes=scratch + plumb["scratch"],
        input_output_aliases=plumb["aliases"], compiler_params=_params("arbitrary", "arbitrary", "arbitrary"),
    )(*args, *plumb["args"])
    return res[:nout], res[nout:]


def _col(arr, tm, width, cb):
    return pl.BlockSpec((tm, width), lambda i, cb=cb: (i, cb))


def _rms_fwd(x, g, *, out_dtype, name, mul=None, res=None, coeff=1.0, tm=512):
    T, D = x.shape
    tm = min(tm, T)
    args, in_specs = [x, g], [pl.BlockSpec((tm, D), lambda i: (i, 0)), pl.BlockSpec((1, D), lambda i: (0, 0))]
    if mul is not None:
        args.append(mul[0])
        in_specs.append(_col(mul[0], tm, D, mul[1]))
    if res is not None:
        args.append(res)
        in_specs.append(pl.BlockSpec((tm, D), lambda i: (i, 0)))

    def body(*refs):
        xv = refs[0][...].astype(F32)
        r = lax.rsqrt(jnp.mean(xv * xv, axis=-1, keepdims=True) + EPS)
        y = (xv * r) * refs[1][...]
        n = 2
        if mul is not None:
            y = y * _silu(refs[n][...])
            n += 1
        if res is not None:
            y = refs[n][...] + coeff * y
        refs[-1][...] = y.astype(out_dtype)

    return pl.pallas_call(
        body, name=name, grid=(T // tm,), in_specs=in_specs, out_specs=pl.BlockSpec((tm, D), lambda i: (i, 0)),
        out_shape=jax.ShapeDtypeStruct((T, D), out_dtype), compiler_params=_params("parallel"),
    )(*args)


def _fold8(v):
    tm, d = v.shape
    return v.reshape(tm // SUBLANE, SUBLANE, d).sum(axis=0)


def _rms_bwd(x, g, dy, *, name, coeff=1.0, add=None, mul=None, dx_dtype=F32, tm=512):
    T, D = x.shape
    tm = min(tm, T)
    row = pl.BlockSpec((tm, D), lambda i: (i, 0))
    args, in_specs = [x, g, dy], [row, pl.BlockSpec((1, D), lambda i: (0, 0)), row]
    if add is not None:
        args.append(add)
        in_specs.append(row)
    if mul is not None:
        args.append(mul[0])
        in_specs.append(_col(mul[0], tm, D, mul[1]))
    nin = len(args)

    def body(*refs):
        xv = refs[0][...].astype(F32)
        gv = refs[1][...]
        dyv = refs[2][...].astype(F32) * coeff
        r = lax.rsqrt(jnp.mean(xv * xv, axis=-1, keepdims=True) + EPS)
        nrm = xv * r
        n = 3
        addv = None
        if add is not None:
            addv = refs[n][...]
            n += 1
        if mul is not None:
            mv = refs[n][...]
            sm = _silu(mv)
            refs[nin + 2][...] = (dyv * nrm * gv * _dsilu(mv)).astype(refs[nin + 2].dtype)
            dyv = dyv * sm
        dn = dyv * gv
        dx = r * (dn - nrm * jnp.mean(dn * nrm, axis=-1, keepdims=True))
        if addv is not None:
            dx = dx + addv
        refs[nin][...] = dx.astype(dx_dtype)
        dg_ref = refs[nin + 1]

        @pl.when(pl.program_id(0) == 0)
        def _():
            dg_ref[...] = jnp.zeros_like(dg_ref)

        dg_ref[...] += _fold8(dyv * nrm)

    out_shape = [jax.ShapeDtypeStruct((T, D), dx_dtype), jax.ShapeDtypeStruct((SUBLANE, D), F32)]
    out_specs = [row, pl.BlockSpec((SUBLANE, D), lambda i: (0, 0))]
    if mul is not None:
        out_shape.append(jax.ShapeDtypeStruct((T, D), BF16))
        out_specs.append(row)
    return pl.pallas_call(
        body, name=name, grid=(T // tm,), in_specs=in_specs, out_specs=out_specs, out_shape=out_shape,
        compiler_params=_params("arbitrary"),
    )(*args)


def _ffn_in(h, wg, wu, *, name, tm=1024, tn=256, side=None):
    T, D = h.shape
    F = wg.shape[1]
    tm = min(tm, T)
    assert F % tn == 0
    grid = (T // tm, F // tn)
    ns_in, ns_out = (len(side.arrays), len(side.out_shapes)) if side is not None else (0, 0)

    def compute(h_ref, wg_ref, wu_ref, a_ref, g_ref, u_ref):
        hv = h_ref[...]
        gt = _dot(hv, wg_ref[...], NN)
        up = _dot(hv, wu_ref[...], NN)
        a_ref[...] = (_silu(gt) * up).astype(BF16)
        g_ref[...] = gt.astype(BF16)
        u_ref[...] = up.astype(BF16)

    def body(*refs):
        if side is None:
            compute(*refs)
        else:
            outs0 = 3 + ns_in
            first, last = _grid_ends(grid)
            side.run_at_ends(refs[3:outs0], refs[outs0 + 3: outs0 + 3 + ns_out], refs[-2:], first, last,
                             lambda: compute(*refs[:3], *refs[outs0: outs0 + 3]))

    o = pl.BlockSpec((tm, tn), lambda i, j: (i, j))
    w = pl.BlockSpec((D, tn), lambda i, j: (0, j))
    in_specs = [pl.BlockSpec((tm, D), lambda i, j: (i, 0)), w, w]
    out_shape = [jax.ShapeDtypeStruct((T, F), BF16)] * 3
    if side is None:
        return pl.pallas_call(body, name=name, grid=grid, in_specs=in_specs, out_specs=[o, o, o], out_shape=out_shape,
                              compiler_params=_params("parallel", "parallel"))(h, wg, wu)
    plumb = side.plumb(3, 3)
    res = pl.pallas_call(
        body, name=name, grid=grid, in_specs=in_specs + plumb["in_specs"], out_specs=[o, o, o] + plumb["out_specs"],
        out_shape=out_shape + side.out_shapes, scratch_shapes=plumb["scratch"], input_output_aliases=plumb["aliases"],
        compiler_params=_params("arbitrary", "arbitrary"),
    )(h, wg, wu, *plumb["args"])
    return res[:3], res[3:]


def _swiglu_bwd_epilogue(da, gt, up):
    gt = gt.astype(F32)
    up = up.astype(F32)
    return da * up * _dsilu(gt), da * _silu(gt)


GATE_CB = 7


def _gatemix_fwd(z, b_gate, ya, yb, *, name, tm=512):
    T, D = ya.shape
    tm = min(tm, T)
    row = pl.BlockSpec((tm, D), lambda i: (i, 0))

    def body(z0, z1, b0, b1, ya_ref, yb_ref, y_ref):
        g0 = _sigmoid(z0[...] + b0[...])
        g1 = _sigmoid(z1[...] + b1[...])
        y_ref[...] = (g0 * ya_ref[...] + g1 * yb_ref[...]).astype(y_ref.dtype)

    bs = lambda c: pl.BlockSpec((1, D), lambda i, c=c: (0, c))
    return pl.pallas_call(
        body, name=name, grid=(T // tm,),
        in_specs=[_col(z, tm, D, GATE_CB), _col(z, tm, D, GATE_CB + 1), bs(0), bs(1), row, row],
        out_specs=row, out_shape=jax.ShapeDtypeStruct((T, D), BF16), compiler_params=_params("parallel"),
    )(z, z, b_gate, b_gate, ya, yb)


def _gatemix_bwd(z, b_gate, ya, yb, dy, *, name, tm=512):
    T, D = ya.shape
    tm = min(tm, T)
    row = pl.BlockSpec((tm, D), lambda i: (i, 0))
    part = pl.BlockSpec((SUBLANE, D), lambda i: (0, 0))

    def body(z0, z1, b0, b1, ya_ref, yb_ref, dy_ref, dya, dyb, dz0, dz1, s0, s1):
        g0 = _sigmoid(z0[...] + b0[...])
        g1 = _sigmoid(z1[...] + b1[...])
        dyv = dy_ref[...]
        dya[...] = (dyv * g0).astype(BF16)
        dyb[...] = (dyv * g1).astype(BF16)
        d0 = dyv * ya_ref[...] * (g0 * (1.0 - g0))
        d1 = dyv * yb_ref[...] * (g1 * (1.0 - g1))
        dz0[...] = d0.astype(BF16)
        dz1[...] = d1.astype(BF16)

        @pl.when(pl.program_id(0) == 0)
        def _():
            s0[...] = jnp.zeros_like(s0)
            s1[...] = jnp.zeros_like(s1)

        s0[...] += _fold8(d0)
        s1[...] += _fold8(d1)

    bs = lambda c: pl.BlockSpec((1, D), lambda i, c=c: (0, c))
    act = jax.ShapeDtypeStruct((T, D), BF16)
    ps = jax.ShapeDtypeStruct((SUBLANE, D), F32)
    return pl.pallas_call(
        body, name=name, grid=(T // tm,),
        in_specs=[_col(z, tm, D, GATE_CB), _col(z, tm, D, GATE_CB + 1), bs(0), bs(1), row, row, row],
        out_specs=[row, row, row, row, part, part], out_shape=[act, act, act, act, ps, ps],
        compiler_params=_params("arbitrary"),
    )(z, z, b_gate, b_gate, ya, yb, dy)


def _loss_head(x, target, *, name, tm=512):
    T, D = x.shape
    tm = min(tm, T)
    row = pl.BlockSpec((tm, D), lambda i: (i, 0))

    def body(x_ref, t_ref, dx_ref, s_ref):
        e = x_ref[...] - t_ref[...]
        dx_ref[...] = e * (1.0 / D)

        @pl.when(pl.program_id(0) == 0)
        def _():
            s_ref[...] = jnp.zeros_like(s_ref)

        s_ref[...] += _fold8(e * e)

    return pl.pallas_call(
        body, name=name, grid=(T // tm,), in_specs=[row, row],
        out_specs=[row, pl.BlockSpec((SUBLANE, D), lambda i: (0, 0))],
        out_shape=[jax.ShapeDtypeStruct((T, D), F32), jax.ShapeDtypeStruct((SUBLANE, D), F32)],
        compiler_params=_params("arbitrary"),
    )(x, target)


def _tri(n, reverse):
    r = lax.broadcasted_iota(jnp.int32, (n, n), 0)
    c = lax.broadcasted_iota(jnp.int32, (n, n), 1)
    return jnp.where((c >= r) if reverse else (c <= r), 1.0, 0.0).astype(F32)


def _cumsum_t(xs, *, name, width, pre, reverse=False, rows=(), post=None, out_dtypes=(F32,), fold=None, tb=256):
    T = xs[0][0].shape[0]
    tb = min(tb, T)
    nb = T // tb
    tblk = (lambda i: nb - 1 - i) if reverse else (lambda i: i)
    args = [a for a, _ in xs] + [a for a, _ in rows]
    in_specs = [pl.BlockSpec((tb, width), lambda i, cb=cb: (tblk(i), cb)) for _, cb in xs]
    in_specs += [pl.BlockSpec((1, width), lambda i, cb=cb: (0, cb)) for _, cb in rows]
    nin, nout = len(args), len(out_dtypes)

    def body(*refs):
        vals = [r[...] for r in refs[:nin]]
        outs = refs[nin:nin + nout]
        carry = refs[-1]
        first = pl.program_id(0) == 0

        @pl.when(first)
        def _():
            carry[...] = jnp.zeros_like(carry)

        cum = jnp.dot(_tri(tb, reverse), pre(*vals), precision=HIGHEST, preferred_element_type=F32) + carry[...]
        carry[...] = cum[0:1, :] if reverse else cum[tb - 1:tb, :]
        res = (cum,) if post is None else post(cum, *vals)
        for o, r in zip(outs, res):
            o[...] = r.astype(o.dtype)
        if fold is not None:
            f_ref = refs[nin + nout]

            @pl.when(first)
            def _():
                f_ref[...] = jnp.zeros_like(f_ref)

            f_ref[...] += _fold8(fold(cum, *vals))

    tspec = pl.BlockSpec((tb, width), lambda i: (tblk(i), 0))
    out_shape = [jax.ShapeDtypeStruct((T, width), dt) for dt in out_dtypes]
    out_specs = [tspec] * nout
    if fold is not None:
        out_shape.append(jax.ShapeDtypeStruct((SUBLANE, width), F32))
        out_specs.append(pl.BlockSpec((SUBLANE, width), lambda i: (0, 0)))
    res = pl.pallas_call(
        body, name=name, grid=(nb,), in_specs=in_specs, out_specs=out_specs, out_shape=out_shape,
        scratch_shapes=[pltpu.VMEM((1, width), F32)], compiler_params=_params("arbitrary"),
    )(*args)
    return res[0] if len(res) == 1 else res


def _logsigmoid(v):
    return jnp.minimum(v, 0.0) - jnp.log(1.0 + jnp.exp(-jnp.abs(v)))


HG_TB = 256
HG_HB = 4
HG_W = HG_HB * DH
HG_GROUPS = HEADS // HG_HB
HG_Q_CB, HG_F_CB, HG_I_CB = 0, HG_GROUPS, 2 * HG_GROUPS
NEG = -1e30


def _scan16(x, rowid, reverse=False):
    for k in [1 << n for n in range(SUB.bit_length() - 1)]:
        if reverse:
            x = x + jnp.where(rowid < SUB - k, pltpu.roll(x, SUB - k, 0), 0.0)
        else:
            x = x + jnp.where(rowid >= k, pltpu.roll(x, k, 0), 0.0)
    return x


def _hg_block(q_ref, f_ref, i_ref, lb_ref, rows, cols, rowid):
    lb = lb_ref[:, cols]
    qr = q_ref[rows, cols]
    sg = _sigmoid(f_ref[rows, cols])
    f = lb + (1.0 - lb) * sg
    b = _scan16(jnp.log(f), rowid)
    return _silu(qr), 1.0 - f, i_ref[rows, cols], b, qr, sg, f, lb


def _hg_specs(tb, tmap):
    return [pl.BlockSpec((tb, HG_W), lambda g, t, *_: (tmap(t), HG_Q_CB + g)),
            pl.BlockSpec((tb, HG_W), lambda g, t, *_: (tmap(t), HG_F_CB + g)),
            pl.BlockSpec((tb, HG_W), lambda g, t, *_: (tmap(t), HG_I_CB + g)),
            pl.BlockSpec((1, HG_W), lambda g, t, *_: (0, g))]


def _hgrn2_fwd(z, lb_row, *, name):
    T = z.shape[0]
    tb = min(HG_TB, T)
    nb, nsub = T // tb, tb // SUB

    def body(q_ref, f_ref, i_ref, lb_ref, o_ref, st_ref, state):
        @pl.when(pl.program_id(1) == 0)
        def _():
            state[...] = jnp.zeros_like(state)

        rowid = lax.broadcasted_iota(jnp.int32, (SUB, DH), 0)

        def step(c, carry):
            rows = pl.ds(pl.multiple_of(c * SUB, SUB), SUB)
            for hh in range(HG_HB):
                cols = slice(hh * DH, (hh + 1) * DH)
                q, k, iv, b = _hg_block(q_ref, f_ref, i_ref, lb_ref, rows, cols, rowid)[:4]
                bl = b[SUB - 1:SUB, :]
                sv = state[hh]
                st_ref[c, hh] = sv
                o = _dot(q * jnp.exp(b), sv, NT)
                for s in range(SUB):
                    e = jnp.exp(jnp.where(rowid >= s, b - b[s:s + 1, :], NEG))
                    a = jnp.sum(q * e * k[s:s + 1, :], axis=-1, keepdims=True)
                    o = o + a * iv[s:s + 1, :]
                o_ref[rows, cols] = o
                state[hh] = sv * jnp.exp(bl) + _dot(iv, k * jnp.exp(bl - b), TN)
            return carry

        lax.fori_loop(0, nsub, step, 0)

    return pl.pallas_call(
        body, name=name, grid=(HG_GROUPS, nb), in_specs=_hg_specs(tb, lambda t: t),
        out_specs=[pl.BlockSpec((tb, HG_W), lambda g, t: (t, g)),
                   pl.BlockSpec((nsub, HG_HB, DH, DH), lambda g, t: (t, g, 0, 0))],
        out_shape=[jax.ShapeDtypeStruct((T, D_MODEL), F32), jax.ShapeDtypeStruct((T // SUB, HEADS, DH, DH), F32)],
        scratch_shapes=[pltpu.VMEM((HG_HB, DH, DH), F32)], compiler_params=_params("parallel", "arbitrary"),
    )(z, z, z, lb_row)


def _hgrn2_bwd(z, lb_row, states, do, *, name):
    T = z.shape[0]
    tb = min(HG_TB, T)
    nb, nsub = T // tb, tb // SUB
    rev = lambda t: nb - 1 - t

    def body(q_ref, f_ref, i_ref, lb_ref, st_ref, do_ref, dq_ref, dfl_ref, di_ref, dlb_ref, dstate, later):
        @pl.when(pl.program_id(1) == 0)
        def _():
            dstate[...] = jnp.zeros_like(dstate)
            later[...] = jnp.zeros_like(later)
            dlb_ref[...] = jnp.zeros_like(dlb_ref)

        rowid = lax.broadcasted_iota(jnp.int32, (SUB, DH), 0)

        def step(cc, carry):
            c = nsub - 1 - cc
            rows = pl.ds(pl.multiple_of(c * SUB, SUB), SUB)
            for hh in range(HG_HB):
                cols = slice(hh * DH, (hh + 1) * DH)
                q, k, iv, b, qr, sg, f, lb = _hg_block(q_ref, f_ref, i_ref, lb_ref, rows, cols, rowid)
                bl = b[SUB - 1:SUB, :]
                eb, ebl = jnp.exp(b), jnp.exp(bl - b)
                sv, dsv = st_ref[c, hh], dstate[hh]
                dov = do_ref[rows, cols]
                dq = _dot(dov, sv, NN) * eb
                dk = _dot(iv, dsv, NN) * ebl
                di = _dot(k * ebl, dsv, NT)
                for s in range(SUB):
                    e = jnp.exp(jnp.where(rowid >= s, b - b[s:s + 1, :], NEG))
                    ks, isv = k[s:s + 1, :], iv[s:s + 1, :]
                    qe = q * e
                    a = jnp.sum(qe * ks, axis=-1, keepdims=True)
                    p = jnp.sum(dov * isv, axis=-1, keepdims=True)
                    dq = dq + p * (e * ks)
                    dks = jnp.sum(p * qe, axis=0, keepdims=True)
                    dis = jnp.sum(a * dov, axis=0, keepdims=True)
                    dk = dk + jnp.where(rowid == s, dks, 0.0)
                    di = di + jnp.where(rowid == s, dis, 0.0)
                dlogf = _scan16(q * dq - k * dk, rowid, reverse=True) + later[hh]
                df = dlogf / f - dk
                dlb_ref[:, cols] += jnp.sum(df * (1.0 - sg), axis=0, keepdims=True)
                dfl_ref[rows, cols] = (df * (1.0 - lb) * (sg * (1.0 - sg))).astype(BF16)
                dq_ref[rows, cols] = (dq * _dsilu(qr)).astype(BF16)
                di_ref[rows, cols] = di.astype(BF16)
                dnew = dsv * jnp.exp(bl) + _dot(dov, q * eb, TN)
                dstate[hh] = dnew
                later[hh] = jnp.sum(dnew * sv, axis=0, keepdims=True)
            return carry

        lax.fori_loop(0, nsub, step, 0)

    tile = pl.BlockSpec((tb, HG_W), lambda g, t: (rev(t), g))
    act = jax.ShapeDtypeStruct((T, D_MODEL), BF16)
    return pl.pallas_call(
        body, name=name, grid=(HG_GROUPS, nb),
        in_specs=_hg_specs(tb, rev) + [pl.BlockSpec((nsub, HG_HB, DH, DH), lambda g, t: (rev(t), g, 0, 0)), tile],
        out_specs=[tile, tile, tile, pl.BlockSpec((1, HG_W), lambda g, t: (0, g))],
        out_shape=[act, act, act, jax.ShapeDtypeStruct((1, D_MODEL), F32)],
        scratch_shapes=[pltpu.VMEM((HG_HB, DH, DH), F32), pltpu.VMEM((HG_HB, 1, DH), F32)],
        compiler_params=_params("parallel", "arbitrary"),
    )(z, z, z, lb_row, states, do)


FOX_Q_CB, FOX_K_CB, FOX_V_CB = 4 * HEADS, 5 * HEADS, 6 * HEADS
FOX_SCALE = 1.0 / math.sqrt(DH)


def _fox_tile(T):
    return 512 if T >= 2048 else 128


def _fox_pairs(nq, by_query):
    if by_query:
        pairs = [(i, j) for i in range(nq) for j in range(i + 1)]
    else:
        pairs = [(i, j) for j in range(nq) for i in range(j, nq)]
    return (jnp.asarray([p[0] for p in pairs], jnp.int32), jnp.asarray([p[1] for p in pairs], jnp.int32))


LOG2E = 1.4426950408889634
FOX_RC = 64
FOX_HB = 2


def _fox_q2(q):
    return (q * (FOX_SCALE * LOG2E)).astype(BF16)


FOX_ZERO = -200.0


def _fox_norms(z, *, name):
    T = z.shape[0]
    tq = _fox_tile(T)
    nq = T // tq

    def body(q_ref, k_ref, nq_ref, nk_ref):
        head_of_col = lax.broadcasted_iota(jnp.int32, (D_MODEL, LANE), 0) // DH
        pick = jnp.where(head_of_col == lax.broadcasted_iota(jnp.int32, (D_MODEL, LANE), 1), 1.0, 0.0).astype(BF16)

        def tile_max(v):
            v = v.astype(F32)
            sq = _dot(v * v, pick, NN)
            return jnp.broadcast_to(jnp.max(jnp.sqrt(sq), axis=0, keepdims=True), (SUBLANE, LANE))

        nq_ref[...] = tile_max(_fox_q2(q_ref[...]))
        nk_ref[...] = tile_max(k_ref[...].astype(BF16))

    out = jax.ShapeDtypeStruct((nq * SUBLANE, LANE), F32)
    spec = pl.BlockSpec((SUBLANE, LANE), lambda i: (i, 0))
    a, b = pl.pallas_call(
        body, name=name, grid=(nq,),
        in_specs=[pl.BlockSpec((tq, D_MODEL), lambda i: (i, FOX_Q_CB // HEADS)),
                  pl.BlockSpec((tq, D_MODEL), lambda i: (i, FOX_K_CB // HEADS))],
        out_specs=[spec, spec], out_shape=[out, out], compiler_params=_params("parallel"),
    )(z, z)
    return a[::SUBLANE, :HEADS], b[::SUBLANE, :HEADS]


def _fox_schedule(norm_q, norm_k, ct, tq):
    nq = ct.shape[1] // tq
    first, last = ct[:, ::tq], ct[:, tq - 1::tq]
    nqh, nkh = norm_q.T * 1.05, norm_k.T * 1.05
    bound = nqh[:, :, None] * (nkh[:, None, :] + nkh[:, :, None]) + first[:, :, None] - last[:, None, :]
    tri = jnp.arange(nq)[:, None] > jnp.arange(nq)[None, :]
    drop = (bound < FOX_ZERO) & tri[None]
    lo = jnp.argmin(drop, axis=2).astype(jnp.int32)
    dropped = jnp.arange(nq)[None, None, :] < lo[:, :, None]
    lo_g = jnp.min(lo.reshape(HEADS // FOX_HB, FOX_HB, nq), axis=1)
    dropped_g = jnp.arange(nq)[None, None, :] < lo_g[:, :, None]
    qf, kf = _fox_pairs(nq, by_query=True)
    qb, kb = _fox_pairs(nq, by_query=False)
    fetch_k = jnp.maximum(kf[None, :], lo_g[:, qf])
    kept_q = jnp.where(dropped_g | ~(tri | jnp.eye(nq, dtype=bool))[None], -1, jnp.arange(nq)[None, :, None])
    last_kept = lax.cummax(kept_q, axis=1)
    fetch_q = last_kept[:, qb, kb]
    i32 = lambda a: a.astype(jnp.int32)
    return i32(fetch_k), i32(dropped[:, qf, kf]), i32(fetch_q), i32(dropped[:, qb, kb])


def _fox_fwd(z, c_col, c_row, fetch_k, skip, *, name):
    T = z.shape[0]
    tq = _fox_tile(T)
    nq = T // tq
    rc = min(FOX_RC, tq)

    qi, kj = _fox_pairs(nq, by_query=True)

    def body(qi_ref, kj_ref, fk_ref, skip_ref, q_ref, k_ref, v_ref, cc_ref, cr_ref, o_ref, lse_ref, m_scr, l_scr, acc,
             a_scr, s_scr, p_scr):
        p_id = pl.program_id(1)
        i, j = qi_ref[p_id], kj_ref[p_id]

        @pl.when(j == 0)
        def _():
            m_scr[...] = jnp.full_like(m_scr, NEG)
            l_scr[...] = jnp.zeros_like(l_scr)
            acc[...] = jnp.zeros_like(acc)

        def update(hh, masked):
            cols = slice(hh * DH, (hh + 1) * DH)
            bias = cc_ref[hh, 0:1, :] - cr_ref[hh]
            s_scr[hh] = _dot(_fox_q2(q_ref[:, cols]), k_ref[:, cols], NT)
            for r in range(tq // rc):
                rows = slice(r * rc, (r + 1) * rc)
                t = s_scr[hh, rows, :] + bias
                if masked:
                    t = jnp.where(lax.broadcasted_iota(jnp.int32, (rc, tq), 1)
                                  <= r * rc + lax.broadcasted_iota(jnp.int32, (rc, tq), 0), t, NEG)
                m_old = m_scr[hh, rows, :]
                m_new = jnp.maximum(m_old, jnp.max(t, axis=-1, keepdims=True))
                alpha = jnp.exp2(m_old - m_new)
                p = jnp.exp2(t - jnp.tile(m_new, (1, tq // LANE)))
                l_scr[hh, rows, :] = alpha * l_scr[hh, rows, :] + jnp.sum(p, axis=-1, keepdims=True)
                a_scr[hh, rows, :] = alpha
                p_scr[hh, rows, :] = p.astype(BF16)
                m_scr[hh, rows, :] = m_new
            acc[hh] = a_scr[hh] * acc[hh] + _dot(p_scr[hh], v_ref[:, cols], NN)

        for hh in range(FOX_HB):
            live = skip_ref[pl.program_id(0) * FOX_HB + hh, p_id] == 0

            @pl.when((j < i) & live)
            def _():
                update(hh, False)

            @pl.when(j == i)
            def _():
                update(hh, True)
                o_ref[:, hh * DH:(hh + 1) * DH] = acc[hh] / l_scr[hh]
                lse_ref[hh] = (m_scr[hh, :, 0:1] + jnp.log2(l_scr[hh, :, 0:1])) + (cc_ref[hh] - cc_ref[hh, 0:1, :])

    wide = FOX_HB * DH
    qtile = lambda cb: pl.BlockSpec((tq, wide), lambda g, p, qi, kj, fk, sk, cb=cb: (qi[p], cb // FOX_HB + g))
    ktile = lambda cb: pl.BlockSpec((tq, wide), lambda g, p, qi, kj, fk, sk, cb=cb: (fk[g, p], cb // FOX_HB + g))
    qcol = pl.BlockSpec((FOX_HB, tq, 1), lambda g, p, qi, kj, fk, sk: (g, qi[p], 0))
    stat = pltpu.VMEM((FOX_HB, tq, LANE), F32)
    return pl.pallas_call(
        body, name=name,
        grid_spec=pltpu.PrefetchScalarGridSpec(
            num_scalar_prefetch=4, grid=(HEADS // FOX_HB, qi.shape[0]),
            in_specs=[qtile(FOX_Q_CB), ktile(FOX_K_CB), ktile(FOX_V_CB), qcol,
                      pl.BlockSpec((FOX_HB, 1, tq), lambda g, p, qi, kj, fk, sk: (g, 0, fk[g, p]))],
            out_specs=[qtile(0), qcol],
            scratch_shapes=[stat, stat, pltpu.VMEM((FOX_HB, tq, DH), F32), stat,
                            pltpu.VMEM((FOX_HB, tq, tq), F32), pltpu.VMEM((FOX_HB, tq, tq), BF16)]),
        out_shape=[jax.ShapeDtypeStruct((T, D_MODEL), F32), jax.ShapeDtypeStruct((HEADS, T, 1), F32)],
        compiler_params=_params("parallel", "arbitrary"),
    )(qi, kj, fetch_k, skip, z, z, z, c_col, c_row)


def _fox_bwd(z, c_col, c_row, o, lse, do, fetch_q, skip, *, name):
    T = z.shape[0]
    tq = _fox_tile(T)
    nq = T // tq
    rc = min(FOX_RC, tq)

    qi, kj = _fox_pairs(nq, by_query=False)

    def body(qi_ref, kj_ref, fq_ref, skip_ref, q_ref, k_ref, v_ref, cc_ref, cr_ref, o_ref, lse_ref, do_ref, dq_ref,
             dk_ref, dv_ref, dc_ref, dcq_ref, dk_acc, dv_acc, dc_acc, s_scr, dp_scr, p_scr, ds_scr, dcq_scr):
        p_id = pl.program_id(1)
        i, j = qi_ref[p_id], kj_ref[p_id]

        @pl.when(p_id == 0)
        def _():
            dq_ref[...] = jnp.zeros_like(dq_ref)
            dcq_scr[...] = jnp.zeros_like(dcq_scr)

        def update(hh, masked):
            cols = slice(hh * DH, (hh + 1) * DH)
            q2, k, dov = _fox_q2(q_ref[:, cols]), k_ref[:, cols], do_ref[:, cols]
            s_scr[hh] = _dot(q2, k, NT)
            dp_scr[hh] = _dot(dov, v_ref[:, cols], NT)
            crow = cr_ref[hh]
            csum = jnp.zeros((SUBLANE, tq), F32)
            wide = lambda col: jnp.tile(jnp.broadcast_to(col, (rc, LANE)), (1, tq // LANE))
            for r in range(tq // rc):
                rows = slice(r * rc, (r + 1) * rc)
                t = (s_scr[hh, rows, :] + wide(cc_ref[hh, rows, :] - lse_ref[hh, rows, :])) - crow
                if masked:
                    t = jnp.where(lax.broadcasted_iota(jnp.int32, (rc, tq), 1)
                                  <= r * rc + lax.broadcasted_iota(jnp.int32, (rc, tq), 0), t, NEG)
                p = jnp.exp2(t)
                delta = jnp.sum(do_ref[rows, cols] * o_ref[rows, cols], axis=-1, keepdims=True)
                ds = p * (dp_scr[hh, rows, :] - wide(delta))
                p_scr[hh, rows, :] = p.astype(BF16)
                ds_scr[hh, rows, :] = ds.astype(BF16)
                grows = pl.ds(pl.multiple_of(i * tq + r * rc, rc), rc)
                dcq_scr[hh, grows, :] += jnp.broadcast_to(jnp.sum(ds, axis=-1, keepdims=True), (rc, LANE))
                csum = csum + _fold8(ds)
            dsb = ds_scr[hh]
            dv_new = _dot(p_scr[hh], dov, TN)
            dk_new = _dot(dsb, q2, TN) * (1.0 / LOG2E)
            dc_new = -jnp.sum(csum, axis=0, keepdims=True)
            rows = pl.ds(pl.multiple_of(i * tq, tq), tq)
            dq_ref[rows, cols] += _dot(dsb, k, NN) * FOX_SCALE
            return dk_new, dv_new, dc_new

        for hh in range(FOX_HB):
            live = skip_ref[pl.program_id(0) * FOX_HB + hh, p_id] == 0

            @pl.when(i == j)
            def _():
                dk_new, dv_new, dc_new = update(hh, True)
                dk_acc[hh] = dk_new
                dv_acc[hh] = dv_new
                dc_acc[hh] = dc_new

            @pl.when((i > j) & live)
            def _():
                dk_new, dv_new, dc_new = update(hh, False)
                dk_acc[hh] += dk_new
                dv_acc[hh] += dv_new
                dc_acc[hh] += dc_new

            @pl.when(i == nq - 1)
            def _():
                dk_ref[:, hh * DH:(hh + 1) * DH] = dk_acc[hh].astype(BF16)
                dv_ref[:, hh * DH:(hh + 1) * DH] = dv_acc[hh].astype(BF16)
                dc_ref[hh] = dc_acc[hh]

            @pl.when(p_id == qi.shape[0] - 1)
            def _():
                for r in range(nq):
                    rows = slice(r * tq, (r + 1) * tq)
                    dcq_ref[hh, :, rows] = jnp.transpose(dcq_scr[hh, rows, :])[0:1, :]

    wide_cols = FOX_HB * DH
    n_groups = HEADS // FOX_HB
    qtile = lambda cb: pl.BlockSpec((tq, wide_cols), lambda g, p, qi, kj, fq, sk, cb=cb: (fq[g, p], cb // FOX_HB + g))
    ktile = lambda cb: pl.BlockSpec((tq, wide_cols), lambda g, p, qi, kj, fq, sk, cb=cb: (kj[p], cb // FOX_HB + g))
    qcol = pl.BlockSpec((FOX_HB, tq, 1), lambda g, p, qi, kj, fq, sk: (g, fq[g, p], 0))
    krow = pl.BlockSpec((FOX_HB, 1, tq), lambda g, p, qi, kj, fq, sk: (g, 0, kj[p]))
    tile_f32 = pltpu.VMEM((FOX_HB, tq, tq), F32)
    tile_bf16 = pltpu.VMEM((FOX_HB, tq, tq), BF16)
    return pl.pallas_call(
        body, name=name,
        grid_spec=pltpu.PrefetchScalarGridSpec(
            num_scalar_prefetch=4, grid=(n_groups, qi.shape[0]),
            in_specs=[qtile(FOX_Q_CB), ktile(FOX_K_CB), ktile(FOX_V_CB), qcol, krow, qtile(0), qcol, qtile(0)],
            out_specs=[pl.BlockSpec((T, wide_cols), lambda g, p, qi, kj, fq, sk: (0, g)), ktile(0), ktile(0), krow,
                       pl.BlockSpec((FOX_HB, 1, T), lambda g, p, qi, kj, fq, sk: (g, 0, 0))],
            scratch_shapes=[pltpu.VMEM((FOX_HB, tq, DH), F32), pltpu.VMEM((FOX_HB, tq, DH), F32),
                            pltpu.VMEM((FOX_HB, 1, tq), F32), tile_f32, tile_f32, tile_bf16, tile_bf16,
                            pltpu.VMEM((FOX_HB, T, LANE), F32)]),
        out_shape=[jax.ShapeDtypeStruct((T, D_MODEL), F32), jax.ShapeDtypeStruct((T, D_MODEL), BF16),
                   jax.ShapeDtypeStruct((T, D_MODEL), BF16), jax.ShapeDtypeStruct((HEADS, 1, T), F32),
                   jax.ShapeDtypeStruct((HEADS, 1, T), F32)],
        compiler_params=_params("parallel", "arbitrary"),
    )(qi, kj, fetch_q, skip, z, z, z, c_col, c_row, o, lse, do)


MEM_SCALE = 1.0 / math.sqrt(MEM_DH)


def _mem_probs(qh, kh):
    s = _dot(qh, kh, NT) * MEM_SCALE
    p = jnp.exp(s - jnp.max(s, axis=-1, keepdims=True))
    return p / jnp.sum(p, axis=-1, keepdims=True)


def _mem_fwd(q, kv, *, name, tq=512):
    T = q.shape[0]
    tq = min(tq, T)

    def body(q_ref, kv_ref, o_ref):
        for h in range(MEM_HEADS):
            cols = slice(h * MEM_DH, (h + 1) * MEM_DH)
            vcols = slice(D_MODEL + h * MEM_DH, D_MODEL + (h + 1) * MEM_DH)
            p = _mem_probs(q_ref[:, cols], kv_ref[:, cols])
            o_ref[:, cols] = _dot(p, kv_ref[:, vcols], NN).astype(o_ref.dtype)

    return pl.pallas_call(
        body, name=name, grid=(T // tq,),
        in_specs=[pl.BlockSpec((tq, D_MODEL), lambda i: (i, 0)), pl.BlockSpec((MEM_LEN, 2 * D_MODEL), lambda i: (0, 0))],
        out_specs=pl.BlockSpec((tq, D_MODEL), lambda i: (i, 0)), out_shape=jax.ShapeDtypeStruct((T, D_MODEL), BF16),
        compiler_params=_params("parallel"),
    )(q, kv)


def _mem_bwd(q, kv, do, *, name, tq=512):
    T = q.shape[0]
    tq = min(tq, T)

    def body(q_ref, kv_ref, do_ref, dq_ref, dkv_ref):
        @pl.when(pl.program_id(0) == 0)
        def _():
            dkv_ref[...] = jnp.zeros_like(dkv_ref)

        for h in range(MEM_HEADS):
            cols = slice(h * MEM_DH, (h + 1) * MEM_DH)
            vcols = slice(D_MODEL + h * MEM_DH, D_MODEL + (h + 1) * MEM_DH)
            qh, kh, doh = q_ref[:, cols], kv_ref[:, cols], do_ref[:, cols]
            p = _mem_probs(qh, kh)
            dp = _dot(doh, kv_ref[:, vcols], NT)
            ds = p * (dp - jnp.sum(p * dp, axis=-1, keepdims=True))
            dq_ref[:, cols] = (_dot(ds, kh, NN) * MEM_SCALE).astype(dq_ref.dtype)
            dkv_ref[:, cols] += _dot(ds, qh, TN) * MEM_SCALE
            dkv_ref[:, vcols] += _dot(p, doh, TN)

    row = pl.BlockSpec((tq, D_MODEL), lambda i: (i, 0))
    full = pl.BlockSpec((MEM_LEN, 2 * D_MODEL), lambda i: (0, 0))
    return pl.pallas_call(
        body, name=name, grid=(T // tq,), in_specs=[row, full, row], out_specs=[row, full],
        out_shape=[jax.ShapeDtypeStruct((T, D_MODEL), BF16), jax.ShapeDtypeStruct((MEM_LEN, 2 * D_MODEL), F32)],
        compiler_params=_params("arbitrary"),
    )(q, kv, do)


def _adamw(w, g, m, v, *, name, tm=256):
    R, C = w.shape
    tm = min(tm, R)
    assert R % tm == 0
    nsum = g.shape[0] if g.ndim == 3 else 0

    def body(w_ref, g_ref, m_ref, v_ref, go_ref, d_ref, mo_ref, vo_ref):
        if nsum:
            gv = g_ref[0]
            for n in range(1, nsum):
                gv = gv + g_ref[n]
        else:
            gv = g_ref[...]
        mv = ADAM_B1 * m_ref[...] + (1.0 - ADAM_B1) * gv
        vv = ADAM_B2 * v_ref[...] + (1.0 - ADAM_B2) * jnp.square(gv)
        m_hat = mv / (1.0 - ADAM_B1 ** ADAM_STEP)
        v_hat = vv / (1.0 - ADAM_B2 ** ADAM_STEP)
        d_ref[...] = -ADAM_LR * (m_hat / (jnp.sqrt(v_hat) + ADAM_EPS) + ADAM_WD * w_ref[...])
        go_ref[...] = gv
        mo_ref[...] = mv
        vo_ref[...] = vv

    row = pl.BlockSpec((tm, C), lambda i: (i, 0))
    gspec = pl.BlockSpec((nsum, tm, C), lambda i: (0, i, 0)) if nsum else row
    return pl.pallas_call(
        body, name=name, grid=(R // tm,), in_specs=[row, gspec, row, row], out_specs=[row] * 4,
        out_shape=[jax.ShapeDtypeStruct((R, C), F32)] * 4, compiler_params=_params("parallel"),
    )(w, g, m, v)


def _act_mm(a, w, name, out_dtype=F32, side=None):
    res = _mm([(a, w)], "nn", tm=1024, tn=512, tk=w.shape[0], out_dtypes=[out_dtype], name=name, side=side)
    return res if side is None else (res[0][0], res[1])


def _act_mm_t(a, w, name, out_dtype=F32, side=None):
    res = _mm([(a, w)], "nt", tm=1024, tn=512, tk=1024, out_dtypes=[out_dtype], name=name, side=side)
    return res if side is None else (res[0][0], res[1])


def _wgrad(a, dy, name, tm=1024):
    tn = D_MODEL if dy.shape[1] % D_MODEL == 0 else D_FF // 2
    return _mm([(a, dy)], "tn", tm=tm, tn=tn, tk=1024, out_dtypes=[F32], name=name)


def _colsum8(p):
    return jnp.sum(p, axis=0, keepdims=True)


def _ffn_fwd(x, pre_g, post_g, wg, wu, wd, tag, gather_beside=None):
    h = _rms_fwd(x, pre_g, out_dtype=BF16, name=tag + "_pre")
    down = functools.partial(_mm, mode="nn", tm=1024, tn=512, tk=D_FF, out_dtypes=[F32], name=tag + "_down")
    gathered = None
    if gather_beside is None:
        act, gate, up = _ffn_in(h, wg, wu, name=tag + "_in")
        d = down([(act, wd)])
    else:
        (act, gate, up), landed = _ffn_in(h, wg, wu, name=tag + "_in", side=_x_gather(gather_beside))
        (d,), gathered = down([(act, wd)], side=_x_forward(landed))
    xo = _rms_fwd(d, post_g, out_dtype=F32, name=tag + "_post", res=x, coeff=0.5)
    return xo, (h, act, gate, up, d), gathered


def _ffn_bwd(x, dxo, saved, pre_g, post_g, wg, wu, wd, tag, reduce_beside=None):
    h, act, gate, up, d = saved
    dd, dg_post = _rms_bwd(d, post_g, dxo, name=tag + "_post_b", coeff=0.5, dx_dtype=BF16)
    act_b = functools.partial(_mm, [(dd, wd)], "nt", tm=1024, tn=256, tk=D_MODEL, out_dtypes=[BF16, BF16],
                              name=tag + "_act_b", epilogue=_swiglu_bwd_epilogue, tiles=(gate, up))
    in_b = lambda dgate, dup, **kw: _mm([(dgate, wg), (dup, wu)], "nt", tm=512, tn=512, tk=D_FF, out_dtypes=[F32],
                                        name=tag + "_in_b", **kw)
    wgrads = lambda dgate, dup: (_wgrad(act, dd, tag + "_dwd", tm=D_FF // 2), _wgrad(h, dgate, tag + "_dwg"),
                                 _wgrad(h, dup, tag + "_dwu"))
    reduced = None
    if reduce_beside is None:
        dgate, dup = act_b()
        dh = in_b(dgate, dup)
        dwd, dwg, dwu = wgrads(dgate, dup)
    else:
        grads, place, core = reduce_beside
        (dgate, dup), swapped = act_b(side=_x_swap(grads))
        pbf, own = _pair_sums(grads, swapped, place, "mid")
        dwd, dwg, dwu = wgrads(dgate, dup)
        mine = _grad_slabs(dict(f1g=dwg, f1u=dwu, f1d=dwd), GROUP_FFN1)
        scatter = _x_scatter(pbf)
        (dh,), landed = in_b(dgate, dup, side=_both(scatter, _x_swap(mine, base=scatter.nsem)))
        reduced_other = _finish_reduce(own, landed[:2], core, "mid")
        pbf, own = _pair_sums(mine, landed[2:], place, "late")
        reduced = (reduced_other, _finish_reduce(own, _run(_x_scatter(pbf), "scatter_partials_late"), core, "late"))
    dx, dg_pre = _rms_bwd(x, pre_g, dh, name=tag + "_pre_b", add=dxo)
    return dx, dict(pre_g=_colsum8(dg_pre), post_g=_colsum8(dg_post), wg=dwg, wu=dwu, wd=dwd), reduced


def _local_step(x, mem, target, P, own, me, place, core):
    T = x.shape[0]
    G = {}
    logits = P["hg_lb_logits"]
    lb = _sigmoid(logits[0] - logits[1])
    lb_row = lb.reshape(1, D_MODEL)
    fbias_row = jnp.pad(P["fox_f_bias"], ((0, 0), (0, LANE - HEADS)))
    arrived = lambda group, others: _local_names(_assemble(group, own[group], others, me))

    W = arrived(GROUP_FFN1, _run(_x_forward(_run(_x_gather(own[GROUP_FFN1]), "gather_ffn1")), "forward_ffn1"))
    x1, ffn1_saved, others = _ffn_fwd(x, P["ffn1_pre_g"], P["ffn1_post_g"], W["f1g"], W["f1u"], W["f1d"], "ffn1",
                                      gather_beside=own[GROUP_MID])
    W.update(arrived(GROUP_MID, others))
    h2 = _rms_fwd(x1, P["mix_pre_g"], out_dtype=BF16, name="mix_pre")
    z, landed = _act_mm(h2, W["w_main"], "mix_in", side=_x_gather(own[GROUP_FFN2]))
    zfb = _mm([(h2, W["w_fb"])], "nn", tm=1024, tn=LANE, tk=D_MODEL, out_dtypes=[F32], name="mix_in_fb")
    oa_pre, states = _hgrn2_fwd(z, lb_row, name="hgrn2_f")
    o_a = _rms_fwd(oa_pre, P["hg_norm_g"], out_dtype=BF16, name="hgrn2_post", mul=(z, 3))
    y_a, others = _act_mm(o_a, W["wa"], "branch_a", side=_x_forward(landed))
    W.update(arrived(GROUP_FFN2, others))
    c = _cumsum_t([(zfb, 0)], name="fox_c", width=LANE, rows=[(fbias_row, 0)], pre=lambda v, r: _logsigmoid(v + r),
                  post=lambda cum, v, r: (cum * LOG2E,))
    ct = c[:, :HEADS].T
    c_col, c_row = ct[:, :, None], ct[:, None, :]
    fetch_k, skip_f, fetch_q, skip_b = _fox_schedule(*_fox_norms(z, name="fox_norms"), ct, _fox_tile(T))
    o_b, lse = _fox_fwd(z, c_col, c_row, fetch_k, skip_f, name="fox_f")
    y_b = _act_mm(o_b, W["wb"], "branch_b")
    y = _gatemix_fwd(z, P["b_gate"], y_a, y_b, name="gatemix")
    m = _act_mm(y, W["wo"], "mix_out")
    x2 = _rms_fwd(m, P["mix_post_g"], out_dtype=F32, name="mix_post", res=x1)
    h3 = _rms_fwd(x2, P["mem_pre_g"], out_dtype=BF16, name="mem_pre")
    mem_n = _rms_fwd(mem, P["mem_kv_g"], out_dtype=BF16, name="mem_kvn")
    qm = _act_mm(h3, W["wmq"], "mem_q")
    kv = _act_mm(mem_n, W["wmkv"], "mem_kv")
    om = _mem_fwd(qm, kv, name="mem_attn")
    mo = _act_mm(om, W["wmo"], "mem_o")
    x3 = _rms_fwd(mo, P["mem_post_g"], out_dtype=F32, name="mem_post", res=x2)
    x4, ffn2_saved, _ = _ffn_fwd(x3, P["ffn2_pre_g"], P["ffn2_post_g"], W["f2g"], W["f2u"], W["f2d"], "ffn2")
    dx4, sq = _loss_head(x4, target, name="loss_head")

    dx3, g, _ = _ffn_bwd(x3, dx4, ffn2_saved, P["ffn2_pre_g"], P["ffn2_post_g"], W["f2g"], W["f2u"], W["f2d"], "ffn2")
    G.update(ffn2_pre_g=g["pre_g"], ffn2_post_g=g["post_g"], f2g=g["wg"], f2u=g["wu"], f2d=g["wd"])

    dmo, dgp = _rms_bwd(mo, P["mem_post_g"], dx3, name="mem_post_b", dx_dtype=BF16)
    G["mem_post_g"] = _colsum8(dgp)
    g_ffn2 = _grad_slabs(G, GROUP_FFN2)
    dom, swapped = _act_mm_t(dmo, W["wmo"], "mem_o_b", BF16, side=_x_swap(g_ffn2))
    pbf_ffn2, own_ffn2 = _pair_sums(g_ffn2, swapped, place, "ffn2")
    G["wmo"] = _wgrad(om, dmo, "mem_o_w")
    dqm, dkv = _mem_bwd(qm, kv, dom, name="mem_attn_b")
    dh3 = _act_mm_t(dqm, W["wmq"], "mem_q_b")
    G["wmq"] = _wgrad(h3, dqm, "mem_q_w")
    G["wmkv"] = _mm([(mem_n, dkv)], "tn", tm=1024, tn=512, tk=MEM_LEN, out_dtypes=[F32], name="mem_kv_w")
    dmem_n = _mm([(dkv, W["wmkv"])], "nt", tm=MEM_LEN, tn=512, tk=2 * D_MODEL, out_dtypes=[F32], name="mem_kv_b")
    _, dgp = _rms_bwd(mem, P["mem_kv_g"], dmem_n, name="mem_kvn_b")
    G["mem_kv_g"] = _colsum8(dgp)
    dx2, dgp = _rms_bwd(x2, P["mem_pre_g"], dh3, name="mem_pre_b", add=dx3)
    G["mem_pre_g"] = _colsum8(dgp)

    dm, dgp = _rms_bwd(m, P["mix_post_g"], dx2, name="mix_post_b", dx_dtype=BF16)
    G["mix_post_g"] = _colsum8(dgp)
    dy = _act_mm_t(dm, W["wo"], "mix_out_b")
    G["wo"] = _wgrad(y, dm, "mix_out_w")
    dya, dyb, dz0, dz1, s0, s1 = _gatemix_bwd(z, P["b_gate"], y_a, y_b, dy, name="gatemix_b")
    G["b_gate"] = jnp.concatenate([_colsum8(s0), _colsum8(s1)], axis=1)
    do_a = _act_mm_t(dya, W["wa"], "branch_a_b")
    G["wa"] = _wgrad(o_a, dya, "branch_a_w")
    do_b = _act_mm_t(dyb, W["wb"], "branch_b_b")
    G["wb"] = _wgrad(o_b, dyb, "branch_b_w")
    doa_pre, dgp, dga = _rms_bwd(oa_pre, P["hg_norm_g"], do_a, name="hgrn2_post_b", mul=(z, 3))
    G["hg_norm_g"] = _colsum8(dgp)
    dq_a, dfl_a, di_a, dlb = _hgrn2_bwd(z, lb_row, states, doa_pre, name="hgrn2_b")
    dl0 = (dlb * lb_row * (1.0 - lb_row)).reshape(1, HEADS, DH)
    G["hg_lb_logits"] = jnp.concatenate([dl0, -dl0], axis=0)
    dq_b, dk_b, dv_b, dcr, dcq = _fox_bwd(z, c_col, c_row, o_b, lse, do_b, fetch_q, skip_b, name="fox_b")
    dc_pad = jnp.pad((dcr[:, 0, :] + dcq[:, 0, :]).T, ((0, 0), (0, LANE - HEADS)))
    gate_b = lambda cum, dc, zf, r: cum * _sigmoid(-(zf + r))
    dfl_b, dfb = _cumsum_t([(dc_pad, 0), (zfb, 0)], name="fox_c_b", width=LANE, reverse=True, rows=[(fbias_row, 0)],
                           pre=lambda dc, zf, r: dc, post=lambda *a: (gate_b(*a),), fold=gate_b)
    G["fox_f_bias"] = _colsum8(dfb)[:, :HEADS]

    pieces = [dq_a, dfl_a, di_a, dga, dq_b, dk_b, dv_b, dz0, dz1]
    dh2 = _mm([(dfl_b, W["w_fb"])], "nt", tm=512, tn=D_MODEL, tk=LANE, out_dtypes=[F32], name="mix_in_fb_b")
    for lo, hi in ((0, 5), (5, 9)):
        res = _mm([(p, W["w_main"]) for p in pieces[lo:hi]], "nt", tm=512, tn=D_MODEL, tk=D_MODEL, out_dtypes=[F32],
                  name=f"mix_in_b{lo}", b_koff=[n * D_MODEL for n in range(lo, hi)],
                  epilogue=lambda acc, t: (acc + t,), tiles=(dh2,), side=_x_scatter(pbf_ffn2) if lo == 0 else None)
        dh2, scattered = (res[0][0], res[1]) if lo == 0 else (res, scattered)
    reduced_ffn2 = _finish_reduce(own_ffn2, scattered, core, "ffn2")
    G["w_main"] = [_wgrad(h2, p, f"mix_in_w{n}") for n, p in enumerate(pieces)]
    G["w_fb"] = _mm([(h2, dfl_b)], "tn", tm=1024, tn=LANE, tk=512, out_dtypes=[F32], name="mix_in_fb_w")
    dx1, dgp = _rms_bwd(x1, P["mix_pre_g"], dh2, name="mix_pre_b", add=dx2)
    G["mix_pre_g"] = _colsum8(dgp)

    dx0, g, reduced = _ffn_bwd(x, dx1, ffn1_saved, P["ffn1_pre_g"], P["ffn1_post_g"], W["f1g"], W["f1u"], W["f1d"],
                               "ffn1", reduce_beside=(_grad_slabs(G, GROUP_MID), place, core))
    G.update(ffn1_pre_g=g["pre_g"], ffn1_post_g=g["post_g"], f1g=g["wg"], f1u=g["wu"], f1d=g["wd"])
    return sq, dx0, G, {GROUP_FFN2: reduced_ffn2, GROUP_MID: reduced[0], GROUP_FFN1: reduced[1]}


N_CHIP = 4
N_DEV = 8
IN_COLS = 9224
FB_COL = 7 * D_MODEL
SHARDED = (
    ("ffn1_w_in", (D_MODEL, 2 * D_FF), 1), ("ffn1_w_down", (D_FF, D_MODEL), 0), ("w_in", (D_MODEL, IN_COLS), 1),
    ("w_branch_a", (D_MODEL, D_MODEL), 0), ("w_branch_b", (D_MODEL, D_MODEL), 0), ("w_out", (D_MODEL, D_MODEL), 0),
    ("w_mq", (D_MODEL, D_MODEL), 0), ("w_mkv", (D_MODEL, 2 * D_MODEL), 1), ("w_mo", (D_MODEL, D_MODEL), 0),
    ("ffn2_w_in", (D_MODEL, 2 * D_FF), 1), ("ffn2_w_down", (D_FF, D_MODEL), 0),
)
SMALL = ("ffn1_pre_g", "ffn1_post_g", "mix_pre_g", "hg_norm_g", "mix_post_g", "mem_pre_g", "mem_kv_g", "mem_post_g",
         "ffn2_pre_g", "ffn2_post_g", "b_gate", "hg_lb_logits", "fox_f_bias")
SMALL_SHAPES = dict(b_gate=(1, 2 * D_MODEL), hg_lb_logits=(2, HEADS, DH), fox_f_bias=(1, HEADS))
SMALL_ROWS = 16
WEIGHT_ORDER = ("ffn1_pre_g", "ffn1_w_in", "ffn1_w_down", "ffn1_post_g", "mix_pre_g", "w_in", "hg_lb_logits", "hg_norm_g",
                "fox_f_bias", "w_branch_a", "w_branch_b", "b_gate", "w_out", "mix_post_g", "mem_pre_g", "mem_kv_g", "w_mq",
                "w_mkv", "w_mo", "mem_post_g", "ffn2_pre_g", "ffn2_w_in", "ffn2_w_down", "ffn2_post_g")


GROUP_FFN1 = ("ffn1_w_in", "ffn1_w_down")
GROUP_MID = ("w_in", "w_branch_a", "w_branch_b", "w_out", "w_mq", "w_mkv", "w_mo")
GROUP_FFN2 = ("ffn2_w_in", "ffn2_w_down")


def _layout(names, axis):
    out, at = [], 0
    for name, shape, ax in SHARDED:
        if ax == axis and name in names:
            n = shape[ax] // N_CHIP
            out.append((name, at, n))
            at += n if axis == 0 else -(-n // LANE) * LANE
    return out


def _pack(shards, names, dtype):
    rows = jnp.concatenate([shards[name].astype(dtype) for name, _, _ in _layout(names, 0)], axis=0)
    cols = [jnp.pad(shards[name].astype(dtype), ((0, 0), (0, -n % LANE))) for name, _, n in _layout(names, 1)]
    return [rows, jnp.concatenate(cols, axis=1)]


def _unpack(slabs, names):
    rows, cols = slabs
    out = {name: rows[at:at + n] for name, at, n in _layout(names, 0)}
    out.update({name: cols[:, at:at + n] for name, at, n in _layout(names, 1)})
    return out


def _pack_small(vals):
    rows = []
    for name in SMALL:
        v = vals[name].astype(F32).reshape(-1)
        rows.append(jnp.pad(v, (0, -v.shape[0] % D_MODEL)).reshape(-1, D_MODEL))
    rows = jnp.concatenate(rows, axis=0)
    return jnp.pad(rows, ((0, SMALL_ROWS - rows.shape[0]), (0, 0)))


def _unpack_small(slab):
    out, r = {}, 0
    for name in SMALL:
        shape = SMALL_SHAPES.get(name, (1, D_MODEL))
        size = math.prod(shape)
        n = -(-size // D_MODEL)
        out[name] = slab[r:r + n].reshape(-1)[:size].reshape(shape)
        r += n
    return out


MESH = pl.DeviceIdType.MESH
CHIP_FLIPS = ((0, 1), (1, 0), (1, 1))


def _place():
    x, y, c = lax.axis_index("x"), lax.axis_index("y"), lax.axis_index("c")
    chips = [(x ^ fx, y ^ fy) for fx, fy in CHIP_FLIPS]
    return x, y, c, chips


def _remote(src, dst, sems, k, dev):
    return pltpu.make_async_remote_copy(src_ref=src, dst_ref=dst, send_sem=sems[0].at[k], recv_sem=sems[1].at[k],
                                        device_id=dev, device_id_type=MESH)


def _exchange(copies, arrays, out_shapes, aliases=None):
    def start(ins, outs, sems):
        for sent, _ in copies(ins, outs, sems):
            sent.start()

    def wait(ins, outs, sems):
        pairs = copies(ins, outs, sems)
        for _, got in pairs:
            got.wait_recv()
        for sent, _ in pairs:
            sent.wait_send()

    return _Side(arrays, out_shapes, copies.count, start, wait, aliases)


def _counted(count):
    def mark(fn):
        fn.count = count
        return fn
    return mark


def _slab_halves(c, rows):
    half = rows // 2
    return pl.ds(c * half, half), pl.ds((1 - c) * half, half)


def _x_gather(slabs):
    @_counted(3 * len(slabs))
    def copies(ins, outs, sems):
        x, y, c, chips = _place()
        res = []
        for s, slab in enumerate(slabs):
            mine, _ = _slab_halves(c, slab.shape[0])
            for k, (px, py) in enumerate(chips):
                res.append((_remote(ins[s].at[mine], outs[s].at[k, mine], sems, 3 * s + k, (px, py, c)),) * 2)
        return res

    return _exchange(copies, slabs, [jax.ShapeDtypeStruct((3,) + s.shape, s.dtype) for s in slabs])


def _x_forward(gathered):
    @_counted(3 * len(gathered))
    def copies(ins, outs, sems):
        x, y, c, _ = _place()
        res = []
        for s, buf in enumerate(gathered):
            mine, theirs = _slab_halves(c, buf.shape[1])
            for k in range(3):
                res.append((_remote(ins[s].at[k, mine], outs[s].at[k, mine], sems, 3 * s + k, (x, y, 1 - c)),
                            _remote(ins[s].at[k, theirs], outs[s].at[k, theirs], sems, 3 * s + k, (x, y, 1 - c))))
        return res

    return _exchange(copies, gathered, [jax.ShapeDtypeStruct(g.shape, g.dtype) for g in gathered],
                     aliases={s: s for s in range(len(gathered))})


def _x_swap(grads, base=0):
    @_counted(base + N_CHIP * len(grads))
    def copies(ins, outs, sems):
        x, y, c, _ = _place()
        res = []
        for s, g in enumerate(grads):
            _, theirs = _slab_halves(c, g.shape[1])
            for j in range(N_CHIP):
                res.append((_remote(ins[s].at[j, theirs], outs[s].at[j], sems, base + N_CHIP * s + j,
                                    (x, y, 1 - c)),) * 2)
        return res

    return _exchange(copies, grads, [jax.ShapeDtypeStruct((N_CHIP, g.shape[1] // 2, g.shape[2]), g.dtype) for g in grads])


def _both(first, second):
    na, no = len(first.arrays), len(first.out_shapes)

    def start(ins, outs, sems):
        first.start(ins[:na], outs[:no], sems)
        second.start(ins[na:], outs[no:], sems)

    def wait(ins, outs, sems):
        first.wait(ins[:na], outs[:no], sems)
        second.wait(ins[na:], outs[no:], sems)

    assert not first.aliases and not second.aliases
    return _Side(first.arrays + second.arrays, first.out_shapes + second.out_shapes, second.nsem, start, wait)


def _x_scatter(partials):
    @_counted(3 * len(partials))
    def copies(ins, outs, sems):
        x, y, c, chips = _place()
        res = []
        for s in range(len(partials)):
            for k, (px, py) in enumerate(chips):
                res.append((_remote(ins[s].at[2 * px + py], outs[s].at[k], sems, 3 * s + k, (px, py, c)),) * 2)
        return res

    return _exchange(copies, partials, [jax.ShapeDtypeStruct((3,) + p.shape[1:], p.dtype) for p in partials])


def _x_join(halves):
    @_counted(len(halves))
    def copies(ins, outs, sems):
        x, y, c, _ = _place()
        return [(_remote(ins[s], outs[s], sems, s, (x, y, 1 - c)),) * 2 for s in range(len(halves))]

    return _exchange(copies, halves, [jax.ShapeDtypeStruct(h.shape, h.dtype) for h in halves])


def _run(side, name):
    n_in, n_out = len(side.arrays), len(side.out_shapes)

    def body(*refs):
        ins, outs, sems = refs[:n_in], refs[n_in:n_in + n_out], refs[-2:]
        side.start(ins, outs, sems)
        side.wait(ins, outs, sems)

    plumb = side.plumb(0, 0)
    return pl.pallas_call(
        body, name=name, in_specs=plumb["in_specs"], out_specs=plumb["out_specs"], out_shape=side.out_shapes,
        scratch_shapes=plumb["scratch"], input_output_aliases=plumb["aliases"],
    )(*side.arrays)


def _pair_sum(g, got, place, tag, *, tm):
    _, half, width = got.shape
    nb = half // tm

    def body(s_ref, g_ref, a_ref, bf_ref, own_ref):
        v = g_ref[...] + a_ref[...]
        bf_ref[...] = v.astype(BF16)

        @pl.when(pl.program_id(1) == s_ref[0])
        def _():
            own_ref[...] = v

    return pl.pallas_call(
        body, name="pair_sum_" + tag,
        grid_spec=pltpu.PrefetchScalarGridSpec(
            num_scalar_prefetch=1, grid=(nb, N_CHIP),
            in_specs=[pl.BlockSpec((None, tm, width), lambda i, j, s: (j, s[1] * nb + i, 0)),
                      pl.BlockSpec((None, tm, width), lambda i, j, s: (j, i, 0))],
            out_specs=[pl.BlockSpec((None, tm, width), lambda i, j, s: (j, i, 0)),
                       pl.BlockSpec((tm, width), lambda i, j, s: (i, 0))]),
        out_shape=[jax.ShapeDtypeStruct((N_CHIP, half, width), BF16), jax.ShapeDtypeStruct((half, width), F32)],
        compiler_params=_params("arbitrary", "arbitrary"),
    )(place, g, got)


def _chip_sum(own, got, tag, *, tm):
    half, width = own.shape

    def body(o_ref, g_ref, r_ref):
        r_ref[...] = ((o_ref[...] + g_ref[0].astype(F32)) + g_ref[1].astype(F32)) + g_ref[2].astype(F32)

    row = pl.BlockSpec((tm, width), lambda i: (i, 0))
    return pl.pallas_call(
        body, name="chip_sum_" + tag, grid=(half // tm,),
        in_specs=[row, pl.BlockSpec((3, tm, width), lambda i: (0, i, 0))], out_specs=row,
        out_shape=jax.ShapeDtypeStruct((half, width), F32), compiler_params=_params("parallel"),
    )(own, got)


def _sum_tiles(slabs):
    return [slabs[0].shape[1] // 4, D_MODEL // 8]


def _pair_sums(grads, swapped, place, tag):
    res = [_pair_sum(g, s, place, f"{tag}_{n}", tm=tm) for n, (g, s, tm) in enumerate(zip(grads, swapped, _sum_tiles(grads)))]
    return [r[0] for r in res], [r[1] for r in res]


def _finish_reduce(own, scattered, core, tag):
    mine = [_chip_sum(o, s, f"{tag}_{n}", tm=o.shape[0] // 2) for n, (o, s) in enumerate(zip(own, scattered))]
    theirs = _run(_x_join(mine), "join_halves_" + tag)
    return [lax.dynamic_update_slice(jnp.concatenate([a, a]), b, ((1 - core) * a.shape[0], 0))
            for a, b in zip(mine, theirs)]


def _gather_small(s):
    flips = [(fx, fy, fc) for fx in (0, 1) for fy in (0, 1) for fc in (0, 1)][1:]

    def body(s_ref, out_ref, send_sems, recv_sems, local_sem):
        x, y, c, _ = _place()
        sems = (send_sems, recv_sems)
        me = 4 * x + 2 * y + c
        local = pltpu.make_async_copy(s_ref, out_ref.at[me], local_sem)
        local.start()
        sent = [_remote(s_ref, out_ref.at[me], sems, k, (x ^ fx, y ^ fy, c ^ fc)) for k, (fx, fy, fc) in enumerate(flips)]
        for cp in sent:
            cp.start()
        for k, (fx, fy, fc) in enumerate(flips):
            peer = (x ^ fx, y ^ fy, c ^ fc)
            _remote(s_ref, out_ref.at[4 * peer[0] + 2 * peer[1] + peer[2]], sems, k, peer).wait_recv()
        for cp in sent:
            cp.wait_send()
        local.wait()

    return pl.pallas_call(
        body, name="gather_small", out_shape=jax.ShapeDtypeStruct((N_DEV, SMALL_ROWS, D_MODEL), s.dtype),
        in_specs=[ANY], out_specs=ANY,
        scratch_shapes=[pltpu.SemaphoreType.DMA((7,)), pltpu.SemaphoreType.DMA((7,)), pltpu.SemaphoreType.DMA],
    )(s)


LOCAL_NAMES = dict(ffn1_w_down="f1d", ffn2_w_down="f2d", w_branch_a="wa", w_branch_b="wb", w_out="wo", w_mq="wmq",
                   w_mkv="wmkv", w_mo="wmo")


def _assemble(names, own, others, me):
    by_flip = [jnp.concatenate([o[None], t], axis=0) for o, t in zip(own, others)]
    per_chip = [_unpack([lax.dynamic_index_in_dim(s, j ^ me, 0, keepdims=False) for s in by_flip], names)
                for j in range(N_CHIP)]
    return {name: jnp.concatenate([pc[name] for pc in per_chip], axis=axis)
            for name, _, axis in SHARDED if name in names}


def _local_names(full):
    out = {LOCAL_NAMES[name]: a for name, a in full.items() if name in LOCAL_NAMES}
    for name, key in (("ffn1_w_in", "f1"), ("ffn2_w_in", "f2")):
        if name in full:
            out[key + "g"], out[key + "u"] = full[name][:, :D_FF], full[name][:, D_FF:]
    if "w_in" in full:
        w_in = full["w_in"]
        out["w_main"] = jnp.concatenate([w_in[:, :FB_COL], w_in[:, FB_COL + HEADS:]], axis=1)
        out["w_fb"] = jnp.pad(w_in[:, FB_COL:FB_COL + HEADS], ((0, 0), (0, LANE - HEADS)))
    return out


def _grad_slabs(G, names):
    full = {name: G[key] for name, key in LOCAL_NAMES.items() if name in names}
    for name, key in (("ffn1_w_in", "f1"), ("ffn2_w_in", "f2")):
        if name in names:
            full[name] = jnp.concatenate([G[key + "g"], G[key + "u"]], axis=1)
    if "w_in" in names:
        main = jnp.concatenate(G["w_main"], axis=1)
        full["w_in"] = jnp.concatenate([main[:, :FB_COL], G["w_fb"][:, :HEADS], main[:, FB_COL:]], axis=1)
    rows, cols = [], []
    for j in range(N_CHIP):
        shards = {}
        for name, shape, axis in SHARDED:
            if name in names:
                n = shape[axis] // N_CHIP
                shards[name] = lax.slice_in_dim(full[name], j * n, (j + 1) * n, axis=axis)
        r, c = _pack(shards, names, F32)
        rows.append(r)
        cols.append(c)
    return [jnp.stack(rows, axis=0), jnp.stack(cols, axis=0)]


def kernel(x, mem, ffn1_pre_g, ffn1_w_in, ffn1_w_down, ffn1_post_g, mix_pre_g, w_in, hg_lb_logits, hg_norm_g, fox_f_bias, w_branch_a, w_branch_b, b_gate, w_out, mix_post_g, mem_pre_g, mem_kv_g, w_mq, w_mkv, w_mo, mem_post_g, ffn2_pre_g, ffn2_w_in, ffn2_w_down, ffn2_post_g, loss_target, m_ffn1_pre_g, m_ffn1_w_in, m_ffn1_w_down, m_ffn1_post_g, m_mix_pre_g, m_w_in, m_hg_lb_logits, m_hg_norm_g, m_fox_f_bias, m_w_branch_a, m_w_branch_b, m_b_gate, m_w_out, m_mix_post_g, m_mem_pre_g, m_mem_kv_g, m_w_mq, m_w_mkv, m_w_mo, m_mem_post_g, m_ffn2_pre_g, m_ffn2_w_in, m_ffn2_w_down, m_ffn2_post_g, v_ffn1_pre_g, v_ffn1_w_in, v_ffn1_w_down, v_ffn1_post_g, v_mix_pre_g, v_w_in, v_hg_lb_logits, v_hg_norm_g, v_fox_f_bias, v_w_branch_a, v_w_branch_b, v_b_gate, v_w_out, v_mix_post_g, v_mem_pre_g, v_mem_kv_g, v_w_mq, v_w_mkv, v_w_mo, v_mem_post_g, v_ffn2_pre_g, v_ffn2_w_in, v_ffn2_w_down, v_ffn2_post_g):
    w = dict(ffn1_pre_g=ffn1_pre_g, ffn1_w_in=ffn1_w_in, ffn1_w_down=ffn1_w_down, ffn1_post_g=ffn1_post_g, mix_pre_g=mix_pre_g, w_in=w_in, hg_lb_logits=hg_lb_logits, hg_norm_g=hg_norm_g, fox_f_bias=fox_f_bias, w_branch_a=w_branch_a, w_branch_b=w_branch_b, b_gate=b_gate, w_out=w_out, mix_post_g=mix_post_g, mem_pre_g=mem_pre_g, mem_kv_g=mem_kv_g, w_mq=w_mq, w_mkv=w_mkv, w_mo=w_mo, mem_post_g=mem_post_g, ffn2_pre_g=ffn2_pre_g, ffn2_w_in=ffn2_w_in, ffn2_w_down=ffn2_w_down, ffn2_post_g=ffn2_post_g)
    m = dict(ffn1_pre_g=m_ffn1_pre_g, ffn1_w_in=m_ffn1_w_in, ffn1_w_down=m_ffn1_w_down, ffn1_post_g=m_ffn1_post_g, mix_pre_g=m_mix_pre_g, w_in=m_w_in, hg_lb_logits=m_hg_lb_logits, hg_norm_g=m_hg_norm_g, fox_f_bias=m_fox_f_bias, w_branch_a=m_w_branch_a, w_branch_b=m_w_branch_b, b_gate=m_b_gate, w_out=m_w_out, mix_post_g=m_mix_post_g, mem_pre_g=m_mem_pre_g, mem_kv_g=m_mem_kv_g, w_mq=m_w_mq, w_mkv=m_w_mkv, w_mo=m_w_mo, mem_post_g=m_mem_post_g, ffn2_pre_g=m_ffn2_pre_g, ffn2_w_in=m_ffn2_w_in, ffn2_w_down=m_ffn2_w_down, ffn2_post_g=m_ffn2_post_g)
    v = dict(ffn1_pre_g=v_ffn1_pre_g, ffn1_w_in=v_ffn1_w_in, ffn1_w_down=v_ffn1_w_down, ffn1_post_g=v_ffn1_post_g, mix_pre_g=v_mix_pre_g, w_in=v_w_in, hg_lb_logits=v_hg_lb_logits, hg_norm_g=v_hg_norm_g, fox_f_bias=v_fox_f_bias, w_branch_a=v_w_branch_a, w_branch_b=v_w_branch_b, b_gate=v_b_gate, w_out=v_w_out, mix_post_g=v_mix_post_g, mem_pre_g=v_mem_pre_g, mem_kv_g=v_mem_kv_g, w_mq=v_w_mq, w_mkv=v_w_mkv, w_mo=v_w_mo, mem_post_g=v_mem_post_g, ffn2_pre_g=v_ffn2_pre_g, ffn2_w_in=v_ffn2_w_in, ffn2_w_down=v_ffn2_w_down, ffn2_post_g=v_ffn2_post_g)
    sharded = [name for name, _, _ in SHARDED]
    shard_of = lambda d: {name: d[name][0] for name in sharded}

    me, core = 2 * lax.axis_index("x") + lax.axis_index("y"), lax.axis_index("c")
    place = jnp.stack([me, core]).astype(jnp.int32)
    own = {group: _pack(shard_of(w), group, BF16) for group in (GROUP_FFN1, GROUP_MID, GROUP_FFN2)}
    P = {name: w[name] for name in SMALL}

    sq, dx0, G, reduced = _local_step(x[0], mem[0], loss_target[0], P, own, me, place, core)
    loss = lax.psum(0.5 * jnp.sum(sq) / D_MODEL, ("x", "y", "c"))

    g_shards = {}
    for group, slabs in reduced.items():
        g_shards.update(_unpack(slabs, group))
    big = {}
    for name, shape, axis in SHARDED:
        rows = shape[0] // (N_CHIP if axis == 0 else 1)
        big[name] = _adamw(w[name][0], g_shards[name], m[name][0], v[name][0], name="adamw_" + name, tm=rows // 8)
    small = _adamw(_pack_small(w), _gather_small(_pack_small(G)), _pack_small(m), _pack_small(v), name="adamw_small",
                   tm=SMALL_ROWS)

    outs = [loss, dx0[None]]
    for n in range(4):
        vals = {name: res[n][None] for name, res in big.items()}
        vals.update(_unpack_small(small[n]))
        outs += [vals[name] for name in WEIGHT_ORDER]
    return tuple(outs)
```

```python
import functools
import math

import jax
import jax.numpy as jnp
from jax import lax
from jax.experimental import pallas as pl
from jax.experimental.pallas import tpu as pltpu

F32 = jnp.float32
BF16 = jnp.bfloat16

D_MODEL = 1024
D_FF = 2816
HEADS = 8
DH = 128
MEM_HEADS = 4
MEM_DH = 256
MEM_LEN = 256
EPS = 1e-6
SUB = 16
LANE = 128
SUBLANE = 8
VMEM_LIMIT = 56 * 1024 * 1024

ADAM_LR = 0.001
ADAM_B1 = 0.9
ADAM_B2 = 0.999
ADAM_EPS = 1e-08
ADAM_WD = 0.01
ADAM_STEP = 10

HIGHEST = lax.Precision.HIGHEST


def _params(*sem):
    return pltpu.CompilerParams(dimension_semantics=sem, vmem_limit_bytes=VMEM_LIMIT)


def _sigmoid(v):
    return 0.5 * jnp.tanh(0.5 * v) + 0.5


def _silu(v):
    return v * _sigmoid(v)


def _dsilu(v):
    s = _sigmoid(v)
    return s * (1.0 + v * (1.0 - s))


def _dot(a, b, dims):
    return lax.dot_general(a.astype(BF16), b.astype(BF16), (dims, ((), ())), preferred_element_type=F32)


NN = ((1,), (0,))
NT = ((1,), (1,))
TN = ((0,), (0,))


ANY = pl.BlockSpec(memory_space=pl.ANY)


class _Side:
    def __init__(self, arrays, out_shapes, nsem, start, wait, aliases=None):
        self.arrays, self.out_shapes, self.nsem = list(arrays), list(out_shapes), nsem
        self.start, self.wait, self.aliases = start, wait, dict(aliases or {})

    def plumb(self, n_in, n_out):
        return dict(args=self.arrays, in_specs=[ANY] * len(self.arrays), out_specs=[ANY] * len(self.out_shapes),
                    scratch=[pltpu.SemaphoreType.DMA((self.nsem,)), pltpu.SemaphoreType.DMA((self.nsem,))],
                    aliases={n_in + i: n_out + o for i, o in self.aliases.items()})

    def run_at_ends(self, ins, outs, sems, first, last, compute):
        @pl.when(first)
        def _():
            self.start(ins, outs, sems)

        compute()

        @pl.when(last)
        def _():
            self.wait(ins, outs, sems)


def _grid_ends(grid):
    first = functools.reduce(lambda a, b: a & b, [pl.program_id(d) == 0 for d in range(len(grid))])
    last = functools.reduce(lambda a, b: a & b, [pl.program_id(d) == grid[d] - 1 for d in range(len(grid))])
    return first, last


def _mm(pairs, mode, *, tm, tn, tk, out_dtypes, name, epilogue=None, tiles=(), b_koff=None, side=None):
    a0, b0 = pairs[0]
    if mode == "nn":
        (M, K), N = a0.shape, b0.shape[1]
    elif mode == "nt":
        (M, K), N = a0.shape, b0.shape[0]
    else:
        (K, M), N = a0.shape, b0.shape[1]
    tm, tn, tk = min(tm, M), min(tn, N), min(tk, K)
    assert M % tm == 0 and N % tn == 0 and K % tk == 0, (name, M, N, K, tm, tn, tk)
    nk = K // tk
    npair = len(pairs)
    koff = [0] * npair if b_koff is None else [o // tk for o in b_koff]
    if b_koff is not None:
        assert all(o % tk == 0 for o in b_koff)
    in_specs, args = [], []
    for p, (a, b) in enumerate(pairs):
        if mode == "nn":
            sa = pl.BlockSpec((tm, tk), lambda i, j, k: (i, k))
            sb = pl.BlockSpec((tk, tn), lambda i, j, k, o=koff[p]: (k + o, j))
            dims = NN
        elif mode == "nt":
            sa = pl.BlockSpec((tm, tk), lambda i, j, k: (i, k))
            sb = pl.BlockSpec((tn, tk), lambda i, j, k, o=koff[p]: (j, k + o))
            dims = NT
        else:
            sa = pl.BlockSpec((tk, tm), lambda i, j, k: (k, i))
            sb = pl.BlockSpec((tk, tn), lambda i, j, k, o=koff[p]: (k + o, j))
            dims = TN
        in_specs += [sa, sb]
        args += [a, b]
    for t in tiles:
        in_specs.append(pl.BlockSpec((tm, tn), lambda i, j, k: (i, j)))
        args.append(t)
    nt_ = len(tiles)
    nout = len(out_dtypes)
    nin = len(args)
    grid = (M // tm, N // tn, nk)
    plumb = side.plumb(nin, nout) if side is not None else None
    ns_in, ns_out = (len(side.arrays), len(side.out_shapes)) if side is not None else (0, 0)

    def body(*refs):
        ab = refs[: 2 * npair]
        tl = refs[2 * npair: nin]
        outs = refs[nin + ns_in: nin + ns_in + nout]
        scratch = refs[nin + ns_in + nout + ns_out:]
        acc_ref = scratch[0] if nk > 1 else None
        if side is None:
            compute(ab, tl, outs, acc_ref)
        else:
            first, last = _grid_ends(grid)
            side.run_at_ends(refs[nin: nin + ns_in], refs[nin + ns_in + nout: nin + ns_in + nout + ns_out],
                             scratch[-2:], first, last, lambda: compute(ab, tl, outs, acc_ref))

    def compute(ab, tl, outs, acc_ref):
        def partial_sum():
            s = _dot(ab[0][...], ab[1][...], dims)
            for p in range(1, npair):
                s = s + _dot(ab[2 * p][...], ab[2 * p + 1][...], dims)
            return s

        def finish(acc):
            res = (acc,) if epilogue is None else epilogue(acc, *[t[...] for t in tl])
            for o, r in zip(outs, res):
                o[...] = r.astype(o.dtype)

        if nk == 1:
            finish(partial_sum())
        else:
            k = pl.program_id(2)

            @pl.when(k == 0)
            def _():
                acc_ref[...] = jnp.zeros_like(acc_ref)

            acc_ref[...] += partial_sum()

            @pl.when(k == nk - 1)
            def _():
                finish(acc_ref[...])

    out_shape = [jax.ShapeDtypeStruct((M, N), dt) for dt in out_dtypes]
    out_specs = [pl.BlockSpec((tm, tn), lambda i, j, k: (i, j)) for _ in out_dtypes]
    scratch = [pltpu.VMEM((tm, tn), F32)] if nk > 1 else []
    if side is None:
        res = pl.pallas_call(
            body, name=name, grid=grid, in_specs=in_specs, out_specs=out_specs, out_shape=out_shape,
            scratch_shapes=scratch, compiler_params=_params("parallel", "parallel", "arbitrary"),
        )(*args)
        return res[0] if nout == 1 else res
    res = pl.pallas_call(
        body, name=name, grid=grid, in_specs=in_specs + plumb["in_specs"], out_specs=out_specs + plumb["out_specs"],
        out_shape=out_shape + side.out_shapes, scratch_shapes=scratch + plumb["scratch"],
        input_output_aliases=plumb["aliases"], compiler_params=_params("arbitrary", "arbitrary", "arbitrary"),
    )(*args, *plumb["args"])
    return res[:nout], res[nout:]


def _col(arr, tm, width, cb):
    return pl.BlockSpec((tm, width), lambda i, cb=cb: (i, cb))


def _rms_fwd(x, g, *, out_dtype, name, mul=None, res=None, coeff=1.0, tm=512):
    T, D = x.shape
    tm = min(tm, T)
    args, in_specs = [x, g], [pl.BlockSpec((tm, D), lambda i: (i, 0)), pl.BlockSpec((1, D), lambda i: (0, 0))]
    if mul is not None:
        args.append(mul[0])
        in_specs.append(_col(mul[0], tm, D, mul[1]))
    if res is not None:
        args.append(res)
        in_specs.append(pl.BlockSpec((tm, D), lambda i: (i, 0)))

    def body(*refs):
        xv = refs[0][...].astype(F32)
        r = lax.rsqrt(jnp.mean(xv * xv, axis=-1, keepdims=True) + EPS)
        y = (xv * r) * refs[1][...]
        n = 2
        if mul is not None:
            y = y * _silu(refs[n][...])
            n += 1
        if res is not None:
            y = refs[n][...] + coeff * y
        refs[-1][...] = y.astype(out_dtype)

    return pl.pallas_call(
        body, name=name, grid=(T // tm,), in_specs=in_specs, out_specs=pl.BlockSpec((tm, D), lambda i: (i, 0)),
        out_shape=jax.ShapeDtypeStruct((T, D), out_dtype), compiler_params=_params("parallel"),
    )(*args)


def _fold8(v):
    tm, d = v.shape
    return v.reshape(tm // SUBLANE, SUBLANE, d).sum(axis=0)


def _rms_bwd(x, g, dy, *, name, coeff=1.0, add=None, mul=None, dx_dtype=F32, tm=512):
    T, D = x.shape
    tm = min(tm, T)
    row = pl.BlockSpec((tm, D), lambda i: (i, 0))
    args, in_specs = [x, g, dy], [row, pl.BlockSpec((1, D), lambda i: (0, 0)), row]
    if add is not None:
        args.append(add)
        in_specs.append(row)
    if mul is not None:
        args.append(mul[0])
        in_specs.append(_col(mul[0], tm, D, mul[1]))
    nin = len(args)

    def body(*refs):
        xv = refs[0][...].astype(F32)
        gv = refs[1][...]
        dyv = refs[2][...].astype(F32) * coeff
        r = lax.rsqrt(jnp.mean(xv * xv, axis=-1, keepdims=True) + EPS)
        nrm = xv * r
        n = 3
        addv = None
        if add is not None:
            addv = refs[n][...]
            n += 1
        if mul is not None:
            mv = refs[n][...]
            sm = _silu(mv)
            refs[nin + 2][...] = (dyv * nrm * gv * _dsilu(mv)).astype(refs[nin + 2].dtype)
            dyv = dyv * sm
        dn = dyv * gv
        dx = r * (dn - nrm * jnp.mean(dn * nrm, axis=-1, keepdims=True))
        if addv is not None:
            dx = dx + addv
        refs[nin][...] = dx.astype(dx_dtype)
        dg_ref = refs[nin + 1]

        @pl.when(pl.program_id(0) == 0)
        def _():
            dg_ref[...] = jnp.zeros_like(dg_ref)

        dg_ref[...] += _fold8(dyv * nrm)

    out_shape = [jax.ShapeDtypeStruct((T, D), dx_dtype), jax.ShapeDtypeStruct((SUBLANE, D), F32)]
    out_specs = [row, pl.BlockSpec((SUBLANE, D), lambda i: (0, 0))]
    if mul is not None:
        out_shape.append(jax.ShapeDtypeStruct((T, D), BF16))
        out_specs.append(row)
    return pl.pallas_call(
        body, name=name, grid=(T // tm,), in_specs=in_specs, out_specs=out_specs, out_shape=out_shape,
        compiler_params=_params("arbitrary"),
    )(*args)


def _ffn_in(h, wg, wu, *, name, tm=1024, tn=256, side=None):
    T, D = h.shape
    F = wg.shape[1]
    tm = min(tm, T)
    assert F % tn == 0
    grid = (T // tm, F // tn)
    ns_in, ns_out = (len(side.arrays), len(side.out_shapes)) if side is not None else (0, 0)

    def compute(h_ref, wg_ref, wu_ref, a_ref, g_ref, u_ref):
        hv = h_ref[...]
        gt = _dot(hv, wg_ref[...], NN)
        up = _dot(hv, wu_ref[...], NN)
        a_ref[...] = (_silu(gt) * up).astype(BF16)
        g_ref[...] = gt.astype(BF16)
        u_ref[...] = up.astype(BF16)

    def body(*refs):
        if side is None:
            compute(*refs)
        else:
            outs0 = 3 + ns_in
            first, last = _grid_ends(grid)
            side.run_at_ends(refs[3:outs0], refs[outs0 + 3: outs0 + 3 + ns_out], refs[-2:], first, last,
                             lambda: compute(*refs[:3], *refs[outs0: outs0 + 3]))

    o = pl.BlockSpec((tm, tn), lambda i, j: (i, j))
    w = pl.BlockSpec((D, tn), lambda i, j: (0, j))
    in_specs = [pl.BlockSpec((tm, D), lambda i, j: (i, 0)), w, w]
    out_shape = [jax.ShapeDtypeStruct((T, F), BF16)] * 3
    if side is None:
        return pl.pallas_call(body, name=name, grid=grid, in_specs=in_specs, out_specs=[o, o, o], out_shape=out_shape,
                              compiler_params=_params("parallel", "parallel"))(h, wg, wu)
    plumb = side.plumb(3, 3)
    res = pl.pallas_call(
        body, name=name, grid=grid, in_specs=in_specs + plumb["in_specs"], out_specs=[o, o, o] + plumb["out_specs"],
        out_shape=out_shape + side.out_shapes, scratch_shapes=plumb["scratch"], input_output_aliases=plumb["aliases"],
        compiler_params=_params("arbitrary", "arbitrary"),
    )(h, wg, wu, *plumb["args"])
    return res[:3], res[3:]


def _swiglu_bwd_epilogue(da, gt, up):
    gt = gt.astype(F32)
    up = up.astype(F32)
    return da * up * _dsilu(gt), da * _silu(gt)


GATE_CB = 7


def _gatemix_fwd(z, b_gate, ya, yb, *, name, tm=512):
    T, D = ya.shape
    tm = min(tm, T)
    row = pl.BlockSpec((tm, D), lambda i: (i, 0))

    def body(z0, z1, b0, b1, ya_ref, yb_ref, y_ref):
        g0 = _sigmoid(z0[...] + b0[...])
        g1 = _sigmoid(z1[...] + b1[...])
        y_ref[...] = (g0 * ya_ref[...] + g1 * yb_ref[...]).astype(y_ref.dtype)

    bs = lambda c: pl.BlockSpec((1, D), lambda i, c=c: (0, c))
    return pl.pallas_call(
        body, name=name, grid=(T // tm,),
        in_specs=[_col(z, tm, D, GATE_CB), _col(z, tm, D, GATE_CB + 1), bs(0), bs(1), row, row],
        out_specs=row, out_shape=jax.ShapeDtypeStruct((T, D), BF16), compiler_params=_params("parallel"),
    )(z, z, b_gate, b_gate, ya, yb)


def _gatemix_bwd(z, b_gate, ya, yb, dy, *, name, tm=512):
    T, D = ya.shape
    tm = min(tm, T)
    row = pl.BlockSpec((tm, D), lambda i: (i, 0))
    part = pl.BlockSpec((SUBLANE, D), lambda i: (0, 0))

    def body(z0, z1, b0, b1, ya_ref, yb_ref, dy_ref, dya, dyb, dz0, dz1, s0, s1):
        g0 = _sigmoid(z0[...] + b0[...])
        g1 = _sigmoid(z1[...] + b1[...])
        dyv = dy_ref[...]
        dya[...] = (dyv * g0).astype(BF16)
        dyb[...] = (dyv * g1).astype(BF16)
        d0 = dyv * ya_ref[...] * (g0 * (1.0 - g0))
        d1 = dyv * yb_ref[...] * (g1 * (1.0 - g1))
        dz0[...] = d0.astype(BF16)
        dz1[...] = d1.astype(BF16)

        @pl.when(pl.program_id(0) == 0)
        def _():
            s0[...] = jnp.zeros_like(s0)
            s1[...] = jnp.zeros_like(s1)

        s0[...] += _fold8(d0)
        s1[...] += _fold8(d1)

    bs = lambda c: pl.BlockSpec((1, D), lambda i, c=c: (0, c))
    act = jax.ShapeDtypeStruct((T, D), BF16)
    ps = jax.ShapeDtypeStruct((SUBLANE, D), F32)
    return pl.pallas_call(
        body, name=name, grid=(T // tm,),
        in_specs=[_col(z, tm, D, GATE_CB), _col(z, tm, D, GATE_CB + 1), bs(0), bs(1), row, row, row],
        out_specs=[row, row, row, row, part, part], out_shape=[act, act, act, act, ps, ps],
        compiler_params=_params("arbitrary"),
    )(z, z, b_gate, b_gate, ya, yb, dy)


def _loss_head(x, target, *, name, tm=512):
    T, D = x.shape
    tm = min(tm, T)
    row = pl.BlockSpec((tm, D), lambda i: (i, 0))

    def body(x_ref, t_ref, dx_ref, s_ref):
        e = x_ref[...] - t_ref[...]
        dx_ref[...] = e * (1.0 / D)

        @pl.when(pl.program_id(0) == 0)
        def _():
            s_ref[...] = jnp.zeros_like(s_ref)

        s_ref[...] += _fold8(e * e)

    return pl.pallas_call(
        body, name=name, grid=(T // tm,), in_specs=[row, row],
        out_specs=[row, pl.BlockSpec((SUBLANE, D), lambda i: (0, 0))],
        out_shape=[jax.ShapeDtypeStruct((T, D), F32), jax.ShapeDtypeStruct((SUBLANE, D), F32)],
        compiler_params=_params("arbitrary"),
    )(x, target)


def _tri(n, reverse):
    r = lax.broadcasted_iota(jnp.int32, (n, n), 0)
    c = lax.broadcasted_iota(jnp.int32, (n, n), 1)
    return jnp.where((c >= r) if reverse else (c <= r), 1.0, 0.0).astype(F32)


def _cumsum_t(xs, *, name, width, pre, reverse=False, rows=(), post=None, out_dtypes=(F32,), fold=None, tb=256):
    T = xs[0][0].shape[0]
    tb = min(tb, T)
    nb = T // tb
    tblk = (lambda i: nb - 1 - i) if reverse else (lambda i: i)
    args = [a for a, _ in xs] + [a for a, _ in rows]
    in_specs = [pl.BlockSpec((tb, width), lambda i, cb=cb: (tblk(i), cb)) for _, cb in xs]
    in_specs += [pl.BlockSpec((1, width), lambda i, cb=cb: (0, cb)) for _, cb in rows]
    nin, nout = len(args), len(out_dtypes)

    def body(*refs):
        vals = [r[...] for r in refs[:nin]]
        outs = refs[nin:nin + nout]
        carry = refs[-1]
        first = pl.program_id(0) == 0

        @pl.when(first)
        def _():
            carry[...] = jnp.zeros_like(carry)

        cum = jnp.dot(_tri(tb, reverse), pre(*vals), precision=HIGHEST, preferred_element_type=F32) + carry[...]
        carry[...] = cum[0:1, :] if reverse else cum[tb - 1:tb, :]
        res = (cum,) if post is None else post(cum, *vals)
        for o, r in zip(outs, res):
            o[...] = r.astype(o.dtype)
        if fold is not None:
            f_ref = refs[nin + nout]

            @pl.when(first)
            def _():
                f_ref[...] = jnp.zeros_like(f_ref)

            f_ref[...] += _fold8(fold(cum, *vals))

    tspec = pl.BlockSpec((tb, width), lambda i: (tblk(i), 0))
    out_shape = [jax.ShapeDtypeStruct((T, width), dt) for dt in out_dtypes]
    out_specs = [tspec] * nout
    if fold is not None:
        out_shape.append(jax.ShapeDtypeStruct((SUBLANE, width), F32))
        out_specs.append(pl.BlockSpec((SUBLANE, width), lambda i: (0, 0)))
    res = pl.pallas_call(
        body, name=name, grid=(nb,), in_specs=in_specs, out_specs=out_specs, out_shape=out_shape,
        scratch_shapes=[pltpu.VMEM((1, width), F32)], compiler_params=_params("arbitrary"),
    )(*args)
    return res[0] if len(res) == 1 else res


def _logsigmoid(v):
    return jnp.minimum(v, 0.0) - jnp.log(1.0 + jnp.exp(-jnp.abs(v)))


HG_TB = 256
HG_HB = 4
HG_W = HG_HB * DH
HG_GROUPS = HEADS // HG_HB
HG_Q_CB, HG_F_CB, HG_I_CB = 0, HG_GROUPS, 2 * HG_GROUPS
NEG = -1e30


def _scan16(x, rowid, reverse=False):
    for k in [1 << n for n in range(SUB.bit_length() - 1)]:
        if reverse:
            x = x + jnp.where(rowid < SUB - k, pltpu.roll(x, SUB - k, 0), 0.0)
        else:
            x = x + jnp.where(rowid >= k, pltpu.roll(x, k, 0), 0.0)
    return x


def _hg_block(q_ref, f_ref, i_ref, lb_ref, rows, cols, rowid):
    lb = lb_ref[:, cols]
    qr = q_ref[rows, cols]
    sg = _sigmoid(f_ref[rows, cols])
    f = lb + (1.0 - lb) * sg
    b = _scan16(jnp.log(f), rowid)
    return _silu(qr), 1.0 - f, i_ref[rows, cols], b, qr, sg, f, lb


def _hg_specs(tb, tmap):
    return [pl.BlockSpec((tb, HG_W), lambda g, t, *_: (tmap(t), HG_Q_CB + g)),
            pl.BlockSpec((tb, HG_W), lambda g, t, *_: (tmap(t), HG_F_CB + g)),
            pl.BlockSpec((tb, HG_W), lambda g, t, *_: (tmap(t), HG_I_CB + g)),
            pl.BlockSpec((1, HG_W), lambda g, t, *_: (0, g))]


def _hgrn2_fwd(z, lb_row, *, name):
    T = z.shape[0]
    tb = min(HG_TB, T)
    nb, nsub = T // tb, tb // SUB

    def body(q_ref, f_ref, i_ref, lb_ref, o_ref, st_ref, state):
        @pl.when(pl.program_id(1) == 0)
        def _():
            state[...] = jnp.zeros_like(state)

        rowid = lax.broadcasted_iota(jnp.int32, (SUB, DH), 0)

        def step(c, carry):
            rows = pl.ds(pl.multiple_of(c * SUB, SUB), SUB)
            for hh in range(HG_HB):
                cols = slice(hh * DH, (hh + 1) * DH)
                q, k, iv, b = _hg_block(q_ref, f_ref, i_ref, lb_ref, rows, cols, rowid)[:4]
                bl = b[SUB - 1:SUB, :]
                sv = state[hh]
                st_ref[c, hh] = sv
                o = _dot(q * jnp.exp(b), sv, NT)
                for s in range(SUB):
                    e = jnp.exp(jnp.where(rowid >= s, b - b[s:s + 1, :], NEG))
                    a = jnp.sum(q * e * k[s:s + 1, :], axis=-1, keepdims=True)
                    o = o + a * iv[s:s + 1, :]
                o_ref[rows, cols] = o
                state[hh] = sv * jnp.exp(bl) + _dot(iv, k * jnp.exp(bl - b), TN)
            return carry

        lax.fori_loop(0, nsub, step, 0)

    return pl.pallas_call(
        body, name=name, grid=(HG_GROUPS, nb), in_specs=_hg_specs(tb, lambda t: t),
        out_specs=[pl.BlockSpec((tb, HG_W), lambda g, t: (t, g)),
                   pl.BlockSpec((nsub, HG_HB, DH, DH), lambda g, t: (t, g, 0, 0))],
        out_shape=[jax.ShapeDtypeStruct((T, D_MODEL), F32), jax.ShapeDtypeStruct((T // SUB, HEADS, DH, DH), F32)],
        scratch_shapes=[pltpu.VMEM((HG_HB, DH, DH), F32)], compiler_params=_params("parallel", "arbitrary"),
    )(z, z, z, lb_row)


def _hgrn2_bwd(z, lb_row, states, do, *, name):
    T = z.shape[0]
    tb = min(HG_TB, T)
    nb, nsub = T // tb, tb // SUB
    rev = lambda t: nb - 1 - t

    def body(q_ref, f_ref, i_ref, lb_ref, st_ref, do_ref, dq_ref, dfl_ref, di_ref, dlb_ref, dstate, later):
        @pl.when(pl.program_id(1) == 0)
        def _():
            dstate[...] = jnp.zeros_like(dstate)
            later[...] = jnp.zeros_like(later)
            dlb_ref[...] = jnp.zeros_like(dlb_ref)

        rowid = lax.broadcasted_iota(jnp.int32, (SUB, DH), 0)

        def step(cc, carry):
            c = nsub - 1 - cc
            rows = pl.ds(pl.multiple_of(c * SUB, SUB), SUB)
            for hh in range(HG_HB):
                cols = slice(hh * DH, (hh + 1) * DH)
                q, k, iv, b, qr, sg, f, lb = _hg_block(q_ref, f_ref, i_ref, lb_ref, rows, cols, rowid)
                bl = b[SUB - 1:SUB, :]
                eb, ebl = jnp.exp(b), jnp.exp(bl - b)
                sv, dsv = st_ref[c, hh], dstate[hh]
                dov = do_ref[rows, cols]
                dq = _dot(dov, sv, NN) * eb
                dk = _dot(iv, dsv, NN) * ebl
                di = _dot(k * ebl, dsv, NT)
                for s in range(SUB):
                    e = jnp.exp(jnp.where(rowid >= s, b - b[s:s + 1, :], NEG))
                    ks, isv = k[s:s + 1, :], iv[s:s + 1, :]
                    qe = q * e
                    a = jnp.sum(qe * ks, axis=-1, keepdims=True)
                    p = jnp.sum(dov * isv, axis=-1, keepdims=True)
                    dq = dq + p * (e * ks)
                    dks = jnp.sum(p * qe, axis=0, keepdims=True)
                    dis = jnp.sum(a * dov, axis=0, keepdims=True)
                    dk = dk + jnp.where(rowid == s, dks, 0.0)
                    di = di + jnp.where(rowid == s, dis, 0.0)
                dlogf = _scan16(q * dq - k * dk, rowid, reverse=True) + later[hh]
                df = dlogf / f - dk
                dlb_ref[:, cols] += jnp.sum(df * (1.0 - sg), axis=0, keepdims=True)
                dfl_ref[rows, cols] = (df * (1.0 - lb) * (sg * (1.0 - sg))).astype(BF16)
                dq_ref[rows, cols] = (dq * _dsilu(qr)).astype(BF16)
                di_ref[rows, cols] = di.astype(BF16)
                dnew = dsv * jnp.exp(bl) + _dot(dov, q * eb, TN)
                dstate[hh] = dnew
                later[hh] = jnp.sum(dnew * sv, axis=0, keepdims=True)
            return carry

        lax.fori_loop(0, nsub, step, 0)

    tile = pl.BlockSpec((tb, HG_W), lambda g, t: (rev(t), g))
    act = jax.ShapeDtypeStruct((T, D_MODEL), BF16)
    return pl.pallas_call(
        body, name=name, grid=(HG_GROUPS, nb),
        in_specs=_hg_specs(tb, rev) + [pl.BlockSpec((nsub, HG_HB, DH, DH), lambda g, t: (rev(t), g, 0, 0)), tile],
        out_specs=[tile, tile, tile, pl.BlockSpec((1, HG_W), lambda g, t: (0, g))],
        out_shape=[act, act, act, jax.ShapeDtypeStruct((1, D_MODEL), F32)],
        scratch_shapes=[pltpu.VMEM((HG_HB, DH, DH), F32), pltpu.VMEM((HG_HB, 1, DH), F32)],
        compiler_params=_params("parallel", "arbitrary"),
    )(z, z, z, lb_row, states, do)


FOX_Q_CB, FOX_K_CB, FOX_V_CB = 4 * HEADS, 5 * HEADS, 6 * HEADS
FOX_SCALE = 1.0 / math.sqrt(DH)


def _fox_tile(T):
    return 512 if T >= 2048 else 128


def _fox_pairs(nq, by_query):
    if by_query:
        pairs = [(i, j) for i in range(nq) for j in range(i + 1)]
    else:
        pairs = [(i, j) for j in range(nq) for i in range(j, nq)]
    return (jnp.asarray([p[0] for p in pairs], jnp.int32), jnp.asarray([p[1] for p in pairs], jnp.int32))


LOG2E = 1.4426950408889634
FOX_RC = 64
FOX_HB = 2
FOX_HB_FWD = 4


def _fox_q2(q):
    return (q * (FOX_SCALE * LOG2E)).astype(BF16)


FOX_ZERO = -200.0


def _fox_norms(z, *, name):
    T = z.shape[0]
    tq = _fox_tile(T)
    nq = T // tq

    def body(q_ref, k_ref, nq_ref, nk_ref):
        head_of_col = lax.broadcasted_iota(jnp.int32, (D_MODEL, LANE), 0) // DH
        pick = jnp.where(head_of_col == lax.broadcasted_iota(jnp.int32, (D_MODEL, LANE), 1), 1.0, 0.0).astype(BF16)

        def tile_max(v):
            v = v.astype(F32)
            sq = _dot(v * v, pick, NN)
            return jnp.broadcast_to(jnp.max(jnp.sqrt(sq), axis=0, keepdims=True), (SUBLANE, LANE))

        nq_ref[...] = tile_max(_fox_q2(q_ref[...]))
        nk_ref[...] = tile_max(k_ref[...].astype(BF16))

    out = jax.ShapeDtypeStruct((nq * SUBLANE, LANE), F32)
    spec = pl.BlockSpec((SUBLANE, LANE), lambda i: (i, 0))
    a, b = pl.pallas_call(
        body, name=name, grid=(nq,),
        in_specs=[pl.BlockSpec((tq, D_MODEL), lambda i: (i, FOX_Q_CB // HEADS)),
                  pl.BlockSpec((tq, D_MODEL), lambda i: (i, FOX_K_CB // HEADS))],
        out_specs=[spec, spec], out_shape=[out, out], compiler_params=_params("parallel"),
    )(z, z)
    return a[::SUBLANE, :HEADS], b[::SUBLANE, :HEADS]


def _fox_schedule(norm_q, norm_k, ct, tq):
    nq = ct.shape[1] // tq
    first, last = ct[:, ::tq], ct[:, tq - 1::tq]
    nqh, nkh = norm_q.T * 1.05, norm_k.T * 1.05
    bound = nqh[:, :, None] * (nkh[:, None, :] + nkh[:, :, None]) + first[:, :, None] - last[:, None, :]
    tri = jnp.arange(nq)[:, None] > jnp.arange(nq)[None, :]
    drop = (bound < FOX_ZERO) & tri[None]
    lo = jnp.argmin(drop, axis=2).astype(jnp.int32)
    dropped = jnp.arange(nq)[None, None, :] < lo[:, :, None]
    group_lo = lambda hb: jnp.min(lo.reshape(HEADS // hb, hb, nq), axis=1)
    dropped_g = jnp.arange(nq)[None, None, :] < group_lo(FOX_HB)[:, :, None]
    qf, kf = _fox_pairs(nq, by_query=True)
    qb, kb = _fox_pairs(nq, by_query=False)
    fetch_k = jnp.maximum(kf[None, :], group_lo(FOX_HB_FWD)[:, qf])
    kept_q = jnp.where(dropped_g | ~(tri | jnp.eye(nq, dtype=bool))[None], -1, jnp.arange(nq)[None, :, None])
    last_kept = lax.cummax(kept_q, axis=1)
    fetch_q = last_kept[:, qb, kb]
    i32 = lambda a: a.astype(jnp.int32)
    return i32(fetch_k), i32(dropped[:, qf, kf]), i32(fetch_q), i32(dropped[:, qb, kb])


def _fox_fwd(z, c_col, c_row, fetch_k, skip, *, name):
    T = z.shape[0]
    tq = _fox_tile(T)
    nq = T // tq
    rc = min(FOX_RC, tq)
    hb = FOX_HB_FWD

    qi, kj = _fox_pairs(nq, by_query=True)

    def body(qi_ref, kj_ref, fk_ref, skip_ref, q_ref, k_ref, v_ref, cc_ref, cr_ref, o_ref, lse_ref, m_scr, l_scr, acc,
             a_scr, s_scr, p_scr):
        p_id = pl.program_id(1)
        i, j = qi_ref[p_id], kj_ref[p_id]

        @pl.when(j == 0)
        def _():
            m_scr[...] = jnp.full_like(m_scr, NEG)
            l_scr[...] = jnp.zeros_like(l_scr)
            acc[...] = jnp.zeros_like(acc)

        def update(hh, masked):
            cols = slice(hh * DH, (hh + 1) * DH)
            bias = cc_ref[hh, 0:1, :] - cr_ref[hh]
            s_scr[hh] = _dot(_fox_q2(q_ref[:, cols]), k_ref[:, cols], NT)
            for r in range(tq // rc):
                rows = slice(r * rc, (r + 1) * rc)
                t = s_scr[hh, rows, :] + bias
                if masked:
                    t = jnp.where(lax.broadcasted_iota(jnp.int32, (rc, tq), 1)
                                  <= r * rc + lax.broadcasted_iota(jnp.int32, (rc, tq), 0), t, NEG)
                m_old = m_scr[hh, rows, :]
                m_new = jnp.maximum(m_old, jnp.max(t, axis=-1, keepdims=True))
                alpha = jnp.exp2(m_old - m_new)
                p = jnp.exp2(t - jnp.tile(m_new, (1, tq // LANE)))
                l_scr[hh, rows, :] = alpha * l_scr[hh, rows, :] + jnp.sum(p, axis=-1, keepdims=True)
                a_scr[hh, rows, :] = alpha
                p_scr[hh, rows, :] = p.astype(BF16)
                m_scr[hh, rows, :] = m_new
            acc[hh] = a_scr[hh] * acc[hh] + _dot(p_scr[hh], v_ref[:, cols], NN)

        for hh in range(hb):
            live = skip_ref[pl.program_id(0) * hb + hh, p_id] == 0

            @pl.when((j < i) & live)
            def _():
                update(hh, False)

            @pl.when(j == i)
            def _():
                update(hh, True)
                o_ref[:, hh * DH:(hh + 1) * DH] = acc[hh] / l_scr[hh]
                lse_ref[hh] = (m_scr[hh, :, 0:1] + jnp.log2(l_scr[hh, :, 0:1])) + (cc_ref[hh] - cc_ref[hh, 0:1, :])

    wide = hb * DH
    qtile = lambda cb: pl.BlockSpec((tq, wide), lambda g, p, qi, kj, fk, sk, cb=cb: (qi[p], cb // hb + g))
    ktile = lambda cb: pl.BlockSpec((tq, wide), lambda g, p, qi, kj, fk, sk, cb=cb: (fk[g, p], cb // hb + g))
    qcol = pl.BlockSpec((hb, tq, 1), lambda g, p, qi, kj, fk, sk: (g, qi[p], 0))
    stat = pltpu.VMEM((hb, tq, LANE), F32)
    return pl.pallas_call(
        body, name=name,
        grid_spec=pltpu.PrefetchScalarGridSpec(
            num_scalar_prefetch=4, grid=(HEADS // hb, qi.shape[0]),
            in_specs=[qtile(FOX_Q_CB), ktile(FOX_K_CB), ktile(FOX_V_CB), qcol,
                      pl.BlockSpec((hb, 1, tq), lambda g, p, qi, kj, fk, sk: (g, 0, fk[g, p]))],
            out_specs=[qtile(0), qcol],
            scratch_shapes=[stat, stat, pltpu.VMEM((hb, tq, DH), F32), stat,
                            pltpu.VMEM((hb, tq, tq), F32), pltpu.VMEM((hb, tq, tq), BF16)]),
        out_shape=[jax.ShapeDtypeStruct((T, D_MODEL), F32), jax.ShapeDtypeStruct((HEADS, T, 1), F32)],
        compiler_params=_params("parallel", "arbitrary"),
    )(qi, kj, fetch_k, skip, z, z, z, c_col, c_row)


def _fox_bwd(z, c_col, c_row, o, lse, do, fetch_q, skip, *, name):
    T = z.shape[0]
    tq = _fox_tile(T)
    nq = T // tq
    rc = min(FOX_RC, tq)

    qi, kj = _fox_pairs(nq, by_query=False)

    def body(qi_ref, kj_ref, fq_ref, skip_ref, q_ref, k_ref, v_ref, cc_ref, cr_ref, o_ref, lse_ref, do_ref, dq_ref,
             dk_ref, dv_ref, dc_ref, dcq_ref, dk_acc, dv_acc, dc_acc, s_scr, dp_scr, p_scr, ds_scr, dcq_scr):
        p_id = pl.program_id(1)
        i, j = qi_ref[p_id], kj_ref[p_id]

        @pl.when(p_id == 0)
        def _():
            dq_ref[...] = jnp.zeros_like(dq_ref)
            dcq_scr[...] = jnp.zeros_like(dcq_scr)

        def update(hh, masked):
            cols = slice(hh * DH, (hh + 1) * DH)
            q2, k, dov = _fox_q2(q_ref[:, cols]), k_ref[:, cols], do_ref[:, cols]
            s_scr[hh] = _dot(q2, k, NT)
            dp_scr[hh] = _dot(dov, v_ref[:, cols], NT)
            crow = cr_ref[hh]
            csum = jnp.zeros((SUBLANE, tq), F32)
            wide = lambda col: jnp.tile(jnp.broadcast_to(col, (rc, LANE)), (1, tq // LANE))
            for r in range(tq // rc):
                rows = slice(r * rc, (r + 1) * rc)
                t = (s_scr[hh, rows, :] + wide(cc_ref[hh, rows, :] - lse_ref[hh, rows, :])) - crow
                if masked:
                    t = jnp.where(lax.broadcasted_iota(jnp.int32, (rc, tq), 1)
                                  <= r * rc + lax.broadcasted_iota(jnp.int32, (rc, tq), 0), t, NEG)
                p = jnp.exp2(t)
                delta = jnp.sum(do_ref[rows, cols] * o_ref[rows, cols], axis=-1, keepdims=True)
                ds = p * (dp_scr[hh, rows, :] - wide(delta))
                p_scr[hh, rows, :] = p.astype(BF16)
                ds_scr[hh, rows, :] = ds.astype(BF16)
                grows = pl.ds(pl.multiple_of(i * tq + r * rc, rc), rc)
                dcq_scr[hh, grows, :] += jnp.broadcast_to(jnp.sum(ds, axis=-1, keepdims=True), (rc, LANE))
                csum = csum + _fold8(ds)
            dsb = ds_scr[hh]
            dv_new = _dot(p_scr[hh], dov, TN)
            dk_new = _dot(dsb, q2, TN) * (1.0 / LOG2E)
            dc_new = -jnp.sum(csum, axis=0, keepdims=True)
            rows = pl.ds(pl.multiple_of(i * tq, tq), tq)
            dq_ref[rows, cols] += _dot(dsb, k, NN) * FOX_SCALE
            return dk_new, dv_new, dc_new

        for hh in range(FOX_HB):
            live = skip_ref[pl.program_id(0) * FOX_HB + hh, p_id] == 0

            @pl.when(i == j)
            def _():
                dk_new, dv_new, dc_new = update(hh, True)
                dk_acc[hh] = dk_new
                dv_acc[hh] = dv_new
                dc_acc[hh] = dc_new

            @pl.when((i > j) & live)
            def _():
                dk_new, dv_new, dc_new = update(hh, False)
                dk_acc[hh] += dk_new
                dv_acc[hh] += dv_new
                dc_acc[hh] += dc_new

            @pl.when(i == nq - 1)
            def _():
                dk_ref[:, hh * DH:(hh + 1) * DH] = dk_acc[hh].astype(BF16)
                dv_ref[:, hh * DH:(hh + 1) * DH] = dv_acc[hh].astype(BF16)
                dc_ref[hh] = dc_acc[hh]

            @pl.when(p_id == qi.shape[0] - 1)
            def _():
                for r in range(nq):
                    rows = slice(r * tq, (r + 1) * tq)
                    dcq_ref[hh, :, rows] = jnp.transpose(dcq_scr[hh, rows, :])[0:1, :]

    wide_cols = FOX_HB * DH
    n_groups = HEADS // FOX_HB
    qtile = lambda cb: pl.BlockSpec((tq, wide_cols), lambda g, p, qi, kj, fq, sk, cb=cb: (fq[g, p], cb // FOX_HB + g))
    ktile = lambda cb: pl.BlockSpec((tq, wide_cols), lambda g, p, qi, kj, fq, sk, cb=cb: (kj[p], cb // FOX_HB + g))
    qcol = pl.BlockSpec((FOX_HB, tq, 1), lambda g, p, qi, kj, fq, sk: (g, fq[g, p], 0))
    krow = pl.BlockSpec((FOX_HB, 1, tq), lambda g, p, qi, kj, fq, sk: (g, 0, kj[p]))
    tile_f32 = pltpu.VMEM((FOX_HB, tq, tq), F32)
    tile_bf16 = pltpu.VMEM((FOX_HB, tq, tq), BF16)
    return pl.pallas_call(
        body, name=name,
        grid_spec=pltpu.PrefetchScalarGridSpec(
            num_scalar_prefetch=4, grid=(n_groups, qi.shape[0]),
            in_specs=[qtile(FOX_Q_CB), ktile(FOX_K_CB), ktile(FOX_V_CB), qcol, krow, qtile(0), qcol, qtile(0)],
            out_specs=[pl.BlockSpec((T, wide_cols), lambda g, p, qi, kj, fq, sk: (0, g)), ktile(0), ktile(0), krow,
                       pl.BlockSpec((FOX_HB, 1, T), lambda g, p, qi, kj, fq, sk: (g, 0, 0))],
            scratch_shapes=[pltpu.VMEM((FOX_HB, tq, DH), F32), pltpu.VMEM((FOX_HB, tq, DH), F32),
                            pltpu.VMEM((FOX_HB, 1, tq), F32), tile_f32, tile_f32, tile_bf16, tile_bf16,
                            pltpu.VMEM((FOX_HB, T, LANE), F32)]),
        out_shape=[jax.ShapeDtypeStruct((T, D_MODEL), F32), jax.ShapeDtypeStruct((T, D_MODEL), BF16),
                   jax.ShapeDtypeStruct((T, D_MODEL), BF16), jax.ShapeDtypeStruct((HEADS, 1, T), F32),
                   jax.ShapeDtypeStruct((HEADS, 1, T), F32)],
        compiler_params=_params("parallel", "arbitrary"),
    )(qi, kj, fetch_q, skip, z, z, z, c_col, c_row, o, lse, do)


MEM_SCALE = 1.0 / math.sqrt(MEM_DH)


def _mem_probs(qh, kh):
    s = _dot(qh, kh, NT) * MEM_SCALE
    p = jnp.exp(s - jnp.max(s, axis=-1, keepdims=True))
    return p / jnp.sum(p, axis=-1, keepdims=True)


def _mem_fwd(q, kv, *, name, tq=512):
    T = q.shape[0]
    tq = min(tq, T)

    def body(q_ref, kv_ref, o_ref):
        for h in range(MEM_HEADS):
            cols = slice(h * MEM_DH, (h + 1) * MEM_DH)
            vcols = slice(D_MODEL + h * MEM_DH, D_MODEL + (h + 1) * MEM_DH)
            p = _mem_probs(q_ref[:, cols], kv_ref[:, cols])
            o_ref[:, cols] = _dot(p, kv_ref[:, vcols], NN).astype(o_ref.dtype)

    return pl.pallas_call(
        body, name=name, grid=(T // tq,),
        in_specs=[pl.BlockSpec((tq, D_MODEL), lambda i: (i, 0)), pl.BlockSpec((MEM_LEN, 2 * D_MODEL), lambda i: (0, 0))],
        out_specs=pl.BlockSpec((tq, D_MODEL), lambda i: (i, 0)), out_shape=jax.ShapeDtypeStruct((T, D_MODEL), BF16),
        compiler_params=_params("parallel"),
    )(q, kv)


def _mem_bwd(q, kv, do, *, name, tq=512):
    T = q.shape[0]
    tq = min(tq, T)

    def body(q_ref, kv_ref, do_ref, dq_ref, dkv_ref):
        @pl.when(pl.program_id(0) == 0)
        def _():
            dkv_ref[...] = jnp.zeros_like(dkv_ref)

        for h in range(MEM_HEADS):
            cols = slice(h * MEM_DH, (h + 1) * MEM_DH)
            vcols = slice(D_MODEL + h * MEM_DH, D_MODEL + (h + 1) * MEM_DH)
            qh, kh, doh = q_ref[:, cols], kv_ref[:, cols], do_ref[:, cols]
            p = _mem_probs(qh, kh)
            dp = _dot(doh, kv_ref[:, vcols], NT)
            ds = p * (dp - jnp.sum(p * dp, axis=-1, keepdims=True))
            dq_ref[:, cols] = (_dot(ds, kh, NN) * MEM_SCALE).astype(dq_ref.dtype)
            dkv_ref[:, cols] += _dot(ds, qh, TN) * MEM_SCALE
            dkv_ref[:, vcols] += _dot(p, doh, TN)

    row = pl.BlockSpec((tq, D_MODEL), lambda i: (i, 0))
    full = pl.BlockSpec((MEM_LEN, 2 * D_MODEL), lambda i: (0, 0))
    return pl.pallas_call(
        body, name=name, grid=(T // tq,), in_specs=[row, full, row], out_specs=[row, full],
        out_shape=[jax.ShapeDtypeStruct((T, D_MODEL), BF16), jax.ShapeDtypeStruct((MEM_LEN, 2 * D_MODEL), F32)],
        compiler_params=_params("arbitrary"),
    )(q, kv, do)


def _adamw(w, g, m, v, *, name, tm=256):
    R, C = w.shape
    tm = min(tm, R)
    assert R % tm == 0
    nsum = g.shape[0] if g.ndim == 3 else 0

    def body(w_ref, g_ref, m_ref, v_ref, go_ref, d_ref, mo_ref, vo_ref):
        if nsum:
            gv = g_ref[0]
            for n in range(1, nsum):
                gv = gv + g_ref[n]
        else:
            gv = g_ref[...]
        mv = ADAM_B1 * m_ref[...] + (1.0 - ADAM_B1) * gv
        vv = ADAM_B2 * v_ref[...] + (1.0 - ADAM_B2) * jnp.square(gv)
        m_hat = mv / (1.0 - ADAM_B1 ** ADAM_STEP)
        v_hat = vv / (1.0 - ADAM_B2 ** ADAM_STEP)
        d_ref[...] = -ADAM_LR * (m_hat / (jnp.sqrt(v_hat) + ADAM_EPS) + ADAM_WD * w_ref[...])
        go_ref[...] = gv
        mo_ref[...] = mv
        vo_ref[...] = vv

    row = pl.BlockSpec((tm, C), lambda i: (i, 0))
    gspec = pl.BlockSpec((nsum, tm, C), lambda i: (0, i, 0)) if nsum else row
    return pl.pallas_call(
        body, name=name, grid=(R // tm,), in_specs=[row, gspec, row, row], out_specs=[row] * 4,
        out_shape=[jax.ShapeDtypeStruct((R, C), F32)] * 4, compiler_params=_params("parallel"),
    )(w, g, m, v)


def _act_mm(a, w, name, out_dtype=F32, side=None):
    res = _mm([(a, w)], "nn", tm=1024, tn=512, tk=w.shape[0], out_dtypes=[out_dtype], name=name, side=side)
    return res if side is None else (res[0][0], res[1])


def _act_mm_t(a, w, name, out_dtype=F32, side=None):
    res = _mm([(a, w)], "nt", tm=1024, tn=512, tk=1024, out_dtypes=[out_dtype], name=name, side=side)
    return res if side is None else (res[0][0], res[1])


def _wgrad(a, dy, name, tm=1024, side=None):
    tn = D_MODEL if dy.shape[1] % D_MODEL == 0 else D_FF // 2
    res = _mm([(a, dy)], "tn", tm=tm, tn=tn, tk=1024, out_dtypes=[F32], name=name, side=side)
    return res if side is None else (res[0][0], res[1])


def _colsum8(p):
    return jnp.sum(p, axis=0, keepdims=True)


def _ffn_fwd(x, pre_g, post_g, wg, wu, wd, tag, gather_beside=None):
    h = _rms_fwd(x, pre_g, out_dtype=BF16, name=tag + "_pre")
    down = functools.partial(_mm, mode="nn", tm=1024, tn=512, tk=D_FF, out_dtypes=[F32], name=tag + "_down")
    gathered = None
    if gather_beside is None:
        act, gate, up = _ffn_in(h, wg, wu, name=tag + "_in")
        d = down([(act, wd)])
    else:
        (act, gate, up), landed = _ffn_in(h, wg, wu, name=tag + "_in", side=_x_gather(gather_beside))
        (d,), gathered = down([(act, wd)], side=_x_forward(landed))
    xo = _rms_fwd(d, post_g, out_dtype=F32, name=tag + "_post", res=x, coeff=0.5)
    return xo, (h, act, gate, up, d), gathered


def _ffn_bwd(x, dxo, saved, pre_g, post_g, wg, wu, wd, tag, reduce_beside=None):
    h, act, gate, up, d = saved
    dd, dg_post = _rms_bwd(d, post_g, dxo, name=tag + "_post_b", coeff=0.5, dx_dtype=BF16)
    act_b = functools.partial(_mm, [(dd, wd)], "nt", tm=1024, tn=256, tk=D_MODEL, out_dtypes=[BF16, BF16],
                              name=tag + "_act_b", epilogue=_swiglu_bwd_epilogue, tiles=(gate, up))
    in_b = lambda dgate, dup, **kw: _mm([(dgate, wg), (dup, wu)], "nt", tm=512, tn=512, tk=D_FF, out_dtypes=[F32],
                                        name=tag + "_in_b", **kw)
    dwd_of = functools.partial(_wgrad, act, dd, tag + "_dwd", tm=D_FF // 2)
    reduced = None
    if reduce_beside is None:
        dgate, dup = act_b()
        dh = in_b(dgate, dup)
        dwd = dwd_of()
    else:
        grads, place, core = reduce_beside
        (dgate, dup), swapped = act_b(side=_x_swap(grads))
        pbf, own = _pair_sums(grads, swapped, place, "mid")
        dwd, landed_rows = dwd_of(side=_x_scatter(pbf[:1]))
    dwg = _wgrad(h, dgate, tag + "_dwg")
    dwu = _wgrad(h, dup, tag + "_dwu")
    if reduce_beside is not None:
        mine = _grad_slabs(dict(f1g=dwg, f1u=dwu, f1d=dwd), GROUP_FFN1)
        scatter = _x_scatter(pbf[1:])
        (dh,), landed = in_b(dgate, dup, side=_both(scatter, _x_swap(mine, base=scatter.nsem)))
        reduced_other = _finish_reduce(own, landed_rows + landed[:1], core, "mid")
        pbf, own = _pair_sums(mine, landed[1:], place, "late")
        reduced = (reduced_other, _finish_reduce(own, _run(_x_scatter(pbf), "scatter_partials_late"), core, "late"))
    dx, dg_pre = _rms_bwd(x, pre_g, dh, name=tag + "_pre_b", add=dxo)
    return dx, dict(pre_g=_colsum8(dg_pre), post_g=_colsum8(dg_post), wg=dwg, wu=dwu, wd=dwd), reduced


def _local_step(x, mem, target, P, own, me, place, core):
    T = x.shape[0]
    G = {}
    logits = P["hg_lb_logits"]
    lb = _sigmoid(logits[0] - logits[1])
    lb_row = lb.reshape(1, D_MODEL)
    fbias_row = jnp.pad(P["fox_f_bias"], ((0, 0), (0, LANE - HEADS)))
    arrived = lambda group, others: _local_names(_assemble(group, own[group], others, me))

    W = arrived(GROUP_FFN1, _run(_x_forward(_run(_x_gather(own[GROUP_FFN1]), "gather_ffn1")), "forward_ffn1"))
    x1, ffn1_saved, others = _ffn_fwd(x, P["ffn1_pre_g"], P["ffn1_post_g"], W["f1g"], W["f1u"], W["f1d"], "ffn1",
                                      gather_beside=own[GROUP_MID])
    W.update(arrived(GROUP_MID, others))
    h2 = _rms_fwd(x1, P["mix_pre_g"], out_dtype=BF16, name="mix_pre")
    z, landed = _act_mm(h2, W["w_main"], "mix_in", side=_x_gather(own[GROUP_FFN2]))
    zfb = _mm([(h2, W["w_fb"])], "nn", tm=1024, tn=LANE, tk=D_MODEL, out_dtypes=[F32], name="mix_in_fb")
    oa_pre, states = _hgrn2_fwd(z, lb_row, name="hgrn2_f")
    o_a = _rms_fwd(oa_pre, P["hg_norm_g"], out_dtype=BF16, name="hgrn2_post", mul=(z, 3))
    y_a, others = _act_mm(o_a, W["wa"], "branch_a", side=_x_forward(landed))
    W.update(arrived(GROUP_FFN2, others))
    c = _cumsum_t([(zfb, 0)], name="fox_c", width=LANE, rows=[(fbias_row, 0)], pre=lambda v, r: _logsigmoid(v + r),
                  post=lambda cum, v, r: (cum * LOG2E,))
    ct = c[:, :HEADS].T
    c_col, c_row = ct[:, :, None], ct[:, None, :]
    fetch_k, skip_f, fetch_q, skip_b = _fox_schedule(*_fox_norms(z, name="fox_norms"), ct, _fox_tile(T))
    o_b, lse = _fox_fwd(z, c_col, c_row, fetch_k, skip_f, name="fox_f")
    y_b = _act_mm(o_b, W["wb"], "branch_b")
    y = _gatemix_fwd(z, P["b_gate"], y_a, y_b, name="gatemix")
    m = _act_mm(y, W["wo"], "mix_out")
    x2 = _rms_fwd(m, P["mix_post_g"], out_dtype=F32, name="mix_post", res=x1)
    h3 = _rms_fwd(x2, P["mem_pre_g"], out_dtype=BF16, name="mem_pre")
    mem_n = _rms_fwd(mem, P["mem_kv_g"], out_dtype=BF16, name="mem_kvn")
    qm = _act_mm(h3, W["wmq"], "mem_q")
    kv = _act_mm(mem_n, W["wmkv"], "mem_kv")
    om = _mem_fwd(qm, kv, name="mem_attn")
    mo = _act_mm(om, W["wmo"], "mem_o")
    x3 = _rms_fwd(mo, P["mem_post_g"], out_dtype=F32, name="mem_post", res=x2)
    x4, ffn2_saved, _ = _ffn_fwd(x3, P["ffn2_pre_g"], P["ffn2_post_g"], W["f2g"], W["f2u"], W["f2d"], "ffn2")
    dx4, sq = _loss_head(x4, target, name="loss_head")

    dx3, g, _ = _ffn_bwd(x3, dx4, ffn2_saved, P["ffn2_pre_g"], P["ffn2_post_g"], W["f2g"], W["f2u"], W["f2d"], "ffn2")
    G.update(ffn2_pre_g=g["pre_g"], ffn2_post_g=g["post_g"], f2g=g["wg"], f2u=g["wu"], f2d=g["wd"])

    dmo, dgp = _rms_bwd(mo, P["mem_post_g"], dx3, name="mem_post_b", dx_dtype=BF16)
    G["mem_post_g"] = _colsum8(dgp)
    g_ffn2 = _grad_slabs(G, GROUP_FFN2)
    dom, swapped = _act_mm_t(dmo, W["wmo"], "mem_o_b", BF16, side=_x_swap(g_ffn2))
    pbf_ffn2, own_ffn2 = _pair_sums(g_ffn2, swapped, place, "ffn2")
    G["wmo"] = _wgrad(om, dmo, "mem_o_w")
    dqm, dkv = _mem_bwd(qm, kv, dom, name="mem_attn_b")
    dh3 = _act_mm_t(dqm, W["wmq"], "mem_q_b")
    G["wmq"] = _wgrad(h3, dqm, "mem_q_w")
    G["wmkv"] = _mm([(mem_n, dkv)], "tn", tm=1024, tn=512, tk=MEM_LEN, out_dtypes=[F32], name="mem_kv_w")
    dmem_n = _mm([(dkv, W["wmkv"])], "nt", tm=MEM_LEN, tn=512, tk=2 * D_MODEL, out_dtypes=[F32], name="mem_kv_b")
    _, dgp = _rms_bwd(mem, P["mem_kv_g"], dmem_n, name="mem_kvn_b")
    G["mem_kv_g"] = _colsum8(dgp)
    dx2, dgp = _rms_bwd(x2, P["mem_pre_g"], dh3, name="mem_pre_b", add=dx3)
    G["mem_pre_g"] = _colsum8(dgp)

    dm, dgp = _rms_bwd(m, P["mix_post_g"], dx2, name="mix_post_b", dx_dtype=BF16)
    G["mix_post_g"] = _colsum8(dgp)
    dy = _act_mm_t(dm, W["wo"], "mix_out_b")
    G["wo"] = _wgrad(y, dm, "mix_out_w")
    dya, dyb, dz0, dz1, s0, s1 = _gatemix_bwd(z, P["b_gate"], y_a, y_b, dy, name="gatemix_b")
    G["b_gate"] = jnp.concatenate([_colsum8(s0), _colsum8(s1)], axis=1)
    do_a = _act_mm_t(dya, W["wa"], "branch_a_b")
    G["wa"] = _wgrad(o_a, dya, "branch_a_w")
    do_b = _act_mm_t(dyb, W["wb"], "branch_b_b")
    G["wb"] = _wgrad(o_b, dyb, "branch_b_w")
    doa_pre, dgp, dga = _rms_bwd(oa_pre, P["hg_norm_g"], do_a, name="hgrn2_post_b", mul=(z, 3))
    G["hg_norm_g"] = _colsum8(dgp)
    dq_a, dfl_a, di_a, dlb = _hgrn2_bwd(z, lb_row, states, doa_pre, name="hgrn2_b")
    dl0 = (dlb * lb_row * (1.0 - lb_row)).reshape(1, HEADS, DH)
    G["hg_lb_logits"] = jnp.concatenate([dl0, -dl0], axis=0)
    dq_b, dk_b, dv_b, dcr, dcq = _fox_bwd(z, c_col, c_row, o_b, lse, do_b, fetch_q, skip_b, name="fox_b")
    dc_pad = jnp.pad((dcr[:, 0, :] + dcq[:, 0, :]).T, ((0, 0), (0, LANE - HEADS)))
    gate_b = lambda cum, dc, zf, r: cum * _sigmoid(-(zf + r))
    dfl_b, dfb = _cumsum_t([(dc_pad, 0), (zfb, 0)], name="fox_c_b", width=LANE, reverse=True, rows=[(fbias_row, 0)],
                           pre=lambda dc, zf, r: dc, post=lambda *a: (gate_b(*a),), fold=gate_b)
    G["fox_f_bias"] = _colsum8(dfb)[:, :HEADS]

    pieces = [dq_a, dfl_a, di_a, dga, dq_b, dk_b, dv_b, dz0, dz1]
    dh2 = _mm([(dfl_b, W["w_fb"])], "nt", tm=512, tn=D_MODEL, tk=LANE, out_dtypes=[F32], name="mix_in_fb_b")
    for lo, hi in ((0, 5), (5, 9)):
        res = _mm([(p, W["w_main"]) for p in pieces[lo:hi]], "nt", tm=512, tn=D_MODEL, tk=D_MODEL, out_dtypes=[F32],
                  name=f"mix_in_b{lo}", b_koff=[n * D_MODEL for n in range(lo, hi)],
                  epilogue=lambda acc, t: (acc + t,), tiles=(dh2,), side=_x_scatter(pbf_ffn2) if lo == 0 else None)
        dh2, scattered = (res[0][0], res[1]) if lo == 0 else (res, scattered)
    reduced_ffn2 = _finish_reduce(own_ffn2, scattered, core, "ffn2")
    G["w_main"] = [_wgrad(h2, p, f"mix_in_w{n}") for n, p in enumerate(pieces)]
    G["w_fb"] = _mm([(h2, dfl_b)], "tn", tm=1024, tn=LANE, tk=512, out_dtypes=[F32], name="mix_in_fb_w")
    dx1, dgp = _rms_bwd(x1, P["mix_pre_g"], dh2, name="mix_pre_b", add=dx2)
    G["mix_pre_g"] = _colsum8(dgp)

    dx0, g, reduced = _ffn_bwd(x, dx1, ffn1_saved, P["ffn1_pre_g"], P["ffn1_post_g"], W["f1g"], W["f1u"], W["f1d"],
                               "ffn1", reduce_beside=(_grad_slabs(G, GROUP_MID), place, core))
    G.update(ffn1_pre_g=g["pre_g"], ffn1_post_g=g["post_g"], f1g=g["wg"], f1u=g["wu"], f1d=g["wd"])
    return sq, dx0, G, {GROUP_FFN2: reduced_ffn2, GROUP_MID: reduced[0], GROUP_FFN1: reduced[1]}


N_CHIP = 4
N_DEV = 8
IN_COLS = 9224
FB_COL = 7 * D_MODEL
SHARDED = (
    ("ffn1_w_in", (D_MODEL, 2 * D_FF), 1), ("ffn1_w_down", (D_FF, D_MODEL), 0), ("w_in", (D_MODEL, IN_COLS), 1),
    ("w_branch_a", (D_MODEL, D_MODEL), 0), ("w_branch_b", (D_MODEL, D_MODEL), 0), ("w_out", (D_MODEL, D_MODEL), 0),
    ("w_mq", (D_MODEL, D_MODEL), 0), ("w_mkv", (D_MODEL, 2 * D_MODEL), 1), ("w_mo", (D_MODEL, D_MODEL), 0),
    ("ffn2_w_in", (D_MODEL, 2 * D_FF), 1), ("ffn2_w_down", (D_FF, D_MODEL), 0),
)
SMALL = ("ffn1_pre_g", "ffn1_post_g", "mix_pre_g", "hg_norm_g", "mix_post_g", "mem_pre_g", "mem_kv_g", "mem_post_g",
         "ffn2_pre_g", "ffn2_post_g", "b_gate", "hg_lb_logits", "fox_f_bias")
SMALL_SHAPES = dict(b_gate=(1, 2 * D_MODEL), hg_lb_logits=(2, HEADS, DH), fox_f_bias=(1, HEADS))
SMALL_ROWS = 16
WEIGHT_ORDER = ("ffn1_pre_g", "ffn1_w_in", "ffn1_w_down", "ffn1_post_g", "mix_pre_g", "w_in", "hg_lb_logits", "hg_norm_g",
                "fox_f_bias", "w_branch_a", "w_branch_b", "b_gate", "w_out", "mix_post_g", "mem_pre_g", "mem_kv_g", "w_mq",
                "w_mkv", "w_mo", "mem_post_g", "ffn2_pre_g", "ffn2_w_in", "ffn2_w_down", "ffn2_post_g")


GROUP_FFN1 = ("ffn1_w_in", "ffn1_w_down")
GROUP_MID = ("w_in", "w_branch_a", "w_branch_b", "w_out", "w_mq", "w_mkv", "w_mo")
GROUP_FFN2 = ("ffn2_w_in", "ffn2_w_down")


def _layout(names, axis):
    out, at = [], 0
    for name, shape, ax in SHARDED:
        if ax == axis and name in names:
            n = shape[ax] // N_CHIP
            out.append((name, at, n))
            at += n if axis == 0 else -(-n // LANE) * LANE
    return out


def _pack(shards, names, dtype):
    rows = jnp.concatenate([shards[name].astype(dtype) for name, _, _ in _layout(names, 0)], axis=0)
    cols = [jnp.pad(shards[name].astype(dtype), ((0, 0), (0, -n % LANE))) for name, _, n in _layout(names, 1)]
    return [rows, jnp.concatenate(cols, axis=1)]


def _unpack(slabs, names):
    rows, cols = slabs
    out = {name: rows[at:at + n] for name, at, n in _layout(names, 0)}
    out.update({name: cols[:, at:at + n] for name, at, n in _layout(names, 1)})
    return out


def _pack_small(vals):
    rows = []
    for name in SMALL:
        v = vals[name].astype(F32).reshape(-1)
        rows.append(jnp.pad(v, (0, -v.shape[0] % D_MODEL)).reshape(-1, D_MODEL))
    rows = jnp.concatenate(rows, axis=0)
    return jnp.pad(rows, ((0, SMALL_ROWS - rows.shape[0]), (0, 0)))


def _unpack_small(slab):
    out, r = {}, 0
    for name in SMALL:
        shape = SMALL_SHAPES.get(name, (1, D_MODEL))
        size = math.prod(shape)
        n = -(-size // D_MODEL)
        out[name] = slab[r:r + n].reshape(-1)[:size].reshape(shape)
        r += n
    return out


MESH = pl.DeviceIdType.MESH
CHIP_FLIPS = ((0, 1), (1, 0), (1, 1))


def _place():
    x, y, c = lax.axis_index("x"), lax.axis_index("y"), lax.axis_index("c")
    chips = [(x ^ fx, y ^ fy) for fx, fy in CHIP_FLIPS]
    return x, y, c, chips


def _remote(src, dst, sems, k, dev):
    return pltpu.make_async_remote_copy(src_ref=src, dst_ref=dst, send_sem=sems[0].at[k], recv_sem=sems[1].at[k],
                                        device_id=dev, device_id_type=MESH)


def _exchange(copies, arrays, out_shapes, aliases=None):
    def start(ins, outs, sems):
        for sent, _ in copies(ins, outs, sems):
            sent.start()

    def wait(ins, outs, sems):
        pairs = copies(ins, outs, sems)
        for _, got in pairs:
            got.wait_recv()
        for sent, _ in pairs:
            sent.wait_send()

    return _Side(arrays, out_shapes, copies.count, start, wait, aliases)


def _counted(count):
    def mark(fn):
        fn.count = count
        return fn
    return mark


def _slab_halves(c, rows):
    half = rows // 2
    return pl.ds(c * half, half), pl.ds((1 - c) * half, half)


def _x_gather(slabs):
    @_counted(3 * len(slabs))
    def copies(ins, outs, sems):
        x, y, c, chips = _place()
        res = []
        for s, slab in enumerate(slabs):
            mine, _ = _slab_halves(c, slab.shape[0])
            for k, (px, py) in enumerate(chips):
                res.append((_remote(ins[s].at[mine], outs[s].at[k, mine], sems, 3 * s + k, (px, py, c)),) * 2)
        return res

    return _exchange(copies, slabs, [jax.ShapeDtypeStruct((3,) + s.shape, s.dtype) for s in slabs])


def _x_forward(gathered):
    @_counted(3 * len(gathered))
    def copies(ins, outs, sems):
        x, y, c, _ = _place()
        res = []
        for s, buf in enumerate(gathered):
            mine, theirs = _slab_halves(c, buf.shape[1])
            for k in range(3):
                res.append((_remote(ins[s].at[k, mine], outs[s].at[k, mine], sems, 3 * s + k, (x, y, 1 - c)),
                            _remote(ins[s].at[k, theirs], outs[s].at[k, theirs], sems, 3 * s + k, (x, y, 1 - c))))
        return res

    return _exchange(copies, gathered, [jax.ShapeDtypeStruct(g.shape, g.dtype) for g in gathered],
                     aliases={s: s for s in range(len(gathered))})


def _x_swap(grads, base=0):
    @_counted(base + N_CHIP * len(grads))
    def copies(ins, outs, sems):
        x, y, c, _ = _place()
        res = []
        for s, g in enumerate(grads):
            _, theirs = _slab_halves(c, g.shape[1])
            for j in range(N_CHIP):
                res.append((_remote(ins[s].at[j, theirs], outs[s].at[j], sems, base + N_CHIP * s + j,
                                    (x, y, 1 - c)),) * 2)
        return res

    return _exchange(copies, grads, [jax.ShapeDtypeStruct((N_CHIP, g.shape[1] // 2, g.shape[2]), g.dtype) for g in grads])


def _both(first, second):
    na, no = len(first.arrays), len(first.out_shapes)

    def start(ins, outs, sems):
        first.start(ins[:na], outs[:no], sems)
        second.start(ins[na:], outs[no:], sems)

    def wait(ins, outs, sems):
        first.wait(ins[:na], outs[:no], sems)
        second.wait(ins[na:], outs[no:], sems)

    assert not first.aliases and not second.aliases
    return _Side(first.arrays + second.arrays, first.out_shapes + second.out_shapes, second.nsem, start, wait)


def _x_scatter(partials):
    @_counted(3 * len(partials))
    def copies(ins, outs, sems):
        x, y, c, chips = _place()
        res = []
        for s in range(len(partials)):
            for k, (px, py) in enumerate(chips):
                res.append((_remote(ins[s].at[2 * px + py], outs[s].at[k], sems, 3 * s + k, (px, py, c)),) * 2)
        return res

    return _exchange(copies, partials, [jax.ShapeDtypeStruct((3,) + p.shape[1:], p.dtype) for p in partials])


def _x_join(halves):
    @_counted(len(halves))
    def copies(ins, outs, sems):
        x, y, c, _ = _place()
        return [(_remote(ins[s], outs[s], sems, s, (x, y, 1 - c)),) * 2 for s in range(len(halves))]

    return _exchange(copies, halves, [jax.ShapeDtypeStruct(h.shape, h.dtype) for h in halves])


def _run(side, name):
    n_in, n_out = len(side.arrays), len(side.out_shapes)

    def body(*refs):
        ins, outs, sems = refs[:n_in], refs[n_in:n_in + n_out], refs[-2:]
        side.start(ins, outs, sems)
        side.wait(ins, outs, sems)

    plumb = side.plumb(0, 0)
    return pl.pallas_call(
        body, name=name, in_specs=plumb["in_specs"], out_specs=plumb["out_specs"], out_shape=side.out_shapes,
        scratch_shapes=plumb["scratch"], input_output_aliases=plumb["aliases"],
    )(*side.arrays)


def _pair_sum(g, got, place, tag, *, tm):
    _, half, width = got.shape
    nb = half // tm

    def body(s_ref, g_ref, a_ref, bf_ref, own_ref):
        v = g_ref[...] + a_ref[...]
        bf_ref[...] = v.astype(BF16)

        @pl.when(pl.program_id(1) == s_ref[0])
        def _():
            own_ref[...] = v

    return pl.pallas_call(
        body, name="pair_sum_" + tag,
        grid_spec=pltpu.PrefetchScalarGridSpec(
            num_scalar_prefetch=1, grid=(nb, N_CHIP),
            in_specs=[pl.BlockSpec((None, tm, width), lambda i, j, s: (j, s[1] * nb + i, 0)),
                      pl.BlockSpec((None, tm, width), lambda i, j, s: (j, i, 0))],
            out_specs=[pl.BlockSpec((None, tm, width), lambda i, j, s: (j, i, 0)),
                       pl.BlockSpec((tm, width), lambda i, j, s: (i, 0))]),
        out_shape=[jax.ShapeDtypeStruct((N_CHIP, half, width), BF16), jax.ShapeDtypeStruct((half, width), F32)],
        compiler_params=_params("arbitrary", "arbitrary"),
    )(place, g, got)


def _chip_sum(own, got, tag, *, tm):
    half, width = own.shape

    def body(o_ref, g_ref, r_ref):
        r_ref[...] = ((o_ref[...] + g_ref[0].astype(F32)) + g_ref[1].astype(F32)) + g_ref[2].astype(F32)

    row = pl.BlockSpec((tm, width), lambda i: (i, 0))
    return pl.pallas_call(
        body, name="chip_sum_" + tag, grid=(half // tm,),
        in_specs=[row, pl.BlockSpec((3, tm, width), lambda i: (0, i, 0))], out_specs=row,
        out_shape=jax.ShapeDtypeStruct((half, width), F32), compiler_params=_params("parallel"),
    )(own, got)


def _sum_tiles(slabs):
    return [slabs[0].shape[1] // 4, D_MODEL // 8]


def _pair_sums(grads, swapped, place, tag):
    res = [_pair_sum(g, s, place, f"{tag}_{n}", tm=tm) for n, (g, s, tm) in enumerate(zip(grads, swapped, _sum_tiles(grads)))]
    return [r[0] for r in res], [r[1] for r in res]


def _finish_reduce(own, scattered, core, tag):
    mine = [_chip_sum(o, s, f"{tag}_{n}", tm=o.shape[0] // 2) for n, (o, s) in enumerate(zip(own, scattered))]
    theirs = _run(_x_join(mine), "join_halves_" + tag)
    return [lax.dynamic_update_slice(jnp.concatenate([a, a]), b, ((1 - core) * a.shape[0], 0))
            for a, b in zip(mine, theirs)]


def _gather_small(s):
    flips = [(fx, fy, fc) for fx in (0, 1) for fy in (0, 1) for fc in (0, 1)][1:]

    def body(s_ref, out_ref, send_sems, recv_sems, local_sem):
        x, y, c, _ = _place()
        sems = (send_sems, recv_sems)
        me = 4 * x + 2 * y + c
        local = pltpu.make_async_copy(s_ref, out_ref.at[me], local_sem)
        local.start()
        sent = [_remote(s_ref, out_ref.at[me], sems, k, (x ^ fx, y ^ fy, c ^ fc)) for k, (fx, fy, fc) in enumerate(flips)]
        for cp in sent:
            cp.start()
        for k, (fx, fy, fc) in enumerate(flips):
            peer = (x ^ fx, y ^ fy, c ^ fc)
            _remote(s_ref, out_ref.at[4 * peer[0] + 2 * peer[1] + peer[2]], sems, k, peer).wait_recv()
        for cp in sent:
            cp.wait_send()
        local.wait()

    return pl.pallas_call(
        body, name="gather_small", out_shape=jax.ShapeDtypeStruct((N_DEV, SMALL_ROWS, D_MODEL), s.dtype),
        in_specs=[ANY], out_specs=ANY,
        scratch_shapes=[pltpu.SemaphoreType.DMA((7,)), pltpu.SemaphoreType.DMA((7,)), pltpu.SemaphoreType.DMA],
    )(s)


LOCAL_NAMES = dict(ffn1_w_down="f1d", ffn2_w_down="f2d", w_branch_a="wa", w_branch_b="wb", w_out="wo", w_mq="wmq",
                   w_mkv="wmkv", w_mo="wmo")


def _assemble(names, own, others, me):
    by_flip = [jnp.concatenate([o[None], t], axis=0) for o, t in zip(own, others)]
    per_chip = [_unpack([lax.dynamic_index_in_dim(s, j ^ me, 0, keepdims=False) for s in by_flip], names)
                for j in range(N_CHIP)]
    return {name: jnp.concatenate([pc[name] for pc in per_chip], axis=axis)
            for name, _, axis in SHARDED if name in names}


def _local_names(full):
    out = {LOCAL_NAMES[name]: a for name, a in full.items() if name in LOCAL_NAMES}
    for name, key in (("ffn1_w_in", "f1"), ("ffn2_w_in", "f2")):
        if name in full:
            out[key + "g"], out[key + "u"] = full[name][:, :D_FF], full[name][:, D_FF:]
    if "w_in" in full:
        w_in = full["w_in"]
        out["w_main"] = jnp.concatenate([w_in[:, :FB_COL], w_in[:, FB_COL + HEADS:]], axis=1)
        out["w_fb"] = jnp.pad(w_in[:, FB_COL:FB_COL + HEADS], ((0, 0), (0, LANE - HEADS)))
    return out


def _grad_slabs(G, names):
    full = {name: G[key] for name, key in LOCAL_NAMES.items() if name in names}
    for name, key in (("ffn1_w_in", "f1"), ("ffn2_w_in", "f2")):
        if name in names:
            full[name] = jnp.concatenate([G[key + "g"], G[key + "u"]], axis=1)
    if "w_in" in names:
        main = jnp.concatenate(G["w_main"], axis=1)
        full["w_in"] = jnp.concatenate([main[:, :FB_COL], G["w_fb"][:, :HEADS], main[:, FB_COL:]], axis=1)
    rows, cols = [], []
    for j in range(N_CHIP):
        shards = {}
        for name, shape, axis in SHARDED:
            if name in names:
                n = shape[axis] // N_CHIP
                shards[name] = lax.slice_in_dim(full[name], j * n, (j + 1) * n, axis=axis)
        r, c = _pack(shards, names, F32)
        rows.append(r)
        cols.append(c)
    return [jnp.stack(rows, axis=0), jnp.stack(cols, axis=0)]


def kernel(x, mem, ffn1_pre_g, ffn1_w_in, ffn1_w_down, ffn1_post_g, mix_pre_g, w_in, hg_lb_logits, hg_norm_g, fox_f_bias, w_branch_a, w_branch_b, b_gate, w_out, mix_post_g, mem_pre_g, mem_kv_g, w_mq, w_mkv, w_mo, mem_post_g, ffn2_pre_g, ffn2_w_in, ffn2_w_down, ffn2_post_g, loss_target, m_ffn1_pre_g, m_ffn1_w_in, m_ffn1_w_down, m_ffn1_post_g, m_mix_pre_g, m_w_in, m_hg_lb_logits, m_hg_norm_g, m_fox_f_bias, m_w_branch_a, m_w_branch_b, m_b_gate, m_w_out, m_mix_post_g, m_mem_pre_g, m_mem_kv_g, m_w_mq, m_w_mkv, m_w_mo, m_mem_post_g, m_ffn2_pre_g, m_ffn2_w_in, m_ffn2_w_down, m_ffn2_post_g, v_ffn1_pre_g, v_ffn1_w_in, v_ffn1_w_down, v_ffn1_post_g, v_mix_pre_g, v_w_in, v_hg_lb_logits, v_hg_norm_g, v_fox_f_bias, v_w_branch_a, v_w_branch_b, v_b_gate, v_w_out, v_mix_post_g, v_mem_pre_g, v_mem_kv_g, v_w_mq, v_w_mkv, v_w_mo, v_mem_post_g, v_ffn2_pre_g, v_ffn2_w_in, v_ffn2_w_down, v_ffn2_post_g):
    w = dict(ffn1_pre_g=ffn1_pre_g, ffn1_w_in=ffn1_w_in, ffn1_w_down=ffn1_w_down, ffn1_post_g=ffn1_post_g, mix_pre_g=mix_pre_g, w_in=w_in, hg_lb_logits=hg_lb_logits, hg_norm_g=hg_norm_g, fox_f_bias=fox_f_bias, w_branch_a=w_branch_a, w_branch_b=w_branch_b, b_gate=b_gate, w_out=w_out, mix_post_g=mix_post_g, mem_pre_g=mem_pre_g, mem_kv_g=mem_kv_g, w_mq=w_mq, w_mkv=w_mkv, w_mo=w_mo, mem_post_g=mem_post_g, ffn2_pre_g=ffn2_pre_g, ffn2_w_in=ffn2_w_in, ffn2_w_down=ffn2_w_down, ffn2_post_g=ffn2_post_g)
    m = dict(ffn1_pre_g=m_ffn1_pre_g, ffn1_w_in=m_ffn1_w_in, ffn1_w_down=m_ffn1_w_down, ffn1_post_g=m_ffn1_post_g, mix_pre_g=m_mix_pre_g, w_in=m_w_in, hg_lb_logits=m_hg_lb_logits, hg_norm_g=m_hg_norm_g, fox_f_bias=m_fox_f_bias, w_branch_a=m_w_branch_a, w_branch_b=m_w_branch_b, b_gate=m_b_gate, w_out=m_w_out, mix_post_g=m_mix_post_g, mem_pre_g=m_mem_pre_g, mem_kv_g=m_mem_kv_g, w_mq=m_w_mq, w_mkv=m_w_mkv, w_mo=m_w_mo, mem_post_g=m_mem_post_g, ffn2_pre_g=m_ffn2_pre_g, ffn2_w_in=m_ffn2_w_in, ffn2_w_down=m_ffn2_w_down, ffn2_post_g=m_ffn2_post_g)
    v = dict(ffn1_pre_g=v_ffn1_pre_g, ffn1_w_in=v_ffn1_w_in, ffn1_w_down=v_ffn1_w_down, ffn1_post_g=v_ffn1_post_g, mix_pre_g=v_mix_pre_g, w_in=v_w_in, hg_lb_logits=v_hg_lb_logits, hg_norm_g=v_hg_norm_g, fox_f_bias=v_fox_f_bias, w_branch_a=v_w_branch_a, w_branch_b=v_w_branch_b, b_gate=v_b_gate, w_out=v_w_out, mix_post_g=v_mix_post_g, mem_pre_g=v_mem_pre_g, mem_kv_g=v_mem_kv_g, w_mq=v_w_mq, w_mkv=v_w_mkv, w_mo=v_w_mo, mem_post_g=v_mem_post_g, ffn2_pre_g=v_ffn2_pre_g, ffn2_w_in=v_ffn2_w_in, ffn2_w_down=v_ffn2_w_down, ffn2_post_g=v_ffn2_post_g)
    sharded = [name for name, _, _ in SHARDED]
    shard_of = lambda d: {name: d[name][0] for name in sharded}

    me, core = 2 * lax.axis_index("x") + lax.axis_index("y"), lax.axis_index("c")
    place = jnp.stack([me, core]).astype(jnp.int32)
    own = {group: _pack(shard_of(w), group, BF16) for group in (GROUP_FFN1, GROUP_MID, GROUP_FFN2)}
    P = {name: w[name] for name in SMALL}

    sq, dx0, G, reduced = _local_step(x[0], mem[0], loss_target[0], P, own, me, place, core)
    loss = lax.psum(0.5 * jnp.sum(sq) / D_MODEL, ("x", "y", "c"))

    g_shards = {}
    for group, slabs in reduced.items():
        g_shards.update(_unpack(slabs, group))
    big = {}
    for name, shape, axis in SHARDED:
        rows = shape[0] // (N_CHIP if axis == 0 else 1)
        big[name] = _adamw(w[name][0], g_shards[name], m[name][0], v[name][0], name="adamw_" + name, tm=rows // 8)
    small = _adamw(_pack_small(w), _gather_small(_pack_small(G)), _pack_small(m), _pack_small(v), name="adamw_small",
                   tm=SMALL_ROWS)

    outs = [loss, dx0[None]]
    for n in range(4):
        vals = {name: res[n][None] for name, res in big.items()}
        vals.update(_unpack_small(small[n]))
        outs += [vals[name] for name in WEIGHT_ORDER]
    return tuple(outs)
```

```python
import functools
import math

import jax
import jax.numpy as jnp
from jax import lax
from jax.experimental import pallas as pl
from jax.experimental.pallas import tpu as pltpu

F32 = jnp.float32
BF16 = jnp.bfloat16

D_MODEL = 1024
D_FF = 2816
HEADS = 8
DH = 128
MEM_HEADS = 4
MEM_DH = 256
MEM_LEN = 256
EPS = 1e-6
SUB = 16
LANE = 128
SUBLANE = 8
VMEM_LIMIT = 56 * 1024 * 1024

ADAM_LR = 0.001
ADAM_B1 = 0.9
ADAM_B2 = 0.999
ADAM_EPS = 1e-08
ADAM_WD = 0.01
ADAM_STEP = 10

HIGHEST = lax.Precision.HIGHEST


def _params(*sem):
    return pltpu.CompilerParams(dimension_semantics=sem, vmem_limit_bytes=VMEM_LIMIT)


def _sigmoid(v):
    return 0.5 * jnp.tanh(0.5 * v) + 0.5


def _silu(v):
    return v * _sigmoid(v)


def _dsilu(v):
    s = _sigmoid(v)
    return s * (1.0 + v * (1.0 - s))


def _dot(a, b, dims):
    return lax.dot_general(a.astype(BF16), b.astype(BF16), (dims, ((), ())), preferred_element_type=F32)


NN = ((1,), (0,))
NT = ((1,), (1,))
TN = ((0,), (0,))


ANY = pl.BlockSpec(memory_space=pl.ANY)


class _Side:
    def __init__(self, arrays, out_shapes, nsem, start, wait, aliases=None):
        self.arrays, self.out_shapes, self.nsem = list(arrays), list(out_shapes), nsem
        self.start, self.wait, self.aliases = start, wait, dict(aliases or {})

    def plumb(self, n_in, n_out):
        return dict(args=self.arrays, in_specs=[ANY] * len(self.arrays), out_specs=[ANY] * len(self.out_shapes),
                    scratch=[pltpu.SemaphoreType.DMA((self.nsem,)), pltpu.SemaphoreType.DMA((self.nsem,))],
                    aliases={n_in + i: n_out + o for i, o in self.aliases.items()})

    def run_at_ends(self, ins, outs, sems, first, last, compute):
        @pl.when(first)
        def _():
            self.start(ins, outs, sems)

        compute()

        @pl.when(last)
        def _():
            self.wait(ins, outs, sems)


def _grid_ends(grid):
    first = functools.reduce(lambda a, b: a & b, [pl.program_id(d) == 0 for d in range(len(grid))])
    last = functools.reduce(lambda a, b: a & b, [pl.program_id(d) == grid[d] - 1 for d in range(len(grid))])
    return first, last


def _mm(pairs, mode, *, tm, tn, tk, out_dtypes, name, epilogue=None, tiles=(), b_koff=None, side=None):
    a0, b0 = pairs[0]
    if mode == "nn":
        (M, K), N = a0.shape, b0.shape[1]
    elif mode == "nt":
        (M, K), N = a0.shape, b0.shape[0]
    else:
        (K, M), N = a0.shape, b0.shape[1]
    tm, tn, tk = min(tm, M), min(tn, N), min(tk, K)
    assert M % tm == 0 and N % tn == 0 and K % tk == 0, (name, M, N, K, tm, tn, tk)
    nk = K // tk
    npair = len(pairs)
    koff = [0] * npair if b_koff is None else [o // tk for o in b_koff]
    if b_koff is not None:
        assert all(o % tk == 0 for o in b_koff)
    in_specs, args = [], []
    for p, (a, b) in enumerate(pairs):
        if mode == "nn":
            sa = pl.BlockSpec((tm, tk), lambda i, j, k: (i, k))
            sb = pl.BlockSpec((tk, tn), lambda i, j, k, o=koff[p]: (k + o, j))
            dims = NN
        elif mode == "nt":
            sa = pl.BlockSpec((tm, tk), lambda i, j, k: (i, k))
            sb = pl.BlockSpec((tn, tk), lambda i, j, k, o=koff[p]: (j, k + o))
            dims = NT
        else:
            sa = pl.BlockSpec((tk, tm), lambda i, j, k: (k, i))
            sb = pl.BlockSpec((tk, tn), lambda i, j, k, o=koff[p]: (k + o, j))
            dims = TN
        in_specs += [sa, sb]
        args += [a, b]
    for t in tiles:
        in_specs.append(pl.BlockSpec((tm, tn), lambda i, j, k: (i, j)))
        args.append(t)
    nt_ = len(tiles)
    nout = len(out_dtypes)
    nin = len(args)
    grid = (M // tm, N // tn, nk)
    plumb = side.plumb(nin, nout) if side is not None else None
    ns_in, ns_out = (len(side.arrays), len(side.out_shapes)) if side is not None else (0, 0)

    def body(*refs):
        ab = refs[: 2 * npair]
        tl = refs[2 * npair: nin]
        outs = refs[nin + ns_in: nin + ns_in + nout]
        scratch = refs[nin + ns_in + nout + ns_out:]
        acc_ref = scratch[0] if nk > 1 else None
        if side is None:
            compute(ab, tl, outs, acc_ref)
        else:
            first, last = _grid_ends(grid)
            side.run_at_ends(refs[nin: nin + ns_in], refs[nin + ns_in + nout: nin + ns_in + nout + ns_out],
                             scratch[-2:], first, last, lambda: compute(ab, tl, outs, acc_ref))

    def compute(ab, tl, outs, acc_ref):
        def partial_sum():
            s = _dot(ab[0][...], ab[1][...], dims)
            for p in range(1, npair):
                s = s + _dot(ab[2 * p][...], ab[2 * p + 1][...], dims)
            return s

        def finish(acc):
            res = (acc,) if epilogue is None else epilogue(acc, *[t[...] for t in tl])
            for o, r in zip(outs, res):
                o[...] = r.astype(o.dtype)

        if nk == 1:
            finish(partial_sum())
        else:
            k = pl.program_id(2)

            @pl.when(k == 0)
            def _():
                acc_ref[...] = jnp.zeros_like(acc_ref)

            acc_ref[...] += partial_sum()

            @pl.when(k == nk - 1)
            def _():
                finish(acc_ref[...])

    out_shape = [jax.ShapeDtypeStruct((M, N), dt) for dt in out_dtypes]
    out_specs = [pl.BlockSpec((tm, tn), lambda i, j, k: (i, j)) for _ in out_dtypes]
    scratch = [pltpu.VMEM((tm, tn), F32)] if nk > 1 else []
    if side is None:
        res = pl.pallas_call(
            body, name=name, grid=grid, in_specs=in_specs, out_specs=out_specs, out_shape=out_shape,
            scratch_shapes=scratch, compiler_params=_params("parallel", "parallel", "arbitrary"),
        )(*args)
        return res[0] if nout == 1 else res
    res = pl.pallas_call(
        body, name=name, grid=grid, in_specs=in_specs + plumb["in_specs"], out_specs=out_specs + plumb["out_specs"],
        out_shape=out_shape + side.out_shapes, scratch_shapes=scratch + plumb["scratch"],
        input_output_aliases=plumb["aliases"], compiler_params=_params("arbitrary", "arbitrary", "arbitrary"),
    )(*args, *plumb["args"])
    return res[:nout], res[nout:]


def _col(arr, tm, width, cb):
    return pl.BlockSpec((tm, width), lambda i, cb=cb: (i, cb))


def _rms_fwd(x, g, *, out_dtype, name, mul=None, res=None, coeff=1.0, tm=512):
    T, D = x.shape
    tm = min(tm, T)
    args, in_specs = [x, g], [pl.BlockSpec((tm, D), lambda i: (i, 0)), pl.BlockSpec((1, D), lambda i: (0, 0))]
    if mul is not None:
        args.append(mul[0])
        in_specs.append(_col(mul[0], tm, D, mul[1]))
    if res is not None:
        args.append(res)
        in_specs.append(pl.BlockSpec((tm, D), lambda i: (i, 0)))

    def body(*refs):
        xv = refs[0][...].astype(F32)
        r = lax.rsqrt(jnp.mean(xv * xv, axis=-1, keepdims=True) + EPS)
        y = (xv * r) * refs[1][...]
        n = 2
        if mul is not None:
            y = y * _silu(refs[n][...])
            n += 1
        if res is not None:
            y = refs[n][...] + coeff * y
        refs[-1][...] = y.astype(out_dtype)

    return pl.pallas_call(
        body, name=name, grid=(T // tm,), in_specs=in_specs, out_specs=pl.BlockSpec((tm, D), lambda i: (i, 0)),
        out_shape=jax.ShapeDtypeStruct((T, D), out_dtype), compiler_params=_params("parallel"),
    )(*args)


def _fold8(v):
    tm, d = v.shape
    return v.reshape(tm // SUBLANE, SUBLANE, d).sum(axis=0)


def _rms_bwd(x, g, dy, *, name, coeff=1.0, add=None, mul=None, dx_dtype=F32, tm=512, side=None):
    T, D = x.shape
    tm = min(tm, T)
    row = pl.BlockSpec((tm, D), lambda i: (i, 0))
    args, in_specs = [x, g, dy], [row, pl.BlockSpec((1, D), lambda i: (0, 0)), row]
    if add is not None:
        args.append(add)
        in_specs.append(row)
    if mul is not None:
        args.append(mul[0])
        in_specs.append(_col(mul[0], tm, D, mul[1]))
    nin = len(args)
    nout = 3 if mul is not None else 2
    ns_in, ns_out = (len(side.arrays), len(side.out_shapes)) if side is not None else (0, 0)

    def body(*refs):
        outs = refs[nin + ns_in: nin + ns_in + nout]
        if side is None:
            compute(refs[:nin] + outs)
        else:
            first, last = _grid_ends((T // tm,))
            side.run_at_ends(refs[nin: nin + ns_in], refs[nin + ns_in + nout: nin + ns_in + nout + ns_out], refs[-2:],
                             first, last, lambda: compute(refs[:nin] + outs))

    def compute(refs):
        xv = refs[0][...].astype(F32)
        gv = refs[1][...]
        dyv = refs[2][...].astype(F32) * coeff
        r = lax.rsqrt(jnp.mean(xv * xv, axis=-1, keepdims=True) + EPS)
        nrm = xv * r
        n = 3
        addv = None
        if add is not None:
            addv = refs[n][...]
            n += 1
        if mul is not None:
            mv = refs[n][...]
            sm = _silu(mv)
            refs[nin + 2][...] = (dyv * nrm * gv * _dsilu(mv)).astype(refs[nin + 2].dtype)
            dyv = dyv * sm
        dn = dyv * gv
        dx = r * (dn - nrm * jnp.mean(dn * nrm, axis=-1, keepdims=True))
        if addv is not None:
            dx = dx + addv
        refs[nin][...] = dx.astype(dx_dtype)
        dg_ref = refs[nin + 1]

        @pl.when(pl.program_id(0) == 0)
        def _():
            dg_ref[...] = jnp.zeros_like(dg_ref)

        dg_ref[...] += _fold8(dyv * nrm)

    out_shape = [jax.ShapeDtypeStruct((T, D), dx_dtype), jax.ShapeDtypeStruct((SUBLANE, D), F32)]
    out_specs = [row, pl.BlockSpec((SUBLANE, D), lambda i: (0, 0))]
    if mul is not None:
        out_shape.append(jax.ShapeDtypeStruct((T, D), BF16))
        out_specs.append(row)
    if side is None:
        return pl.pallas_call(
            body, name=name, grid=(T // tm,), in_specs=in_specs, out_specs=out_specs, out_shape=out_shape,
            compiler_params=_params("arbitrary"),
        )(*args)
    plumb = side.plumb(nin, nout)
    res = pl.pallas_call(
        body, name=name, grid=(T // tm,), in_specs=in_specs + plumb["in_specs"],
        out_specs=out_specs + plumb["out_specs"], out_shape=out_shape + side.out_shapes,
        scratch_shapes=plumb["scratch"], input_output_aliases=plumb["aliases"], compiler_params=_params("arbitrary"),
    )(*args, *plumb["args"])
    res = list(res)
    return res[:nout] + [res[nout:]]


def _ffn_in(h, wg, wu, *, name, tm=1024, tn=256, side=None):
    T, D = h.shape
    F = wg.shape[1]
    tm = min(tm, T)
    assert F % tn == 0
    grid = (T // tm, F // tn)
    ns_in, ns_out = (len(side.arrays), len(side.out_shapes)) if side is not None else (0, 0)

    def compute(h_ref, wg_ref, wu_ref, a_ref, g_ref, u_ref):
        hv = h_ref[...]
        gt = _dot(hv, wg_ref[...], NN)
        up = _dot(hv, wu_ref[...], NN)
        a_ref[...] = (_silu(gt) * up).astype(BF16)
        g_ref[...] = gt.astype(BF16)
        u_ref[...] = up.astype(BF16)

    def body(*refs):
        if side is None:
            compute(*refs)
        else:
            outs0 = 3 + ns_in
            first, last = _grid_ends(grid)
            side.run_at_ends(refs[3:outs0], refs[outs0 + 3: outs0 + 3 + ns_out], refs[-2:], first, last,
                             lambda: compute(*refs[:3], *refs[outs0: outs0 + 3]))

    o = pl.BlockSpec((tm, tn), lambda i, j: (i, j))
    w = pl.BlockSpec((D, tn), lambda i, j: (0, j))
    in_specs = [pl.BlockSpec((tm, D), lambda i, j: (i, 0)), w, w]
    out_shape = [jax.ShapeDtypeStruct((T, F), BF16)] * 3
    if side is None:
        return pl.pallas_call(body, name=name, grid=grid, in_specs=in_specs, out_specs=[o, o, o], out_shape=out_shape,
                              compiler_params=_params("parallel", "parallel"))(h, wg, wu)
    plumb = side.plumb(3, 3)
    res = pl.pallas_call(
        body, name=name, grid=grid, in_specs=in_specs + plumb["in_specs"], out_specs=[o, o, o] + plumb["out_specs"],
        out_shape=out_shape + side.out_shapes, scratch_shapes=plumb["scratch"], input_output_aliases=plumb["aliases"],
        compiler_params=_params("arbitrary", "arbitrary"),
    )(h, wg, wu, *plumb["args"])
    return res[:3], res[3:]


def _swiglu_bwd_epilogue(da, gt, up):
    gt = gt.astype(F32)
    up = up.astype(F32)
    return da * up * _dsilu(gt), da * _silu(gt)


GATE_CB = 7


def _gatemix_fwd(z, b_gate, ya, yb, *, name, tm=512):
    T, D = ya.shape
    tm = min(tm, T)
    row = pl.BlockSpec((tm, D), lambda i: (i, 0))

    def body(z0, z1, b0, b1, ya_ref, yb_ref, y_ref):
        g0 = _sigmoid(z0[...] + b0[...])
        g1 = _sigmoid(z1[...] + b1[...])
        y_ref[...] = (g0 * ya_ref[...] + g1 * yb_ref[...]).astype(y_ref.dtype)

    bs = lambda c: pl.BlockSpec((1, D), lambda i, c=c: (0, c))
    return pl.pallas_call(
        body, name=name, grid=(T // tm,),
        in_specs=[_col(z, tm, D, GATE_CB), _col(z, tm, D, GATE_CB + 1), bs(0), bs(1), row, row],
        out_specs=row, out_shape=jax.ShapeDtypeStruct((T, D), BF16), compiler_params=_params("parallel"),
    )(z, z, b_gate, b_gate, ya, yb)


def _gatemix_bwd(z, b_gate, ya, yb, dy, *, name, tm=512):
    T, D = ya.shape
    tm = min(tm, T)
    row = pl.BlockSpec((tm, D), lambda i: (i, 0))
    part = pl.BlockSpec((SUBLANE, D), lambda i: (0, 0))

    def body(z0, z1, b0, b1, ya_ref, yb_ref, dy_ref, dya, dyb, dz0, dz1, s0, s1):
        g0 = _sigmoid(z0[...] + b0[...])
        g1 = _sigmoid(z1[...] + b1[...])
        dyv = dy_ref[...]
        dya[...] = (dyv * g0).astype(BF16)
        dyb[...] = (dyv * g1).astype(BF16)
        d0 = dyv * ya_ref[...] * (g0 * (1.0 - g0))
        d1 = dyv * yb_ref[...] * (g1 * (1.0 - g1))
        dz0[...] = d0.astype(BF16)
        dz1[...] = d1.astype(BF16)

        @pl.when(pl.program_id(0) == 0)
        def _():
            s0[...] = jnp.zeros_like(s0)
            s1[...] = jnp.zeros_like(s1)

        s0[...] += _fold8(d0)
        s1[...] += _fold8(d1)

    bs = lambda c: pl.BlockSpec((1, D), lambda i, c=c: (0, c))
    act = jax.ShapeDtypeStruct((T, D), BF16)
    ps = jax.ShapeDtypeStruct((SUBLANE, D), F32)
    return pl.pallas_call(
        body, name=name, grid=(T // tm,),
        in_specs=[_col(z, tm, D, GATE_CB), _col(z, tm, D, GATE_CB + 1), bs(0), bs(1), row, row, row],
        out_specs=[row, row, row, row, part, part], out_shape=[act, act, act, act, ps, ps],
        compiler_params=_params("arbitrary"),
    )(z, z, b_gate, b_gate, ya, yb, dy)


def _loss_head(x, target, *, name, tm=512):
    T, D = x.shape
    tm = min(tm, T)
    row = pl.BlockSpec((tm, D), lambda i: (i, 0))

    def body(x_ref, t_ref, dx_ref, s_ref):
        e = x_ref[...] - t_ref[...]
        dx_ref[...] = e * (1.0 / D)

        @pl.when(pl.program_id(0) == 0)
        def _():
            s_ref[...] = jnp.zeros_like(s_ref)

        s_ref[...] += _fold8(e * e)

    return pl.pallas_call(
        body, name=name, grid=(T // tm,), in_specs=[row, row],
        out_specs=[row, pl.BlockSpec((SUBLANE, D), lambda i: (0, 0))],
        out_shape=[jax.ShapeDtypeStruct((T, D), F32), jax.ShapeDtypeStruct((SUBLANE, D), F32)],
        compiler_params=_params("arbitrary"),
    )(x, target)


def _tri(n, reverse):
    r = lax.broadcasted_iota(jnp.int32, (n, n), 0)
    c = lax.broadcasted_iota(jnp.int32, (n, n), 1)
    return jnp.where((c >= r) if reverse else (c <= r), 1.0, 0.0).astype(F32)


def _cumsum_t(xs, *, name, width, pre, reverse=False, rows=(), post=None, out_dtypes=(F32,), fold=None, tb=256):
    T = xs[0][0].shape[0]
    tb = min(tb, T)
    nb = T // tb
    tblk = (lambda i: nb - 1 - i) if reverse else (lambda i: i)
    args = [a for a, _ in xs] + [a for a, _ in rows]
    in_specs = [pl.BlockSpec((tb, width), lambda i, cb=cb: (tblk(i), cb)) for _, cb in xs]
    in_specs += [pl.BlockSpec((1, width), lambda i, cb=cb: (0, cb)) for _, cb in rows]
    nin, nout = len(args), len(out_dtypes)

    def body(*refs):
        vals = [r[...] for r in refs[:nin]]
        outs = refs[nin:nin + nout]
        carry = refs[-1]
        first = pl.program_id(0) == 0

        @pl.when(first)
        def _():
            carry[...] = jnp.zeros_like(carry)

        cum = jnp.dot(_tri(tb, reverse), pre(*vals), precision=HIGHEST, preferred_element_type=F32) + carry[...]
        carry[...] = cum[0:1, :] if reverse else cum[tb - 1:tb, :]
        res = (cum,) if post is None else post(cum, *vals)
        for o, r in zip(outs, res):
            o[...] = r.astype(o.dtype)
        if fold is not None:
            f_ref = refs[nin + nout]

            @pl.when(first)
            def _():
                f_ref[...] = jnp.zeros_like(f_ref)

            f_ref[...] += _fold8(fold(cum, *vals))

    tspec = pl.BlockSpec((tb, width), lambda i: (tblk(i), 0))
    out_shape = [jax.ShapeDtypeStruct((T, width), dt) for dt in out_dtypes]
    out_specs = [tspec] * nout
    if fold is not None:
        out_shape.append(jax.ShapeDtypeStruct((SUBLANE, width), F32))
        out_specs.append(pl.BlockSpec((SUBLANE, width), lambda i: (0, 0)))
    res = pl.pallas_call(
        body, name=name, grid=(nb,), in_specs=in_specs, out_specs=out_specs, out_shape=out_shape,
        scratch_shapes=[pltpu.VMEM((1, width), F32)], compiler_params=_params("arbitrary"),
    )(*args)
    return res[0] if len(res) == 1 else res


def _logsigmoid(v):
    return jnp.minimum(v, 0.0) - jnp.log(1.0 + jnp.exp(-jnp.abs(v)))


HG_TB = 256
HG_HB = 4
HG_W = HG_HB * DH
HG_GROUPS = HEADS // HG_HB
HG_Q_CB, HG_F_CB, HG_I_CB = 0, HG_GROUPS, 2 * HG_GROUPS
NEG = -1e30


def _scan16(x, rowid, reverse=False):
    for k in [1 << n for n in range(SUB.bit_length() - 1)]:
        if reverse:
            x = x + jnp.where(rowid < SUB - k, pltpu.roll(x, SUB - k, 0), 0.0)
        else:
            x = x + jnp.where(rowid >= k, pltpu.roll(x, k, 0), 0.0)
    return x


def _hg_block(q_ref, f_ref, i_ref, lb_ref, rows, cols, rowid):
    lb = lb_ref[:, cols]
    qr = q_ref[rows, cols]
    sg = _sigmoid(f_ref[rows, cols])
    f = lb + (1.0 - lb) * sg
    b = _scan16(jnp.log(f), rowid)
    return _silu(qr), 1.0 - f, i_ref[rows, cols], b, qr, sg, f, lb


def _hg_specs(tb, tmap):
    return [pl.BlockSpec((tb, HG_W), lambda g, t, *_: (tmap(t), HG_Q_CB + g)),
            pl.BlockSpec((tb, HG_W), lambda g, t, *_: (tmap(t), HG_F_CB + g)),
            pl.BlockSpec((tb, HG_W), lambda g, t, *_: (tmap(t), HG_I_CB + g)),
            pl.BlockSpec((1, HG_W), lambda g, t, *_: (0, g))]


def _hgrn2_fwd(z, lb_row, *, name):
    T = z.shape[0]
    tb = min(HG_TB, T)
    nb, nsub = T // tb, tb // SUB

    def body(q_ref, f_ref, i_ref, lb_ref, o_ref, st_ref, state):
        @pl.when(pl.program_id(1) == 0)
        def _():
            state[...] = jnp.zeros_like(state)

        rowid = lax.broadcasted_iota(jnp.int32, (SUB, DH), 0)

        def step(c, carry):
            rows = pl.ds(pl.multiple_of(c * SUB, SUB), SUB)
            for hh in range(HG_HB):
                cols = slice(hh * DH, (hh + 1) * DH)
                q, k, iv, b = _hg_block(q_ref, f_ref, i_ref, lb_ref, rows, cols, rowid)[:4]
                bl = b[SUB - 1:SUB, :]
                sv = state[hh]
                st_ref[c, hh] = sv
                o = _dot(q * jnp.exp(b), sv, NT)
                for s in range(SUB):
                    e = jnp.exp(jnp.where(rowid >= s, b - b[s:s + 1, :], NEG))
                    a = jnp.sum(q * e * k[s:s + 1, :], axis=-1, keepdims=True)
                    o = o + a * iv[s:s + 1, :]
                o_ref[rows, cols] = o
                state[hh] = sv * jnp.exp(bl) + _dot(iv, k * jnp.exp(bl - b), TN)
            return carry

        lax.fori_loop(0, nsub, step, 0)

    return pl.pallas_call(
        body, name=name, grid=(HG_GROUPS, nb), in_specs=_hg_specs(tb, lambda t: t),
        out_specs=[pl.BlockSpec((tb, HG_W), lambda g, t: (t, g)),
                   pl.BlockSpec((nsub, HG_HB, DH, DH), lambda g, t: (t, g, 0, 0))],
        out_shape=[jax.ShapeDtypeStruct((T, D_MODEL), F32), jax.ShapeDtypeStruct((T // SUB, HEADS, DH, DH), F32)],
        scratch_shapes=[pltpu.VMEM((HG_HB, DH, DH), F32)], compiler_params=_params("parallel", "arbitrary"),
    )(z, z, z, lb_row)


def _hgrn2_bwd(z, lb_row, states, do, *, name):
    T = z.shape[0]
    tb = min(HG_TB, T)
    nb, nsub = T // tb, tb // SUB
    rev = lambda t: nb - 1 - t

    def body(q_ref, f_ref, i_ref, lb_ref, st_ref, do_ref, dq_ref, dfl_ref, di_ref, dlb_ref, dstate, later):
        @pl.when(pl.program_id(1) == 0)
        def _():
            dstate[...] = jnp.zeros_like(dstate)
            later[...] = jnp.zeros_like(later)
            dlb_ref[...] = jnp.zeros_like(dlb_ref)

        rowid = lax.broadcasted_iota(jnp.int32, (SUB, DH), 0)

        def step(cc, carry):
            c = nsub - 1 - cc
            rows = pl.ds(pl.multiple_of(c * SUB, SUB), SUB)
            for hh in range(HG_HB):
                cols = slice(hh * DH, (hh + 1) * DH)
                q, k, iv, b, qr, sg, f, lb = _hg_block(q_ref, f_ref, i_ref, lb_ref, rows, cols, rowid)
                bl = b[SUB - 1:SUB, :]
                eb, ebl = jnp.exp(b), jnp.exp(bl - b)
                sv, dsv = st_ref[c, hh], dstate[hh]
                dov = do_ref[rows, cols]
                dq = _dot(dov, sv, NN) * eb
                dk = _dot(iv, dsv, NN) * ebl
                di = _dot(k * ebl, dsv, NT)
                for s in range(SUB):
                    e = jnp.exp(jnp.where(rowid >= s, b - b[s:s + 1, :], NEG))
                    ks, isv = k[s:s + 1, :], iv[s:s + 1, :]
                    qe = q * e
                    a = jnp.sum(qe * ks, axis=-1, keepdims=True)
                    p = jnp.sum(dov * isv, axis=-1, keepdims=True)
                    dq = dq + p * (e * ks)
                    dks = jnp.sum(p * qe, axis=0, keepdims=True)
                    dis = jnp.sum(a * dov, axis=0, keepdims=True)
                    dk = dk + jnp.where(rowid == s, dks, 0.0)
                    di = di + jnp.where(rowid == s, dis, 0.0)
                dlogf = _scan16(q * dq - k * dk, rowid, reverse=True) + later[hh]
                df = dlogf / f - dk
                dlb_ref[:, cols] += jnp.sum(df * (1.0 - sg), axis=0, keepdims=True)
                dfl_ref[rows, cols] = (df * (1.0 - lb) * (sg * (1.0 - sg))).astype(BF16)
                dq_ref[rows, cols] = (dq * _dsilu(qr)).astype(BF16)
                di_ref[rows, cols] = di.astype(BF16)
                dnew = dsv * jnp.exp(bl) + _dot(dov, q * eb, TN)
                dstate[hh] = dnew
                later[hh] = jnp.sum(dnew * sv, axis=0, keepdims=True)
            return carry

        lax.fori_loop(0, nsub, step, 0)

    tile = pl.BlockSpec((tb, HG_W), lambda g, t: (rev(t), g))
    act = jax.ShapeDtypeStruct((T, D_MODEL), BF16)
    return pl.pallas_call(
        body, name=name, grid=(HG_GROUPS, nb),
        in_specs=_hg_specs(tb, rev) + [pl.BlockSpec((nsub, HG_HB, DH, DH), lambda g, t: (rev(t), g, 0, 0)), tile],
        out_specs=[tile, tile, tile, pl.BlockSpec((1, HG_W), lambda g, t: (0, g))],
        out_shape=[act, act, act, jax.ShapeDtypeStruct((1, D_MODEL), F32)],
        scratch_shapes=[pltpu.VMEM((HG_HB, DH, DH), F32), pltpu.VMEM((HG_HB, 1, DH), F32)],
        compiler_params=_params("parallel", "arbitrary"),
    )(z, z, z, lb_row, states, do)


FOX_Q_CB, FOX_K_CB, FOX_V_CB = 4 * HEADS, 5 * HEADS, 6 * HEADS
FOX_SCALE = 1.0 / math.sqrt(DH)


def _fox_tile(T):
    return 512 if T >= 2048 else 128


def _fox_pairs(nq, by_query):
    if by_query:
        pairs = [(i, j) for i in range(nq) for j in range(i + 1)]
    else:
        pairs = [(i, j) for j in range(nq) for i in range(j, nq)]
    return (jnp.asarray([p[0] for p in pairs], jnp.int32), jnp.asarray([p[1] for p in pairs], jnp.int32))


LOG2E = 1.4426950408889634
FOX_RC = 64
FOX_HB = 2
FOX_HB_FWD = 4


def _fox_q2(q):
    return (q * (FOX_SCALE * LOG2E)).astype(BF16)


FOX_ZERO = -200.0


def _fox_norms(z, *, name):
    T = z.shape[0]
    tq = _fox_tile(T)
    nq = T // tq

    def body(q_ref, k_ref, nq_ref, nk_ref):
        head_of_col = lax.broadcasted_iota(jnp.int32, (D_MODEL, LANE), 0) // DH
        pick = jnp.where(head_of_col == lax.broadcasted_iota(jnp.int32, (D_MODEL, LANE), 1), 1.0, 0.0).astype(BF16)

        def tile_max(v):
            v = v.astype(F32)
            sq = _dot(v * v, pick, NN)
            return jnp.broadcast_to(jnp.max(jnp.sqrt(sq), axis=0, keepdims=True), (SUBLANE, LANE))

        nq_ref[...] = tile_max(_fox_q2(q_ref[...]))
        nk_ref[...] = tile_max(k_ref[...].astype(BF16))

    out = jax.ShapeDtypeStruct((nq * SUBLANE, LANE), F32)
    spec = pl.BlockSpec((SUBLANE, LANE), lambda i: (i, 0))
    a, b = pl.pallas_call(
        body, name=name, grid=(nq,),
        in_specs=[pl.BlockSpec((tq, D_MODEL), lambda i: (i, FOX_Q_CB // HEADS)),
                  pl.BlockSpec((tq, D_MODEL), lambda i: (i, FOX_K_CB // HEADS))],
        out_specs=[spec, spec], out_shape=[out, out], compiler_params=_params("parallel"),
    )(z, z)
    return a[::SUBLANE, :HEADS], b[::SUBLANE, :HEADS]


def _fox_schedule(norm_q, norm_k, ct, tq):
    nq = ct.shape[1] // tq
    first, last = ct[:, ::tq], ct[:, tq - 1::tq]
    nqh, nkh = norm_q.T * 1.05, norm_k.T * 1.05
    bound = nqh[:, :, None] * (nkh[:, None, :] + nkh[:, :, None]) + first[:, :, None] - last[:, None, :]
    tri = jnp.arange(nq)[:, None] > jnp.arange(nq)[None, :]
    drop = (bound < FOX_ZERO) & tri[None]
    lo = jnp.argmin(drop, axis=2).astype(jnp.int32)
    dropped = jnp.arange(nq)[None, None, :] < lo[:, :, None]
    group_lo = lambda hb: jnp.min(lo.reshape(HEADS // hb, hb, nq), axis=1)
    dropped_g = jnp.arange(nq)[None, None, :] < group_lo(FOX_HB)[:, :, None]
    qf, kf = _fox_pairs(nq, by_query=True)
    qb, kb = _fox_pairs(nq, by_query=False)
    fetch_k = jnp.maximum(kf[None, :], group_lo(FOX_HB_FWD)[:, qf])
    kept_q = jnp.where(dropped_g | ~(tri | jnp.eye(nq, dtype=bool))[None], -1, jnp.arange(nq)[None, :, None])
    last_kept = lax.cummax(kept_q, axis=1)
    fetch_q = last_kept[:, qb, kb]
    i32 = lambda a: a.astype(jnp.int32)
    return i32(fetch_k), i32(dropped[:, qf, kf]), i32(fetch_q), i32(dropped[:, qb, kb])


def _fox_fwd(z, c_col, c_row, fetch_k, skip, *, name):
    T = z.shape[0]
    tq = _fox_tile(T)
    nq = T // tq
    rc = min(FOX_RC, tq)
    hb = FOX_HB_FWD

    qi, kj = _fox_pairs(nq, by_query=True)

    def body(qi_ref, kj_ref, fk_ref, skip_ref, q_ref, k_ref, v_ref, cc_ref, cr_ref, o_ref, lse_ref, m_scr, l_scr, acc,
             a_scr, s_scr, p_scr):
        p_id = pl.program_id(1)
        i, j = qi_ref[p_id], kj_ref[p_id]

        @pl.when(j == 0)
        def _():
            m_scr[...] = jnp.full_like(m_scr, NEG)
            l_scr[...] = jnp.zeros_like(l_scr)
            acc[...] = jnp.zeros_like(acc)

        def update(hh, masked):
            cols = slice(hh * DH, (hh + 1) * DH)
            bias = cc_ref[hh, 0:1, :] - cr_ref[hh]
            s_scr[hh] = _dot(_fox_q2(q_ref[:, cols]), k_ref[:, cols], NT)
            for r in range(tq // rc):
                rows = slice(r * rc, (r + 1) * rc)
                t = s_scr[hh, rows, :] + bias
                if masked:
                    t = jnp.where(lax.broadcasted_iota(jnp.int32, (rc, tq), 1)
                                  <= r * rc + lax.broadcasted_iota(jnp.int32, (rc, tq), 0), t, NEG)
                m_old = m_scr[hh, rows, :]
                m_new = jnp.maximum(m_old, jnp.max(t, axis=-1, keepdims=True))
                alpha = jnp.exp2(m_old - m_new)
                p = jnp.exp2(t - jnp.tile(m_new, (1, tq // LANE)))
                l_scr[hh, rows, :] = alpha * l_scr[hh, rows, :] + jnp.sum(p, axis=-1, keepdims=True)
                a_scr[hh, rows, :] = alpha
                p_scr[hh, rows, :] = p.astype(BF16)
                m_scr[hh, rows, :] = m_new
            acc[hh] = a_scr[hh] * acc[hh] + _dot(p_scr[hh], v_ref[:, cols], NN)

        for hh in range(hb):
            live = skip_ref[pl.program_id(0) * hb + hh, p_id] == 0

            @pl.when((j < i) & live)
            def _():
                update(hh, False)

            @pl.when(j == i)
            def _():
                update(hh, True)
                o_ref[:, hh * DH:(hh + 1) * DH] = acc[hh] / l_scr[hh]
                lse_ref[hh] = (m_scr[hh, :, 0:1] + jnp.log2(l_scr[hh, :, 0:1])) + (cc_ref[hh] - cc_ref[hh, 0:1, :])

    wide = hb * DH
    qtile = lambda cb: pl.BlockSpec((tq, wide), lambda g, p, qi, kj, fk, sk, cb=cb: (qi[p], cb // hb + g))
    ktile = lambda cb: pl.BlockSpec((tq, wide), lambda g, p, qi, kj, fk, sk, cb=cb: (fk[g, p], cb // hb + g))
    qcol = pl.BlockSpec((hb, tq, 1), lambda g, p, qi, kj, fk, sk: (g, qi[p], 0))
    stat = pltpu.VMEM((hb, tq, LANE), F32)
    return pl.pallas_call(
        body, name=name,
        grid_spec=pltpu.PrefetchScalarGridSpec(
            num_scalar_prefetch=4, grid=(HEADS // hb, qi.shape[0]),
            in_specs=[qtile(FOX_Q_CB), ktile(FOX_K_CB), ktile(FOX_V_CB), qcol,
                      pl.BlockSpec((hb, 1, tq), lambda g, p, qi, kj, fk, sk: (g, 0, fk[g, p]))],
            out_specs=[qtile(0), qcol],
            scratch_shapes=[stat, stat, pltpu.VMEM((hb, tq, DH), F32), stat,
                            pltpu.VMEM((hb, tq, tq), F32), pltpu.VMEM((hb, tq, tq), BF16)]),
        out_shape=[jax.ShapeDtypeStruct((T, D_MODEL), F32), jax.ShapeDtypeStruct((HEADS, T, 1), F32)],
        compiler_params=_params("parallel", "arbitrary"),
    )(qi, kj, fetch_k, skip, z, z, z, c_col, c_row)


def _fox_bwd(z, c_col, c_row, o, lse, do, fetch_q, skip, *, name):
    T = z.shape[0]
    tq = _fox_tile(T)
    nq = T // tq
    rc = min(FOX_RC, tq)

    qi, kj = _fox_pairs(nq, by_query=False)

    def body(qi_ref, kj_ref, fq_ref, skip_ref, q_ref, k_ref, v_ref, cc_ref, cr_ref, o_ref, lse_ref, do_ref, dq_ref,
             dk_ref, dv_ref, dc_ref, dcq_ref, dk_acc, dv_acc, dc_acc, s_scr, dp_scr, p_scr, ds_scr, dcq_scr):
        p_id = pl.program_id(1)
        i, j = qi_ref[p_id], kj_ref[p_id]

        @pl.when(p_id == 0)
        def _():
            dq_ref[...] = jnp.zeros_like(dq_ref)
            dcq_scr[...] = jnp.zeros_like(dcq_scr)

        def update(hh, masked):
            cols = slice(hh * DH, (hh + 1) * DH)
            q2, k, dov = _fox_q2(q_ref[:, cols]), k_ref[:, cols], do_ref[:, cols]
            s_scr[hh] = _dot(q2, k, NT)
            dp_scr[hh] = _dot(dov, v_ref[:, cols], NT)
            crow = cr_ref[hh]
            csum = jnp.zeros((SUBLANE, tq), F32)
            wide = lambda col: jnp.tile(jnp.broadcast_to(col, (rc, LANE)), (1, tq // LANE))
            for r in range(tq // rc):
                rows = slice(r * rc, (r + 1) * rc)
                t = (s_scr[hh, rows, :] + wide(cc_ref[hh, rows, :] - lse_ref[hh, rows, :])) - crow
                if masked:
                    t = jnp.where(lax.broadcasted_iota(jnp.int32, (rc, tq), 1)
                                  <= r * rc + lax.broadcasted_iota(jnp.int32, (rc, tq), 0), t, NEG)
                p = jnp.exp2(t)
                delta = jnp.sum(do_ref[rows, cols] * o_ref[rows, cols], axis=-1, keepdims=True)
                ds = p * (dp_scr[hh, rows, :] - wide(delta))
                p_scr[hh, rows, :] = p.astype(BF16)
                ds_scr[hh, rows, :] = ds.astype(BF16)
                grows = pl.ds(pl.multiple_of(i * tq + r * rc, rc), rc)
                dcq_scr[hh, grows, :] += jnp.broadcast_to(jnp.sum(ds, axis=-1, keepdims=True), (rc, LANE))
                csum = csum + _fold8(ds)
            dsb = ds_scr[hh]
            dv_new = _dot(p_scr[hh], dov, TN)
            dk_new = _dot(dsb, q2, TN) * (1.0 / LOG2E)
            dc_new = -jnp.sum(csum, axis=0, keepdims=True)
            rows = pl.ds(pl.multiple_of(i * tq, tq), tq)
            dq_ref[rows, cols] += _dot(dsb, k, NN) * FOX_SCALE
            return dk_new, dv_new, dc_new

        for hh in range(FOX_HB):
            live = skip_ref[pl.program_id(0) * FOX_HB + hh, p_id] == 0

            @pl.when(i == j)
            def _():
                dk_new, dv_new, dc_new = update(hh, True)
                dk_acc[hh] = dk_new
                dv_acc[hh] = dv_new
                dc_acc[hh] = dc_new

            @pl.when((i > j) & live)
            def _():
                dk_new, dv_new, dc_new = update(hh, False)
                dk_acc[hh] += dk_new
                dv_acc[hh] += dv_new
                dc_acc[hh] += dc_new

            @pl.when(i == nq - 1)
            def _():
                dk_ref[:, hh * DH:(hh + 1) * DH] = dk_acc[hh].astype(BF16)
                dv_ref[:, hh * DH:(hh + 1) * DH] = dv_acc[hh].astype(BF16)
                dc_ref[hh] = dc_acc[hh]

            @pl.when(p_id == qi.shape[0] - 1)
            def _():
                for r in range(nq):
                    rows = slice(r * tq, (r + 1) * tq)
                    dcq_ref[hh, :, rows] = jnp.transpose(dcq_scr[hh, rows, :])[0:1, :]

    wide_cols = FOX_HB * DH
    n_groups = HEADS // FOX_HB
    qtile = lambda cb: pl.BlockSpec((tq, wide_cols), lambda g, p, qi, kj, fq, sk, cb=cb: (fq[g, p], cb // FOX_HB + g))
    ktile = lambda cb: pl.BlockSpec((tq, wide_cols), lambda g, p, qi, kj, fq, sk, cb=cb: (kj[p], cb // FOX_HB + g))
    qcol = pl.BlockSpec((FOX_HB, tq, 1), lambda g, p, qi, kj, fq, sk: (g, fq[g, p], 0))
    krow = pl.BlockSpec((FOX_HB, 1, tq), lambda g, p, qi, kj, fq, sk: (g, 0, kj[p]))
    tile_f32 = pltpu.VMEM((FOX_HB, tq, tq), F32)
    tile_bf16 = pltpu.VMEM((FOX_HB, tq, tq), BF16)
    return pl.pallas_call(
        body, name=name,
        grid_spec=pltpu.PrefetchScalarGridSpec(
            num_scalar_prefetch=4, grid=(n_groups, qi.shape[0]),
            in_specs=[qtile(FOX_Q_CB), ktile(FOX_K_CB), ktile(FOX_V_CB), qcol, krow, qtile(0), qcol, qtile(0)],
            out_specs=[pl.BlockSpec((T, wide_cols), lambda g, p, qi, kj, fq, sk: (0, g)), ktile(0), ktile(0), krow,
                       pl.BlockSpec((FOX_HB, 1, T), lambda g, p, qi, kj, fq, sk: (g, 0, 0))],
            scratch_shapes=[pltpu.VMEM((FOX_HB, tq, DH), F32), pltpu.VMEM((FOX_HB, tq, DH), F32),
                            pltpu.VMEM((FOX_HB, 1, tq), F32), tile_f32, tile_f32, tile_bf16, tile_bf16,
                            pltpu.VMEM((FOX_HB, T, LANE), F32)]),
        out_shape=[jax.ShapeDtypeStruct((T, D_MODEL), F32), jax.ShapeDtypeStruct((T, D_MODEL), BF16),
                   jax.ShapeDtypeStruct((T, D_MODEL), BF16), jax.ShapeDtypeStruct((HEADS, 1, T), F32),
                   jax.ShapeDtypeStruct((HEADS, 1, T), F32)],
        compiler_params=_params("parallel", "arbitrary"),
    )(qi, kj, fetch_q, skip, z, z, z, c_col, c_row, o, lse, do)


MEM_SCALE = 1.0 / math.sqrt(MEM_DH)


def _mem_probs(qh, kh):
    s = _dot(qh, kh, NT) * MEM_SCALE
    p = jnp.exp(s - jnp.max(s, axis=-1, keepdims=True))
    return p / jnp.sum(p, axis=-1, keepdims=True)


def _mem_fwd(q, kv, *, name, tq=512):
    T = q.shape[0]
    tq = min(tq, T)

    def body(q_ref, kv_ref, o_ref):
        for h in range(MEM_HEADS):
            cols = slice(h * MEM_DH, (h + 1) * MEM_DH)
            vcols = slice(D_MODEL + h * MEM_DH, D_MODEL + (h + 1) * MEM_DH)
            p = _mem_probs(q_ref[:, cols], kv_ref[:, cols])
            o_ref[:, cols] = _dot(p, kv_ref[:, vcols], NN).astype(o_ref.dtype)

    return pl.pallas_call(
        body, name=name, grid=(T // tq,),
        in_specs=[pl.BlockSpec((tq, D_MODEL), lambda i: (i, 0)), pl.BlockSpec((MEM_LEN, 2 * D_MODEL), lambda i: (0, 0))],
        out_specs=pl.BlockSpec((tq, D_MODEL), lambda i: (i, 0)), out_shape=jax.ShapeDtypeStruct((T, D_MODEL), BF16),
        compiler_params=_params("parallel"),
    )(q, kv)


def _mem_bwd(q, kv, do, *, name, tq=512):
    T = q.shape[0]
    tq = min(tq, T)

    def body(q_ref, kv_ref, do_ref, dq_ref, dkv_ref):
        @pl.when(pl.program_id(0) == 0)
        def _():
            dkv_ref[...] = jnp.zeros_like(dkv_ref)

        for h in range(MEM_HEADS):
            cols = slice(h * MEM_DH, (h + 1) * MEM_DH)
            vcols = slice(D_MODEL + h * MEM_DH, D_MODEL + (h + 1) * MEM_DH)
            qh, kh, doh = q_ref[:, cols], kv_ref[:, cols], do_ref[:, cols]
            p = _mem_probs(qh, kh)
            dp = _dot(doh, kv_ref[:, vcols], NT)
            ds = p * (dp - jnp.sum(p * dp, axis=-1, keepdims=True))
            dq_ref[:, cols] = (_dot(ds, kh, NN) * MEM_SCALE).astype(dq_ref.dtype)
            dkv_ref[:, cols] += _dot(ds, qh, TN) * MEM_SCALE
            dkv_ref[:, vcols] += _dot(p, doh, TN)

    row = pl.BlockSpec((tq, D_MODEL), lambda i: (i, 0))
    full = pl.BlockSpec((MEM_LEN, 2 * D_MODEL), lambda i: (0, 0))
    return pl.pallas_call(
        body, name=name, grid=(T // tq,), in_specs=[row, full, row], out_specs=[row, full],
        out_shape=[jax.ShapeDtypeStruct((T, D_MODEL), BF16), jax.ShapeDtypeStruct((MEM_LEN, 2 * D_MODEL), F32)],
        compiler_params=_params("arbitrary"),
    )(q, kv, do)


def _adamw(w, g, m, v, *, name, tm=256):
    R, C = w.shape
    tm = min(tm, R)
    assert R % tm == 0
    nsum = g.shape[0] if g.ndim == 3 else 0

    def body(w_ref, g_ref, m_ref, v_ref, go_ref, d_ref, mo_ref, vo_ref):
        if nsum:
            gv = g_ref[0]
            for n in range(1, nsum):
                gv = gv + g_ref[n]
        else:
            gv = g_ref[...]
        mv = ADAM_B1 * m_ref[...] + (1.0 - ADAM_B1) * gv
        vv = ADAM_B2 * v_ref[...] + (1.0 - ADAM_B2) * jnp.square(gv)
        m_hat = mv / (1.0 - ADAM_B1 ** ADAM_STEP)
        v_hat = vv / (1.0 - ADAM_B2 ** ADAM_STEP)
        d_ref[...] = -ADAM_LR * (m_hat / (jnp.sqrt(v_hat) + ADAM_EPS) + ADAM_WD * w_ref[...])
        go_ref[...] = gv
        mo_ref[...] = mv
        vo_ref[...] = vv

    row = pl.BlockSpec((tm, C), lambda i: (i, 0))
    gspec = pl.BlockSpec((nsum, tm, C), lambda i: (0, i, 0)) if nsum else row
    return pl.pallas_call(
        body, name=name, grid=(R // tm,), in_specs=[row, gspec, row, row], out_specs=[row] * 4,
        out_shape=[jax.ShapeDtypeStruct((R, C), F32)] * 4, compiler_params=_params("parallel"),
    )(w, g, m, v)


def _act_mm(a, w, name, out_dtype=F32, side=None):
    res = _mm([(a, w)], "nn", tm=1024, tn=512, tk=w.shape[0], out_dtypes=[out_dtype], name=name, side=side)
    return res if side is None else (res[0][0], res[1])


def _act_mm_t(a, w, name, out_dtype=F32, side=None):
    res = _mm([(a, w)], "nt", tm=1024, tn=512, tk=1024, out_dtypes=[out_dtype], name=name, side=side)
    return res if side is None else (res[0][0], res[1])


def _wgrad(a, dy, name, tm=1024, side=None):
    tn = D_MODEL if dy.shape[1] % D_MODEL == 0 else D_FF // 2
    res = _mm([(a, dy)], "tn", tm=tm, tn=tn, tk=1024, out_dtypes=[F32], name=name, side=side)
    return res if side is None else (res[0][0], res[1])


def _colsum8(p):
    return jnp.sum(p, axis=0, keepdims=True)


def _ffn_fwd(x, pre_g, post_g, wg, wu, wd, tag, gather_beside=None):
    h = _rms_fwd(x, pre_g, out_dtype=BF16, name=tag + "_pre")
    down = functools.partial(_mm, mode="nn", tm=1024, tn=512, tk=D_FF, out_dtypes=[F32], name=tag + "_down")
    gathered = None
    if gather_beside is None:
        act, gate, up = _ffn_in(h, wg, wu, name=tag + "_in")
        d = down([(act, wd)])
    else:
        (act, gate, up), landed = _ffn_in(h, wg, wu, name=tag + "_in", side=_x_gather(gather_beside))
        (d,), gathered = down([(act, wd)], side=_x_forward(landed))
    xo = _rms_fwd(d, post_g, out_dtype=F32, name=tag + "_post", res=x, coeff=0.5)
    return xo, (h, act, gate, up, d), gathered


def _ffn_bwd(x, dxo, saved, pre_g, post_g, wg, wu, wd, tag, reduce_beside=None):
    h, act, gate, up, d = saved
    dd, dg_post = _rms_bwd(d, post_g, dxo, name=tag + "_post_b", coeff=0.5, dx_dtype=BF16)
    act_b = functools.partial(_mm, [(dd, wd)], "nt", tm=1024, tn=256, tk=D_MODEL, out_dtypes=[BF16, BF16],
                              name=tag + "_act_b", epilogue=_swiglu_bwd_epilogue, tiles=(gate, up))
    in_b = lambda dgate, dup, **kw: _mm([(dgate, wg), (dup, wu)], "nt", tm=512, tn=512, tk=D_FF, out_dtypes=[F32],
                                        name=tag + "_in_b", **kw)
    dwd_of = functools.partial(_wgrad, act, dd, tag + "_dwd", tm=D_FF // 2)
    reduced = None
    if reduce_beside is None:
        dgate, dup = act_b()
        dh = in_b(dgate, dup)
        dwd = dwd_of()
    else:
        grads, place, core = reduce_beside
        (dgate, dup), swapped = act_b(side=_x_swap(grads))
        pbf, own = _pair_sums(grads, swapped, place, "mid")
        dwd, landed_rows = dwd_of(side=_x_scatter(pbf[:1]))
    dwg = _wgrad(h, dgate, tag + "_dwg")
    dwu = _wgrad(h, dup, tag + "_dwu")
    if reduce_beside is not None:
        mine = _grad_slabs(dict(f1g=dwg, f1u=dwu, f1d=dwd), GROUP_FFN1)
        scatter = _x_scatter(pbf[1:])
        (dh,), landed = in_b(dgate, dup, side=_both(scatter, _x_swap(mine, base=scatter.nsem)))
        reduced_other = _finish_reduce(own, landed_rows + landed[:1], core, "mid")
        pbf, own = _pair_sums(mine, landed[1:], place, "late")
        dx, dg_pre, landed = _rms_bwd(x, pre_g, dh, name=tag + "_pre_b", add=dxo, side=_x_scatter(pbf))
        reduced = (reduced_other, _finish_reduce(own, landed, core, "late"))
    else:
        dx, dg_pre = _rms_bwd(x, pre_g, dh, name=tag + "_pre_b", add=dxo)
    return dx, dict(pre_g=_colsum8(dg_pre), post_g=_colsum8(dg_post), wg=dwg, wu=dwu, wd=dwd), reduced


def _local_step(x, mem, target, P, own, me, place, core):
    T = x.shape[0]
    G = {}
    logits = P["hg_lb_logits"]
    lb = _sigmoid(logits[0] - logits[1])
    lb_row = lb.reshape(1, D_MODEL)
    fbias_row = jnp.pad(P["fox_f_bias"], ((0, 0), (0, LANE - HEADS)))
    arrived = lambda group, others: _local_names(_assemble(group, own[group], others, me))

    W = arrived(GROUP_FFN1, _run(_x_forward(_run(_x_gather(own[GROUP_FFN1]), "gather_ffn1")), "forward_ffn1"))
    x1, ffn1_saved, others = _ffn_fwd(x, P["ffn1_pre_g"], P["ffn1_post_g"], W["f1g"], W["f1u"], W["f1d"], "ffn1",
                                      gather_beside=own[GROUP_MID])
    W.update(arrived(GROUP_MID, others))
    h2 = _rms_fwd(x1, P["mix_pre_g"], out_dtype=BF16, name="mix_pre")
    z, landed = _act_mm(h2, W["w_main"], "mix_in", side=_x_gather(own[GROUP_FFN2]))
    zfb = _mm([(h2, W["w_fb"])], "nn", tm=1024, tn=LANE, tk=D_MODEL, out_dtypes=[F32], name="mix_in_fb")
    oa_pre, states = _hgrn2_fwd(z, lb_row, name="hgrn2_f")
    o_a = _rms_fwd(oa_pre, P["hg_norm_g"], out_dtype=BF16, name="hgrn2_post", mul=(z, 3))
    y_a, others = _act_mm(o_a, W["wa"], "branch_a", side=_x_forward(landed))
    W.update(arrived(GROUP_FFN2, others))
    c = _cumsum_t([(zfb, 0)], name="fox_c", width=LANE, rows=[(fbias_row, 0)], pre=lambda v, r: _logsigmoid(v + r),
                  post=lambda cum, v, r: (cum * LOG2E,))
    ct = c[:, :HEADS].T
    c_col, c_row = ct[:, :, None], ct[:, None, :]
    fetch_k, skip_f, fetch_q, skip_b = _fox_schedule(*_fox_norms(z, name="fox_norms"), ct, _fox_tile(T))
    o_b, lse = _fox_fwd(z, c_col, c_row, fetch_k, skip_f, name="fox_f")
    y_b = _act_mm(o_b, W["wb"], "branch_b")
    y = _gatemix_fwd(z, P["b_gate"], y_a, y_b, name="gatemix")
    m = _act_mm(y, W["wo"], "mix_out")
    x2 = _rms_fwd(m, P["mix_post_g"], out_dtype=F32, name="mix_post", res=x1)
    h3 = _rms_fwd(x2, P["mem_pre_g"], out_dtype=BF16, name="mem_pre")
    mem_n = _rms_fwd(mem, P["mem_kv_g"], out_dtype=BF16, name="mem_kvn")
    qm = _act_mm(h3, W["wmq"], "mem_q")
    kv = _act_mm(mem_n, W["wmkv"], "mem_kv")
    om = _mem_fwd(qm, kv, name="mem_attn")
    mo = _act_mm(om, W["wmo"], "mem_o")
    x3 = _rms_fwd(mo, P["mem_post_g"], out_dtype=F32, name="mem_post", res=x2)
    x4, ffn2_saved, _ = _ffn_fwd(x3, P["ffn2_pre_g"], P["ffn2_post_g"], W["f2g"], W["f2u"], W["f2d"], "ffn2")
    dx4, sq = _loss_head(x4, target, name="loss_head")

    dx3, g, _ = _ffn_bwd(x3, dx4, ffn2_saved, P["ffn2_pre_g"], P["ffn2_post_g"], W["f2g"], W["f2u"], W["f2d"], "ffn2")
    G.update(ffn2_pre_g=g["pre_g"], ffn2_post_g=g["post_g"], f2g=g["wg"], f2u=g["wu"], f2d=g["wd"])

    dmo, dgp = _rms_bwd(mo, P["mem_post_g"], dx3, name="mem_post_b", dx_dtype=BF16)
    G["mem_post_g"] = _colsum8(dgp)
    g_ffn2 = _grad_slabs(G, GROUP_FFN2)
    dom, swapped = _act_mm_t(dmo, W["wmo"], "mem_o_b", BF16, side=_x_swap(g_ffn2))
    pbf_ffn2, own_ffn2 = _pair_sums(g_ffn2, swapped, place, "ffn2")
    G["wmo"] = _wgrad(om, dmo, "mem_o_w")
    dqm, dkv = _mem_bwd(qm, kv, dom, name="mem_attn_b")
    dh3 = _act_mm_t(dqm, W["wmq"], "mem_q_b")
    G["wmq"] = _wgrad(h3, dqm, "mem_q_w")
    G["wmkv"] = _mm([(mem_n, dkv)], "tn", tm=1024, tn=512, tk=MEM_LEN, out_dtypes=[F32], name="mem_kv_w")
    dmem_n = _mm([(dkv, W["wmkv"])], "nt", tm=MEM_LEN, tn=512, tk=2 * D_MODEL, out_dtypes=[F32], name="mem_kv_b")
    _, dgp = _rms_bwd(mem, P["mem_kv_g"], dmem_n, name="mem_kvn_b")
    G["mem_kv_g"] = _colsum8(dgp)
    dx2, dgp = _rms_bwd(x2, P["mem_pre_g"], dh3, name="mem_pre_b", add=dx3)
    G["mem_pre_g"] = _colsum8(dgp)

    dm, dgp = _rms_bwd(m, P["mix_post_g"], dx2, name="mix_post_b", dx_dtype=BF16)
    G["mix_post_g"] = _colsum8(dgp)
    dy = _act_mm_t(dm, W["wo"], "mix_out_b")
    G["wo"] = _wgrad(y, dm, "mix_out_w")
    dya, dyb, dz0, dz1, s0, s1 = _gatemix_bwd(z, P["b_gate"], y_a, y_b, dy, name="gatemix_b")
    G["b_gate"] = jnp.concatenate([_colsum8(s0), _colsum8(s1)], axis=1)
    do_a = _act_mm_t(dya, W["wa"], "branch_a_b")
    G["wa"] = _wgrad(o_a, dya, "branch_a_w")
    do_b = _act_mm_t(dyb, W["wb"], "branch_b_b")
    G["wb"] = _wgrad(o_b, dyb, "branch_b_w")
    doa_pre, dgp, dga = _rms_bwd(oa_pre, P["hg_norm_g"], do_a, name="hgrn2_post_b", mul=(z, 3))
    G["hg_norm_g"] = _colsum8(dgp)
    dq_a, dfl_a, di_a, dlb = _hgrn2_bwd(z, lb_row, states, doa_pre, name="hgrn2_b")
    dl0 = (dlb * lb_row * (1.0 - lb_row)).reshape(1, HEADS, DH)
    G["hg_lb_logits"] = jnp.concatenate([dl0, -dl0], axis=0)
    dq_b, dk_b, dv_b, dcr, dcq = _fox_bwd(z, c_col, c_row, o_b, lse, do_b, fetch_q, skip_b, name="fox_b")
    dc_pad = jnp.pad((dcr[:, 0, :] + dcq[:, 0, :]).T, ((0, 0), (0, LANE - HEADS)))
    gate_b = lambda cum, dc, zf, r: cum * _sigmoid(-(zf + r))
    dfl_b, dfb = _cumsum_t([(dc_pad, 0), (zfb, 0)], name="fox_c_b", width=LANE, reverse=True, rows=[(fbias_row, 0)],
                           pre=lambda dc, zf, r: dc, post=lambda *a: (gate_b(*a),), fold=gate_b)
    G["fox_f_bias"] = _colsum8(dfb)[:, :HEADS]

    pieces = [dq_a, dfl_a, di_a, dga, dq_b, dk_b, dv_b, dz0, dz1]
    dh2 = _mm([(dfl_b, W["w_fb"])], "nt", tm=512, tn=D_MODEL, tk=LANE, out_dtypes=[F32], name="mix_in_fb_b")
    for lo, hi in ((0, 5), (5, 9)):
        res = _mm([(p, W["w_main"]) for p in pieces[lo:hi]], "nt", tm=512, tn=D_MODEL, tk=D_MODEL, out_dtypes=[F32],
                  name=f"mix_in_b{lo}", b_koff=[n * D_MODEL for n in range(lo, hi)],
                  epilogue=lambda acc, t: (acc + t,), tiles=(dh2,), side=_x_scatter(pbf_ffn2) if lo == 0 else None)
        dh2, scattered = (res[0][0], res[1]) if lo == 0 else (res, scattered)
    reduced_ffn2 = _finish_reduce(own_ffn2, scattered, core, "ffn2")
    G["w_main"] = [_wgrad(h2, p, f"mix_in_w{n}") for n, p in enumerate(pieces)]
    G["w_fb"] = _mm([(h2, dfl_b)], "tn", tm=1024, tn=LANE, tk=512, out_dtypes=[F32], name="mix_in_fb_w")
    dx1, dgp = _rms_bwd(x1, P["mix_pre_g"], dh2, name="mix_pre_b", add=dx2)
    G["mix_pre_g"] = _colsum8(dgp)

    dx0, g, reduced = _ffn_bwd(x, dx1, ffn1_saved, P["ffn1_pre_g"], P["ffn1_post_g"], W["f1g"], W["f1u"], W["f1d"],
                               "ffn1", reduce_beside=(_grad_slabs(G, GROUP_MID), place, core))
    G.update(ffn1_pre_g=g["pre_g"], ffn1_post_g=g["post_g"], f1g=g["wg"], f1u=g["wu"], f1d=g["wd"])
    return sq, dx0, G, {GROUP_FFN2: reduced_ffn2, GROUP_MID: reduced[0], GROUP_FFN1: reduced[1]}


N_CHIP = 4
N_DEV = 8
IN_COLS = 9224
FB_COL = 7 * D_MODEL
SHARDED = (
    ("ffn1_w_in", (D_MODEL, 2 * D_FF), 1), ("ffn1_w_down", (D_FF, D_MODEL), 0), ("w_in", (D_MODEL, IN_COLS), 1),
    ("w_branch_a", (D_MODEL, D_MODEL), 0), ("w_branch_b", (D_MODEL, D_MODEL), 0), ("w_out", (D_MODEL, D_MODEL), 0),
    ("w_mq", (D_MODEL, D_MODEL), 0), ("w_mkv", (D_MODEL, 2 * D_MODEL), 1), ("w_mo", (D_MODEL, D_MODEL), 0),
    ("ffn2_w_in", (D_MODEL, 2 * D_FF), 1), ("ffn2_w_down", (D_FF, D_MODEL), 0),
)
SMALL = ("ffn1_pre_g", "ffn1_post_g", "mix_pre_g", "hg_norm_g", "mix_post_g", "mem_pre_g", "mem_kv_g", "mem_post_g",
         "ffn2_pre_g", "ffn2_post_g", "b_gate", "hg_lb_logits", "fox_f_bias")
SMALL_SHAPES = dict(b_gate=(1, 2 * D_MODEL), hg_lb_logits=(2, HEADS, DH), fox_f_bias=(1, HEADS))
SMALL_ROWS = 16
WEIGHT_ORDER = ("ffn1_pre_g", "ffn1_w_in", "ffn1_w_down", "ffn1_post_g", "mix_pre_g", "w_in", "hg_lb_logits", "hg_norm_g",
                "fox_f_bias", "w_branch_a", "w_branch_b", "b_gate", "w_out", "mix_post_g", "mem_pre_g", "mem_kv_g", "w_mq",
                "w_mkv", "w_mo", "mem_post_g", "ffn2_pre_g", "ffn2_w_in", "ffn2_w_down", "ffn2_post_g")


GROUP_FFN1 = ("ffn1_w_in", "ffn1_w_down")
GROUP_MID = ("w_in", "w_branch_a", "w_branch_b", "w_out", "w_mq", "w_mkv", "w_mo")
GROUP_FFN2 = ("ffn2_w_in", "ffn2_w_down")


def _layout(names, axis):
    out, at = [], 0
    for name, shape, ax in SHARDED:
        if ax == axis and name in names:
            n = shape[ax] // N_CHIP
            out.append((name, at, n))
            at += n if axis == 0 else -(-n // LANE) * LANE
    return out


def _pack(shards, names, dtype):
    rows = jnp.concatenate([shards[name].astype(dtype) for name, _, _ in _layout(names, 0)], axis=0)
    cols = [jnp.pad(shards[name].astype(dtype), ((0, 0), (0, -n % LANE))) for name, _, n in _layout(names, 1)]
    return [rows, jnp.concatenate(cols, axis=1)]


def _unpack(slabs, names):
    rows, cols = slabs
    out = {name: rows[at:at + n] for name, at, n in _layout(names, 0)}
    out.update({name: cols[:, at:at + n] for name, at, n in _layout(names, 1)})
    return out


def _pack_small(vals):
    rows = []
    for name in SMALL:
        v = vals[name].astype(F32).reshape(-1)
        rows.append(jnp.pad(v, (0, -v.shape[0] % D_MODEL)).reshape(-1, D_MODEL))
    rows = jnp.concatenate(rows, axis=0)
    return jnp.pad(rows, ((0, SMALL_ROWS - rows.shape[0]), (0, 0)))


def _unpack_small(slab):
    out, r = {}, 0
    for name in SMALL:
        shape = SMALL_SHAPES.get(name, (1, D_MODEL))
        size = math.prod(shape)
        n = -(-size // D_MODEL)
        out[name] = slab[r:r + n].reshape(-1)[:size].reshape(shape)
        r += n
    return out


MESH = pl.DeviceIdType.MESH
CHIP_FLIPS = ((0, 1), (1, 0), (1, 1))


def _place():
    x, y, c = lax.axis_index("x"), lax.axis_index("y"), lax.axis_index("c")
    chips = [(x ^ fx, y ^ fy) for fx, fy in CHIP_FLIPS]
    return x, y, c, chips


def _remote(src, dst, sems, k, dev):
    return pltpu.make_async_remote_copy(src_ref=src, dst_ref=dst, send_sem=sems[0].at[k], recv_sem=sems[1].at[k],
                                        device_id=dev, device_id_type=MESH)


def _exchange(copies, arrays, out_shapes, aliases=None):
    def start(ins, outs, sems):
        for sent, _ in copies(ins, outs, sems):
            sent.start()

    def wait(ins, outs, sems):
        pairs = copies(ins, outs, sems)
        for _, got in pairs:
            got.wait_recv()
        for sent, _ in pairs:
            sent.wait_send()

    return _Side(arrays, out_shapes, copies.count, start, wait, aliases)


def _counted(count):
    def mark(fn):
        fn.count = count
        return fn
    return mark


def _slab_halves(c, rows):
    half = rows // 2
    return pl.ds(c * half, half), pl.ds((1 - c) * half, half)


def _x_gather(slabs):
    @_counted(3 * len(slabs))
    def copies(ins, outs, sems):
        x, y, c, chips = _place()
        res = []
        for s, slab in enumerate(slabs):
            mine, _ = _slab_halves(c, slab.shape[0])
            for k, (px, py) in enumerate(chips):
                res.append((_remote(ins[s].at[mine], outs[s].at[k, mine], sems, 3 * s + k, (px, py, c)),) * 2)
        return res

    return _exchange(copies, slabs, [jax.ShapeDtypeStruct((3,) + s.shape, s.dtype) for s in slabs])


def _x_forward(gathered):
    @_counted(3 * len(gathered))
    def copies(ins, outs, sems):
        x, y, c, _ = _place()
        res = []
        for s, buf in enumerate(gathered):
            mine, theirs = _slab_halves(c, buf.shape[1])
            for k in range(3):
                res.append((_remote(ins[s].at[k, mine], outs[s].at[k, mine], sems, 3 * s + k, (x, y, 1 - c)),
                            _remote(ins[s].at[k, theirs], outs[s].at[k, theirs], sems, 3 * s + k, (x, y, 1 - c))))
        return res

    return _exchange(copies, gathered, [jax.ShapeDtypeStruct(g.shape, g.dtype) for g in gathered],
                     aliases={s: s for s in range(len(gathered))})


def _x_swap(grads, base=0):
    @_counted(base + N_CHIP * len(grads))
    def copies(ins, outs, sems):
        x, y, c, _ = _place()
        res = []
        for s, g in enumerate(grads):
            _, theirs = _slab_halves(c, g.shape[1])
            for j in range(N_CHIP):
                res.append((_remote(ins[s].at[j, theirs], outs[s].at[j], sems, base + N_CHIP * s + j,
                                    (x, y, 1 - c)),) * 2)
        return res

    return _exchange(copies, grads, [jax.ShapeDtypeStruct((N_CHIP, g.shape[1] // 2, g.shape[2]), g.dtype) for g in grads])


def _both(first, second):
    na, no = len(first.arrays), len(first.out_shapes)

    def start(ins, outs, sems):
        first.start(ins[:na], outs[:no], sems)
        second.start(ins[na:], outs[no:], sems)

    def wait(ins, outs, sems):
        first.wait(ins[:na], outs[:no], sems)
        second.wait(ins[na:], outs[no:], sems)

    assert not first.aliases and not second.aliases
    return _Side(first.arrays + second.arrays, first.out_shapes + second.out_shapes, second.nsem, start, wait)


def _x_scatter(partials):
    @_counted(3 * len(partials))
    def copies(ins, outs, sems):
        x, y, c, chips = _place()
        res = []
        for s in range(len(partials)):
            for k, (px, py) in enumerate(chips):
                res.append((_remote(ins[s].at[2 * px + py], outs[s].at[k], sems, 3 * s + k, (px, py, c)),) * 2)
        return res

    return _exchange(copies, partials, [jax.ShapeDtypeStruct((3,) + p.shape[1:], p.dtype) for p in partials])


def _x_join(halves):
    @_counted(len(halves))
    def copies(ins, outs, sems):
        x, y, c, _ = _place()
        return [(_remote(ins[s], outs[s], sems, s, (x, y, 1 - c)),) * 2 for s in range(len(halves))]

    return _exchange(copies, halves, [jax.ShapeDtypeStruct(h.shape, h.dtype) for h in halves])


def _run(side, name):
    n_in, n_out = len(side.arrays), len(side.out_shapes)

    def body(*refs):
        ins, outs, sems = refs[:n_in], refs[n_in:n_in + n_out], refs[-2:]
        side.start(ins, outs, sems)
        side.wait(ins, outs, sems)

    plumb = side.plumb(0, 0)
    return pl.pallas_call(
        body, name=name, in_specs=plumb["in_specs"], out_specs=plumb["out_specs"], out_shape=side.out_shapes,
        scratch_shapes=plumb["scratch"], input_output_aliases=plumb["aliases"],
    )(*side.arrays)


def _pair_sum(g, got, place, tag, *, tm):
    _, half, width = got.shape
    nb = half // tm

    def body(s_ref, g_ref, a_ref, bf_ref, own_ref):
        v = g_ref[...] + a_ref[...]
        bf_ref[...] = v.astype(BF16)

        @pl.when(pl.program_id(1) == s_ref[0])
        def _():
            own_ref[...] = v

    return pl.pallas_call(
        body, name="pair_sum_" + tag,
        grid_spec=pltpu.PrefetchScalarGridSpec(
            num_scalar_prefetch=1, grid=(nb, N_CHIP),
            in_specs=[pl.BlockSpec((None, tm, width), lambda i, j, s: (j, s[1] * nb + i, 0)),
                      pl.BlockSpec((None, tm, width), lambda i, j, s: (j, i, 0))],
            out_specs=[pl.BlockSpec((None, tm, width), lambda i, j, s: (j, i, 0)),
                       pl.BlockSpec((tm, width), lambda i, j, s: (i, 0))]),
        out_shape=[jax.ShapeDtypeStruct((N_CHIP, half, width), BF16), jax.ShapeDtypeStruct((half, width), F32)],
        compiler_params=_params("arbitrary", "arbitrary"),
    )(place, g, got)


def _chip_sum(own, got, tag, *, tm):
    half, width = own.shape

    def body(o_ref, g_ref, r_ref):
        r_ref[...] = ((o_ref[...] + g_ref[0].astype(F32)) + g_ref[1].astype(F32)) + g_ref[2].astype(F32)

    row = pl.BlockSpec((tm, width), lambda i: (i, 0))
    return pl.pallas_call(
        body, name="chip_sum_" + tag, grid=(half // tm,),
        in_specs=[row, pl.BlockSpec((3, tm, width), lambda i: (0, i, 0))], out_specs=row,
        out_shape=jax.ShapeDtypeStruct((half, width), F32), compiler_params=_params("parallel"),
    )(own, got)


def _sum_tiles(slabs):
    return [slabs[0].shape[1] // 4, D_MODEL // 8]


def _pair_sums(grads, swapped, place, tag):
    res = [_pair_sum(g, s, place, f"{tag}_{n}", tm=tm) for n, (g, s, tm) in enumerate(zip(grads, swapped, _sum_tiles(grads)))]
    return [r[0] for r in res], [r[1] for r in res]


def _finish_reduce(own, scattered, core, tag):
    mine = [_chip_sum(o, s, f"{tag}_{n}", tm=o.shape[0] // 2) for n, (o, s) in enumerate(zip(own, scattered))]
    theirs = _run(_x_join(mine), "join_halves_" + tag)
    return [lax.dynamic_update_slice(jnp.concatenate([a, a]), b, ((1 - core) * a.shape[0], 0))
            for a, b in zip(mine, theirs)]


def _gather_small(s):
    flips = [(fx, fy, fc) for fx in (0, 1) for fy in (0, 1) for fc in (0, 1)][1:]

    def body(s_ref, out_ref, send_sems, recv_sems, local_sem):
        x, y, c, _ = _place()
        sems = (send_sems, recv_sems)
        me = 4 * x + 2 * y + c
        local = pltpu.make_async_copy(s_ref, out_ref.at[me], local_sem)
        local.start()
        sent = [_remote(s_ref, out_ref.at[me], sems, k, (x ^ fx, y ^ fy, c ^ fc)) for k, (fx, fy, fc) in enumerate(flips)]
        for cp in sent:
            cp.start()
        for k, (fx, fy, fc) in enumerate(flips):
            peer = (x ^ fx, y ^ fy, c ^ fc)
            _remote(s_ref, out_ref.at[4 * peer[0] + 2 * peer[1] + peer[2]], sems, k, peer).wait_recv()
        for cp in sent:
            cp.wait_send()
        local.wait()

    return pl.pallas_call(
        body, name="gather_small", out_shape=jax.ShapeDtypeStruct((N_DEV, SMALL_ROWS, D_MODEL), s.dtype),
        in_specs=[ANY], out_specs=ANY,
        scratch_shapes=[pltpu.SemaphoreType.DMA((7,)), pltpu.SemaphoreType.DMA((7,)), pltpu.SemaphoreType.DMA],
    )(s)


LOCAL_NAMES = dict(ffn1_w_down="f1d", ffn2_w_down="f2d", w_branch_a="wa", w_branch_b="wb", w_out="wo", w_mq="wmq",
                   w_mkv="wmkv", w_mo="wmo")


def _assemble(names, own, others, me):
    by_flip = [jnp.concatenate([o[None], t], axis=0) for o, t in zip(own, others)]
    per_chip = [_unpack([lax.dynamic_index_in_dim(s, j ^ me, 0, keepdims=False) for s in by_flip], names)
                for j in range(N_CHIP)]
    return {name: jnp.concatenate([pc[name] for pc in per_chip], axis=axis)
            for name, _, axis in SHARDED if name in names}


def _local_names(full):
    out = {LOCAL_NAMES[name]: a for name, a in full.items() if name in LOCAL_NAMES}
    for name, key in (("ffn1_w_in", "f1"), ("ffn2_w_in", "f2")):
        if name in full:
            out[key + "g"], out[key + "u"] = full[name][:, :D_FF], full[name][:, D_FF:]
    if "w_in" in full:
        w_in = full["w_in"]
        out["w_main"] = jnp.concatenate([w_in[:, :FB_COL], w_in[:, FB_COL + HEADS:]], axis=1)
        out["w_fb"] = jnp.pad(w_in[:, FB_COL:FB_COL + HEADS], ((0, 0), (0, LANE - HEADS)))
    return out


def _grad_slabs(G, names):
    full = {name: G[key] for name, key in LOCAL_NAMES.items() if name in names}
    for name, key in (("ffn1_w_in", "f1"), ("ffn2_w_in", "f2")):
        if name in names:
            full[name] = jnp.concatenate([G[key + "g"], G[key + "u"]], axis=1)
    if "w_in" in names:
        main = jnp.concatenate(G["w_main"], axis=1)
        full["w_in"] = jnp.concatenate([main[:, :FB_COL], G["w_fb"][:, :HEADS], main[:, FB_COL:]], axis=1)
    rows, cols = [], []
    for j in range(N_CHIP):
        shards = {}
        for name, shape, axis in SHARDED:
            if name in names:
                n = shape[axis] // N_CHIP
                shards[name] = lax.slice_in_dim(full[name], j * n, (j + 1) * n, axis=axis)
        r, c = _pack(shards, names, F32)
        rows.append(r)
        cols.append(c)
    return [jnp.stack(rows, axis=0), jnp.stack(cols, axis=0)]


def kernel(x, mem, ffn1_pre_g, ffn1_w_in, ffn1_w_down, ffn1_post_g, mix_pre_g, w_in, hg_lb_logits, hg_norm_g, fox_f_bias, w_branch_a, w_branch_b, b_gate, w_out, mix_post_g, mem_pre_g, mem_kv_g, w_mq, w_mkv, w_mo, mem_post_g, ffn2_pre_g, ffn2_w_in, ffn2_w_down, ffn2_post_g, loss_target, m_ffn1_pre_g, m_ffn1_w_in, m_ffn1_w_down, m_ffn1_post_g, m_mix_pre_g, m_w_in, m_hg_lb_logits, m_hg_norm_g, m_fox_f_bias, m_w_branch_a, m_w_branch_b, m_b_gate, m_w_out, m_mix_post_g, m_mem_pre_g, m_mem_kv_g, m_w_mq, m_w_mkv, m_w_mo, m_mem_post_g, m_ffn2_pre_g, m_ffn2_w_in, m_ffn2_w_down, m_ffn2_post_g, v_ffn1_pre_g, v_ffn1_w_in, v_ffn1_w_down, v_ffn1_post_g, v_mix_pre_g, v_w_in, v_hg_lb_logits, v_hg_norm_g, v_fox_f_bias, v_w_branch_a, v_w_branch_b, v_b_gate, v_w_out, v_mix_post_g, v_mem_pre_g, v_mem_kv_g, v_w_mq, v_w_mkv, v_w_mo, v_mem_post_g, v_ffn2_pre_g, v_ffn2_w_in, v_ffn2_w_down, v_ffn2_post_g):
    w = dict(ffn1_pre_g=ffn1_pre_g, ffn1_w_in=ffn1_w_in, ffn1_w_down=ffn1_w_down, ffn1_post_g=ffn1_post_g, mix_pre_g=mix_pre_g, w_in=w_in, hg_lb_logits=hg_lb_logits, hg_norm_g=hg_norm_g, fox_f_bias=fox_f_bias, w_branch_a=w_branch_a, w_branch_b=w_branch_b, b_gate=b_gate, w_out=w_out, mix_post_g=mix_post_g, mem_pre_g=mem_pre_g, mem_kv_g=mem_kv_g, w_mq=w_mq, w_mkv=w_mkv, w_mo=w_mo, mem_post_g=mem_post_g, ffn2_pre_g=ffn2_pre_g, ffn2_w_in=ffn2_w_in, ffn2_w_down=ffn2_w_down, ffn2_post_g=ffn2_post_g)
    m = dict(ffn1_pre_g=m_ffn1_pre_g, ffn1_w_in=m_ffn1_w_in, ffn1_w_down=m_ffn1_w_down, ffn1_post_g=m_ffn1_post_g, mix_pre_g=m_mix_pre_g, w_in=m_w_in, hg_lb_logits=m_hg_lb_logits, hg_norm_g=m_hg_norm_g, fox_f_bias=m_fox_f_bias, w_branch_a=m_w_branch_a, w_branch_b=m_w_branch_b, b_gate=m_b_gate, w_out=m_w_out, mix_post_g=m_mix_post_g, mem_pre_g=m_mem_pre_g, mem_kv_g=m_mem_kv_g, w_mq=m_w_mq, w_mkv=m_w_mkv, w_mo=m_w_mo, mem_post_g=m_mem_post_g, ffn2_pre_g=m_ffn2_pre_g, ffn2_w_in=m_ffn2_w_in, ffn2_w_down=m_ffn2_w_down, ffn2_post_g=m_ffn2_post_g)
    v = dict(ffn1_pre_g=v_ffn1_pre_g, ffn1_w_in=v_ffn1_w_in, ffn1_w_down=v_ffn1_w_down, ffn1_post_g=v_ffn1_post_g, mix_pre_g=v_mix_pre_g, w_in=v_w_in, hg_lb_logits=v_hg_lb_logits, hg_norm_g=v_hg_norm_g, fox_f_bias=v_fox_f_bias, w_branch_a=v_w_branch_a, w_branch_b=v_w_branch_b, b_gate=v_b_gate, w_out=v_w_out, mix_post_g=v_mix_post_g, mem_pre_g=v_mem_pre_g, mem_kv_g=v_mem_kv_g, w_mq=v_w_mq, w_mkv=v_w_mkv, w_mo=v_w_mo, mem_post_g=v_mem_post_g, ffn2_pre_g=v_ffn2_pre_g, ffn2_w_in=v_ffn2_w_in, ffn2_w_down=v_ffn2_w_down, ffn2_post_g=v_ffn2_post_g)
    sharded = [name for name, _, _ in SHARDED]
    shard_of = lambda d: {name: d[name][0] for name in sharded}

    me, core = 2 * lax.axis_index("x") + lax.axis_index("y"), lax.axis_index("c")
    place = jnp.stack([me, core]).astype(jnp.int32)
    own = {group: _pack(shard_of(w), group, BF16) for group in (GROUP_FFN1, GROUP_MID, GROUP_FFN2)}
    P = {name: w[name] for name in SMALL}

    sq, dx0, G, reduced = _local_step(x[0], mem[0], loss_target[0], P, own, me, place, core)
    loss = lax.psum(0.5 * jnp.sum(sq) / D_MODEL, ("x", "y", "c"))

    g_shards = {}
    for group, slabs in reduced.items():
        g_shards.update(_unpack(slabs, group))
    big = {}
    for name, shape, axis in SHARDED:
        rows = shape[0] // (N_CHIP if axis == 0 else 1)
        big[name] = _adamw(w[name][0], g_shards[name], m[name][0], v[name][0], name="adamw_" + name, tm=rows // 8)
    small = _adamw(_pack_small(w), _gather_small(_pack_small(G)), _pack_small(m), _pack_small(v), name="adamw_small",
                   tm=SMALL_ROWS)

    outs = [loss, dx0[None]]
    for n in range(4):
        vals = {name: res[n][None] for name, res in big.items()}
        vals.update(_unpack_small(small[n]))
        outs += [vals[name] for name in WEIGHT_ORDER]
    return tuple(outs)
```

```python
import functools
import math

import jax
import jax.numpy as jnp
from jax import lax
from jax.experimental import pallas as pl
from jax.experimental.pallas import tpu as pltpu

F32 = jnp.float32
BF16 = jnp.bfloat16

D_MODEL = 1024
D_FF = 2816
HEADS = 8
DH = 128
MEM_HEADS = 4
MEM_DH = 256
MEM_LEN = 256
EPS = 1e-6
SUB = 16
LANE = 128
SUBLANE = 8
VMEM_LIMIT = 56 * 1024 * 1024

ADAM_LR = 0.001
ADAM_B1 = 0.9
ADAM_B2 = 0.999
ADAM_EPS = 1e-08
ADAM_WD = 0.01
ADAM_STEP = 10

HIGHEST = lax.Precision.HIGHEST


def _params(*sem):
    return pltpu.CompilerParams(dimension_semantics=sem, vmem_limit_bytes=VMEM_LIMIT)


def _sigmoid(v):
    return 0.5 * jnp.tanh(0.5 * v) + 0.5


def _silu(v):
    return v * _sigmoid(v)


def _dsilu(v):
    s = _sigmoid(v)
    return s * (1.0 + v * (1.0 - s))


def _dot(a, b, dims):
    return lax.dot_general(a.astype(BF16), b.astype(BF16), (dims, ((), ())), preferred_element_type=F32)


NN = ((1,), (0,))
NT = ((1,), (1,))
TN = ((0,), (0,))


ANY = pl.BlockSpec(memory_space=pl.ANY)


class _Side:
    def __init__(self, arrays, out_shapes, nsem, start, wait, aliases=None):
        self.arrays, self.out_shapes, self.nsem = list(arrays), list(out_shapes), nsem
        self.start, self.wait, self.aliases = start, wait, dict(aliases or {})

    def plumb(self, n_in, n_out):
        return dict(args=self.arrays, in_specs=[ANY] * len(self.arrays), out_specs=[ANY] * len(self.out_shapes),
                    scratch=[pltpu.SemaphoreType.DMA((self.nsem,)), pltpu.SemaphoreType.DMA((self.nsem,))],
                    aliases={n_in + i: n_out + o for i, o in self.aliases.items()})

    def run_at_ends(self, ins, outs, sems, first, last, compute):
        @pl.when(first)
        def _():
            self.start(ins, outs, sems)

        compute()

        @pl.when(last)
        def _():
            self.wait(ins, outs, sems)


def _grid_ends(grid):
    first = functools.reduce(lambda a, b: a & b, [pl.program_id(d) == 0 for d in range(len(grid))])
    last = functools.reduce(lambda a, b: a & b, [pl.program_id(d) == grid[d] - 1 for d in range(len(grid))])
    return first, last


def _mm(pairs, mode, *, tm, tn, tk, out_dtypes, name, epilogue=None, tiles=(), b_koff=None, side=None):
    a0, b0 = pairs[0]
    if mode == "nn":
        (M, K), N = a0.shape, b0.shape[1]
    elif mode == "nt":
        (M, K), N = a0.shape, b0.shape[0]
    else:
        (K, M), N = a0.shape, b0.shape[1]
    tm, tn, tk = min(tm, M), min(tn, N), min(tk, K)
    assert M % tm == 0 and N % tn == 0 and K % tk == 0, (name, M, N, K, tm, tn, tk)
    nk = K // tk
    npair = len(pairs)
    koff = [0] * npair if b_koff is None else [o // tk for o in b_koff]
    if b_koff is not None:
        assert all(o % tk == 0 for o in b_koff)
    in_specs, args = [], []
    for p, (a, b) in enumerate(pairs):
        if mode == "nn":
            sa = pl.BlockSpec((tm, tk), lambda i, j, k: (i, k))
            sb = pl.BlockSpec((tk, tn), lambda i, j, k, o=koff[p]: (k + o, j))
            dims = NN
        elif mode == "nt":
            sa = pl.BlockSpec((tm, tk), lambda i, j, k: (i, k))
            sb = pl.BlockSpec((tn, tk), lambda i, j, k, o=koff[p]: (j, k + o))
            dims = NT
        else:
            sa = pl.BlockSpec((tk, tm), lambda i, j, k: (k, i))
            sb = pl.BlockSpec((tk, tn), lambda i, j, k, o=koff[p]: (k + o, j))
            dims = TN
        in_specs += [sa, sb]
        args += [a, b]
    for t in tiles:
        in_specs.append(pl.BlockSpec((tm, tn), lambda i, j, k: (i, j)))
        args.append(t)
    nt_ = len(tiles)
    nout = len(out_dtypes)
    nin = len(args)
    grid = (M // tm, N // tn, nk)
    plumb = side.plumb(nin, nout) if side is not None else None
    ns_in, ns_out = (len(side.arrays), len(side.out_shapes)) if side is not None else (0, 0)

    def body(*refs):
        ab = refs[: 2 * npair]
        tl = refs[2 * npair: nin]
        outs = refs[nin + ns_in: nin + ns_in + nout]
        scratch = refs[nin + ns_in + nout + ns_out:]
        acc_ref = scratch[0] if nk > 1 else None
        if side is None:
            compute(ab, tl, outs, acc_ref)
        else:
            first, last = _grid_ends(grid)
            side.run_at_ends(refs[nin: nin + ns_in], refs[nin + ns_in + nout: nin + ns_in + nout + ns_out],
                             scratch[-2:], first, last, lambda: compute(ab, tl, outs, acc_ref))

    def compute(ab, tl, outs, acc_ref):
        def partial_sum():
            s = _dot(ab[0][...], ab[1][...], dims)
            for p in range(1, npair):
                s = s + _dot(ab[2 * p][...], ab[2 * p + 1][...], dims)
            return s

        def finish(acc):
            res = (acc,) if epilogue is None else epilogue(acc, *[t[...] for t in tl])
            for o, r in zip(outs, res):
                o[...] = r.astype(o.dtype)

        if nk == 1:
            finish(partial_sum())
        else:
            k = pl.program_id(2)

            @pl.when(k == 0)
            def _():
                acc_ref[...] = jnp.zeros_like(acc_ref)

            acc_ref[...] += partial_sum()

            @pl.when(k == nk - 1)
            def _():
                finish(acc_ref[...])

    out_shape = [jax.ShapeDtypeStruct((M, N), dt) for dt in out_dtypes]
    out_specs = [pl.BlockSpec((tm, tn), lambda i, j, k: (i, j)) for _ in out_dtypes]
    scratch = [pltpu.VMEM((tm, tn), F32)] if nk > 1 else []
    if side is None:
        res = pl.pallas_call(
            body, name=name, grid=grid, in_specs=in_specs, out_specs=out_specs, out_shape=out_shape,
            scratch_shapes=scratch, compiler_params=_params("parallel", "parallel", "arbitrary"),
        )(*args)
        return res[0] if nout == 1 else res
    res = pl.pallas_call(
        body, name=name, grid=grid, in_specs=in_specs + plumb["in_specs"], out_specs=out_specs + plumb["out_specs"],
        out_shape=out_shape + side.out_shapes, scratch_shapes=scratch + plumb["scratch"],
        input_output_aliases=plumb["aliases"], compiler_params=_params("arbitrary", "arbitrary", "arbitrary"),
    )(*args, *plumb["args"])
    return res[:nout], res[nout:]


def _col(arr, tm, width, cb):
    return pl.BlockSpec((tm, width), lambda i, cb=cb: (i, cb))


def _rms_fwd(x, g, *, out_dtype, name, mul=None, res=None, coeff=1.0, tm=512):
    T, D = x.shape
    tm = min(tm, T)
    args, in_specs = [x, g], [pl.BlockSpec((tm, D), lambda i: (i, 0)), pl.BlockSpec((1, D), lambda i: (0, 0))]
    if mul is not None:
        args.append(mul[0])
        in_specs.append(_col(mul[0], tm, D, mul[1]))
    if res is not None:
        args.append(res)
        in_specs.append(pl.BlockSpec((tm, D), lambda i: (i, 0)))

    def body(*refs):
        xv = refs[0][...].astype(F32)
        r = lax.rsqrt(jnp.mean(xv * xv, axis=-1, keepdims=True) + EPS)
        y = (xv * r) * refs[1][...]
        n = 2
        if mul is not None:
            y = y * _silu(refs[n][...])
            n += 1
        if res is not None:
            y = refs[n][...] + coeff * y
        refs[-1][...] = y.astype(out_dtype)

    return pl.pallas_call(
        body, name=name, grid=(T // tm,), in_specs=in_specs, out_specs=pl.BlockSpec((tm, D), lambda i: (i, 0)),
        out_shape=jax.ShapeDtypeStruct((T, D), out_dtype), compiler_params=_params("parallel"),
    )(*args)


def _fold8(v):
    tm, d = v.shape
    return v.reshape(tm // SUBLANE, SUBLANE, d).sum(axis=0)


def _rms_bwd(x, g, dy, *, name, coeff=1.0, add=None, mul=None, dx_dtype=F32, tm=512, side=None):
    T, D = x.shape
    tm = min(tm, T)
    row = pl.BlockSpec((tm, D), lambda i: (i, 0))
    args, in_specs = [x, g, dy], [row, pl.BlockSpec((1, D), lambda i: (0, 0)), row]
    if add is not None:
        args.append(add)
        in_specs.append(row)
    if mul is not None:
        args.append(mul[0])
        in_specs.append(_col(mul[0], tm, D, mul[1]))
    nin = len(args)
    nout = 3 if mul is not None else 2
    ns_in, ns_out = (len(side.arrays), len(side.out_shapes)) if side is not None else (0, 0)

    def body(*refs):
        outs = refs[nin + ns_in: nin + ns_in + nout]
        if side is None:
            compute(refs[:nin] + outs)
        else:
            first, last = _grid_ends((T // tm,))
            side.run_at_ends(refs[nin: nin + ns_in], refs[nin + ns_in + nout: nin + ns_in + nout + ns_out], refs[-2:],
                             first, last, lambda: compute(refs[:nin] + outs))

    def compute(refs):
        xv = refs[0][...].astype(F32)
        gv = refs[1][...]
        dyv = refs[2][...].astype(F32) * coeff
        r = lax.rsqrt(jnp.mean(xv * xv, axis=-1, keepdims=True) + EPS)
        nrm = xv * r
        n = 3
        addv = None
        if add is not None:
            addv = refs[n][...]
            n += 1
        if mul is not None:
            mv = refs[n][...]
            sm = _silu(mv)
            refs[nin + 2][...] = (dyv * nrm * gv * _dsilu(mv)).astype(refs[nin + 2].dtype)
            dyv = dyv * sm
        dn = dyv * gv
        dx = r * (dn - nrm * jnp.mean(dn * nrm, axis=-1, keepdims=True))
        if addv is not None:
            dx = dx + addv
        refs[nin][...] = dx.astype(dx_dtype)
        dg_ref = refs[nin + 1]

        @pl.when(pl.program_id(0) == 0)
        def _():
            dg_ref[...] = jnp.zeros_like(dg_ref)

        dg_ref[...] += _fold8(dyv * nrm)

    out_shape = [jax.ShapeDtypeStruct((T, D), dx_dtype), jax.ShapeDtypeStruct((SUBLANE, D), F32)]
    out_specs = [row, pl.BlockSpec((SUBLANE, D), lambda i: (0, 0))]
    if mul is not None:
        out_shape.append(jax.ShapeDtypeStruct((T, D), BF16))
        out_specs.append(row)
    if side is None:
        return pl.pallas_call(
            body, name=name, grid=(T // tm,), in_specs=in_specs, out_specs=out_specs, out_shape=out_shape,
            compiler_params=_params("arbitrary"),
        )(*args)
    plumb = side.plumb(nin, nout)
    res = pl.pallas_call(
        body, name=name, grid=(T // tm,), in_specs=in_specs + plumb["in_specs"],
        out_specs=out_specs + plumb["out_specs"], out_shape=out_shape + side.out_shapes,
        scratch_shapes=plumb["scratch"], input_output_aliases=plumb["aliases"], compiler_params=_params("arbitrary"),
    )(*args, *plumb["args"])
    res = list(res)
    return res[:nout] + [res[nout:]]


def _ffn_in(h, wg, wu, *, name, tm=1024, tn=256, side=None):
    T, D = h.shape
    F = wg.shape[1]
    tm = min(tm, T)
    assert F % tn == 0
    grid = (T // tm, F // tn)
    ns_in, ns_out = (len(side.arrays), len(side.out_shapes)) if side is not None else (0, 0)

    def compute(h_ref, wg_ref, wu_ref, a_ref, g_ref, u_ref):
        hv = h_ref[...]
        gt = _dot(hv, wg_ref[...], NN)
        up = _dot(hv, wu_ref[...], NN)
        a_ref[...] = (_silu(gt) * up).astype(BF16)
        g_ref[...] = gt.astype(BF16)
        u_ref[...] = up.astype(BF16)

    def body(*refs):
        if side is None:
            compute(*refs)
        else:
            outs0 = 3 + ns_in
            first, last = _grid_ends(grid)
            side.run_at_ends(refs[3:outs0], refs[outs0 + 3: outs0 + 3 + ns_out], refs[-2:], first, last,
                             lambda: compute(*refs[:3], *refs[outs0: outs0 + 3]))

    o = pl.BlockSpec((tm, tn), lambda i, j: (i, j))
    w = pl.BlockSpec((D, tn), lambda i, j: (0, j))
    in_specs = [pl.BlockSpec((tm, D), lambda i, j: (i, 0)), w, w]
    out_shape = [jax.ShapeDtypeStruct((T, F), BF16)] * 3
    if side is None:
        return pl.pallas_call(body, name=name, grid=grid, in_specs=in_specs, out_specs=[o, o, o], out_shape=out_shape,
                              compiler_params=_params("parallel", "parallel"))(h, wg, wu)
    plumb = side.plumb(3, 3)
    res = pl.pallas_call(
        body, name=name, grid=grid, in_specs=in_specs + plumb["in_specs"], out_specs=[o, o, o] + plumb["out_specs"],
        out_shape=out_shape + side.out_shapes, scratch_shapes=plumb["scratch"], input_output_aliases=plumb["aliases"],
        compiler_params=_params("arbitrary", "arbitrary"),
    )(h, wg, wu, *plumb["args"])
    return res[:3], res[3:]


def _swiglu_bwd_epilogue(da, gt, up):
    gt = gt.astype(F32)
    up = up.astype(F32)
    return da * up * _dsilu(gt), da * _silu(gt)


GATE_CB = 7


def _gatemix_fwd(z, b_gate, ya, yb, *, name, tm=512):
    T, D = ya.shape
    tm = min(tm, T)
    row = pl.BlockSpec((tm, D), lambda i: (i, 0))

    def body(z0, z1, b0, b1, ya_ref, yb_ref, y_ref):
        g0 = _sigmoid(z0[...] + b0[...])
        g1 = _sigmoid(z1[...] + b1[...])
        y_ref[...] = (g0 * ya_ref[...] + g1 * yb_ref[...]).astype(y_ref.dtype)

    bs = lambda c: pl.BlockSpec((1, D), lambda i, c=c: (0, c))
    return pl.pallas_call(
        body, name=name, grid=(T // tm,),
        in_specs=[_col(z, tm, D, GATE_CB), _col(z, tm, D, GATE_CB + 1), bs(0), bs(1), row, row],
        out_specs=row, out_shape=jax.ShapeDtypeStruct((T, D), BF16), compiler_params=_params("parallel"),
    )(z, z, b_gate, b_gate, ya, yb)


def _gatemix_bwd(z, b_gate, ya, yb, dy, *, name, tm=512):
    T, D = ya.shape
    tm = min(tm, T)
    row = pl.BlockSpec((tm, D), lambda i: (i, 0))
    part = pl.BlockSpec((SUBLANE, D), lambda i: (0, 0))

    def body(z0, z1, b0, b1, ya_ref, yb_ref, dy_ref, dya, dyb, dz0, dz1, s0, s1):
        g0 = _sigmoid(z0[...] + b0[...])
        g1 = _sigmoid(z1[...] + b1[...])
        dyv = dy_ref[...]
        dya[...] = (dyv * g0).astype(BF16)
        dyb[...] = (dyv * g1).astype(BF16)
        d0 = dyv * ya_ref[...] * (g0 * (1.0 - g0))
        d1 = dyv * yb_ref[...] * (g1 * (1.0 - g1))
        dz0[...] = d0.astype(BF16)
        dz1[...] = d1.astype(BF16)

        @pl.when(pl.program_id(0) == 0)
        def _():
            s0[...] = jnp.zeros_like(s0)
            s1[...] = jnp.zeros_like(s1)

        s0[...] += _fold8(d0)
        s1[...] += _fold8(d1)

    bs = lambda c: pl.BlockSpec((1, D), lambda i, c=c: (0, c))
    act = jax.ShapeDtypeStruct((T, D), BF16)
    ps = jax.ShapeDtypeStruct((SUBLANE, D), F32)
    return pl.pallas_call(
        body, name=name, grid=(T // tm,),
        in_specs=[_col(z, tm, D, GATE_CB), _col(z, tm, D, GATE_CB + 1), bs(0), bs(1), row, row, row],
        out_specs=[row, row, row, row, part, part], out_shape=[act, act, act, act, ps, ps],
        compiler_params=_params("arbitrary"),
    )(z, z, b_gate, b_gate, ya, yb, dy)


def _loss_head(x, target, *, name, tm=512):
    T, D = x.shape
    tm = min(tm, T)
    row = pl.BlockSpec((tm, D), lambda i: (i, 0))

    def body(x_ref, t_ref, dx_ref, s_ref):
        e = x_ref[...] - t_ref[...]
        dx_ref[...] = e * (1.0 / D)

        @pl.when(pl.program_id(0) == 0)
        def _():
            s_ref[...] = jnp.zeros_like(s_ref)

        s_ref[...] += _fold8(e * e)

    return pl.pallas_call(
        body, name=name, grid=(T // tm,), in_specs=[row, row],
        out_specs=[row, pl.BlockSpec((SUBLANE, D), lambda i: (0, 0))],
        out_shape=[jax.ShapeDtypeStruct((T, D), F32), jax.ShapeDtypeStruct((SUBLANE, D), F32)],
        compiler_params=_params("arbitrary"),
    )(x, target)


def _tri(n, reverse):
    r = lax.broadcasted_iota(jnp.int32, (n, n), 0)
    c = lax.broadcasted_iota(jnp.int32, (n, n), 1)
    return jnp.where((c >= r) if reverse else (c <= r), 1.0, 0.0).astype(F32)


def _cumsum_t(xs, *, name, width, pre, reverse=False, rows=(), post=None, out_dtypes=(F32,), fold=None, tb=256):
    T = xs[0][0].shape[0]
    tb = min(tb, T)
    nb = T // tb
    tblk = (lambda i: nb - 1 - i) if reverse else (lambda i: i)
    args = [a for a, _ in xs] + [a for a, _ in rows]
    in_specs = [pl.BlockSpec((tb, width), lambda i, cb=cb: (tblk(i), cb)) for _, cb in xs]
    in_specs += [pl.BlockSpec((1, width), lambda i, cb=cb: (0, cb)) for _, cb in rows]
    nin, nout = len(args), len(out_dtypes)

    def body(*refs):
        vals = [r[...] for r in refs[:nin]]
        outs = refs[nin:nin + nout]
        carry = refs[-1]
        first = pl.program_id(0) == 0

        @pl.when(first)
        def _():
            carry[...] = jnp.zeros_like(carry)

        cum = jnp.dot(_tri(tb, reverse), pre(*vals), precision=HIGHEST, preferred_element_type=F32) + carry[...]
        carry[...] = cum[0:1, :] if reverse else cum[tb - 1:tb, :]
        res = (cum,) if post is None else post(cum, *vals)
        for o, r in zip(outs, res):
            o[...] = r.astype(o.dtype)
        if fold is not None:
            f_ref = refs[nin + nout]

            @pl.when(first)
            def _():
                f_ref[...] = jnp.zeros_like(f_ref)

            f_ref[...] += _fold8(fold(cum, *vals))

    tspec = pl.BlockSpec((tb, width), lambda i: (tblk(i), 0))
    out_shape = [jax.ShapeDtypeStruct((T, width), dt) for dt in out_dtypes]
    out_specs = [tspec] * nout
    if fold is not None:
        out_shape.append(jax.ShapeDtypeStruct((SUBLANE, width), F32))
        out_specs.append(pl.BlockSpec((SUBLANE, width), lambda i: (0, 0)))
    res = pl.pallas_call(
        body, name=name, grid=(nb,), in_specs=in_specs, out_specs=out_specs, out_shape=out_shape,
        scratch_shapes=[pltpu.VMEM((1, width), F32)], compiler_params=_params("arbitrary"),
    )(*args)
    return res[0] if len(res) == 1 else res


def _logsigmoid(v):
    return jnp.minimum(v, 0.0) - jnp.log(1.0 + jnp.exp(-jnp.abs(v)))


HG_TB = 256
HG_HB = 4
HG_W = HG_HB * DH
HG_GROUPS = HEADS // HG_HB
HG_Q_CB, HG_F_CB, HG_I_CB = 0, HG_GROUPS, 2 * HG_GROUPS
NEG = -1e30


def _scan16(x, rowid, reverse=False):
    for k in [1 << n for n in range(SUB.bit_length() - 1)]:
        if reverse:
            x = x + jnp.where(rowid < SUB - k, pltpu.roll(x, SUB - k, 0), 0.0)
        else:
            x = x + jnp.where(rowid >= k, pltpu.roll(x, k, 0), 0.0)
    return x


def _hg_block(q_ref, f_ref, i_ref, lb_ref, rows, cols, rowid):
    lb = lb_ref[:, cols]
    qr = q_ref[rows, cols]
    sg = _sigmoid(f_ref[rows, cols])
    f = lb + (1.0 - lb) * sg
    b = _scan16(jnp.log(f), rowid)
    return _silu(qr), 1.0 - f, i_ref[rows, cols], b, qr, sg, f, lb


def _hg_specs(tb, tmap):
    return [pl.BlockSpec((tb, HG_W), lambda g, t, *_: (tmap(t), HG_Q_CB + g)),
            pl.BlockSpec((tb, HG_W), lambda g, t, *_: (tmap(t), HG_F_CB + g)),
            pl.BlockSpec((tb, HG_W), lambda g, t, *_: (tmap(t), HG_I_CB + g)),
            pl.BlockSpec((1, HG_W), lambda g, t, *_: (0, g))]


def _hgrn2_fwd(z, lb_row, *, name):
    T = z.shape[0]
    tb = min(HG_TB, T)
    nb, nsub = T // tb, tb // SUB

    def body(q_ref, f_ref, i_ref, lb_ref, o_ref, st_ref, state):
        @pl.when(pl.program_id(1) == 0)
        def _():
            state[...] = jnp.zeros_like(state)

        rowid = lax.broadcasted_iota(jnp.int32, (SUB, DH), 0)

        def step(c, carry):
            rows = pl.ds(pl.multiple_of(c * SUB, SUB), SUB)
            for hh in range(HG_HB):
                cols = slice(hh * DH, (hh + 1) * DH)
                q, k, iv, b = _hg_block(q_ref, f_ref, i_ref, lb_ref, rows, cols, rowid)[:4]
                bl = b[SUB - 1:SUB, :]
                sv = state[hh]
                st_ref[c, hh] = sv
                o = _dot(q * jnp.exp(b), sv, NT)
                for s in range(SUB):
                    e = jnp.exp(jnp.where(rowid >= s, b - b[s:s + 1, :], NEG))
                    a = jnp.sum(q * e * k[s:s + 1, :], axis=-1, keepdims=True)
                    o = o + a * iv[s:s + 1, :]
                o_ref[rows, cols] = o
                state[hh] = sv * jnp.exp(bl) + _dot(iv, k * jnp.exp(bl - b), TN)
            return carry

        lax.fori_loop(0, nsub, step, 0)

    return pl.pallas_call(
        body, name=name, grid=(HG_GROUPS, nb), in_specs=_hg_specs(tb, lambda t: t),
        out_specs=[pl.BlockSpec((tb, HG_W), lambda g, t: (t, g)),
                   pl.BlockSpec((nsub, HG_HB, DH, DH), lambda g, t: (t, g, 0, 0))],
        out_shape=[jax.ShapeDtypeStruct((T, D_MODEL), F32), jax.ShapeDtypeStruct((T // SUB, HEADS, DH, DH), F32)],
        scratch_shapes=[pltpu.VMEM((HG_HB, DH, DH), F32)], compiler_params=_params("parallel", "arbitrary"),
    )(z, z, z, lb_row)


def _hgrn2_bwd(z, lb_row, states, do, *, name):
    T = z.shape[0]
    tb = min(HG_TB, T)
    nb, nsub = T // tb, tb // SUB
    rev = lambda t: nb - 1 - t

    def body(q_ref, f_ref, i_ref, lb_ref, st_ref, do_ref, dq_ref, dfl_ref, di_ref, dlb_ref, dstate, later):
        @pl.when(pl.program_id(1) == 0)
        def _():
            dstate[...] = jnp.zeros_like(dstate)
            later[...] = jnp.zeros_like(later)
            dlb_ref[...] = jnp.zeros_like(dlb_ref)

        rowid = lax.broadcasted_iota(jnp.int32, (SUB, DH), 0)

        def step(cc, carry):
            c = nsub - 1 - cc
            rows = pl.ds(pl.multiple_of(c * SUB, SUB), SUB)
            for hh in range(HG_HB):
                cols = slice(hh * DH, (hh + 1) * DH)
                q, k, iv, b, qr, sg, f, lb = _hg_block(q_ref, f_ref, i_ref, lb_ref, rows, cols, rowid)
                bl = b[SUB - 1:SUB, :]
                eb, ebl = jnp.exp(b), jnp.exp(bl - b)
                sv, dsv = st_ref[c, hh], dstate[hh]
                dov = do_ref[rows, cols]
                dq = _dot(dov, sv, NN) * eb
                dk = _dot(iv, dsv, NN) * ebl
                di = _dot(k * ebl, dsv, NT)
                for s in range(SUB):
                    e = jnp.exp(jnp.where(rowid >= s, b - b[s:s + 1, :], NEG))
                    ks, isv = k[s:s + 1, :], iv[s:s + 1, :]
                    qe = q * e
                    a = jnp.sum(qe * ks, axis=-1, keepdims=True)
                    p = jnp.sum(dov * isv, axis=-1, keepdims=True)
                    dq = dq + p * (e * ks)
                    dks = jnp.sum(p * qe, axis=0, keepdims=True)
                    dis = jnp.sum(a * dov, axis=0, keepdims=True)
                    dk = dk + jnp.where(rowid == s, dks, 0.0)
                    di = di + jnp.where(rowid == s, dis, 0.0)
                dlogf = _scan16(q * dq - k * dk, rowid, reverse=True) + later[hh]
                df = dlogf / f - dk
                dlb_ref[:, cols] += jnp.sum(df * (1.0 - sg), axis=0, keepdims=True)
                dfl_ref[rows, cols] = (df * (1.0 - lb) * (sg * (1.0 - sg))).astype(BF16)
                dq_ref[rows, cols] = (dq * _dsilu(qr)).astype(BF16)
                di_ref[rows, cols] = di.astype(BF16)
                dnew = dsv * jnp.exp(bl) + _dot(dov, q * eb, TN)
                dstate[hh] = dnew
                later[hh] = jnp.sum(dnew * sv, axis=0, keepdims=True)
            return carry

        lax.fori_loop(0, nsub, step, 0)

    tile = pl.BlockSpec((tb, HG_W), lambda g, t: (rev(t), g))
    act = jax.ShapeDtypeStruct((T, D_MODEL), BF16)
    return pl.pallas_call(
        body, name=name, grid=(HG_GROUPS, nb),
        in_specs=_hg_specs(tb, rev) + [pl.BlockSpec((nsub, HG_HB, DH, DH), lambda g, t: (rev(t), g, 0, 0)), tile],
        out_specs=[tile, tile, tile, pl.BlockSpec((1, HG_W), lambda g, t: (0, g))],
        out_shape=[act, act, act, jax.ShapeDtypeStruct((1, D_MODEL), F32)],
        scratch_shapes=[pltpu.VMEM((HG_HB, DH, DH), F32), pltpu.VMEM((HG_HB, 1, DH), F32)],
        compiler_params=_params("parallel", "arbitrary"),
    )(z, z, z, lb_row, states, do)


FOX_Q_CB, FOX_K_CB, FOX_V_CB = 4 * HEADS, 5 * HEADS, 6 * HEADS
FOX_SCALE = 1.0 / math.sqrt(DH)


def _fox_tile(T):
    return 512 if T >= 2048 else 128


def _fox_pairs(nq, by_query):
    if by_query:
        pairs = [(i, j) for i in range(nq) for j in range(i + 1)]
    else:
        pairs = [(i, j) for j in range(nq) for i in range(j, nq)]
    return (jnp.asarray([p[0] for p in pairs], jnp.int32), jnp.asarray([p[1] for p in pairs], jnp.int32))


LOG2E = 1.4426950408889634
FOX_RC = 64
FOX_HB = 2
FOX_HB_FWD = 4


def _fox_q2(q):
    return (q * (FOX_SCALE * LOG2E)).astype(BF16)


FOX_ZERO = -200.0


def _fox_norms(z, *, name):
    T = z.shape[0]
    tq = _fox_tile(T)
    nq = T // tq

    def body(q_ref, k_ref, nq_ref, nk_ref):
        head_of_col = lax.broadcasted_iota(jnp.int32, (D_MODEL, LANE), 0) // DH
        pick = jnp.where(head_of_col == lax.broadcasted_iota(jnp.int32, (D_MODEL, LANE), 1), 1.0, 0.0).astype(BF16)

        def tile_max(v):
            v = v.astype(F32)
            sq = _dot(v * v, pick, NN)
            return jnp.broadcast_to(jnp.max(jnp.sqrt(sq), axis=0, keepdims=True), (SUBLANE, LANE))

        nq_ref[...] = tile_max(_fox_q2(q_ref[...]))
        nk_ref[...] = tile_max(k_ref[...].astype(BF16))

    out = jax.ShapeDtypeStruct((nq * SUBLANE, LANE), F32)
    spec = pl.BlockSpec((SUBLANE, LANE), lambda i: (i, 0))
    a, b = pl.pallas_call(
        body, name=name, grid=(nq,),
        in_specs=[pl.BlockSpec((tq, D_MODEL), lambda i: (i, FOX_Q_CB // HEADS)),
                  pl.BlockSpec((tq, D_MODEL), lambda i: (i, FOX_K_CB // HEADS))],
        out_specs=[spec, spec], out_shape=[out, out], compiler_params=_params("parallel"),
    )(z, z)
    return a[::SUBLANE, :HEADS], b[::SUBLANE, :HEADS]


def _fox_schedule(norm_q, norm_k, ct, tq):
    nq = ct.shape[1] // tq
    first, last = ct[:, ::tq], ct[:, tq - 1::tq]
    nqh, nkh = norm_q.T * 1.05, norm_k.T * 1.05
    bound = nqh[:, :, None] * (nkh[:, None, :] + nkh[:, :, None]) + first[:, :, None] - last[:, None, :]
    tri = jnp.arange(nq)[:, None] > jnp.arange(nq)[None, :]
    drop = (bound < FOX_ZERO) & tri[None]
    lo = jnp.argmin(drop, axis=2).astype(jnp.int32)
    dropped = jnp.arange(nq)[None, None, :] < lo[:, :, None]
    group_lo = lambda hb: jnp.min(lo.reshape(HEADS // hb, hb, nq), axis=1)
    dropped_g = jnp.arange(nq)[None, None, :] < group_lo(FOX_HB)[:, :, None]
    qf, kf = _fox_pairs(nq, by_query=True)
    qb, kb = _fox_pairs(nq, by_query=False)
    fetch_k = jnp.maximum(kf[None, :], group_lo(FOX_HB_FWD)[:, qf])
    kept_q = jnp.where(dropped_g | ~(tri | jnp.eye(nq, dtype=bool))[None], -1, jnp.arange(nq)[None, :, None])
    last_kept = lax.cummax(kept_q, axis=1)
    fetch_q = last_kept[:, qb, kb]
    i32 = lambda a: a.astype(jnp.int32)
    return i32(fetch_k), i32(dropped[:, qf, kf]), i32(fetch_q), i32(dropped[:, qb, kb])


def _fox_fwd(z, c_col, c_row, fetch_k, skip, *, name):
    T = z.shape[0]
    tq = _fox_tile(T)
    nq = T // tq
    rc = min(FOX_RC, tq)
    hb = FOX_HB_FWD

    qi, kj = _fox_pairs(nq, by_query=True)

    def body(qi_ref, kj_ref, fk_ref, skip_ref, q_ref, k_ref, v_ref, cc_ref, cr_ref, o_ref, lse_ref, m_scr, l_scr, acc,
             a_scr, s_scr, p_scr):
        p_id = pl.program_id(1)
        i, j = qi_ref[p_id], kj_ref[p_id]

        @pl.when(j == 0)
        def _():
            m_scr[...] = jnp.full_like(m_scr, NEG)
            l_scr[...] = jnp.zeros_like(l_scr)
            acc[...] = jnp.zeros_like(acc)

        def update(hh, masked):
            cols = slice(hh * DH, (hh + 1) * DH)
            bias = cc_ref[hh, 0:1, :] - cr_ref[hh]
            s_scr[hh] = _dot(_fox_q2(q_ref[:, cols]), k_ref[:, cols], NT)
            for r in range(tq // rc):
                rows = slice(r * rc, (r + 1) * rc)
                t = s_scr[hh, rows, :] + bias
                if masked:
                    t = jnp.where(lax.broadcasted_iota(jnp.int32, (rc, tq), 1)
                                  <= r * rc + lax.broadcasted_iota(jnp.int32, (rc, tq), 0), t, NEG)
                m_old = m_scr[hh, rows, :]
                m_new = jnp.maximum(m_old, jnp.max(t, axis=-1, keepdims=True))
                alpha = jnp.exp2(m_old - m_new)
                p = jnp.exp2(t - jnp.tile(m_new, (1, tq // LANE)))
                l_scr[hh, rows, :] = alpha * l_scr[hh, rows, :] + jnp.sum(p, axis=-1, keepdims=True)
                a_scr[hh, rows, :] = alpha
                p_scr[hh, rows, :] = p.astype(BF16)
                m_scr[hh, rows, :] = m_new
            acc[hh] = a_scr[hh] * acc[hh] + _dot(p_scr[hh], v_ref[:, cols], NN)

        for hh in range(hb):
            live = skip_ref[pl.program_id(0) * hb + hh, p_id] == 0

            @pl.when((j < i) & live)
            def _():
                update(hh, False)

            @pl.when(j == i)
            def _():
                update(hh, True)
                o_ref[:, hh * DH:(hh + 1) * DH] = acc[hh] / l_scr[hh]
                lse_ref[hh] = (m_scr[hh, :, 0:1] + jnp.log2(l_scr[hh, :, 0:1])) + (cc_ref[hh] - cc_ref[hh, 0:1, :])

    wide = hb * DH
    qtile = lambda cb: pl.BlockSpec((tq, wide), lambda g, p, qi, kj, fk, sk, cb=cb: (qi[p], cb // hb + g))
    ktile = lambda cb: pl.BlockSpec((tq, wide), lambda g, p, qi, kj, fk, sk, cb=cb: (fk[g, p], cb // hb + g))
    qcol = pl.BlockSpec((hb, tq, 1), lambda g, p, qi, kj, fk, sk: (g, qi[p], 0))
    stat = pltpu.VMEM((hb, tq, LANE), F32)
    return pl.pallas_call(
        body, name=name,
        grid_spec=pltpu.PrefetchScalarGridSpec(
            num_scalar_prefetch=4, grid=(HEADS // hb, qi.shape[0]),
            in_specs=[qtile(FOX_Q_CB), ktile(FOX_K_CB), ktile(FOX_V_CB), qcol,
                      pl.BlockSpec((hb, 1, tq), lambda g, p, qi, kj, fk, sk: (g, 0, fk[g, p]))],
            out_specs=[qtile(0), qcol],
            scratch_shapes=[stat, stat, pltpu.VMEM((hb, tq, DH), F32), stat,
                            pltpu.VMEM((hb, tq, tq), F32), pltpu.VMEM((hb, tq, tq), BF16)]),
        out_shape=[jax.ShapeDtypeStruct((T, D_MODEL), F32), jax.ShapeDtypeStruct((HEADS, T, 1), F32)],
        compiler_params=_params("parallel", "arbitrary"),
    )(qi, kj, fetch_k, skip, z, z, z, c_col, c_row)


def _fox_bwd(z, c_col, c_row, o, lse, do, fetch_q, skip, *, name):
    T = z.shape[0]
    tq = _fox_tile(T)
    nq = T // tq
    rc = min(FOX_RC, tq)

    qi, kj = _fox_pairs(nq, by_query=False)

    def body(qi_ref, kj_ref, fq_ref, skip_ref, q_ref, k_ref, v_ref, cc_ref, cr_ref, o_ref, lse_ref, do_ref, dq_ref,
             dk_ref, dv_ref, dc_ref, dcq_ref, dk_acc, dv_acc, dc_acc, s_scr, dp_scr, p_scr, ds_scr, dcq_scr):
        p_id = pl.program_id(1)
        i, j = qi_ref[p_id], kj_ref[p_id]

        @pl.when(p_id == 0)
        def _():
            dq_ref[...] = jnp.zeros_like(dq_ref)
            dcq_scr[...] = jnp.zeros_like(dcq_scr)

        def update(hh, masked):
            cols = slice(hh * DH, (hh + 1) * DH)
            q2, k, dov = _fox_q2(q_ref[:, cols]), k_ref[:, cols], do_ref[:, cols]
            s_scr[hh] = _dot(q2, k, NT)
            dp_scr[hh] = _dot(dov, v_ref[:, cols], NT)
            crow = cr_ref[hh]
            csum = jnp.zeros((SUBLANE, tq), F32)
            wide = lambda col: jnp.tile(jnp.broadcast_to(col, (rc, LANE)), (1, tq // LANE))
            for r in range(tq // rc):
                rows = slice(r * rc, (r + 1) * rc)
                t = (s_scr[hh, rows, :] + wide(cc_ref[hh, rows, :] - lse_ref[hh, rows, :])) - crow
                if masked:
                    t = jnp.where(lax.broadcasted_iota(jnp.int32, (rc, tq), 1)
                                  <= r * rc + lax.broadcasted_iota(jnp.int32, (rc, tq), 0), t, NEG)
                p = jnp.exp2(t)
                delta = jnp.sum(do_ref[rows, cols] * o_ref[rows, cols], axis=-1, keepdims=True)
                ds = p * (dp_scr[hh, rows, :] - wide(delta))
                p_scr[hh, rows, :] = p.astype(BF16)
                ds_scr[hh, rows, :] = ds.astype(BF16)
                grows = pl.ds(pl.multiple_of(i * tq + r * rc, rc), rc)
                dcq_scr[hh, grows, :] += jnp.broadcast_to(jnp.sum(ds, axis=-1, keepdims=True), (rc, LANE))
                csum = csum + _fold8(ds)
            dsb = ds_scr[hh]
            dv_new = _dot(p_scr[hh], dov, TN)
            dk_new = _dot(dsb, q2, TN) * (1.0 / LOG2E)
            dc_new = -jnp.sum(csum, axis=0, keepdims=True)
            rows = pl.ds(pl.multiple_of(i * tq, tq), tq)
            dq_ref[rows, cols] += _dot(dsb, k, NN) * FOX_SCALE
            return dk_new, dv_new, dc_new

        for hh in range(FOX_HB):
            live = skip_ref[pl.program_id(0) * FOX_HB + hh, p_id] == 0

            @pl.when(i == j)
            def _():
                dk_new, dv_new, dc_new = update(hh, True)
                dk_acc[hh] = dk_new
                dv_acc[hh] = dv_new
                dc_acc[hh] = dc_new

            @pl.when((i > j) & live)
            def _():
                dk_new, dv_new, dc_new = update(hh, False)
                dk_acc[hh] += dk_new
                dv_acc[hh] += dv_new
                dc_acc[hh] += dc_new

            @pl.when(i == nq - 1)
            def _():
                dk_ref[:, hh * DH:(hh + 1) * DH] = dk_acc[hh].astype(BF16)
                dv_ref[:, hh * DH:(hh + 1) * DH] = dv_acc[hh].astype(BF16)
                dc_ref[hh] = dc_acc[hh]

            @pl.when(p_id == qi.shape[0] - 1)
            def _():
                for r in range(nq):
                    rows = slice(r * tq, (r + 1) * tq)
                    dcq_ref[hh, :, rows] = jnp.transpose(dcq_scr[hh, rows, :])[0:1, :]

    wide_cols = FOX_HB * DH
    n_groups = HEADS // FOX_HB
    qtile = lambda cb: pl.BlockSpec((tq, wide_cols), lambda g, p, qi, kj, fq, sk, cb=cb: (fq[g, p], cb // FOX_HB + g))
    ktile = lambda cb: pl.BlockSpec((tq, wide_cols), lambda g, p, qi, kj, fq, sk, cb=cb: (kj[p], cb // FOX_HB + g))
    qcol = pl.BlockSpec((FOX_HB, tq, 1), lambda g, p, qi, kj, fq, sk: (g, fq[g, p], 0))
    krow = pl.BlockSpec((FOX_HB, 1, tq), lambda g, p, qi, kj, fq, sk: (g, 0, kj[p]))
    tile_f32 = pltpu.VMEM((FOX_HB, tq, tq), F32)
    tile_bf16 = pltpu.VMEM((FOX_HB, tq, tq), BF16)
    return pl.pallas_call(
        body, name=name,
        grid_spec=pltpu.PrefetchScalarGridSpec(
            num_scalar_prefetch=4, grid=(n_groups, qi.shape[0]),
            in_specs=[qtile(FOX_Q_CB), ktile(FOX_K_CB), ktile(FOX_V_CB), qcol, krow, qtile(0), qcol, qtile(0)],
            out_specs=[pl.BlockSpec((T, wide_cols), lambda g, p, qi, kj, fq, sk: (0, g)), ktile(0), ktile(0), krow,
                       pl.BlockSpec((FOX_HB, 1, T), lambda g, p, qi, kj, fq, sk: (g, 0, 0))],
            scratch_shapes=[pltpu.VMEM((FOX_HB, tq, DH), F32), pltpu.VMEM((FOX_HB, tq, DH), F32),
                            pltpu.VMEM((FOX_HB, 1, tq), F32), tile_f32, tile_f32, tile_bf16, tile_bf16,
                            pltpu.VMEM((FOX_HB, T, LANE), F32)]),
        out_shape=[jax.ShapeDtypeStruct((T, D_MODEL), F32), jax.ShapeDtypeStruct((T, D_MODEL), BF16),
                   jax.ShapeDtypeStruct((T, D_MODEL), BF16), jax.ShapeDtypeStruct((HEADS, 1, T), F32),
                   jax.ShapeDtypeStruct((HEADS, 1, T), F32)],
        compiler_params=_params("parallel", "arbitrary"),
    )(qi, kj, fetch_q, skip, z, z, z, c_col, c_row, o, lse, do)


MEM_SCALE = 1.0 / math.sqrt(MEM_DH)


def _mem_probs(qh, kh):
    s = _dot(qh, kh, NT) * MEM_SCALE
    p = jnp.exp(s - jnp.max(s, axis=-1, keepdims=True))
    return p / jnp.sum(p, axis=-1, keepdims=True)


def _mem_fwd(q, kv, *, name, tq=512):
    T = q.shape[0]
    tq = min(tq, T)

    def body(q_ref, kv_ref, o_ref):
        for h in range(MEM_HEADS):
            cols = slice(h * MEM_DH, (h + 1) * MEM_DH)
            vcols = slice(D_MODEL + h * MEM_DH, D_MODEL + (h + 1) * MEM_DH)
            p = _mem_probs(q_ref[:, cols], kv_ref[:, cols])
            o_ref[:, cols] = _dot(p, kv_ref[:, vcols], NN).astype(o_ref.dtype)

    return pl.pallas_call(
        body, name=name, grid=(T // tq,),
        in_specs=[pl.BlockSpec((tq, D_MODEL), lambda i: (i, 0)), pl.BlockSpec((MEM_LEN, 2 * D_MODEL), lambda i: (0, 0))],
        out_specs=pl.BlockSpec((tq, D_MODEL), lambda i: (i, 0)), out_shape=jax.ShapeDtypeStruct((T, D_MODEL), BF16),
        compiler_params=_params("parallel"),
    )(q, kv)


def _mem_bwd(q, kv, do, *, name, tq=512):
    T = q.shape[0]
    tq = min(tq, T)

    def body(q_ref, kv_ref, do_ref, dq_ref, dkv_ref):
        @pl.when(pl.program_id(0) == 0)
        def _():
            dkv_ref[...] = jnp.zeros_like(dkv_ref)

        for h in range(MEM_HEADS):
            cols = slice(h * MEM_DH, (h + 1) * MEM_DH)
            vcols = slice(D_MODEL + h * MEM_DH, D_MODEL + (h + 1) * MEM_DH)
            qh, kh, doh = q_ref[:, cols], kv_ref[:, cols], do_ref[:, cols]
            p = _mem_probs(qh, kh)
            dp = _dot(doh, kv_ref[:, vcols], NT)
            ds = p * (dp - jnp.sum(p * dp, axis=-1, keepdims=True))
            dq_ref[:, cols] = (_dot(ds, kh, NN) * MEM_SCALE).astype(dq_ref.dtype)
            dkv_ref[:, cols] += _dot(ds, qh, TN) * MEM_SCALE
            dkv_ref[:, vcols] += _dot(p, doh, TN)

    row = pl.BlockSpec((tq, D_MODEL), lambda i: (i, 0))
    full = pl.BlockSpec((MEM_LEN, 2 * D_MODEL), lambda i: (0, 0))
    return pl.pallas_call(
        body, name=name, grid=(T // tq,), in_specs=[row, full, row], out_specs=[row, full],
        out_shape=[jax.ShapeDtypeStruct((T, D_MODEL), BF16), jax.ShapeDtypeStruct((MEM_LEN, 2 * D_MODEL), F32)],
        compiler_params=_params("arbitrary"),
    )(q, kv, do)


def _adamw(w, g, m, v, *, name, tm=256):
    R, C = w.shape
    tm = min(tm, R)
    assert R % tm == 0
    nsum = g.shape[0] if g.ndim == 3 else 0

    def body(w_ref, g_ref, m_ref, v_ref, go_ref, d_ref, mo_ref, vo_ref):
        if nsum:
            gv = g_ref[0]
            for n in range(1, nsum):
                gv = gv + g_ref[n]
        else:
            gv = g_ref[...]
        mv = ADAM_B1 * m_ref[...] + (1.0 - ADAM_B1) * gv
        vv = ADAM_B2 * v_ref[...] + (1.0 - ADAM_B2) * jnp.square(gv)
        m_hat = mv / (1.0 - ADAM_B1 ** ADAM_STEP)
        v_hat = vv / (1.0 - ADAM_B2 ** ADAM_STEP)
        d_ref[...] = -ADAM_LR * (m_hat / (jnp.sqrt(v_hat) + ADAM_EPS) + ADAM_WD * w_ref[...])
        go_ref[...] = gv
        mo_ref[...] = mv
        vo_ref[...] = vv

    row = pl.BlockSpec((tm, C), lambda i: (i, 0))
    gspec = pl.BlockSpec((nsum, tm, C), lambda i: (0, i, 0)) if nsum else row
    return pl.pallas_call(
        body, name=name, grid=(R // tm,), in_specs=[row, gspec, row, row], out_specs=[row] * 4,
        out_shape=[jax.ShapeDtypeStruct((R, C), F32)] * 4, compiler_params=_params("parallel"),
    )(w, g, m, v)


def _act_mm(a, w, name, out_dtype=F32, side=None):
    res = _mm([(a, w)], "nn", tm=1024, tn=1024, tk=w.shape[0], out_dtypes=[out_dtype], name=name, side=side)
    return res if side is None else (res[0][0], res[1])


def _act_mm_t(a, w, name, out_dtype=F32, side=None):
    res = _mm([(a, w)], "nt", tm=1024, tn=1024, tk=1024, out_dtypes=[out_dtype], name=name, side=side)
    return res if side is None else (res[0][0], res[1])


def _wgrad(a, dy, name, tm=1024, side=None):
    tn = D_MODEL if dy.shape[1] % D_MODEL == 0 else D_FF // 2
    res = _mm([(a, dy)], "tn", tm=tm, tn=tn, tk=1024, out_dtypes=[F32], name=name, side=side)
    return res if side is None else (res[0][0], res[1])


def _colsum8(p):
    return jnp.sum(p, axis=0, keepdims=True)


def _ffn_fwd(x, pre_g, post_g, wg, wu, wd, tag, gather_beside=None):
    h = _rms_fwd(x, pre_g, out_dtype=BF16, name=tag + "_pre")
    down = functools.partial(_mm, mode="nn", tm=1024, tn=512, tk=D_FF, out_dtypes=[F32], name=tag + "_down")
    gathered = None
    if gather_beside is None:
        act, gate, up = _ffn_in(h, wg, wu, name=tag + "_in")
        d = down([(act, wd)])
    else:
        (act, gate, up), landed = _ffn_in(h, wg, wu, name=tag + "_in", side=_x_gather(gather_beside))
        (d,), gathered = down([(act, wd)], side=_x_forward(landed))
    xo = _rms_fwd(d, post_g, out_dtype=F32, name=tag + "_post", res=x, coeff=0.5)
    return xo, (h, act, gate, up, d), gathered


def _ffn_bwd(x, dxo, saved, pre_g, post_g, wg, wu, wd, tag, reduce_beside=None):
    h, act, gate, up, d = saved
    dd, dg_post = _rms_bwd(d, post_g, dxo, name=tag + "_post_b", coeff=0.5, dx_dtype=BF16)
    act_b = functools.partial(_mm, [(dd, wd)], "nt", tm=1024, tn=256, tk=D_MODEL, out_dtypes=[BF16, BF16],
                              name=tag + "_act_b", epilogue=_swiglu_bwd_epilogue, tiles=(gate, up))
    in_b = lambda dgate, dup, **kw: _mm([(dgate, wg), (dup, wu)], "nt", tm=512, tn=512, tk=D_FF, out_dtypes=[F32],
                                        name=tag + "_in_b", **kw)
    dwd_of = functools.partial(_wgrad, act, dd, tag + "_dwd", tm=D_FF // 2)
    reduced = None
    if reduce_beside is None:
        dgate, dup = act_b()
        dh = in_b(dgate, dup)
        dwd = dwd_of()
    else:
        grads, place, core = reduce_beside
        (dgate, dup), swapped = act_b(side=_x_swap(grads))
        pbf, own = _pair_sums(grads, swapped, place, "mid")
        dwd, landed_rows = dwd_of(side=_x_scatter(pbf[:1]))
    dwg = _wgrad(h, dgate, tag + "_dwg")
    dwu = _wgrad(h, dup, tag + "_dwu")
    if reduce_beside is not None:
        mine = _grad_slabs(dict(f1g=dwg, f1u=dwu, f1d=dwd), GROUP_FFN1)
        scatter = _x_scatter(pbf[1:])
        (dh,), landed = in_b(dgate, dup, side=_both(scatter, _x_swap(mine, base=scatter.nsem)))
        reduced_other = _finish_reduce(own, landed_rows + landed[:1], core, "mid")
        pbf, own = _pair_sums(mine, landed[1:], place, "late")
        dx, dg_pre, landed = _rms_bwd(x, pre_g, dh, name=tag + "_pre_b", add=dxo, side=_x_scatter(pbf))
        reduced = (reduced_other, _finish_reduce(own, landed, core, "late"))
    else:
        dx, dg_pre = _rms_bwd(x, pre_g, dh, name=tag + "_pre_b", add=dxo)
    return dx, dict(pre_g=_colsum8(dg_pre), post_g=_colsum8(dg_post), wg=dwg, wu=dwu, wd=dwd), reduced


def _local_step(x, mem, target, P, own, me, place, core):
    T = x.shape[0]
    G = {}
    logits = P["hg_lb_logits"]
    lb = _sigmoid(logits[0] - logits[1])
    lb_row = lb.reshape(1, D_MODEL)
    fbias_row = jnp.pad(P["fox_f_bias"], ((0, 0), (0, LANE - HEADS)))
    arrived = lambda group, others: _local_names(_assemble(group, own[group], others, me))

    W = arrived(GROUP_FFN1, _run(_x_forward(_run(_x_gather(own[GROUP_FFN1]), "gather_ffn1")), "forward_ffn1"))
    x1, ffn1_saved, others = _ffn_fwd(x, P["ffn1_pre_g"], P["ffn1_post_g"], W["f1g"], W["f1u"], W["f1d"], "ffn1",
                                      gather_beside=own[GROUP_MID])
    W.update(arrived(GROUP_MID, others))
    h2 = _rms_fwd(x1, P["mix_pre_g"], out_dtype=BF16, name="mix_pre")
    z, landed = _act_mm(h2, W["w_main"], "mix_in", side=_x_gather(own[GROUP_FFN2]))
    zfb = _mm([(h2, W["w_fb"])], "nn", tm=1024, tn=LANE, tk=D_MODEL, out_dtypes=[F32], name="mix_in_fb")
    oa_pre, states = _hgrn2_fwd(z, lb_row, name="hgrn2_f")
    o_a = _rms_fwd(oa_pre, P["hg_norm_g"], out_dtype=BF16, name="hgrn2_post", mul=(z, 3))
    y_a, others = _act_mm(o_a, W["wa"], "branch_a", side=_x_forward(landed))
    W.update(arrived(GROUP_FFN2, others))
    c = _cumsum_t([(zfb, 0)], name="fox_c", width=LANE, rows=[(fbias_row, 0)], pre=lambda v, r: _logsigmoid(v + r),
                  post=lambda cum, v, r: (cum * LOG2E,))
    ct = c[:, :HEADS].T
    c_col, c_row = ct[:, :, None], ct[:, None, :]
    fetch_k, skip_f, fetch_q, skip_b = _fox_schedule(*_fox_norms(z, name="fox_norms"), ct, _fox_tile(T))
    o_b, lse = _fox_fwd(z, c_col, c_row, fetch_k, skip_f, name="fox_f")
    y_b = _act_mm(o_b, W["wb"], "branch_b")
    y = _gatemix_fwd(z, P["b_gate"], y_a, y_b, name="gatemix")
    m = _act_mm(y, W["wo"], "mix_out")
    x2 = _rms_fwd(m, P["mix_post_g"], out_dtype=F32, name="mix_post", res=x1)
    h3 = _rms_fwd(x2, P["mem_pre_g"], out_dtype=BF16, name="mem_pre")
    mem_n = _rms_fwd(mem, P["mem_kv_g"], out_dtype=BF16, name="mem_kvn")
    qm = _act_mm(h3, W["wmq"], "mem_q")
    kv = _act_mm(mem_n, W["wmkv"], "mem_kv")
    om = _mem_fwd(qm, kv, name="mem_attn")
    mo = _act_mm(om, W["wmo"], "mem_o")
    x3 = _rms_fwd(mo, P["mem_post_g"], out_dtype=F32, name="mem_post", res=x2)
    x4, ffn2_saved, _ = _ffn_fwd(x3, P["ffn2_pre_g"], P["ffn2_post_g"], W["f2g"], W["f2u"], W["f2d"], "ffn2")
    dx4, sq = _loss_head(x4, target, name="loss_head")

    dx3, g, _ = _ffn_bwd(x3, dx4, ffn2_saved, P["ffn2_pre_g"], P["ffn2_post_g"], W["f2g"], W["f2u"], W["f2d"], "ffn2")
    G.update(ffn2_pre_g=g["pre_g"], ffn2_post_g=g["post_g"], f2g=g["wg"], f2u=g["wu"], f2d=g["wd"])

    dmo, dgp = _rms_bwd(mo, P["mem_post_g"], dx3, name="mem_post_b", dx_dtype=BF16)
    G["mem_post_g"] = _colsum8(dgp)
    g_ffn2 = _grad_slabs(G, GROUP_FFN2)
    dom, swapped = _act_mm_t(dmo, W["wmo"], "mem_o_b", BF16, side=_x_swap(g_ffn2))
    pbf_ffn2, own_ffn2 = _pair_sums(g_ffn2, swapped, place, "ffn2")
    G["wmo"] = _wgrad(om, dmo, "mem_o_w")
    dqm, dkv = _mem_bwd(qm, kv, dom, name="mem_attn_b")
    dh3 = _act_mm_t(dqm, W["wmq"], "mem_q_b")
    G["wmq"] = _wgrad(h3, dqm, "mem_q_w")
    G["wmkv"] = _mm([(mem_n, dkv)], "tn", tm=1024, tn=512, tk=MEM_LEN, out_dtypes=[F32], name="mem_kv_w")
    dmem_n = _mm([(dkv, W["wmkv"])], "nt", tm=MEM_LEN, tn=512, tk=2 * D_MODEL, out_dtypes=[F32], name="mem_kv_b")
    _, dgp = _rms_bwd(mem, P["mem_kv_g"], dmem_n, name="mem_kvn_b")
    G["mem_kv_g"] = _colsum8(dgp)
    dx2, dgp = _rms_bwd(x2, P["mem_pre_g"], dh3, name="mem_pre_b", add=dx3)
    G["mem_pre_g"] = _colsum8(dgp)

    dm, dgp = _rms_bwd(m, P["mix_post_g"], dx2, name="mix_post_b", dx_dtype=BF16)
    G["mix_post_g"] = _colsum8(dgp)
    dy = _act_mm_t(dm, W["wo"], "mix_out_b")
    G["wo"] = _wgrad(y, dm, "mix_out_w")
    dya, dyb, dz0, dz1, s0, s1 = _gatemix_bwd(z, P["b_gate"], y_a, y_b, dy, name="gatemix_b")
    G["b_gate"] = jnp.concatenate([_colsum8(s0), _colsum8(s1)], axis=1)
    do_a = _act_mm_t(dya, W["wa"], "branch_a_b")
    G["wa"] = _wgrad(o_a, dya, "branch_a_w")
    do_b = _act_mm_t(dyb, W["wb"], "branch_b_b")
    G["wb"] = _wgrad(o_b, dyb, "branch_b_w")
    doa_pre, dgp, dga = _rms_bwd(oa_pre, P["hg_norm_g"], do_a, name="hgrn2_post_b", mul=(z, 3))
    G["hg_norm_g"] = _colsum8(dgp)
    dq_a, dfl_a, di_a, dlb = _hgrn2_bwd(z, lb_row, states, doa_pre, name="hgrn2_b")
    dl0 = (dlb * lb_row * (1.0 - lb_row)).reshape(1, HEADS, DH)
    G["hg_lb_logits"] = jnp.concatenate([dl0, -dl0], axis=0)
    dq_b, dk_b, dv_b, dcr, dcq = _fox_bwd(z, c_col, c_row, o_b, lse, do_b, fetch_q, skip_b, name="fox_b")
    dc_pad = jnp.pad((dcr[:, 0, :] + dcq[:, 0, :]).T, ((0, 0), (0, LANE - HEADS)))
    gate_b = lambda cum, dc, zf, r: cum * _sigmoid(-(zf + r))
    dfl_b, dfb = _cumsum_t([(dc_pad, 0), (zfb, 0)], name="fox_c_b", width=LANE, reverse=True, rows=[(fbias_row, 0)],
                           pre=lambda dc, zf, r: dc, post=lambda *a: (gate_b(*a),), fold=gate_b)
    G["fox_f_bias"] = _colsum8(dfb)[:, :HEADS]

    pieces = [dq_a, dfl_a, di_a, dga, dq_b, dk_b, dv_b, dz0, dz1]
    dh2 = _mm([(dfl_b, W["w_fb"])], "nt", tm=512, tn=D_MODEL, tk=LANE, out_dtypes=[F32], name="mix_in_fb_b")
    for lo, hi in ((0, 5), (5, 9)):
        res = _mm([(p, W["w_main"]) for p in pieces[lo:hi]], "nt", tm=512, tn=D_MODEL, tk=D_MODEL, out_dtypes=[F32],
                  name=f"mix_in_b{lo}", b_koff=[n * D_MODEL for n in range(lo, hi)],
                  epilogue=lambda acc, t: (acc + t,), tiles=(dh2,), side=_x_scatter(pbf_ffn2) if lo == 0 else None)
        dh2, scattered = (res[0][0], res[1]) if lo == 0 else (res, scattered)
    reduced_ffn2 = _finish_reduce(own_ffn2, scattered, core, "ffn2")
    G["w_main"] = [_wgrad(h2, p, f"mix_in_w{n}") for n, p in enumerate(pieces)]
    G["w_fb"] = _mm([(h2, dfl_b)], "tn", tm=1024, tn=LANE, tk=512, out_dtypes=[F32], name="mix_in_fb_w")
    dx1, dgp = _rms_bwd(x1, P["mix_pre_g"], dh2, name="mix_pre_b", add=dx2)
    G["mix_pre_g"] = _colsum8(dgp)

    dx0, g, reduced = _ffn_bwd(x, dx1, ffn1_saved, P["ffn1_pre_g"], P["ffn1_post_g"], W["f1g"], W["f1u"], W["f1d"],
                               "ffn1", reduce_beside=(_grad_slabs(G, GROUP_MID), place, core))
    G.update(ffn1_pre_g=g["pre_g"], ffn1_post_g=g["post_g"], f1g=g["wg"], f1u=g["wu"], f1d=g["wd"])
    return sq, dx0, G, {GROUP_FFN2: reduced_ffn2, GROUP_MID: reduced[0], GROUP_FFN1: reduced[1]}


N_CHIP = 4
N_DEV = 8
IN_COLS = 9224
FB_COL = 7 * D_MODEL
SHARDED = (
    ("ffn1_w_in", (D_MODEL, 2 * D_FF), 1), ("ffn1_w_down", (D_FF, D_MODEL), 0), ("w_in", (D_MODEL, IN_COLS), 1),
    ("w_branch_a", (D_MODEL, D_MODEL), 0), ("w_branch_b", (D_MODEL, D_MODEL), 0), ("w_out", (D_MODEL, D_MODEL), 0),
    ("w_mq", (D_MODEL, D_MODEL), 0), ("w_mkv", (D_MODEL, 2 * D_MODEL), 1), ("w_mo", (D_MODEL, D_MODEL), 0),
    ("ffn2_w_in", (D_MODEL, 2 * D_FF), 1), ("ffn2_w_down", (D_FF, D_MODEL), 0),
)
SMALL = ("ffn1_pre_g", "ffn1_post_g", "mix_pre_g", "hg_norm_g", "mix_post_g", "mem_pre_g", "mem_kv_g", "mem_post_g",
         "ffn2_pre_g", "ffn2_post_g", "b_gate", "hg_lb_logits", "fox_f_bias")
SMALL_SHAPES = dict(b_gate=(1, 2 * D_MODEL), hg_lb_logits=(2, HEADS, DH), fox_f_bias=(1, HEADS))
SMALL_ROWS = 16
WEIGHT_ORDER = ("ffn1_pre_g", "ffn1_w_in", "ffn1_w_down", "ffn1_post_g", "mix_pre_g", "w_in", "hg_lb_logits", "hg_norm_g",
                "fox_f_bias", "w_branch_a", "w_branch_b", "b_gate", "w_out", "mix_post_g", "mem_pre_g", "mem_kv_g", "w_mq",
                "w_mkv", "w_mo", "mem_post_g", "ffn2_pre_g", "ffn2_w_in", "ffn2_w_down", "ffn2_post_g")


GROUP_FFN1 = ("ffn1_w_in", "ffn1_w_down")
GROUP_MID = ("w_in", "w_branch_a", "w_branch_b", "w_out", "w_mq", "w_mkv", "w_mo")
GROUP_FFN2 = ("ffn2_w_in", "ffn2_w_down")


def _layout(names, axis):
    out, at = [], 0
    for name, shape, ax in SHARDED:
        if ax == axis and name in names:
            n = shape[ax] // N_CHIP
            out.append((name, at, n))
            at += n if axis == 0 else -(-n // LANE) * LANE
    return out


def _pack(shards, names, dtype):
    rows = jnp.concatenate([shards[name].astype(dtype) for name, _, _ in _layout(names, 0)], axis=0)
    cols = [jnp.pad(shards[name].astype(dtype), ((0, 0), (0, -n % LANE))) for name, _, n in _layout(names, 1)]
    return [rows, jnp.concatenate(cols, axis=1)]


def _unpack(slabs, names):
    rows, cols = slabs
    out = {name: rows[at:at + n] for name, at, n in _layout(names, 0)}
    out.update({name: cols[:, at:at + n] for name, at, n in _layout(names, 1)})
    return out


def _pack_small(vals):
    rows = []
    for name in SMALL:
        v = vals[name].astype(F32).reshape(-1)
        rows.append(jnp.pad(v, (0, -v.shape[0] % D_MODEL)).reshape(-1, D_MODEL))
    rows = jnp.concatenate(rows, axis=0)
    return jnp.pad(rows, ((0, SMALL_ROWS - rows.shape[0]), (0, 0)))


def _unpack_small(slab):
    out, r = {}, 0
    for name in SMALL:
        shape = SMALL_SHAPES.get(name, (1, D_MODEL))
        size = math.prod(shape)
        n = -(-size // D_MODEL)
        out[name] = slab[r:r + n].reshape(-1)[:size].reshape(shape)
        r += n
    return out


MESH = pl.DeviceIdType.MESH
CHIP_FLIPS = ((0, 1), (1, 0), (1, 1))


def _place():
    x, y, c = lax.axis_index("x"), lax.axis_index("y"), lax.axis_index("c")
    chips = [(x ^ fx, y ^ fy) for fx, fy in CHIP_FLIPS]
    return x, y, c, chips


def _remote(src, dst, sems, k, dev):
    return pltpu.make_async_remote_copy(src_ref=src, dst_ref=dst, send_sem=sems[0].at[k], recv_sem=sems[1].at[k],
                                        device_id=dev, device_id_type=MESH)


def _exchange(copies, arrays, out_shapes, aliases=None):
    def start(ins, outs, sems):
        for sent, _ in copies(ins, outs, sems):
            sent.start()

    def wait(ins, outs, sems):
        pairs = copies(ins, outs, sems)
        for _, got in pairs:
            got.wait_recv()
        for sent, _ in pairs:
            sent.wait_send()

    return _Side(arrays, out_shapes, copies.count, start, wait, aliases)


def _counted(count):
    def mark(fn):
        fn.count = count
        return fn
    return mark


def _slab_halves(c, rows):
    half = rows // 2
    return pl.ds(c * half, half), pl.ds((1 - c) * half, half)


def _x_gather(slabs):
    @_counted(3 * len(slabs))
    def copies(ins, outs, sems):
        x, y, c, chips = _place()
        res = []
        for s, slab in enumerate(slabs):
            mine, _ = _slab_halves(c, slab.shape[0])
            for k, (px, py) in enumerate(chips):
                res.append((_remote(ins[s].at[mine], outs[s].at[k, mine], sems, 3 * s + k, (px, py, c)),) * 2)
        return res

    return _exchange(copies, slabs, [jax.ShapeDtypeStruct((3,) + s.shape, s.dtype) for s in slabs])


def _x_forward(gathered):
    @_counted(3 * len(gathered))
    def copies(ins, outs, sems):
        x, y, c, _ = _place()
        res = []
        for s, buf in enumerate(gathered):
            mine, theirs = _slab_halves(c, buf.shape[1])
            for k in range(3):
                res.append((_remote(ins[s].at[k, mine], outs[s].at[k, mine], sems, 3 * s + k, (x, y, 1 - c)),
                            _remote(ins[s].at[k, theirs], outs[s].at[k, theirs], sems, 3 * s + k, (x, y, 1 - c))))
        return res

    return _exchange(copies, gathered, [jax.ShapeDtypeStruct(g.shape, g.dtype) for g in gathered],
                     aliases={s: s for s in range(len(gathered))})


def _x_swap(grads, base=0):
    @_counted(base + N_CHIP * len(grads))
    def copies(ins, outs, sems):
        x, y, c, _ = _place()
        res = []
        for s, g in enumerate(grads):
            _, theirs = _slab_halves(c, g.shape[1])
            for j in range(N_CHIP):
                res.append((_remote(ins[s].at[j, theirs], outs[s].at[j], sems, base + N_CHIP * s + j,
                                    (x, y, 1 - c)),) * 2)
        return res

    return _exchange(copies, grads, [jax.ShapeDtypeStruct((N_CHIP, g.shape[1] // 2, g.shape[2]), g.dtype) for g in grads])


def _both(first, second):
    na, no = len(first.arrays), len(first.out_shapes)

    def start(ins, outs, sems):
        first.start(ins[:na], outs[:no], sems)
        second.start(ins[na:], outs[no:], sems)

    def wait(ins, outs, sems):
        first.wait(ins[:na], outs[:no], sems)
        second.wait(ins[na:], outs[no:], sems)

    assert not first.aliases and not second.aliases
    return _Side(first.arrays + second.arrays, first.out_shapes + second.out_shapes, second.nsem, start, wait)


def _x_scatter(partials):
    @_counted(3 * len(partials))
    def copies(ins, outs, sems):
        x, y, c, chips = _place()
        res = []
        for s in range(len(partials)):
            for k, (px, py) in enumerate(chips):
                res.append((_remote(ins[s].at[2 * px + py], outs[s].at[k], sems, 3 * s + k, (px, py, c)),) * 2)
        return res

    return _exchange(copies, partials, [jax.ShapeDtypeStruct((3,) + p.shape[1:], p.dtype) for p in partials])


def _x_join(halves):
    @_counted(len(halves))
    def copies(ins, outs, sems):
        x, y, c, _ = _place()
        return [(_remote(ins[s], outs[s], sems, s, (x, y, 1 - c)),) * 2 for s in range(len(halves))]

    return _exchange(copies, halves, [jax.ShapeDtypeStruct(h.shape, h.dtype) for h in halves])


def _run(side, name):
    n_in, n_out = len(side.arrays), len(side.out_shapes)

    def body(*refs):
        ins, outs, sems = refs[:n_in], refs[n_in:n_in + n_out], refs[-2:]
        side.start(ins, outs, sems)
        side.wait(ins, outs, sems)

    plumb = side.plumb(0, 0)
    return pl.pallas_call(
        body, name=name, in_specs=plumb["in_specs"], out_specs=plumb["out_specs"], out_shape=side.out_shapes,
        scratch_shapes=plumb["scratch"], input_output_aliases=plumb["aliases"],
    )(*side.arrays)


def _pair_sum(g, got, place, tag, *, tm):
    _, half, width = got.shape
    nb = half // tm

    def body(s_ref, g_ref, a_ref, bf_ref, own_ref):
        v = g_ref[...] + a_ref[...]
        bf_ref[...] = v.astype(BF16)

        @pl.when(pl.program_id(1) == s_ref[0])
        def _():
            own_ref[...] = v

    return pl.pallas_call(
        body, name="pair_sum_" + tag,
        grid_spec=pltpu.PrefetchScalarGridSpec(
            num_scalar_prefetch=1, grid=(nb, N_CHIP),
            in_specs=[pl.BlockSpec((None, tm, width), lambda i, j, s: (j, s[1] * nb + i, 0)),
                      pl.BlockSpec((None, tm, width), lambda i, j, s: (j, i, 0))],
            out_specs=[pl.BlockSpec((None, tm, width), lambda i, j, s: (j, i, 0)),
                       pl.BlockSpec((tm, width), lambda i, j, s: (i, 0))]),
        out_shape=[jax.ShapeDtypeStruct((N_CHIP, half, width), BF16), jax.ShapeDtypeStruct((half, width), F32)],
        compiler_params=_params("arbitrary", "arbitrary"),
    )(place, g, got)


def _chip_sum(own, got, tag, *, tm):
    half, width = own.shape

    def body(o_ref, g_ref, r_ref):
        r_ref[...] = ((o_ref[...] + g_ref[0].astype(F32)) + g_ref[1].astype(F32)) + g_ref[2].astype(F32)

    row = pl.BlockSpec((tm, width), lambda i: (i, 0))
    return pl.pallas_call(
        body, name="chip_sum_" + tag, grid=(half // tm,),
        in_specs=[row, pl.BlockSpec((3, tm, width), lambda i: (0, i, 0))], out_specs=row,
        out_shape=jax.ShapeDtypeStruct((half, width), F32), compiler_params=_params("parallel"),
    )(own, got)


def _sum_tiles(slabs):
    return [slabs[0].shape[1] // 4, D_MODEL // 8]


def _pair_sums(grads, swapped, place, tag):
    res = [_pair_sum(g, s, place, f"{tag}_{n}", tm=tm) for n, (g, s, tm) in enumerate(zip(grads, swapped, _sum_tiles(grads)))]
    return [r[0] for r in res], [r[1] for r in res]


def _finish_reduce(own, scattered, core, tag):
    mine = [_chip_sum(o, s, f"{tag}_{n}", tm=o.shape[0] // 2) for n, (o, s) in enumerate(zip(own, scattered))]
    theirs = _run(_x_join(mine), "join_halves_" + tag)
    return [lax.dynamic_update_slice(jnp.concatenate([a, a]), b, ((1 - core) * a.shape[0], 0))
            for a, b in zip(mine, theirs)]


def _gather_small(s):
    flips = [(fx, fy, fc) for fx in (0, 1) for fy in (0, 1) for fc in (0, 1)][1:]

    def body(s_ref, out_ref, send_sems, recv_sems, local_sem):
        x, y, c, _ = _place()
        sems = (send_sems, recv_sems)
        me = 4 * x + 2 * y + c
        local = pltpu.make_async_copy(s_ref, out_ref.at[me], local_sem)
        local.start()
        sent = [_remote(s_ref, out_ref.at[me], sems, k, (x ^ fx, y ^ fy, c ^ fc)) for k, (fx, fy, fc) in enumerate(flips)]
        for cp in sent:
            cp.start()
        for k, (fx, fy, fc) in enumerate(flips):
            peer = (x ^ fx, y ^ fy, c ^ fc)
            _remote(s_ref, out_ref.at[4 * peer[0] + 2 * peer[1] + peer[2]], sems, k, peer).wait_recv()
        for cp in sent:
            cp.wait_send()
        local.wait()

    return pl.pallas_call(
        body, name="gather_small", out_shape=jax.ShapeDtypeStruct((N_DEV, SMALL_ROWS, D_MODEL), s.dtype),
        in_specs=[ANY], out_specs=ANY,
        scratch_shapes=[pltpu.SemaphoreType.DMA((7,)), pltpu.SemaphoreType.DMA((7,)), pltpu.SemaphoreType.DMA],
    )(s)


LOCAL_NAMES = dict(ffn1_w_down="f1d", ffn2_w_down="f2d", w_branch_a="wa", w_branch_b="wb", w_out="wo", w_mq="wmq",
                   w_mkv="wmkv", w_mo="wmo")


def _assemble(names, own, others, me):
    by_flip = [jnp.concatenate([o[None], t], axis=0) for o, t in zip(own, others)]
    per_chip = [_unpack([lax.dynamic_index_in_dim(s, j ^ me, 0, keepdims=False) for s in by_flip], names)
                for j in range(N_CHIP)]
    return {name: jnp.concatenate([pc[name] for pc in per_chip], axis=axis)
            for name, _, axis in SHARDED if name in names}


def _local_names(full):
    out = {LOCAL_NAMES[name]: a for name, a in full.items() if name in LOCAL_NAMES}
    for name, key in (("ffn1_w_in", "f1"), ("ffn2_w_in", "f2")):
        if name in full:
            out[key + "g"], out[key + "u"] = full[name][:, :D_FF], full[name][:, D_FF:]
    if "w_in" in full:
        w_in = full["w_in"]
        out["w_main"] = jnp.concatenate([w_in[:, :FB_COL], w_in[:, FB_COL + HEADS:]], axis=1)
        out["w_fb"] = jnp.pad(w_in[:, FB_COL:FB_COL + HEADS], ((0, 0), (0, LANE - HEADS)))
    return out


def _grad_slabs(G, names):
    full = {name: G[key] for name, key in LOCAL_NAMES.items() if name in names}
    for name, key in (("ffn1_w_in", "f1"), ("ffn2_w_in", "f2")):
        if name in names:
            full[name] = jnp.concatenate([G[key + "g"], G[key + "u"]], axis=1)
    if "w_in" in names:
        main = jnp.concatenate(G["w_main"], axis=1)
        full["w_in"] = jnp.concatenate([main[:, :FB_COL], G["w_fb"][:, :HEADS], main[:, FB_COL:]], axis=1)
    rows, cols = [], []
    for j in range(N_CHIP):
        shards = {}
        for name, shape, axis in SHARDED:
            if name in names:
                n = shape[axis] // N_CHIP
                shards[name] = lax.slice_in_dim(full[name], j * n, (j + 1) * n, axis=axis)
        r, c = _pack(shards, names, F32)
        rows.append(r)
        cols.append(c)
    return [jnp.stack(rows, axis=0), jnp.stack(cols, axis=0)]


def kernel(x, mem, ffn1_pre_g, ffn1_w_in, ffn1_w_down, ffn1_post_g, mix_pre_g, w_in, hg_lb_logits, hg_norm_g, fox_f_bias, w_branch_a, w_branch_b, b_gate, w_out, mix_post_g, mem_pre_g, mem_kv_g, w_mq, w_mkv, w_mo, mem_post_g, ffn2_pre_g, ffn2_w_in, ffn2_w_down, ffn2_post_g, loss_target, m_ffn1_pre_g, m_ffn1_w_in, m_ffn1_w_down, m_ffn1_post_g, m_mix_pre_g, m_w_in, m_hg_lb_logits, m_hg_norm_g, m_fox_f_bias, m_w_branch_a, m_w_branch_b, m_b_gate, m_w_out, m_mix_post_g, m_mem_pre_g, m_mem_kv_g, m_w_mq, m_w_mkv, m_w_mo, m_mem_post_g, m_ffn2_pre_g, m_ffn2_w_in, m_ffn2_w_down, m_ffn2_post_g, v_ffn1_pre_g, v_ffn1_w_in, v_ffn1_w_down, v_ffn1_post_g, v_mix_pre_g, v_w_in, v_hg_lb_logits, v_hg_norm_g, v_fox_f_bias, v_w_branch_a, v_w_branch_b, v_b_gate, v_w_out, v_mix_post_g, v_mem_pre_g, v_mem_kv_g, v_w_mq, v_w_mkv, v_w_mo, v_mem_post_g, v_ffn2_pre_g, v_ffn2_w_in, v_ffn2_w_down, v_ffn2_post_g):
    w = dict(ffn1_pre_g=ffn1_pre_g, ffn1_w_in=ffn1_w_in, ffn1_w_down=ffn1_w_down, ffn1_post_g=ffn1_post_g, mix_pre_g=mix_pre_g, w_in=w_in, hg_lb_logits=hg_lb_logits, hg_norm_g=hg_norm_g, fox_f_bias=fox_f_bias, w_branch_a=w_branch_a, w_branch_b=w_branch_b, b_gate=b_gate, w_out=w_out, mix_post_g=mix_post_g, mem_pre_g=mem_pre_g, mem_kv_g=mem_kv_g, w_mq=w_mq, w_mkv=w_mkv, w_mo=w_mo, mem_post_g=mem_post_g, ffn2_pre_g=ffn2_pre_g, ffn2_w_in=ffn2_w_in, ffn2_w_down=ffn2_w_down, ffn2_post_g=ffn2_post_g)
    m = dict(ffn1_pre_g=m_ffn1_pre_g, ffn1_w_in=m_ffn1_w_in, ffn1_w_down=m_ffn1_w_down, ffn1_post_g=m_ffn1_post_g, mix_pre_g=m_mix_pre_g, w_in=m_w_in, hg_lb_logits=m_hg_lb_logits, hg_norm_g=m_hg_norm_g, fox_f_bias=m_fox_f_bias, w_branch_a=m_w_branch_a, w_branch_b=m_w_branch_b, b_gate=m_b_gate, w_out=m_w_out, mix_post_g=m_mix_post_g, mem_pre_g=m_mem_pre_g, mem_kv_g=m_mem_kv_g, w_mq=m_w_mq, w_mkv=m_w_mkv, w_mo=m_w_mo, mem_post_g=m_mem_post_g, ffn2_pre_g=m_ffn2_pre_g, ffn2_w_in=m_ffn2_w_in, ffn2_w_down=m_ffn2_w_down, ffn2_post_g=m_ffn2_post_g)
    v = dict(ffn1_pre_g=v_ffn1_pre_g, ffn1_w_in=v_ffn1_w_in, ffn1_w_down=v_ffn1_w_down, ffn1_post_g=v_ffn1_post_g, mix_pre_g=v_mix_pre_g, w_in=v_w_in, hg_lb_logits=v_hg_lb_logits, hg_norm_g=v_hg_norm_g, fox_f_bias=v_fox_f_bias, w_branch_a=v_w_branch_a, w_branch_b=v_w_branch_b, b_gate=v_b_gate, w_out=v_w_out, mix_post_g=v_mix_post_g, mem_pre_g=v_mem_pre_g, mem_kv_g=v_mem_kv_g, w_mq=v_w_mq, w_mkv=v_w_mkv, w_mo=v_w_mo, mem_post_g=v_mem_post_g, ffn2_pre_g=v_ffn2_pre_g, ffn2_w_in=v_ffn2_w_in, ffn2_w_down=v_ffn2_w_down, ffn2_post_g=v_ffn2_post_g)
    sharded = [name for name, _, _ in SHARDED]
    shard_of = lambda d: {name: d[name][0] for name in sharded}

    me, core = 2 * lax.axis_index("x") + lax.axis_index("y"), lax.axis_index("c")
    place = jnp.stack([me, core]).astype(jnp.int32)
    own = {group: _pack(shard_of(w), group, BF16) for group in (GROUP_FFN1, GROUP_MID, GROUP_FFN2)}
    P = {name: w[name] for name in SMALL}

    sq, dx0, G, reduced = _local_step(x[0], mem[0], loss_target[0], P, own, me, place, core)
    loss = lax.psum(0.5 * jnp.sum(sq) / D_MODEL, ("x", "y", "c"))

    g_shards = {}
    for group, slabs in reduced.items():
        g_shards.update(_unpack(slabs, group))
    big = {}
    for name, shape, axis in SHARDED:
        rows = shape[0] // (N_CHIP if axis == 0 else 1)
        big[name] = _adamw(w[name][0], g_shards[name], m[name][0], v[name][0], name="adamw_" + name, tm=rows // 8)
    small = _adamw(_pack_small(w), _gather_small(_pack_small(G)), _pack_small(m), _pack_small(v), name="adamw_small",
                   tm=SMALL_ROWS)

    outs = [loss, dx0[None]]
    for n in range(4):
        vals = {name: res[n][None] for name, res in big.items()}
        vals.update(_unpack_small(small[n]))
        outs += [vals[name] for name in WEIGHT_ORDER]
    return tuple(outs)
```
